```python
import math
import jax, jax.numpy as jnp
from jax import lax
import numpy as np

D_MODEL = 1024
BATCH = 8
SEQ = 2048
DEPTH = 1
DEC_BATCH = 128
DEC_SEQ = 1
PAST_LEN = 16384
PAGE_SIZE = 128

D_LRU = D_MODEL // 2
D_S5 = D_MODEL - D_LRU
D_MIX = D_LRU + D_S5
LRU_HEADS = 8
LRU_HEAD_DIM = D_LRU // LRU_HEADS
CONV_W = 4
LRU_C = 8.0
S5_GROUP = 16
S5_GROUPS = D_S5 // S5_GROUP
S5_STATE = 64
D_FF = int(math.ceil(8 * D_MODEL / 3 / 256)) * 256
EPS = 1e-6
DT_MIN = 1e-3
DT_MAX = 1e-1

kernel_name = 'hymba_rglru_s5_adaln_decode_step'

F32 = jnp.float32


def rmsnorm(x, g):
    x32 = x.astype(F32)
    y = x32 * lax.rsqrt(jnp.mean(x32 * x32, axis=-1, keepdims=True) + EPS)
    return y * g.astype(F32)


def _real_combine(l, r):
    a_l, b_l = l
    a_r, b_r = r
    return a_l * a_r, a_r * b_l + b_r


def linear_scan(a, b, h0):
    a_cum, h = lax.associative_scan(_real_combine, (a, b), axis=1)
    return h + a_cum * h0[:, None]


def _complex_combine(l, r):
    ar_l, ai_l, br_l, bi_l = l
    ar_r, ai_r, br_r, bi_r = r
    ar = ar_l * ar_r - ai_l * ai_r
    ai = ar_l * ai_r + ai_l * ar_r
    br = ar_r * br_l - ai_r * bi_l + br_r
    bi = ar_r * bi_l + ai_r * br_l + bi_r
    return ar, ai, br, bi


def complex_linear_scan(ar, ai, br, bi, h0r, h0i):
    cr, ci, hr, hi = lax.associative_scan(_complex_combine, (ar, ai, br, bi), axis=1)
    h0r_ = h0r[:, None]
    h0i_ = h0i[:, None]
    return hr + cr * h0r_ - ci * h0i_, hi + cr * h0i_ + ci * h0r_


def rg_lru(xc, h0, wa, ba, wx, bx, lam):
    B, T, _ = xc.shape
    xh = xc.reshape(B, T, LRU_HEADS, LRU_HEAD_DIM)
    r = jax.nn.sigmoid(jnp.einsum('bthi,hij->bthj', xh, wa.astype(F32)).reshape(B, T, D_LRU) + ba.astype(F32))
    i = jax.nn.sigmoid(jnp.einsum('bthi,hij->bthj', xh, wx.astype(F32)).reshape(B, T, D_LRU) + bx.astype(F32))
    log_a = LRU_C * r * jax.nn.log_sigmoid(lam.astype(F32))
    a = jnp.exp(log_a)
    mult = jnp.sqrt(-jnp.expm1(2.0 * log_a))
    hs = linear_scan(a, mult * (i * xc), h0.astype(F32))
    return hs, hs[:, -1]


def s5_layer(u, h0r, h0i, lam_re, lam_im, log_dt, b_re, b_im, c_re, c_im, d, w_glu):
    B, T, _ = u.shape
    ug = u.reshape(B, T, S5_GROUPS, S5_GROUP)
    lr = lam_re.astype(F32)
    li = lam_im.astype(F32)
    dt = jnp.exp(log_dt.astype(F32))[:, None]
    mag = jnp.exp(lr * dt)
    ab_r = mag * jnp.cos(li * dt)
    ab_i = mag * jnp.sin(li * dt)
    den = lr * lr + li * li
    fr = ((ab_r - 1.0) * lr + ab_i * li) / den
    fi = (ab_i * lr - (ab_r - 1.0) * li) / den
    br32 = b_re.astype(F32)
    bi32 = b_im.astype(F32)
    bb_r = fr[..., None] * br32 - fi[..., None] * bi32
    bb_i = fr[..., None] * bi32 + fi[..., None] * br32
    bu_r = jnp.einsum('btgc,gnc->btgn', ug, bb_r)
    bu_i = jnp.einsum('btgc,gnc->btgn', ug, bb_i)
    ar = jnp.broadcast_to(ab_r, bu_r.shape)
    ai = jnp.broadcast_to(ab_i, bu_i.shape)
    hr, hi = complex_linear_scan(ar, ai, bu_r, bu_i, h0r.astype(F32), h0i.astype(F32))
    y = jnp.einsum('btgn,gcn->btgc', hr, c_re.astype(F32)) - jnp.einsum('btgn,gcn->btgc', hi, c_im.astype(F32))
    y = y.reshape(B, T, D_S5) + d.astype(F32) * u
    z = jax.nn.gelu(y) @ w_glu.astype(F32)
    za, zb = jnp.split(z, 2, axis=-1)
    return za * jax.nn.sigmoid(zb), hr[:, -1], hi[:, -1]


def decoder_layer(x, c, conv_buf, lru_h0, s5_h0r, s5_h0i, p):
    T = x.shape[1]
    mod = jax.nn.silu(c.astype(F32)) @ p['ada_w'].astype(F32) + p['ada_b'].astype(F32)
    sh1, sc1, g1, sh2, sc2, g2 = jnp.split(mod[:, None, :], 6, axis=-1)

    hn = rmsnorm(x, p['norm1_g']) * (1.0 + sc1) + sh1
    proj = hn @ p['w_in'].astype(F32)
    lru_x = proj[..., :D_LRU]
    lru_gate = proj[..., D_LRU:2 * D_LRU]
    s5_u = proj[..., 2 * D_LRU:]

    xc_full = jnp.concatenate([conv_buf.astype(F32), lru_x], axis=1)
    conv_w = p['conv_w'].astype(F32)
    conv = p['conv_b'].astype(F32) + sum(conv_w[k] * xc_full[:, k:k + T] for k in range(CONV_W))
    new_conv = xc_full[:, -(CONV_W - 1):]
    hs, lru_last = rg_lru(conv, lru_h0, p['lru_wa'], p['lru_ba'], p['lru_wx'], p['lru_bx'], p['lru_lambda'])
    lru_out = hs * jax.nn.gelu(lru_gate)

    s5_out, s5_r, s5_i = s5_layer(s5_u, s5_h0r, s5_h0i, p['s5_lambda_re'], p['s5_lambda_im'], p['s5_log_dt'],
                                  p['s5_b_re'], p['s5_b_im'], p['s5_c_re'], p['s5_c_im'], p['s5_d'], p['s5_w_glu'])

    merged = jnp.concatenate([rmsnorm(lru_out, p['g_lru_out']), rmsnorm(s5_out, p['g_s5_out'])], axis=-1)
    x32 = x.astype(F32) + g1 * (merged @ p['w_out'].astype(F32))

    hn2 = rmsnorm(x32, p['norm2_g']) * (1.0 + sc2) + sh2
    ffn = (jax.nn.silu(hn2 @ p['ffn_w_gate'].astype(F32)) * (hn2 @ p['ffn_w_up'].astype(F32))) @ p['ffn_w_down'].astype(F32)
    x32 = x32 + g2 * ffn
    return x32.astype(x.dtype), new_conv, lru_last, s5_r, s5_i


def setup_inputs(seed: int = 0) -> dict:
    key = jax.random.key(seed)
    ks = jax.random.split(key, 40)
    nrm = lambda k, shape, s: jax.random.normal(k, shape, F32) * s
    u = jax.random.uniform(ks[12], (DEPTH, D_LRU), F32, 0.9, 0.999)
    s = u ** (1.0 / LRU_C)
    lru_lambda = jnp.log(s) - jnp.log1p(-s)
    n_idx = jnp.arange(S5_STATE, dtype=F32)
    s5_lambda_re = -0.5 + nrm(ks[13], (DEPTH, S5_GROUPS, S5_STATE), 0.01)
    s5_lambda_im = math.pi * n_idx + nrm(ks[14], (DEPTH, S5_GROUPS, S5_STATE), 0.01)
    s5_log_dt = jax.random.uniform(ks[15], (DEPTH, S5_GROUPS), F32, math.log(DT_MIN), math.log(DT_MAX))
    return {
        'x_prompt': nrm(ks[0], (BATCH, SEQ, D_MODEL), 1.0),
        'x_sample': nrm(ks[1], (DEC_BATCH, DEC_SEQ, D_MODEL), 1.0),
        'state_conv': nrm(ks[2], (DEPTH, DEC_BATCH, CONV_W - 1, D_LRU), 1.0),
        'state_lru': nrm(ks[3], (DEPTH, DEC_BATCH, D_LRU), 0.5),
        'state_s5_re': nrm(ks[4], (DEPTH, DEC_BATCH, S5_GROUPS, S5_STATE), 0.5),
        'state_s5_im': nrm(ks[5], (DEPTH, DEC_BATCH, S5_GROUPS, S5_STATE), 0.5),
        'c_prompt': nrm(ks[6], (BATCH, D_MODEL), 1.0),
        'c_sample': nrm(ks[7], (DEC_BATCH, D_MODEL), 1.0),
        'ada_w': nrm(ks[8], (DEPTH, D_MODEL, 6 * D_MODEL), 0.5 * D_MODEL ** -0.5),
        'ada_b': nrm(ks[9], (DEPTH, 6 * D_MODEL), 0.01),
        'norm1_g': 1.0 + nrm(ks[10], (DEPTH, D_MODEL), 0.01),
        'w_in': nrm(ks[11], (DEPTH, D_MODEL, 2 * D_LRU + D_S5), D_MODEL ** -0.5),
        'conv_w': nrm(ks[16], (DEPTH, CONV_W, D_LRU), CONV_W ** -0.5),
        'conv_b': nrm(ks[17], (DEPTH, D_LRU), 0.01),
        'lru_wa': nrm(ks[18], (DEPTH, LRU_HEADS, LRU_HEAD_DIM, LRU_HEAD_DIM), LRU_HEAD_DIM ** -0.5),
        'lru_ba': nrm(ks[19], (DEPTH, D_LRU), 0.01),
        'lru_wx': nrm(ks[20], (DEPTH, LRU_HEADS, LRU_HEAD_DIM, LRU_HEAD_DIM), LRU_HEAD_DIM ** -0.5),
        'lru_bx': nrm(ks[21], (DEPTH, D_LRU), 0.01),
        'lru_lambda': lru_lambda,
        's5_lambda_re': s5_lambda_re,
        's5_lambda_im': s5_lambda_im,
        's5_log_dt': s5_log_dt,
        's5_b_re': nrm(ks[22], (DEPTH, S5_GROUPS, S5_STATE, S5_GROUP), (2 * S5_GROUP) ** -0.5),
        's5_b_im': nrm(ks[23], (DEPTH, S5_GROUPS, S5_STATE, S5_GROUP), (2 * S5_GROUP) ** -0.5),
        's5_c_re': nrm(ks[24], (DEPTH, S5_GROUPS, S5_GROUP, S5_STATE), (2 * S5_STATE) ** -0.5),
        's5_c_im': nrm(ks[25], (DEPTH, S5_GROUPS, S5_GROUP, S5_STATE), (2 * S5_STATE) ** -0.5),
        's5_d': nrm(ks[26], (DEPTH, D_S5), 1.0),
        's5_w_glu': nrm(ks[27], (DEPTH, D_S5, 2 * D_S5), D_S5 ** -0.5),
        'g_lru_out': 1.0 + nrm(ks[28], (DEPTH, D_LRU), 0.01),
        'g_s5_out': 1.0 + nrm(ks[29], (DEPTH, D_S5), 0.01),
        'w_out': nrm(ks[30], (DEPTH, D_MIX, D_MODEL), D_MIX ** -0.5),
        'norm2_g': 1.0 + nrm(ks[31], (DEPTH, D_MODEL), 0.01),
        'ffn_w_gate': nrm(ks[32], (DEPTH, D_MODEL, D_FF), D_MODEL ** -0.5),
        'ffn_w_up': nrm(ks[33], (DEPTH, D_MODEL, D_FF), D_MODEL ** -0.5),
        'ffn_w_down': nrm(ks[34], (DEPTH, D_FF, D_MODEL), D_FF ** -0.5),
        'final_norm_g': 1.0 + nrm(ks[35], (D_MODEL,), 0.01),
    }


def reference(x_prompt, x_sample, state_conv, state_lru, state_s5_re, state_s5_im, c_prompt, c_sample,
              ada_w, ada_b, norm1_g, w_in, conv_w, conv_b, lru_wa, lru_ba, lru_wx, lru_bx, lru_lambda,
              s5_lambda_re, s5_lambda_im, s5_log_dt, s5_b_re, s5_b_im, s5_c_re, s5_c_im, s5_d, s5_w_glu,
              g_lru_out, g_s5_out, w_out, norm2_g, ffn_w_gate, ffn_w_up, ffn_w_down, final_norm_g):
    bp = x_prompt.shape[0]
    yp = x_prompt
    ys = x_sample
    conv_p, lru_p, s5r_p, s5i_p = [], [], [], []
    conv_s, lru_s, s5r_s, s5i_s = [], [], [], []
    for l in range(DEPTH):
        p = dict(ada_w=ada_w[l], ada_b=ada_b[l], norm1_g=norm1_g[l], w_in=w_in[l], conv_w=conv_w[l],
                 conv_b=conv_b[l], lru_wa=lru_wa[l], lru_ba=lru_ba[l], lru_wx=lru_wx[l], lru_bx=lru_bx[l],
                 lru_lambda=lru_lambda[l], s5_lambda_re=s5_lambda_re[l], s5_lambda_im=s5_lambda_im[l],
                 s5_log_dt=s5_log_dt[l], s5_b_re=s5_b_re[l], s5_b_im=s5_b_im[l], s5_c_re=s5_c_re[l],
                 s5_c_im=s5_c_im[l], s5_d=s5_d[l], s5_w_glu=s5_w_glu[l], g_lru_out=g_lru_out[l],
                 g_s5_out=g_s5_out[l], w_out=w_out[l], norm2_g=norm2_g[l], ffn_w_gate=ffn_w_gate[l],
                 ffn_w_up=ffn_w_up[l], ffn_w_down=ffn_w_down[l])
        yp, cp, hp, rp, ip = decoder_layer(
            yp, c_prompt,
            jnp.zeros((bp, CONV_W - 1, D_LRU), F32), jnp.zeros((bp, D_LRU), F32),
            jnp.zeros((bp, S5_GROUPS, S5_STATE), F32), jnp.zeros((bp, S5_GROUPS, S5_STATE), F32), p)
        ys, cs, hs, rs, is_ = decoder_layer(
            ys, c_sample, state_conv[l], state_lru[l], state_s5_re[l], state_s5_im[l], p)
        conv_p.append(cp); lru_p.append(hp); s5r_p.append(rp); s5i_p.append(ip)
        conv_s.append(cs); lru_s.append(hs); s5r_s.append(rs); s5i_s.append(is_)
    y_prompt = rmsnorm(yp, final_norm_g).astype(x_prompt.dtype)
    y_sample = rmsnorm(ys, final_norm_g).astype(x_sample.dtype)
    conv_prompt = jnp.stack(conv_p, 0).astype(state_conv.dtype)
    lru_prompt = jnp.stack(lru_p, 0).astype(state_lru.dtype)
    s5_re_prompt = jnp.stack(s5r_p, 0).astype(state_s5_re.dtype)
    s5_im_prompt = jnp.stack(s5i_p, 0).astype(state_s5_im.dtype)
    conv_sample = jnp.stack(conv_s, 0).astype(state_conv.dtype)
    lru_sample = jnp.stack(lru_s, 0).astype(state_lru.dtype)
    s5_re_sample = jnp.stack(s5r_s, 0).astype(state_s5_re.dtype)
    s5_im_sample = jnp.stack(s5i_s, 0).astype(state_s5_im.dtype)
    return (y_prompt, y_sample, conv_prompt, lru_prompt, s5_re_prompt, s5_im_prompt,
            conv_sample, lru_sample, s5_re_sample, s5_im_sample)
```

```python
import functools
import math

import jax
import jax.numpy as jnp
from jax import lax
from jax.experimental import pallas as pl
from jax.experimental.pallas import tpu as pltpu

F32 = jnp.float32
BF16 = jnp.bfloat16

D_MODEL = 1024
D_LRU = 512
D_S5 = 512
LRU_HEADS = 8
LRU_HEAD_DIM = D_LRU // LRU_HEADS
CONV_W = 4
LRU_C = 8.0
S5_GROUP = 16
S5_GROUPS = D_S5 // S5_GROUP
S5_STATE = 64
S5_LANES = S5_GROUPS * S5_STATE
D_FF = 2816
EPS = 1e-6

SUBLANES = 8
MXU_DIM = 256
VMEM_LIMIT_BYTES = 56 * 1024 * 1024

TIME_CHUNK = 64
FFN_TIME_CHUNK = 64
GATE_BLOCK = MXU_DIM
GATE_BLOCKS = D_LRU // GATE_BLOCK
S5_BLOCKS = 4
S5_BLOCK_CH = D_S5 // S5_BLOCKS
S5_BLOCK_LANES = S5_LANES // S5_BLOCKS
FFN_CHUNK = MXU_DIM
FFN_CHUNKS = D_FF // FFN_CHUNK


def _dot(a, b):
    return jnp.dot(a, b, preferred_element_type=F32)


def _rms(x, g):
    return x * lax.rsqrt(jnp.mean(x * x, axis=-1, keepdims=True) + EPS) * g


def _log_sigmoid(x):
    return jnp.minimum(x, 0.0) - jnp.log1p(jnp.exp(-jnp.abs(x)))


def _lru_gate_block(conv_j, wg_j, ba_j, bx_j, lam_j):
    gz = _dot(conv_j.astype(BF16), wg_j)
    r = jax.nn.sigmoid(gz[:, :GATE_BLOCK] + ba_j)
    ig = jax.nn.sigmoid(gz[:, GATE_BLOCK:] + bx_j)
    log_a = LRU_C * r * _log_sigmoid(lam_j)
    a = jnp.exp(log_a)
    mult = jnp.sqrt(-jnp.tanh(log_a) * (a * a + 1.0))
    return a, mult * (ig * conv_j)


def _adaln_kernel(c_ref, w_ref, b_ref, o_ref):
    c = c_ref[...]
    s = (c * jax.nn.sigmoid(c)).astype(BF16)
    o_ref[...] = _dot(s, w_ref[...].astype(BF16)) + b_ref[...]


def _adaln(c_all, ada_w, ada_b):
    rows = c_all.shape[0]
    n = ada_w.shape[1]
    tn = 512
    return pl.pallas_call(
        _adaln_kernel,
        grid=(n // tn,),
        in_specs=[
            pl.BlockSpec((rows, D_MODEL), lambda j: (0, 0)),
            pl.BlockSpec((D_MODEL, tn), lambda j: (0, j)),
            pl.BlockSpec((1, tn), lambda j: (0, j)),
        ],
        out_specs=pl.BlockSpec((rows, tn), lambda j: (0, j)),
        out_shape=jax.ShapeDtypeStruct((rows, n), F32),
        compiler_params=pltpu.CompilerParams(dimension_semantics=("arbitrary",)),
        name="adaln",
    )(c_all, ada_w, ada_b)


def _s5_prep_kernel(lr_ref, li_ref, ldt_ref, bre_ref, bim_ref, abr_ref, abi_ref, bbr_ref, bbi_ref):
    lr = lr_ref[...]
    li = li_ref[...]
    dt = jnp.exp(ldt_ref[...])
    mag = jnp.exp(lr * dt)
    ab_r = mag * jnp.cos(li * dt)
    ab_i = mag * jnp.sin(li * dt)
    den = lr * lr + li * li
    fr = ((ab_r - 1.0) * lr + ab_i * li) / den
    fi = (ab_i * lr - (ab_r - 1.0) * li) / den
    br = bre_ref[...]
    bi = bim_ref[...]
    abr_ref[...] = ab_r
    abi_ref[...] = ab_i
    bbr_ref[...] = fr * br - fi * bi
    bbi_ref[...] = fr * bi + fi * br


def _s5_prep(lam_re, lam_im, log_dt, b_re, b_im):
    lr = lam_re.reshape(1, S5_LANES)
    li = lam_im.reshape(1, S5_LANES)
    ldt = jnp.repeat(log_dt, S5_STATE).reshape(1, S5_LANES)
    bre_t = b_re.reshape(S5_LANES, S5_GROUP).T
    bim_t = b_im.reshape(S5_LANES, S5_GROUP).T
    row = jax.ShapeDtypeStruct((1, S5_LANES), F32)
    mat = jax.ShapeDtypeStruct((S5_GROUP, S5_LANES), F32)
    return pl.pallas_call(_s5_prep_kernel, out_shape=(row, row, mat, mat), name="s5_prep")(
        lr, li, ldt, bre_t, bim_t)


def _s5_input_proj(u, wb_ref, hr_ref, hi_ref):
    ub = u.astype(BF16)
    for j in range(S5_BLOCKS):
        bu = _dot(ub[:, j * S5_BLOCK_CH:(j + 1) * S5_BLOCK_CH], wb_ref[j])
        sl = slice(j * S5_BLOCK_LANES, (j + 1) * S5_BLOCK_LANES)
        hr_ref[:, sl] = bu[:, :S5_BLOCK_LANES]
        hi_ref[:, sl] = bu[:, S5_BLOCK_LANES:]


def _s5_output(u, hr_ref, hi_ref, wc_ref, d, wglu_ref, y_ref):
    for j in range(S5_BLOCKS):
        sl = slice(j * S5_BLOCK_LANES, (j + 1) * S5_BLOCK_LANES)
        yj = (_dot(hr_ref[:, sl].astype(BF16), wc_ref[j, :S5_BLOCK_LANES, :])
              + _dot(hi_ref[:, sl].astype(BF16), wc_ref[j, S5_BLOCK_LANES:, :]))
        y_ref[:, j * S5_BLOCK_CH:(j + 1) * S5_BLOCK_CH] = yj
    y = y_ref[...] + d * u
    z = _dot(jax.nn.gelu(y).astype(BF16), wglu_ref[...])
    return z[:, :D_S5] * jax.nn.sigmoid(z[:, D_S5:])


def _mixer_prompt_kernel(
        x_ref, mod_ref, n1g_ref, win_ref, cw_ref, cb_ref, wg_ref, ba_ref, bx_ref, lam_ref,
        abr_ref, abi_ref, wb_ref, wc_ref, d_ref, wglu_ref, glru_ref, gs5_ref, wout_ref,
        o_ref, conv_o, lru_o, s5r_o, s5i_o,
        xc_buf, h_lru, h_s5r, h_s5i, p_buf, hr_buf, hi_buf, a_buf, b_buf, y_buf, m_buf):
    step = pl.program_id(0)
    tl = x_ref.shape[0]
    nb = x_ref.shape[1]
    rows = tl * nb
    hist = (CONV_W - 1) * nb

    @pl.when(step == 0)
    def _():
        xc_buf[0:hist, :] = jnp.zeros((hist, D_LRU), F32)
        h_lru[...] = jnp.zeros_like(h_lru)
        h_s5r[...] = jnp.zeros_like(h_s5r)
        h_s5i[...] = jnp.zeros_like(h_s5i)

    sh1 = mod_ref[:, 0:D_MODEL]
    sc1 = mod_ref[:, D_MODEL:2 * D_MODEL]
    g1 = mod_ref[:, 2 * D_MODEL:3 * D_MODEL]

    x3 = x_ref[...]
    hn = _rms(x3, n1g_ref[...]) * (1.0 + sc1)[None] + sh1[None]
    p_buf[...] = _dot(hn.reshape(rows, D_MODEL).astype(BF16), win_ref[...])

    xc_buf[hist:hist + rows, :] = p_buf[:, 0:D_LRU]
    for j in range(GATE_BLOCKS):
        sl = slice(j * GATE_BLOCK, (j + 1) * GATE_BLOCK)
        conv_j = cb_ref[:, sl]
        for k in range(CONV_W):
            conv_j = conv_j + cw_ref[k:k + 1, sl] * xc_buf[k * nb:k * nb + rows, sl]
        a, b = _lru_gate_block(conv_j, wg_ref[j], ba_ref[:, sl], bx_ref[:, sl], lam_ref[:, sl])
        a_buf[:, sl] = a
        b_buf[:, sl] = b

    def lru_step(t, h):
        r0 = pl.multiple_of(t * nb, SUBLANES)
        h = a_buf[pl.ds(r0, nb), :] * h + b_buf[pl.ds(r0, nb), :]
        b_buf[pl.ds(r0, nb), :] = h
        return h

    h_lru[...] = lax.fori_loop(0, tl, lru_step, h_lru[...], unroll=8)

    lru_out = b_buf[...] * jax.nn.gelu(p_buf[:, D_LRU:2 * D_LRU])
    m_buf[:, 0:D_LRU] = _rms(lru_out, glru_ref[...]).astype(BF16)

    u = p_buf[:, 2 * D_LRU:]
    _s5_input_proj(u, wb_ref, hr_buf, hi_buf)
    for j in range(S5_BLOCKS):
        sl = slice(j * S5_BLOCK_LANES, (j + 1) * S5_BLOCK_LANES)
        ar = jnp.broadcast_to(abr_ref[:, sl], (nb, S5_BLOCK_LANES))
        ai = jnp.broadcast_to(abi_ref[:, sl], (nb, S5_BLOCK_LANES))

        def s5_step(t, carry, sl=sl, ar=ar, ai=ai):
            hr, hi = carry
            r0 = pl.multiple_of(t * nb, SUBLANES)
            nr = ar * hr - ai * hi + hr_buf[pl.ds(r0, nb), sl]
            ni = ar * hi + ai * hr + hi_buf[pl.ds(r0, nb), sl]
            hr_buf[pl.ds(r0, nb), sl] = nr
            hi_buf[pl.ds(r0, nb), sl] = ni
            return nr, ni

        hr, hi = lax.fori_loop(0, tl, s5_step, (h_s5r[:, sl], h_s5i[:, sl]), unroll=4)
        h_s5r[:, sl] = hr
        h_s5i[:, sl] = hi

    s5_out = _s5_output(u, hr_buf, hi_buf, wc_ref, d_ref[...], wglu_ref, y_buf)
    m_buf[:, D_LRU:] = _rms(s5_out, gs5_ref[...]).astype(BF16)

    mix = _dot(m_buf[...], wout_ref[...]).reshape(tl, nb, D_MODEL)
    o_ref[...] = x3 + g1[None] * mix

    tail = xc_buf[rows:rows + hist, :]
    xc_buf[0:hist, :] = tail

    @pl.when(step == pl.num_programs(0) - 1)
    def _():
        conv_o[...] = tail
        lru_o[...] = h_lru[...]
        s5r_o[...] = h_s5r[...]
        s5i_o[...] = h_s5i[...]


def _const_spec(shape):
    nd = len(shape)
    return pl.BlockSpec(shape, lambda i, _nd=nd: (0,) * _nd, pipeline_mode=pl.Buffered(1))


def _mixer_prompt(x_t, mod_p, w):
    t_len, nb, _ = x_t.shape
    tl = TIME_CHUNK
    rows = tl * nb
    hist = (CONV_W - 1) * nb
    consts = [mod_p, w["n1g"], w["win"], w["cw"], w["cb"], w["wg"], w["ba"], w["bx"], w["lam"],
              w["abr"], w["abi"], w["wb"], w["wc"], w["d"], w["wglu"], w["glru"], w["gs5"], w["wout"]]
    io_spec = pl.BlockSpec((tl, nb, D_MODEL), lambda i: (i, 0, 0))
    out_shape = (
        jax.ShapeDtypeStruct((t_len, nb, D_MODEL), F32),
        jax.ShapeDtypeStruct((hist, D_LRU), F32),
        jax.ShapeDtypeStruct((nb, D_LRU), F32),
        jax.ShapeDtypeStruct((nb, S5_LANES), F32),
        jax.ShapeDtypeStruct((nb, S5_LANES), F32),
    )
    out_specs = (
        io_spec,
        pl.BlockSpec((hist, D_LRU), lambda i: (0, 0)),
        pl.BlockSpec((nb, D_LRU), lambda i: (0, 0)),
        pl.BlockSpec((nb, S5_LANES), lambda i: (0, 0)),
        pl.BlockSpec((nb, S5_LANES), lambda i: (0, 0)),
    )
    scratch = [
        pltpu.VMEM((hist + rows, D_LRU), F32),
        pltpu.VMEM((nb, D_LRU), F32),
        pltpu.VMEM((nb, S5_LANES), F32),
        pltpu.VMEM((nb, S5_LANES), F32),
        pltpu.VMEM((rows, 2 * D_LRU + D_S5), F32),
        pltpu.VMEM((rows, S5_LANES), F32),
        pltpu.VMEM((rows, S5_LANES), F32),
        pltpu.VMEM((rows, D_LRU), F32),
        pltpu.VMEM((rows, D_LRU), F32),
        pltpu.VMEM((rows, D_S5), F32),
        pltpu.VMEM((rows, D_LRU + D_S5), BF16),
    ]
    return pl.pallas_call(
        _mixer_prompt_kernel,
        grid=(t_len // tl,),
        in_specs=[io_spec] + [_const_spec(c.shape) for c in consts],
        out_specs=out_specs,
        out_shape=out_shape,
        scratch_shapes=scratch,
        compiler_params=pltpu.CompilerParams(
            dimension_semantics=("arbitrary",), vmem_limit_bytes=VMEM_LIMIT_BYTES),
        name="mixer_prompt",
    )(x_t, *consts)


def _mixer_sample_kernel(
        x_ref, mod_ref, sconv_ref, slru_ref, ss5r_ref, ss5i_ref,
        n1g_ref, win_ref, cw_ref, cb_ref, wg_ref, ba_ref, bx_ref, lam_ref,
        abr_ref, abi_ref, wb_ref, wc_ref, d_ref, wglu_ref, glru_ref, gs5_ref, wout_ref,
        o_ref, conv_o, lru_o, s5r_o, s5i_o,
        y_buf, m_buf):
    sh1 = mod_ref[:, 0:D_MODEL]
    sc1 = mod_ref[:, D_MODEL:2 * D_MODEL]
    g1 = mod_ref[:, 2 * D_MODEL:3 * D_MODEL]

    x = x_ref[...]
    hn = _rms(x, n1g_ref[...]) * (1.0 + sc1) + sh1
    p = _dot(hn.astype(BF16), win_ref[...])
    lru_x = p[:, 0:D_LRU]

    hist_w = (CONV_W - 1) * D_LRU
    conv_o[:, 0:hist_w - D_LRU] = sconv_ref[:, D_LRU:hist_w]
    conv_o[:, hist_w - D_LRU:hist_w] = lru_x
    for j in range(GATE_BLOCKS):
        sl = slice(j * GATE_BLOCK, (j + 1) * GATE_BLOCK)
        conv_j = cb_ref[:, sl] + cw_ref[CONV_W - 1:CONV_W, sl] * lru_x[:, sl]
        for k in range(CONV_W - 1):
            conv_j = conv_j + cw_ref[k:k + 1, sl] * sconv_ref[:, k * D_LRU + j * GATE_BLOCK:
                                                               k * D_LRU + (j + 1) * GATE_BLOCK]
        a, b = _lru_gate_block(conv_j, wg_ref[j], ba_ref[:, sl], bx_ref[:, sl], lam_ref[:, sl])
        lru_o[:, sl] = a * slru_ref[:, sl] + b
    lru_out = lru_o[...] * jax.nn.gelu(p[:, D_LRU:2 * D_LRU])
    m_buf[:, 0:D_LRU] = _rms(lru_out, glru_ref[...]).astype(BF16)

    u = p[:, 2 * D_LRU:]
    _s5_input_proj(u, wb_ref, s5r_o, s5i_o)
    ar = abr_ref[...]
    ai = abi_ref[...]
    h0r = ss5r_ref[...]
    h0i = ss5i_ref[...]
    s5r_o[...] = ar * h0r - ai * h0i + s5r_o[...]
    s5i_o[...] = ar * h0i + ai * h0r + s5i_o[...]

    s5_out = _s5_output(u, s5r_o, s5i_o, wc_ref, d_ref[...], wglu_ref, y_buf)
    m_buf[:, D_LRU:] = _rms(s5_out, gs5_ref[...]).astype(BF16)
    o_ref[...] = x + g1 * _dot(m_buf[...], wout_ref[...])


def _mixer_sample(x_s, mod_s, sconv, slru, ss5r, ss5i, w):
    nb = x_s.shape[0]
    out_shape = (
        jax.ShapeDtypeStruct((nb, D_MODEL), F32),
        jax.ShapeDtypeStruct((nb, (CONV_W - 1) * D_LRU), F32),
        jax.ShapeDtypeStruct((nb, D_LRU), F32),
        jax.ShapeDtypeStruct((nb, S5_LANES), F32),
        jax.ShapeDtypeStruct((nb, S5_LANES), F32),
    )
    scratch = [pltpu.VMEM((nb, D_S5), F32), pltpu.VMEM((nb, D_LRU + D_S5), BF16)]
    return pl.pallas_call(
        _mixer_sample_kernel,
        out_shape=out_shape,
        scratch_shapes=scratch,
        compiler_params=pltpu.CompilerParams(vmem_limit_bytes=VMEM_LIMIT_BYTES),
        name="mixer_sample",
    )(x_s, mod_s, sconv, slru, ss5r, ss5i,
      w["n1g"], w["win"], w["cw"], w["cb"], w["wg"], w["ba"], w["bx"], w["lam"],
      w["abr"], w["abi"], w["wb"], w["wc"], w["d"], w["wglu"], w["glru"], w["gs5"], w["wout"])


def _ffn_kernel(x_ref, mod_ref, n2g_ref, wgate_ref, wup_ref, wdown_ref, gf_ref, o_ref, act_buf):
    groups, mrows, _ = x_ref.shape
    rows = groups * mrows
    sh2 = mod_ref[:, 3 * D_MODEL:4 * D_MODEL]
    sc2 = mod_ref[:, 4 * D_MODEL:5 * D_MODEL]
    g2 = mod_ref[:, 5 * D_MODEL:6 * D_MODEL]

    x3 = x_ref[...]
    hn = _rms(x3, n2g_ref[...]) * (1.0 + sc2)[None] + sh2[None]
    hb = hn.reshape(rows, D_MODEL).astype(BF16)
    for c in range(FFN_CHUNKS):
        g = _dot(hb, wgate_ref[c])
        up = _dot(hb, wup_ref[c])
        act_buf[:, c * FFN_CHUNK:(c + 1) * FFN_CHUNK] = (g * jax.nn.sigmoid(g) * up).astype(BF16)
    ffn = _dot(act_buf[...], wdown_ref[...]).reshape(groups, mrows, D_MODEL)
    o_ref[...] = _rms(x3 + g2[None] * ffn, gf_ref[...])


def _ffn(x3, mod, w, groups_per_step):
    n_groups, mrows, _ = x3.shape
    rows = groups_per_step * mrows
    io_spec = pl.BlockSpec((groups_per_step, mrows, D_MODEL), lambda i: (i, 0, 0))
    consts = [mod, w["n2g"], w["wgate"], w["wup"], w["wdown"], w["gf"]]
    return pl.pallas_call(
        _ffn_kernel,
        grid=(n_groups // groups_per_step,),
        in_specs=[io_spec] + [_const_spec(c.shape) for c in consts],
        out_specs=io_spec,
        out_shape=jax.ShapeDtypeStruct(x3.shape, F32),
        scratch_shapes=[pltpu.VMEM((rows, D_FF), BF16)],
        compiler_params=pltpu.CompilerParams(
            dimension_semantics=("arbitrary",), vmem_limit_bytes=VMEM_LIMIT_BYTES),
        name="ffn",
    )(x3, *consts)


def _block_diag(blocks):
    n, r, c = blocks.shape
    eye = jnp.eye(n, dtype=blocks.dtype)
    return (eye[:, None, :, None] * blocks[:, :, None, :]).reshape(n * r, n * c)


def _prep_weights(p, abr, abi, bbr_t, bbi_t):
    w = {}
    row = lambda v: v.reshape(1, -1)
    w["n1g"] = row(p["norm1_g"])
    w["win"] = p["w_in"].astype(BF16)
    w["cw"] = p["conv_w"]
    w["cb"] = row(p["conv_b"])
    wa = _block_diag(p["lru_wa"])
    wx = _block_diag(p["lru_wx"])
    w["wg"] = jnp.stack([
        jnp.concatenate([wa[j * GATE_BLOCK:(j + 1) * GATE_BLOCK, j * GATE_BLOCK:(j + 1) * GATE_BLOCK],
                         wx[j * GATE_BLOCK:(j + 1) * GATE_BLOCK, j * GATE_BLOCK:(j + 1) * GATE_BLOCK]], axis=1)
        for j in range(GATE_BLOCKS)]).astype(BF16)
    w["ba"] = row(p["lru_ba"])
    w["bx"] = row(p["lru_bx"])
    w["lam"] = row(p["lru_lambda"])
    w["abr"] = abr
    w["abi"] = abi
    to_blocks = lambda m: m.reshape(S5_GROUP, S5_GROUPS, S5_STATE).transpose(1, 0, 2)
    wbr = _block_diag(to_blocks(bbr_t))
    wbi = _block_diag(to_blocks(bbi_t))
    w["wb"] = jnp.stack([
        jnp.concatenate([wbr[j * S5_BLOCK_CH:(j + 1) * S5_BLOCK_CH, j * S5_BLOCK_LANES:(j + 1) * S5_BLOCK_LANES],
                         wbi[j * S5_BLOCK_CH:(j + 1) * S5_BLOCK_CH, j * S5_BLOCK_LANES:(j + 1) * S5_BLOCK_LANES]],
                        axis=1)
        for j in range(S5_BLOCKS)]).astype(BF16)
    wcr = _block_diag(p["s5_c_re"].transpose(0, 2, 1))
    wci = _block_diag(-p["s5_c_im"].transpose(0, 2, 1))
    w["wc"] = jnp.stack([
        jnp.concatenate([wcr[j * S5_BLOCK_LANES:(j + 1) * S5_BLOCK_LANES, j * S5_BLOCK_CH:(j + 1) * S5_BLOCK_CH],
                         wci[j * S5_BLOCK_LANES:(j + 1) * S5_BLOCK_LANES, j * S5_BLOCK_CH:(j + 1) * S5_BLOCK_CH]],
                        axis=0)
        for j in range(S5_BLOCKS)]).astype(BF16)
    w["d"] = row(p["s5_d"])
    w["wglu"] = p["s5_w_glu"].astype(BF16)
    w["glru"] = row(p["g_lru_out"])
    w["gs5"] = row(p["g_s5_out"])
    w["wout"] = p["w_out"].astype(BF16)
    w["n2g"] = row(p["norm2_g"])
    to_chunks = lambda m: m.reshape(D_MODEL, FFN_CHUNKS, FFN_CHUNK).transpose(1, 0, 2)
    w["wgate"] = to_chunks(p["ffn_w_gate"]).astype(BF16)
    w["wup"] = to_chunks(p["ffn_w_up"]).astype(BF16)
    w["wdown"] = p["ffn_w_down"].astype(BF16)
    w["gf"] = row(p["final_norm_g"])
    return w


def kernel(x_prompt, x_sample, state_conv, state_lru, state_s5_re, state_s5_im, c_prompt, c_sample,
           ada_w, ada_b, norm1_g, w_in, conv_w, conv_b, lru_wa, lru_ba, lru_wx, lru_bx, lru_lambda,
           s5_lambda_re, s5_lambda_im, s5_log_dt, s5_b_re, s5_b_im, s5_c_re, s5_c_im, s5_d, s5_w_glu,
           g_lru_out, g_s5_out, w_out, norm2_g, ffn_w_gate, ffn_w_up, ffn_w_down, final_norm_g):
    depth = ada_w.shape[0]
    assert depth == 1, "single-layer decoder step"
    bp, t_len, _ = x_prompt.shape
    bs, dec_t, _ = x_sample.shape
    assert bp == SUBLANES and dec_t == 1 and t_len % TIME_CHUNK == 0 and t_len % FFN_TIME_CHUNK == 0

    p = dict(norm1_g=norm1_g[0], w_in=w_in[0], conv_w=conv_w[0], conv_b=conv_b[0], lru_wa=lru_wa[0],
             lru_ba=lru_ba[0], lru_wx=lru_wx[0], lru_bx=lru_bx[0], lru_lambda=lru_lambda[0],
             s5_c_re=s5_c_re[0], s5_c_im=s5_c_im[0], s5_d=s5_d[0], s5_w_glu=s5_w_glu[0],
             g_lru_out=g_lru_out[0], g_s5_out=g_s5_out[0], w_out=w_out[0], norm2_g=norm2_g[0],
             ffn_w_gate=ffn_w_gate[0], ffn_w_up=ffn_w_up[0], ffn_w_down=ffn_w_down[0],
             final_norm_g=final_norm_g)

    mod = _adaln(jnp.concatenate([c_prompt, c_sample], axis=0), ada_w[0], ada_b[0].reshape(1, -1))
    mod_p, mod_s = mod[:bp], mod[bp:]
    abr, abi, bbr_t, bbi_t = _s5_prep(s5_lambda_re[0], s5_lambda_im[0], s5_log_dt[0], s5_b_re[0], s5_b_im[0])
    w = _prep_weights(p, abr, abi, bbr_t, bbi_t)

    x_t = x_prompt.transpose(1, 0, 2)
    x32_t, conv_p, lru_p, s5r_p, s5i_p = _mixer_prompt(x_t, mod_p, w)
    y_t = _ffn(x32_t, mod_p, w, FFN_TIME_CHUNK)
    y_prompt = y_t.transpose(1, 0, 2)

    x32_s, conv_s, lru_s, s5r_s, s5i_s = _mixer_sample(
        x_sample.reshape(bs, D_MODEL), mod_s,
        state_conv[0].reshape(bs, (CONV_W - 1) * D_LRU), state_lru[0],
        state_s5_re[0].reshape(bs, S5_LANES), state_s5_im[0].reshape(bs, S5_LANES), w)
    y_sample = _ffn(x32_s.reshape(1, bs, D_MODEL), mod_s, w, 1).reshape(bs, 1, D_MODEL)

    conv_prompt = conv_p.reshape(CONV_W - 1, bp, D_LRU).transpose(1, 0, 2)[None]
    s5_shape_p = (1, bp, S5_GROUPS, S5_STATE)
    s5_shape_s = (1, bs, S5_GROUPS, S5_STATE)
    return (y_prompt, y_sample,
            conv_prompt, lru_p[None], s5r_p.reshape(s5_shape_p), s5i_p.reshape(s5_shape_p),
            conv_s.reshape(1, bs, CONV_W - 1, D_LRU), lru_s[None],
            s5r_s.reshape(s5_shape_s), s5i_s.reshape(s5_shape_s))
```

```python
import functools
import math

import jax
import jax.numpy as jnp
from jax import lax
from jax.experimental import pallas as pl
from jax.experimental.pallas import tpu as pltpu

F32 = jnp.float32
BF16 = jnp.bfloat16

D_MODEL = 1024
D_LRU = 512
D_S5 = 512
LRU_HEADS = 8
LRU_HEAD_DIM = D_LRU // LRU_HEADS
CONV_W = 4
LRU_C = 8.0
S5_GROUP = 16
S5_GROUPS = D_S5 // S5_GROUP
S5_STATE = 64
S5_LANES = S5_GROUPS * S5_STATE
D_FF = 2816
EPS = 1e-6

SUBLANES = 8
MXU_DIM = 256
VMEM_LIMIT_BYTES = 56 * 1024 * 1024

TIME_CHUNK = 64
FFN_TIME_CHUNK = 64
GATE_BLOCK = MXU_DIM
GATE_BLOCKS = D_LRU // GATE_BLOCK
S5_BLOCKS = 4
S5_BLOCK_CH = D_S5 // S5_BLOCKS
S5_BLOCK_LANES = S5_LANES // S5_BLOCKS
FFN_CHUNK = MXU_DIM
FFN_CHUNKS = D_FF // FFN_CHUNK


def _dot(a, b):
    return jnp.dot(a, b, preferred_element_type=F32)


def _rms(x, g):
    return x * lax.rsqrt(jnp.mean(x * x, axis=-1, keepdims=True) + EPS) * g


def _log_sigmoid(x):
    return jnp.minimum(x, 0.0) - jnp.log1p(jnp.exp(-jnp.abs(x)))


def _lru_gate_block(conv_j, wg_j, ba_j, bx_j, lam_j):
    gz = _dot(conv_j.astype(BF16), wg_j)
    r = jax.nn.sigmoid(gz[:, :GATE_BLOCK] + ba_j)
    ig = jax.nn.sigmoid(gz[:, GATE_BLOCK:] + bx_j)
    log_a = LRU_C * r * _log_sigmoid(lam_j)
    a = jnp.exp(log_a)
    mult = jnp.sqrt(-jnp.tanh(log_a) * (a * a + 1.0))
    return a, mult * (ig * conv_j)


def _adaln_kernel(c_ref, w_ref, b_ref, o_ref):
    c = c_ref[...]
    s = (c * jax.nn.sigmoid(c)).astype(BF16)
    o_ref[...] = _dot(s, w_ref[...].astype(BF16)) + b_ref[...]


def _adaln(c_all, ada_w, ada_b):
    rows = c_all.shape[0]
    n = ada_w.shape[1]
    tn = 512
    return pl.pallas_call(
        _adaln_kernel,
        grid=(n // tn,),
        in_specs=[
            pl.BlockSpec((rows, D_MODEL), lambda j: (0, 0)),
            pl.BlockSpec((D_MODEL, tn), lambda j: (0, j)),
            pl.BlockSpec((1, tn), lambda j: (0, j)),
        ],
        out_specs=pl.BlockSpec((rows, tn), lambda j: (0, j)),
        out_shape=jax.ShapeDtypeStruct((rows, n), F32),
        compiler_params=pltpu.CompilerParams(dimension_semantics=("arbitrary",)),
        name="adaln",
    )(c_all, ada_w, ada_b)


def _s5_prep_kernel(lr_ref, li_ref, ldt_ref, bre_ref, bim_ref, abr_ref, abi_ref, bbr_ref, bbi_ref):
    lr = lr_ref[...]
    li = li_ref[...]
    dt = jnp.exp(ldt_ref[...])
    mag = jnp.exp(lr * dt)
    ab_r = mag * jnp.cos(li * dt)
    ab_i = mag * jnp.sin(li * dt)
    den = lr * lr + li * li
    fr = ((ab_r - 1.0) * lr + ab_i * li) / den
    fi = (ab_i * lr - (ab_r - 1.0) * li) / den
    br = bre_ref[...]
    bi = bim_ref[...]
    abr_ref[...] = ab_r
    abi_ref[...] = ab_i
    bbr_ref[...] = fr * br - fi * bi
    bbi_ref[...] = fr * bi + fi * br


def _s5_prep(lam_re, lam_im, log_dt, b_re, b_im):
    lr = lam_re.reshape(1, S5_LANES)
    li = lam_im.reshape(1, S5_LANES)
    ldt = jnp.repeat(log_dt, S5_STATE).reshape(1, S5_LANES)
    bre_t = b_re.reshape(S5_LANES, S5_GROUP).T
    bim_t = b_im.reshape(S5_LANES, S5_GROUP).T
    row = jax.ShapeDtypeStruct((1, S5_LANES), F32)
    mat = jax.ShapeDtypeStruct((S5_GROUP, S5_LANES), F32)
    return pl.pallas_call(_s5_prep_kernel, out_shape=(row, row, mat, mat), name="s5_prep")(
        lr, li, ldt, bre_t, bim_t)


def _s5_input_proj(u, wb_ref, hr_ref, hi_ref):
    ub = u.astype(BF16)
    for j in range(S5_BLOCKS):
        bu = _dot(ub[:, j * S5_BLOCK_CH:(j + 1) * S5_BLOCK_CH], wb_ref[j])
        sl = slice(j * S5_BLOCK_LANES, (j + 1) * S5_BLOCK_LANES)
        hr_ref[:, sl] = bu[:, :S5_BLOCK_LANES]
        hi_ref[:, sl] = bu[:, S5_BLOCK_LANES:]


def _s5_output(u, hr_ref, hi_ref, wc_ref, d, wglu_ref, y_ref):
    for j in range(S5_BLOCKS):
        sl = slice(j * S5_BLOCK_LANES, (j + 1) * S5_BLOCK_LANES)
        yj = (_dot(hr_ref[:, sl].astype(BF16), wc_ref[j, :S5_BLOCK_LANES, :])
              + _dot(hi_ref[:, sl].astype(BF16), wc_ref[j, S5_BLOCK_LANES:, :]))
        y_ref[:, j * S5_BLOCK_CH:(j + 1) * S5_BLOCK_CH] = yj
    y = y_ref[...] + d * u
    z = _dot(jax.nn.gelu(y).astype(BF16), wglu_ref[...])
    return z[:, :D_S5] * jax.nn.sigmoid(z[:, D_S5:])


def _mixer_prompt_kernel(
        x_hbm, mod_ref, n1g_ref, win_ref, cw_ref, cb_ref, wg_ref, ba_ref, bx_ref, lam_ref,
        abr_ref, abi_ref, wb_ref, wc_ref, d_ref, wglu_ref, glru_ref, gs5_ref, wout_ref,
        o_ref, conv_o, lru_o, s5r_o, s5i_o,
        x_buf, x_sems, xc_buf, h_lru, h_s5r, h_s5i, p_buf, hr_buf, hi_buf, a_buf, b_buf, y_buf, m_buf):
    step = pl.program_id(0)
    last = pl.num_programs(0) - 1
    slot = step % 2
    _, tl, nb, _ = x_buf.shape
    rows = tl * nb
    hist = (CONV_W - 1) * nb

    @pl.when(step == 0)
    def _():
        for c in _batch_major_copies(x_buf, 0, x_hbm, x_sems, 0, to_hbm=False):
            c.start()
        xc_buf[0:hist, :] = jnp.zeros((hist, D_LRU), F32)
        h_lru[...] = jnp.zeros_like(h_lru)
        h_s5r[...] = jnp.zeros_like(h_s5r)
        h_s5i[...] = jnp.zeros_like(h_s5i)

    @pl.when(step < last)
    def _():
        for c in _batch_major_copies(x_buf, 1 - slot, x_hbm, x_sems, step + 1, to_hbm=False):
            c.start()

    for c in _batch_major_copies(x_buf, slot, x_hbm, x_sems, step, to_hbm=False):
        c.wait()

    sh1 = mod_ref[:, 0:D_MODEL]
    sc1 = mod_ref[:, D_MODEL:2 * D_MODEL]
    g1 = mod_ref[:, 2 * D_MODEL:3 * D_MODEL]

    x3 = x_buf[slot]
    hn = _rms(x3, n1g_ref[...]) * (1.0 + sc1)[None] + sh1[None]
    p_buf[...] = _dot(hn.reshape(rows, D_MODEL).astype(BF16), win_ref[...])

    xc_buf[hist:hist + rows, :] = p_buf[:, 0:D_LRU]
    for j in range(GATE_BLOCKS):
        sl = slice(j * GATE_BLOCK, (j + 1) * GATE_BLOCK)
        conv_j = cb_ref[:, sl]
        for k in range(CONV_W):
            conv_j = conv_j + cw_ref[k:k + 1, sl] * xc_buf[k * nb:k * nb + rows, sl]
        a, b = _lru_gate_block(conv_j, wg_ref[j], ba_ref[:, sl], bx_ref[:, sl], lam_ref[:, sl])
        a_buf[:, sl] = a
        b_buf[:, sl] = b

    def lru_step(t, h):
        r0 = pl.multiple_of(t * nb, SUBLANES)
        h = a_buf[pl.ds(r0, nb), :] * h + b_buf[pl.ds(r0, nb), :]
        b_buf[pl.ds(r0, nb), :] = h
        return h

    h_lru[...] = lax.fori_loop(0, tl, lru_step, h_lru[...], unroll=8)

    lru_out = b_buf[...] * jax.nn.gelu(p_buf[:, D_LRU:2 * D_LRU])
    m_buf[:, 0:D_LRU] = _rms(lru_out, glru_ref[...]).astype(BF16)

    u = p_buf[:, 2 * D_LRU:]
    _s5_input_proj(u, wb_ref, hr_buf, hi_buf)
    for j in range(S5_BLOCKS):
        sl = slice(j * S5_BLOCK_LANES, (j + 1) * S5_BLOCK_LANES)
        ar = jnp.broadcast_to(abr_ref[:, sl], (nb, S5_BLOCK_LANES))
        ai = jnp.broadcast_to(abi_ref[:, sl], (nb, S5_BLOCK_LANES))

        def s5_step(t, carry, sl=sl, ar=ar, ai=ai):
            hr, hi = carry
            r0 = pl.multiple_of(t * nb, SUBLANES)
            nr = ar * hr - ai * hi + hr_buf[pl.ds(r0, nb), sl]
            ni = ar * hi + ai * hr + hi_buf[pl.ds(r0, nb), sl]
            hr_buf[pl.ds(r0, nb), sl] = nr
            hi_buf[pl.ds(r0, nb), sl] = ni
            return nr, ni

        hr, hi = lax.fori_loop(0, tl, s5_step, (h_s5r[:, sl], h_s5i[:, sl]), unroll=4)
        h_s5r[:, sl] = hr
        h_s5i[:, sl] = hi

    s5_out = _s5_output(u, hr_buf, hi_buf, wc_ref, d_ref[...], wglu_ref, y_buf)
    m_buf[:, D_LRU:] = _rms(s5_out, gs5_ref[...]).astype(BF16)

    mix = _dot(m_buf[...], wout_ref[...]).reshape(tl, nb, D_MODEL)
    o_ref[...] = x3 + g1[None] * mix

    tail = xc_buf[rows:rows + hist, :]
    xc_buf[0:hist, :] = tail

    @pl.when(step == pl.num_programs(0) - 1)
    def _():
        conv_o[...] = tail
        lru_o[...] = h_lru[...]
        s5r_o[...] = h_s5r[...]
        s5i_o[...] = h_s5i[...]


def _const_spec(shape):
    nd = len(shape)
    return pl.BlockSpec(shape, lambda i, _nd=nd: (0,) * _nd, pipeline_mode=pl.Buffered(1))


def _mixer_prompt(x_bm, mod_p, w):
    nb, t_len, _ = x_bm.shape
    tl = TIME_CHUNK
    rows = tl * nb
    hist = (CONV_W - 1) * nb
    consts = [mod_p, w["n1g"], w["win"], w["cw"], w["cb"], w["wg"], w["ba"], w["bx"], w["lam"],
              w["abr"], w["abi"], w["wb"], w["wc"], w["d"], w["wglu"], w["glru"], w["gs5"], w["wout"]]
    io_spec = pl.BlockSpec((tl, nb, D_MODEL), lambda i: (i, 0, 0))
    out_shape = (
        jax.ShapeDtypeStruct((t_len, nb, D_MODEL), F32),
        jax.ShapeDtypeStruct((hist, D_LRU), F32),
        jax.ShapeDtypeStruct((nb, D_LRU), F32),
        jax.ShapeDtypeStruct((nb, S5_LANES), F32),
        jax.ShapeDtypeStruct((nb, S5_LANES), F32),
    )
    out_specs = (
        io_spec,
        pl.BlockSpec((hist, D_LRU), lambda i: (0, 0)),
        pl.BlockSpec((nb, D_LRU), lambda i: (0, 0)),
        pl.BlockSpec((nb, S5_LANES), lambda i: (0, 0)),
        pl.BlockSpec((nb, S5_LANES), lambda i: (0, 0)),
    )
    scratch = [
        pltpu.VMEM((2, tl, nb, D_MODEL), F32),
        pltpu.SemaphoreType.DMA((2, nb)),
        pltpu.VMEM((hist + rows, D_LRU), F32),
        pltpu.VMEM((nb, D_LRU), F32),
        pltpu.VMEM((nb, S5_LANES), F32),
        pltpu.VMEM((nb, S5_LANES), F32),
        pltpu.VMEM((rows, 2 * D_LRU + D_S5), F32),
        pltpu.VMEM((rows, S5_LANES), F32),
        pltpu.VMEM((rows, S5_LANES), F32),
        pltpu.VMEM((rows, D_LRU), F32),
        pltpu.VMEM((rows, D_LRU), F32),
        pltpu.VMEM((rows, D_S5), F32),
        pltpu.VMEM((rows, D_LRU + D_S5), BF16),
    ]
    return pl.pallas_call(
        _mixer_prompt_kernel,
        grid=(t_len // tl,),
        in_specs=[pl.BlockSpec(memory_space=pl.ANY)] + [_const_spec(c.shape) for c in consts],
        out_specs=out_specs,
        out_shape=out_shape,
        scratch_shapes=scratch,
        compiler_params=pltpu.CompilerParams(
            dimension_semantics=("arbitrary",), vmem_limit_bytes=VMEM_LIMIT_BYTES),
        name="mixer_prompt",
    )(x_bm, *consts)


def _mixer_sample_kernel(
        x_ref, mod_ref, sconv_ref, slru_ref, ss5r_ref, ss5i_ref,
        n1g_ref, win_ref, cw_ref, cb_ref, wg_ref, ba_ref, bx_ref, lam_ref,
        abr_ref, abi_ref, wb_ref, wc_ref, d_ref, wglu_ref, glru_ref, gs5_ref, wout_ref,
        o_ref, conv_o, lru_o, s5r_o, s5i_o,
        y_buf, m_buf):
    sh1 = mod_ref[:, 0:D_MODEL]
    sc1 = mod_ref[:, D_MODEL:2 * D_MODEL]
    g1 = mod_ref[:, 2 * D_MODEL:3 * D_MODEL]

    x = x_ref[...]
    hn = _rms(x, n1g_ref[...]) * (1.0 + sc1) + sh1
    p = _dot(hn.astype(BF16), win_ref[...])
    lru_x = p[:, 0:D_LRU]

    hist_w = (CONV_W - 1) * D_LRU
    conv_o[:, 0:hist_w - D_LRU] = sconv_ref[:, D_LRU:hist_w]
    conv_o[:, hist_w - D_LRU:hist_w] = lru_x
    for j in range(GATE_BLOCKS):
        sl = slice(j * GATE_BLOCK, (j + 1) * GATE_BLOCK)
        conv_j = cb_ref[:, sl] + cw_ref[CONV_W - 1:CONV_W, sl] * lru_x[:, sl]
        for k in range(CONV_W - 1):
            conv_j = conv_j + cw_ref[k:k + 1, sl] * sconv_ref[:, k * D_LRU + j * GATE_BLOCK:
                                                               k * D_LRU + (j + 1) * GATE_BLOCK]
        a, b = _lru_gate_block(conv_j, wg_ref[j], ba_ref[:, sl], bx_ref[:, sl], lam_ref[:, sl])
        lru_o[:, sl] = a * slru_ref[:, sl] + b
    lru_out = lru_o[...] * jax.nn.gelu(p[:, D_LRU:2 * D_LRU])
    m_buf[:, 0:D_LRU] = _rms(lru_out, glru_ref[...]).astype(BF16)

    u = p[:, 2 * D_LRU:]
    _s5_input_proj(u, wb_ref, s5r_o, s5i_o)
    ar = abr_ref[...]
    ai = abi_ref[...]
    h0r = ss5r_ref[...]
    h0i = ss5i_ref[...]
    s5r_o[...] = ar * h0r - ai * h0i + s5r_o[...]
    s5i_o[...] = ar * h0i + ai * h0r + s5i_o[...]

    s5_out = _s5_output(u, s5r_o, s5i_o, wc_ref, d_ref[...], wglu_ref, y_buf)
    m_buf[:, D_LRU:] = _rms(s5_out, gs5_ref[...]).astype(BF16)
    o_ref[...] = x + g1 * _dot(m_buf[...], wout_ref[...])


def _mixer_sample(x_s, mod_s, sconv, slru, ss5r, ss5i, w):
    nb = x_s.shape[0]
    out_shape = (
        jax.ShapeDtypeStruct((nb, D_MODEL), F32),
        jax.ShapeDtypeStruct((nb, (CONV_W - 1) * D_LRU), F32),
        jax.ShapeDtypeStruct((nb, D_LRU), F32),
        jax.ShapeDtypeStruct((nb, S5_LANES), F32),
        jax.ShapeDtypeStruct((nb, S5_LANES), F32),
    )
    scratch = [pltpu.VMEM((nb, D_S5), F32), pltpu.VMEM((nb, D_LRU + D_S5), BF16)]
    return pl.pallas_call(
        _mixer_sample_kernel,
        out_shape=out_shape,
        scratch_shapes=scratch,
        compiler_params=pltpu.CompilerParams(vmem_limit_bytes=VMEM_LIMIT_BYTES),
        name="mixer_sample",
    )(x_s, mod_s, sconv, slru, ss5r, ss5i,
      w["n1g"], w["win"], w["cw"], w["cb"], w["wg"], w["ba"], w["bx"], w["lam"],
      w["abr"], w["abi"], w["wb"], w["wc"], w["d"], w["wglu"], w["glru"], w["gs5"], w["wout"])


def _ffn_rows(x3, mod_ref, n2g_ref, wgate_ref, wup_ref, wdown_ref, gf_ref, act_buf):
    groups, mrows, _ = x3.shape
    rows = groups * mrows
    sh2 = mod_ref[:, 3 * D_MODEL:4 * D_MODEL]
    sc2 = mod_ref[:, 4 * D_MODEL:5 * D_MODEL]
    g2 = mod_ref[:, 5 * D_MODEL:6 * D_MODEL]

    hn = _rms(x3, n2g_ref[...]) * (1.0 + sc2)[None] + sh2[None]
    hb = hn.reshape(rows, D_MODEL).astype(BF16)
    for c in range(FFN_CHUNKS):
        g = _dot(hb, wgate_ref[c])
        up = _dot(hb, wup_ref[c])
        act_buf[:, c * FFN_CHUNK:(c + 1) * FFN_CHUNK] = (g * jax.nn.sigmoid(g) * up).astype(BF16)
    ffn = _dot(act_buf[...], wdown_ref[...]).reshape(groups, mrows, D_MODEL)
    return _rms(x3 + g2[None] * ffn, gf_ref[...])


def _ffn_kernel(x_ref, mod_ref, n2g_ref, wgate_ref, wup_ref, wdown_ref, gf_ref, o_ref, act_buf):
    o_ref[...] = _ffn_rows(x_ref[...], mod_ref, n2g_ref, wgate_ref, wup_ref, wdown_ref, gf_ref, act_buf)


def _batch_major_copies(tm_bufs, slot, bm_hbm, sems, chunk, to_hbm):
    _, tl, nb, _ = tm_bufs.shape
    copies = []
    for b in range(nb):
        hbm = bm_hbm.at[b, pl.ds(chunk * tl, tl), :]
        vmem = tm_bufs.at[slot, :, b, :]
        src, dst = (vmem, hbm) if to_hbm else (hbm, vmem)
        copies.append(pltpu.make_async_copy(src, dst, sems.at[slot, b]))
    return copies


def _ffn_prompt_kernel(x_ref, mod_ref, n2g_ref, wgate_ref, wup_ref, wdown_ref, gf_ref, y_hbm,
                       act_buf, y_buf, y_sems):
    step = pl.program_id(0)
    last = pl.num_programs(0) - 1
    slot = step % 2
    y3 = _ffn_rows(x_ref[...], mod_ref, n2g_ref, wgate_ref, wup_ref, wdown_ref, gf_ref, act_buf)

    @pl.when(step >= 2)
    def _():
        for c in _batch_major_copies(y_buf, slot, y_hbm, y_sems, step - 2, to_hbm=True):
            c.wait()

    y_buf[slot] = y3
    for c in _batch_major_copies(y_buf, slot, y_hbm, y_sems, step, to_hbm=True):
        c.start()

    @pl.when(step == last)
    def _():
        @pl.when(step >= 1)
        def _():
            for c in _batch_major_copies(y_buf, 1 - slot, y_hbm, y_sems, step - 1, to_hbm=True):
                c.wait()
        for c in _batch_major_copies(y_buf, slot, y_hbm, y_sems, step, to_hbm=True):
            c.wait()


def _ffn_prompt(x_t, mod, w, nb_out_shape):
    t_len, nb, _ = x_t.shape
    tl = FFN_TIME_CHUNK
    rows = tl * nb
    io_spec = pl.BlockSpec((tl, nb, D_MODEL), lambda i: (i, 0, 0))
    consts = [mod, w["n2g"], w["wgate"], w["wup"], w["wdown"], w["gf"]]
    return pl.pallas_call(
        _ffn_prompt_kernel,
        grid=(t_len // tl,),
        in_specs=[io_spec] + [_const_spec(c.shape) for c in consts],
        out_specs=pl.BlockSpec(memory_space=pl.ANY),
        out_shape=jax.ShapeDtypeStruct(nb_out_shape, F32),
        scratch_shapes=[
            pltpu.VMEM((rows, D_FF), BF16),
            pltpu.VMEM((2, tl, nb, D_MODEL), F32),
            pltpu.SemaphoreType.DMA((2, nb)),
        ],
        compiler_params=pltpu.CompilerParams(
            dimension_semantics=("arbitrary",), vmem_limit_bytes=VMEM_LIMIT_BYTES),
        name="ffn_prompt",
    )(x_t, *consts)


def _ffn(x3, mod, w, groups_per_step):
    n_groups, mrows, _ = x3.shape
    rows = groups_per_step * mrows
    io_spec = pl.BlockSpec((groups_per_step, mrows, D_MODEL), lambda i: (i, 0, 0))
    consts = [mod, w["n2g"], w["wgate"], w["wup"], w["wdown"], w["gf"]]
    return pl.pallas_call(
        _ffn_kernel,
        grid=(n_groups // groups_per_step,),
        in_specs=[io_spec] + [_const_spec(c.shape) for c in consts],
        out_specs=io_spec,
        out_shape=jax.ShapeDtypeStruct(x3.shape, F32),
        scratch_shapes=[pltpu.VMEM((rows, D_FF), BF16)],
        compiler_params=pltpu.CompilerParams(
            dimension_semantics=("arbitrary",), vmem_limit_bytes=VMEM_LIMIT_BYTES),
        name="ffn",
    )(x3, *consts)


def _block_diag(blocks):
    n, r, c = blocks.shape
    eye = jnp.eye(n, dtype=blocks.dtype)
    return (eye[:, None, :, None] * blocks[:, :, None, :]).reshape(n * r, n * c)


def _prep_weights(p, abr, abi, bbr_t, bbi_t):
    w = {}
    row = lambda v: v.reshape(1, -1)
    w["n1g"] = row(p["norm1_g"])
    w["win"] = p["w_in"].astype(BF16)
    w["cw"] = p["conv_w"]
    w["cb"] = row(p["conv_b"])
    wa = _block_diag(p["lru_wa"])
    wx = _block_diag(p["lru_wx"])
    w["wg"] = jnp.stack([
        jnp.concatenate([wa[j * GATE_BLOCK:(j + 1) * GATE_BLOCK, j * GATE_BLOCK:(j + 1) * GATE_BLOCK],
                         wx[j * GATE_BLOCK:(j + 1) * GATE_BLOCK, j * GATE_BLOCK:(j + 1) * GATE_BLOCK]], axis=1)
        for j in range(GATE_BLOCKS)]).astype(BF16)
    w["ba"] = row(p["lru_ba"])
    w["bx"] = row(p["lru_bx"])
    w["lam"] = row(p["lru_lambda"])
    w["abr"] = abr
    w["abi"] = abi
    to_blocks = lambda m: m.reshape(S5_GROUP, S5_GROUPS, S5_STATE).transpose(1, 0, 2)
    wbr = _block_diag(to_blocks(bbr_t))
    wbi = _block_diag(to_blocks(bbi_t))
    w["wb"] = jnp.stack([
        jnp.concatenate([wbr[j * S5_BLOCK_CH:(j + 1) * S5_BLOCK_CH, j * S5_BLOCK_LANES:(j + 1) * S5_BLOCK_LANES],
                         wbi[j * S5_BLOCK_CH:(j + 1) * S5_BLOCK_CH, j * S5_BLOCK_LANES:(j + 1) * S5_BLOCK_LANES]],
                        axis=1)
        for j in range(S5_BLOCKS)]).astype(BF16)
    wcr = _block_diag(p["s5_c_re"].transpose(0, 2, 1))
    wci = _block_diag(-p["s5_c_im"].transpose(0, 2, 1))
    w["wc"] = jnp.stack([
        jnp.concatenate([wcr[j * S5_BLOCK_LANES:(j + 1) * S5_BLOCK_LANES, j * S5_BLOCK_CH:(j + 1) * S5_BLOCK_CH],
                         wci[j * S5_BLOCK_LANES:(j + 1) * S5_BLOCK_LANES, j * S5_BLOCK_CH:(j + 1) * S5_BLOCK_CH]],
                        axis=0)
        for j in range(S5_BLOCKS)]).astype(BF16)
    w["d"] = row(p["s5_d"])
    w["wglu"] = p["s5_w_glu"].astype(BF16)
    w["glru"] = row(p["g_lru_out"])
    w["gs5"] = row(p["g_s5_out"])
    w["wout"] = p["w_out"].astype(BF16)
    w["n2g"] = row(p["norm2_g"])
    to_chunks = lambda m: m.reshape(D_MODEL, FFN_CHUNKS, FFN_CHUNK).transpose(1, 0, 2)
    w["wgate"] = to_chunks(p["ffn_w_gate"]).astype(BF16)
    w["wup"] = to_chunks(p["ffn_w_up"]).astype(BF16)
    w["wdown"] = p["ffn_w_down"].astype(BF16)
    w["gf"] = row(p["final_norm_g"])
    return w


def kernel(x_prompt, x_sample, state_conv, state_lru, state_s5_re, state_s5_im, c_prompt, c_sample,
           ada_w, ada_b, norm1_g, w_in, conv_w, conv_b, lru_wa, lru_ba, lru_wx, lru_bx, lru_lambda,
           s5_lambda_re, s5_lambda_im, s5_log_dt, s5_b_re, s5_b_im, s5_c_re, s5_c_im, s5_d, s5_w_glu,
           g_lru_out, g_s5_out, w_out, norm2_g, ffn_w_gate, ffn_w_up, ffn_w_down, final_norm_g):
    depth = ada_w.shape[0]
    assert depth == 1, "single-layer decoder step"
    bp, t_len, _ = x_prompt.shape
    bs, dec_t, _ = x_sample.shape
    assert bp == SUBLANES and dec_t == 1 and t_len % TIME_CHUNK == 0 and t_len % FFN_TIME_CHUNK == 0

    p = dict(norm1_g=norm1_g[0], w_in=w_in[0], conv_w=conv_w[0], conv_b=conv_b[0], lru_wa=lru_wa[0],
             lru_ba=lru_ba[0], lru_wx=lru_wx[0], lru_bx=lru_bx[0], lru_lambda=lru_lambda[0],
             s5_c_re=s5_c_re[0], s5_c_im=s5_c_im[0], s5_d=s5_d[0], s5_w_glu=s5_w_glu[0],
             g_lru_out=g_lru_out[0], g_s5_out=g_s5_out[0], w_out=w_out[0], norm2_g=norm2_g[0],
             ffn_w_gate=ffn_w_gate[0], ffn_w_up=ffn_w_up[0], ffn_w_down=ffn_w_down[0],
             final_norm_g=final_norm_g)

    mod = _adaln(jnp.concatenate([c_prompt, c_sample], axis=0), ada_w[0], ada_b[0].reshape(1, -1))
    mod_p, mod_s = mod[:bp], mod[bp:]
    abr, abi, bbr_t, bbi_t = _s5_prep(s5_lambda_re[0], s5_lambda_im[0], s5_log_dt[0], s5_b_re[0], s5_b_im[0])
    w = _prep_weights(p, abr, abi, bbr_t, bbi_t)

    x32_t, conv_p, lru_p, s5r_p, s5i_p = _mixer_prompt(x_prompt, mod_p, w)
    y_prompt = _ffn_prompt(x32_t, mod_p, w, x_prompt.shape)

    x32_s, conv_s, lru_s, s5r_s, s5i_s = _mixer_sample(
        x_sample.reshape(bs, D_MODEL), mod_s,
        state_conv[0].reshape(bs, (CONV_W - 1) * D_LRU), state_lru[0],
        state_s5_re[0].reshape(bs, S5_LANES), state_s5_im[0].reshape(bs, S5_LANES), w)
    y_sample = _ffn(x32_s.reshape(1, bs, D_MODEL), mod_s, w, 1).reshape(bs, 1, D_MODEL)

    conv_prompt = conv_p.reshape(CONV_W - 1, bp, D_LRU).transpose(1, 0, 2)[None]
    s5_shape_p = (1, bp, S5_GROUPS, S5_STATE)
    s5_shape_s = (1, bs, S5_GROUPS, S5_STATE)
    return (y_prompt, y_sample,
            conv_prompt, lru_p[None], s5r_p.reshape(s5_shape_p), s5i_p.reshape(s5_shape_p),
            conv_s.reshape(1, bs, CONV_W - 1, D_LRU), lru_s[None],
            s5r_s.reshape(s5_shape_s), s5i_s.reshape(s5_shape_s))
```

```python
import functools

import jax
import jax.numpy as jnp
from jax import lax
from jax.experimental import pallas as pl
from jax.experimental.pallas import tpu as pltpu

F32 = jnp.float32
BF16 = jnp.bfloat16

D_MODEL = 1024
D_LRU = 512
D_S5 = 512
D_IN = 2 * D_LRU + D_S5
CONV_W = 4
LRU_C = 8.0
S5_GROUP = 16
S5_GROUPS = D_S5 // S5_GROUP
S5_STATE = 64
S5_LANES = S5_GROUPS * S5_STATE
D_FF = 2816
N_MOD = 6
EPS = 1e-6

SUBLANES = 8
MXU_DIM = 256
VMEM_LIMIT_BYTES = 58 * 1024 * 1024

TIME_CHUNK = 64
GATE_BLOCK = MXU_DIM
GATE_BLOCKS = D_LRU // GATE_BLOCK
S5_BLOCKS = 4
S5_BLOCK_CH = D_S5 // S5_BLOCKS
S5_BLOCK_LANES = S5_LANES // S5_BLOCKS
FFN_CHUNK = MXU_DIM
FFN_CHUNKS = D_FF // FFN_CHUNK
STAGE_ROWS = 1024
STAGE_COLS = 512


def _dot(a, b):
    return jnp.dot(a, b, preferred_element_type=F32)


def _rms(x, g):
    return x * lax.rsqrt(jnp.mean(x * x, axis=-1, keepdims=True) + EPS) * g


def _log_sigmoid(x):
    return jnp.minimum(x, 0.0) - jnp.log1p(jnp.exp(-jnp.abs(x)))


def _lru_gate_block(conv_j, wg_j, ba_j, bx_j, lam_j):
    gz = _dot(conv_j.astype(BF16), wg_j)
    r = jax.nn.sigmoid(gz[:, :GATE_BLOCK] + ba_j)
    ig = jax.nn.sigmoid(gz[:, GATE_BLOCK:] + bx_j)
    log_a = LRU_C * r * _log_sigmoid(lam_j)
    a = jnp.exp(log_a)
    mult = jnp.sqrt(-jnp.tanh(log_a) * (a * a + 1.0))
    return a, mult * (ig * conv_j)


def _s5_bu_block(ub, wb_ref, j):
    bu = _dot(ub[:, j * S5_BLOCK_CH:(j + 1) * S5_BLOCK_CH], wb_ref[j])
    return bu[:, :S5_BLOCK_LANES], bu[:, S5_BLOCK_LANES:]


def _s5_y_block(hr, hi, wc_ref, j):
    return (_dot(hr.astype(BF16), wc_ref[j, :S5_BLOCK_LANES, :])
            + _dot(hi.astype(BF16), wc_ref[j, S5_BLOCK_LANES:, :]))


def _s5_glu(y, u, d, wglu_ref):
    z = _dot(jax.nn.gelu(y + d * u).astype(BF16), wglu_ref[...])
    return z[:, :D_S5] * jax.nn.sigmoid(z[:, D_S5:])


def _stream_pieces(pieces, stage, sems):
    def dst(k):
        r, c = pieces[k][0].shape
        return stage.at[k % 2, pl.ds(0, r), pl.ds(0, c)]

    copies = [pltpu.make_async_copy(pieces[k][0], dst(k), sems.at[k % 2]) for k in range(len(pieces))]
    copies[0].start()
    for k, (_, consume) in enumerate(pieces):
        if k + 1 < len(pieces):
            copies[k + 1].start()
        copies[k].wait()
        consume(dst(k))


def _batch_major_copies(tm_bufs, slot, bm_hbm, sems, chunk, to_hbm):
    _, tl, nb, _ = tm_bufs.shape
    copies = []
    for b in range(nb):
        hbm = bm_hbm.at[b, pl.ds(chunk * tl, tl), :]
        vmem = tm_bufs.at[slot, :, b, :]
        src, dst = (vmem, hbm) if to_hbm else (hbm, vmem)
        copies.append(pltpu.make_async_copy(src, dst, sems.at[slot, b]))
    return copies


def _s5_prep_kernel(lr_ref, li_ref, ldt_ref, bre_ref, bim_ref, abr_ref, abi_ref, bbr_ref, bbi_ref):
    lr = lr_ref[...]
    li = li_ref[...]
    dt = jnp.exp(ldt_ref[...])
    mag = jnp.exp(lr * dt)
    ab_r = mag * jnp.cos(li * dt)
    ab_i = mag * jnp.sin(li * dt)
    den = lr * lr + li * li
    fr = ((ab_r - 1.0) * lr + ab_i * li) / den
    fi = (ab_i * lr - (ab_r - 1.0) * li) / den
    br = bre_ref[...]
    bi = bim_ref[...]
    abr_ref[...] = ab_r
    abi_ref[...] = ab_i
    bbr_ref[...] = fr * br - fi * bi
    bbi_ref[...] = fr * bi + fi * br


def _s5_prep(lam_re, lam_im, log_dt, b_re, b_im):
    lr = lam_re.reshape(1, S5_LANES)
    li = lam_im.reshape(1, S5_LANES)
    ldt = jnp.repeat(log_dt, S5_STATE).reshape(1, S5_LANES)
    bre_t = b_re.reshape(S5_LANES, S5_GROUP).T
    bim_t = b_im.reshape(S5_LANES, S5_GROUP).T
    row = jax.ShapeDtypeStruct((1, S5_LANES), F32)
    mat = jax.ShapeDtypeStruct((S5_GROUP, S5_LANES), F32)
    return pl.pallas_call(_s5_prep_kernel, out_shape=(row, row, mat, mat), name="s5_prep")(
        lr, li, ldt, bre_t, bim_t)


def _mixer_kernel(
        n_chunks,
        x_hbm, cp_ref, cs_ref, adaw_hbm, adab_ref,
        xs_ref, sconv_ref, slru_ref, ss5r_ref, ss5i_ref,
        n1g_ref, win_hbm, cw_ref, cb_ref, wg_ref, ba_ref, bx_ref, lam_ref,
        abr_ref, abi_ref, wb_ref, wc_ref, d_ref, wglu_hbm, glru_ref, gs5_ref, wout_hbm,
        o_ref, mod2p_o, mod2s_o, convp_o, lrup_o, s5rp_o, s5ip_o,
        xs_o, convs_o, lrus_o, s5rs_o, s5is_o,
        x_buf, x_sems, stage, stage_sems, win_b, wglu_b, wout_b, mod1p, mod1s,
        xc_buf, h_lru, h_s5r, h_s5i, p_buf, hr_buf, hi_buf, a_buf, b_buf, y_buf, m_buf):
    step = pl.program_id(0)
    slot = step % 2
    _, tl, nb, _ = x_buf.shape
    rows = tl * nb
    hist = (CONV_W - 1) * nb
    half_mod = (N_MOD // 2) * D_MODEL

    @pl.when(step == 0)
    def _prologue():
        for c in _batch_major_copies(x_buf, 0, x_hbm, x_sems, 0, to_hbm=False):
            c.start()
        xc_buf[0:hist, :] = jnp.zeros((hist, D_LRU), F32)
        h_lru[...] = jnp.zeros_like(h_lru)
        h_s5r[...] = jnp.zeros_like(h_s5r)
        h_s5i[...] = jnp.zeros_like(h_s5i)

        cp = cp_ref[...]
        cs = cs_ref[...]
        silu_p = (cp * jax.nn.sigmoid(cp)).astype(BF16)
        silu_s = (cs * jax.nn.sigmoid(cs)).astype(BF16)
        pieces = []

        def ada_piece(k):
            cols = slice(k * STAGE_COLS, (k + 1) * STAGE_COLS)
            first = k * STAGE_COLS < half_mod
            dst_p, dst_s = (mod1p, mod1s) if first else (mod2p_o, mod2s_o)
            dcols = cols if first else slice(cols.start - half_mod, cols.stop - half_mod)

            def consume(view):
                w = view[...].astype(BF16)
                dst_p[:, dcols] = _dot(silu_p, w) + adab_ref[:, cols]
                dst_s[:, dcols] = _dot(silu_s, w) + adab_ref[:, cols]
            return adaw_hbm.at[:, pl.ds(k * STAGE_COLS, STAGE_COLS)], consume

        def cast_piece(src_hbm, dst, r0, nr, c0, nc):
            def consume(view):
                dst[r0:r0 + nr, c0:c0 + nc] = view[...].astype(BF16)
            return src_hbm.at[pl.ds(r0, nr), pl.ds(c0, nc)], consume

        for k in range(N_MOD * D_MODEL // STAGE_COLS):
            pieces.append(ada_piece(k))
        for k in range(D_IN // STAGE_COLS):
            pieces.append(cast_piece(win_hbm, win_b, 0, D_MODEL, k * STAGE_COLS, STAGE_COLS))
        for k in range(2 * D_S5 // STAGE_COLS):
            pieces.append(cast_piece(wglu_hbm, wglu_b, 0, D_S5, k * STAGE_COLS, STAGE_COLS))
        for k in range(D_MODEL // STAGE_COLS):
            pieces.append(cast_piece(wout_hbm, wout_b, 0, D_LRU + D_S5, k * STAGE_COLS, STAGE_COLS))
        _stream_pieces(pieces, stage, stage_sems)

    @pl.when(step < n_chunks)
    def _prompt_step():
        @pl.when(step + 1 < n_chunks)
        def _():
            for c in _batch_major_copies(x_buf, 1 - slot, x_hbm, x_sems, step + 1, to_hbm=False):
                c.start()

        for c in _batch_major_copies(x_buf, slot, x_hbm, x_sems, step, to_hbm=False):
            c.wait()

        sh1 = mod1p[:, 0:D_MODEL]
        sc1 = mod1p[:, D_MODEL:2 * D_MODEL]
        g1 = mod1p[:, 2 * D_MODEL:3 * D_MODEL]

        x3 = x_buf[slot]
        hn = _rms(x3, n1g_ref[...]) * (1.0 + sc1)[None] + sh1[None]
        p_buf[...] = _dot(hn.reshape(rows, D_MODEL).astype(BF16), win_b[...])

        xc_buf[hist:hist + rows, :] = p_buf[:, 0:D_LRU]
        for j in range(GATE_BLOCKS):
            sl = slice(j * GATE_BLOCK, (j + 1) * GATE_BLOCK)
            conv_j = cb_ref[:, sl]
            for k in range(CONV_W):
                conv_j = conv_j + cw_ref[k:k + 1, sl] * xc_buf[k * nb:k * nb + rows, sl]
            a, b = _lru_gate_block(conv_j, wg_ref[j], ba_ref[:, sl], bx_ref[:, sl], lam_ref[:, sl])
            a_buf[:, sl] = a
            b_buf[:, sl] = b

        def lru_step(t, h):
            r0 = pl.multiple_of(t * nb, SUBLANES)
            h = a_buf[pl.ds(r0, nb), :] * h + b_buf[pl.ds(r0, nb), :]
            b_buf[pl.ds(r0, nb), :] = h
            return h

        h_lru[...] = lax.fori_loop(0, tl, lru_step, h_lru[...], unroll=8)

        lru_out = b_buf[...] * jax.nn.gelu(p_buf[:, D_LRU:2 * D_LRU])
        m_buf[:, 0:D_LRU] = _rms(lru_out, glru_ref[...]).astype(BF16)

        u = p_buf[:, 2 * D_LRU:]
        ub = u.astype(BF16)
        for j in range(S5_BLOCKS):
            sl = slice(j * S5_BLOCK_LANES, (j + 1) * S5_BLOCK_LANES)
            hr_j = hr_buf.at[j % 2]
            hi_j = hi_buf.at[j % 2]
            bu_r, bu_i = _s5_bu_block(ub, wb_ref, j)
            hr_j[...] = bu_r
            hi_j[...] = bu_i
            ar = jnp.broadcast_to(abr_ref[:, sl], (nb, S5_BLOCK_LANES))
            ai = jnp.broadcast_to(abi_ref[:, sl], (nb, S5_BLOCK_LANES))

            def s5_step(t, carry, hr_j=hr_j, hi_j=hi_j, ar=ar, ai=ai):
                hr, hi = carry
                r0 = pl.multiple_of(t * nb, SUBLANES)
                nr = ar * hr - ai * hi + hr_j[pl.ds(r0, nb), :]
                ni = ar * hi + ai * hr + hi_j[pl.ds(r0, nb), :]
                hr_j[pl.ds(r0, nb), :] = nr
                hi_j[pl.ds(r0, nb), :] = ni
                return nr, ni

            hr, hi = lax.fori_loop(0, tl, s5_step, (h_s5r[:, sl], h_s5i[:, sl]), unroll=4)
            h_s5r[:, sl] = hr
            h_s5i[:, sl] = hi
            y_buf[:, j * S5_BLOCK_CH:(j + 1) * S5_BLOCK_CH] = _s5_y_block(hr_j[...], hi_j[...], wc_ref, j)

        s5_out = _s5_glu(y_buf[...], u, d_ref[...], wglu_b)
        m_buf[:, D_LRU:] = _rms(s5_out, gs5_ref[...]).astype(BF16)

        mix = _dot(m_buf[...], wout_b[...]).reshape(tl, nb, D_MODEL)
        o_ref[...] = x3 + g1[None] * mix

        xc_buf[0:hist, :] = xc_buf[rows:rows + hist, :]

    @pl.when(step == n_chunks)
    def _sample_step():
        convp_o[...] = xc_buf[0:hist, :]
        lrup_o[...] = h_lru[...]
        s5rp_o[...] = h_s5r[...]
        s5ip_o[...] = h_s5i[...]

        ns = xs_ref.shape[0]
        sh1 = mod1s[:, 0:D_MODEL]
        sc1 = mod1s[:, D_MODEL:2 * D_MODEL]
        g1 = mod1s[:, 2 * D_MODEL:3 * D_MODEL]

        x = xs_ref[...]
        hn = _rms(x, n1g_ref[...]) * (1.0 + sc1) + sh1
        p = _dot(hn.astype(BF16), win_b[...])
        lru_x = p[:, 0:D_LRU]

        hist_w = (CONV_W - 1) * D_LRU
        convs_o[:, 0:hist_w - D_LRU] = sconv_ref[:, D_LRU:hist_w]
        convs_o[:, hist_w - D_LRU:hist_w] = lru_x
        for j in range(GATE_BLOCKS):
            sl = slice(j * GATE_BLOCK, (j + 1) * GATE_BLOCK)
            conv_j = cb_ref[:, sl] + cw_ref[CONV_W - 1:CONV_W, sl] * lru_x[:, sl]
            for k in range(CONV_W - 1):
                conv_j = conv_j + cw_ref[k:k + 1, sl] * sconv_ref[:, k * D_LRU + j * GATE_BLOCK:
                                                                   k * D_LRU + (j + 1) * GATE_BLOCK]
            a, b = _lru_gate_block(conv_j, wg_ref[j], ba_ref[:, sl], bx_ref[:, sl], lam_ref[:, sl])
            lrus_o[:, sl] = a * slru_ref[:, sl] + b
        lru_out = lrus_o[...] * jax.nn.gelu(p[:, D_LRU:2 * D_LRU])
        ms_buf = m_buf.at[pl.ds(0, ns), :]
        ms_buf[:, 0:D_LRU] = _rms(lru_out, glru_ref[...]).astype(BF16)

        u = p[:, 2 * D_LRU:]
        ub = u.astype(BF16)
        ys_buf = y_buf.at[pl.ds(0, ns), :]
        for j in range(S5_BLOCKS):
            sl = slice(j * S5_BLOCK_LANES, (j + 1) * S5_BLOCK_LANES)
            bu_r, bu_i = _s5_bu_block(ub, wb_ref, j)
            ar = abr_ref[:, sl]
            ai = abi_ref[:, sl]
            h0r = ss5r_ref[:, sl]
            h0i = ss5i_ref[:, sl]
            hr = ar * h0r - ai * h0i + bu_r
            hi = ar * h0i + ai * h0r + bu_i
            s5rs_o[:, sl] = hr
            s5is_o[:, sl] = hi
            ys_buf[:, j * S5_BLOCK_CH:(j + 1) * S5_BLOCK_CH] = _s5_y_block(hr, hi, wc_ref, j)
        s5_out = _s5_glu(ys_buf[...], u, d_ref[...], wglu_b)
        ms_buf[:, D_LRU:] = _rms(s5_out, gs5_ref[...]).astype(BF16)
        xs_o[...] = x + g1 * _dot(ms_buf[...], wout_b[...])


def _const_spec(shape):
    nd = len(shape)
    return pl.BlockSpec(shape, lambda i, _nd=nd: (0,) * _nd, pipeline_mode=pl.Buffered(1))


def _out_const_spec(shape):
    nd = len(shape)
    return pl.BlockSpec(shape, lambda i, _nd=nd: (0,) * _nd)


_HBM = pl.BlockSpec(memory_space=pl.ANY)


def _mixer(x_prompt, c_prompt, c_sample, ada_w, ada_b, x_s, sconv, slru, ss5r, ss5i, w):
    nb, t_len, _ = x_prompt.shape
    ns = x_s.shape[0]
    tl = TIME_CHUNK
    n_chunks = t_len // tl
    rows = tl * nb
    hist = (CONV_W - 1) * nb
    half_mod = (N_MOD // 2) * D_MODEL

    inputs = [
        (x_prompt, _HBM), (c_prompt, None), (c_sample, None), (ada_w, _HBM), (ada_b, None),
        (x_s, None), (sconv, None), (slru, None), (ss5r, None), (ss5i, None),
        (w["n1g"], None), (w["win"], _HBM), (w["cw"], None), (w["cb"], None), (w["wg"], None),
        (w["ba"], None), (w["bx"], None), (w["lam"], None),
        (w["abr"], None), (w["abi"], None), (w["wb"], None), (w["wc"], None), (w["d"], None),
        (w["wglu"], _HBM), (w["glru"], None), (w["gs5"], None), (w["wout"], _HBM),
    ]
    in_specs = [spec if spec is not None else _const_spec(a.shape) for a, spec in inputs]

    out_shapes = [
        ((t_len, nb, D_MODEL), pl.BlockSpec((tl, nb, D_MODEL), lambda i: (jnp.minimum(i, n_chunks - 1), 0, 0))),
        ((nb, half_mod), None), ((ns, half_mod), None),
        ((hist, D_LRU), None), ((nb, D_LRU), None), ((nb, S5_LANES), None), ((nb, S5_LANES), None),
        ((ns, D_MODEL), None), ((ns, (CONV_W - 1) * D_LRU), None), ((ns, D_LRU), None),
        ((ns, S5_LANES), None), ((ns, S5_LANES), None),
    ]
    out_specs = [spec if spec is not None else _out_const_spec(s) for s, spec in out_shapes]
    out_shape = [jax.ShapeDtypeStruct(s, F32) for s, _ in out_shapes]

    scratch = [
        pltpu.VMEM((2, tl, nb, D_MODEL), F32),
        pltpu.SemaphoreType.DMA((2, nb)),
        pltpu.VMEM((2, STAGE_ROWS, STAGE_COLS), F32),
        pltpu.SemaphoreType.DMA((2,)),
        pltpu.VMEM((D_MODEL, D_IN), BF16),
        pltpu.VMEM((D_S5, 2 * D_S5), BF16),
        pltpu.VMEM((D_LRU + D_S5, D_MODEL), BF16),
        pltpu.VMEM((nb, half_mod), F32),
        pltpu.VMEM((ns, half_mod), F32),
        pltpu.VMEM((hist + rows, D_LRU), F32),
        pltpu.VMEM((nb, D_LRU), F32),
        pltpu.VMEM((nb, S5_LANES), F32),
        pltpu.VMEM((nb, S5_LANES), F32),
        pltpu.VMEM((rows, D_IN), F32),
        pltpu.VMEM((2, rows, S5_BLOCK_LANES), F32),
        pltpu.VMEM((2, rows, S5_BLOCK_LANES), F32),
        pltpu.VMEM((rows, D_LRU), F32),
        pltpu.VMEM((rows, D_LRU), F32),
        pltpu.VMEM((rows, D_S5), F32),
        pltpu.VMEM((rows, D_LRU + D_S5), BF16),
    ]
    return pl.pallas_call(
        functools.partial(_mixer_kernel, n_chunks),
        grid=(n_chunks + 1,),
        in_specs=in_specs,
        out_specs=out_specs,
        out_shape=out_shape,
        scratch_shapes=scratch,
        compiler_params=pltpu.CompilerParams(
            dimension_semantics=("arbitrary",), vmem_limit_bytes=VMEM_LIMIT_BYTES),
        name="mixer",
    )(*[a for a, _ in inputs])


def _ffn_rows(x3, mod_ref, n2g_ref, wgate_b, wup_b, wdown_b, gf_ref, act_ref):
    groups, mrows, _ = x3.shape
    rows = groups * mrows
    sh2 = mod_ref[:, 0:D_MODEL]
    sc2 = mod_ref[:, D_MODEL:2 * D_MODEL]
    g2 = mod_ref[:, 2 * D_MODEL:3 * D_MODEL]

    hn = _rms(x3, n2g_ref[...]) * (1.0 + sc2)[None] + sh2[None]
    hb = hn.reshape(rows, D_MODEL).astype(BF16)
    for c in range(FFN_CHUNKS):
        g = _dot(hb, wgate_b[c])
        up = _dot(hb, wup_b[c])
        act_ref[:, c * FFN_CHUNK:(c + 1) * FFN_CHUNK] = (g * jax.nn.sigmoid(g) * up).astype(BF16)
    ffn = _dot(act_ref[...], wdown_b[...]).reshape(groups, mrows, D_MODEL)
    return _rms(x3 + g2[None] * ffn, gf_ref[...])


def _ffn_kernel(
        n_chunks,
        x_ref, mod2p_ref, mod2s_ref, xs_ref, n2g_ref, wgate_hbm, wup_hbm, wdown_hbm, gf_ref,
        y_hbm, ys_o,
        stage_a, stage_b, stage_a_sems, stage_b_sems, wgate_b, wup_b, wdown_b, act_buf, y_buf, y_sems):
    step = pl.program_id(0)
    slot = step % 2

    @pl.when(step == 0)
    def _prologue():
        def col_piece(src_hbm, dst, c):
            def consume(view):
                dst[c] = view[...].astype(BF16)
            return src_hbm.at[:, pl.ds(c * FFN_CHUNK, FFN_CHUNK)], consume

        def row_piece(c):
            def consume(view):
                wdown_b[c * FFN_CHUNK:(c + 1) * FFN_CHUNK, :] = view[...].astype(BF16)
            return wdown_hbm.at[pl.ds(c * FFN_CHUNK, FFN_CHUNK), :], consume

        pieces = []
        for c in range(FFN_CHUNKS):
            pieces.append(col_piece(wgate_hbm, wgate_b, c))
            pieces.append(col_piece(wup_hbm, wup_b, c))
        _stream_pieces(pieces, stage_a, stage_a_sems)
        _stream_pieces([row_piece(c) for c in range(FFN_CHUNKS)], stage_b, stage_b_sems)

    @pl.when(step < n_chunks)
    def _prompt_step():
        y3 = _ffn_rows(x_ref[...], mod2p_ref, n2g_ref, wgate_b, wup_b, wdown_b, gf_ref, act_buf)

        @pl.when(step >= 2)
        def _():
            for c in _batch_major_copies(y_buf, slot, y_hbm, y_sems, step - 2, to_hbm=True):
                c.wait()

        y_buf[slot] = y3
        for c in _batch_major_copies(y_buf, slot, y_hbm, y_sems, step, to_hbm=True):
            c.start()

    @pl.when(step == n_chunks)
    def _sample_step():
        ns = xs_ref.shape[0]
        ys_o[...] = _ffn_rows(xs_ref[...][None], mod2s_ref, n2g_ref, wgate_b, wup_b, wdown_b, gf_ref,
                              act_buf.at[pl.ds(0, ns), :])[0]
        for s in range(max(n_chunks - 2, 0), n_chunks):
            for c in _batch_major_copies(y_buf, s % 2, y_hbm, y_sems, s, to_hbm=True):
                c.wait()


def _ffn(x32_t, mod2p, mod2s, x32_s, w, out_shape_p):
    t_len, nb, _ = x32_t.shape
    ns = x32_s.shape[0]
    tl = TIME_CHUNK
    n_chunks = t_len // tl
    rows = tl * nb
    inputs = [
        (x32_t, pl.BlockSpec((tl, nb, D_MODEL), lambda i: (jnp.minimum(i, n_chunks - 1), 0, 0))),
        (mod2p, None), (mod2s, None), (x32_s, None), (w["n2g"], None),
        (w["wgate"], _HBM), (w["wup"], _HBM), (w["wdown"], _HBM), (w["gf"], None),
    ]
    in_specs = [spec if spec is not None else _const_spec(a.shape) for a, spec in inputs]
    scratch = [
        pltpu.VMEM((2, D_MODEL, FFN_CHUNK), F32),
        pltpu.VMEM((2, FFN_CHUNK, D_MODEL), F32),
        pltpu.SemaphoreType.DMA((2,)),
        pltpu.SemaphoreType.DMA((2,)),
        pltpu.VMEM((FFN_CHUNKS, D_MODEL, FFN_CHUNK), BF16),
        pltpu.VMEM((FFN_CHUNKS, D_MODEL, FFN_CHUNK), BF16),
        pltpu.VMEM((D_FF, D_MODEL), BF16),
        pltpu.VMEM((rows, D_FF), BF16),
        pltpu.VMEM((2, tl, nb, D_MODEL), F32),
        pltpu.SemaphoreType.DMA((2, nb)),
    ]
    return pl.pallas_call(
        functools.partial(_ffn_kernel, n_chunks),
        grid=(n_chunks + 1,),
        in_specs=in_specs,
        out_specs=[_HBM, _out_const_spec((ns, D_MODEL))],
        out_shape=[jax.ShapeDtypeStruct(out_shape_p, F32), jax.ShapeDtypeStruct((ns, D_MODEL), F32)],
        scratch_shapes=scratch,
        compiler_params=pltpu.CompilerParams(
            dimension_semantics=("arbitrary",), vmem_limit_bytes=VMEM_LIMIT_BYTES),
        name="ffn",
    )(*[a for a, _ in inputs])


def _block_diag(blocks):
    n, r, c = blocks.shape
    eye = jnp.eye(n, dtype=blocks.dtype)
    return (eye[:, None, :, None] * blocks[:, :, None, :]).reshape(n * r, n * c)


def _prep_weights(p, abr, abi, bbr_t, bbi_t):
    w = {}
    row = lambda v: v.reshape(1, -1)
    w["n1g"] = row(p["norm1_g"])
    w["win"] = p["w_in"]
    w["cw"] = p["conv_w"]
    w["cb"] = row(p["conv_b"])
    wa = _block_diag(p["lru_wa"])
    wx = _block_diag(p["lru_wx"])
    w["wg"] = jnp.stack([
        jnp.concatenate([wa[j * GATE_BLOCK:(j + 1) * GATE_BLOCK, j * GATE_BLOCK:(j + 1) * GATE_BLOCK],
                         wx[j * GATE_BLOCK:(j + 1) * GATE_BLOCK, j * GATE_BLOCK:(j + 1) * GATE_BLOCK]], axis=1)
        for j in range(GATE_BLOCKS)]).astype(BF16)
    w["ba"] = row(p["lru_ba"])
    w["bx"] = row(p["lru_bx"])
    w["lam"] = row(p["lru_lambda"])
    w["abr"] = abr
    w["abi"] = abi
    to_blocks = lambda m: m.reshape(S5_GROUP, S5_GROUPS, S5_STATE).transpose(1, 0, 2)
    wbr = _block_diag(to_blocks(bbr_t))
    wbi = _block_diag(to_blocks(bbi_t))
    w["wb"] = jnp.stack([
        jnp.concatenate([wbr[j * S5_BLOCK_CH:(j + 1) * S5_BLOCK_CH, j * S5_BLOCK_LANES:(j + 1) * S5_BLOCK_LANES],
                         wbi[j * S5_BLOCK_CH:(j + 1) * S5_BLOCK_CH, j * S5_BLOCK_LANES:(j + 1) * S5_BLOCK_LANES]],
                        axis=1)
        for j in range(S5_BLOCKS)]).astype(BF16)
    wcr = _block_diag(p["s5_c_re"].transpose(0, 2, 1))
    wci = _block_diag(-p["s5_c_im"].transpose(0, 2, 1))
    w["wc"] = jnp.stack([
        jnp.concatenate([wcr[j * S5_BLOCK_LANES:(j + 1) * S5_BLOCK_LANES, j * S5_BLOCK_CH:(j + 1) * S5_BLOCK_CH],
                         wci[j * S5_BLOCK_LANES:(j + 1) * S5_BLOCK_LANES, j * S5_BLOCK_CH:(j + 1) * S5_BLOCK_CH]],
                        axis=0)
        for j in range(S5_BLOCKS)]).astype(BF16)
    w["d"] = row(p["s5_d"])
    w["wglu"] = p["s5_w_glu"]
    w["glru"] = row(p["g_lru_out"])
    w["gs5"] = row(p["g_s5_out"])
    w["wout"] = p["w_out"]
    w["n2g"] = row(p["norm2_g"])
    w["wgate"] = p["ffn_w_gate"]
    w["wup"] = p["ffn_w_up"]
    w["wdown"] = p["ffn_w_down"]
    w["gf"] = row(p["final_norm_g"])
    return w


def kernel(x_prompt, x_sample, state_conv, state_lru, state_s5_re, state_s5_im, c_prompt, c_sample,
           ada_w, ada_b, norm1_g, w_in, conv_w, conv_b, lru_wa, lru_ba, lru_wx, lru_bx, lru_lambda,
           s5_lambda_re, s5_lambda_im, s5_log_dt, s5_b_re, s5_b_im, s5_c_re, s5_c_im, s5_d, s5_w_glu,
           g_lru_out, g_s5_out, w_out, norm2_g, ffn_w_gate, ffn_w_up, ffn_w_down, final_norm_g):
    depth = ada_w.shape[0]
    assert depth == 1, "single-layer decoder step"
    bp, t_len, _ = x_prompt.shape
    bs, dec_t, _ = x_sample.shape
    assert bp == SUBLANES and dec_t == 1 and t_len % TIME_CHUNK == 0

    p = dict(norm1_g=norm1_g[0], w_in=w_in[0], conv_w=conv_w[0], conv_b=conv_b[0], lru_wa=lru_wa[0],
             lru_ba=lru_ba[0], lru_wx=lru_wx[0], lru_bx=lru_bx[0], lru_lambda=lru_lambda[0],
             s5_c_re=s5_c_re[0], s5_c_im=s5_c_im[0], s5_d=s5_d[0], s5_w_glu=s5_w_glu[0],
             g_lru_out=g_lru_out[0], g_s5_out=g_s5_out[0], w_out=w_out[0], norm2_g=norm2_g[0],
             ffn_w_gate=ffn_w_gate[0], ffn_w_up=ffn_w_up[0], ffn_w_down=ffn_w_down[0],
             final_norm_g=final_norm_g)

    abr, abi, bbr_t, bbi_t = _s5_prep(s5_lambda_re[0], s5_lambda_im[0], s5_log_dt[0], s5_b_re[0], s5_b_im[0])
    w = _prep_weights(p, abr, abi, bbr_t, bbi_t)

    (x32_t, mod2p, mod2s, conv_p, lru_p, s5r_p, s5i_p,
     x32_s, conv_s, lru_s, s5r_s, s5i_s) = _mixer(
        x_prompt, c_prompt, c_sample, ada_w[0], ada_b[0].reshape(1, -1),
        x_sample.reshape(bs, D_MODEL), state_conv[0].reshape(bs, (CONV_W - 1) * D_LRU), state_lru[0],
        state_s5_re[0].reshape(bs, S5_LANES), state_s5_im[0].reshape(bs, S5_LANES), w)
    y_prompt, y_s = _ffn(x32_t, mod2p, mod2s, x32_s, w, x_prompt.shape)

    conv_prompt = conv_p.reshape(CONV_W - 1, bp, D_LRU).transpose(1, 0, 2)[None]
    s5_shape_p = (1, bp, S5_GROUPS, S5_STATE)
    s5_shape_s = (1, bs, S5_GROUPS, S5_STATE)
    return (y_prompt, y_s.reshape(bs, 1, D_MODEL),
            conv_prompt, lru_p[None], s5r_p.reshape(s5_shape_p), s5i_p.reshape(s5_shape_p),
            conv_s.reshape(1, bs, CONV_W - 1, D_LRU), lru_s[None],
            s5r_s.reshape(s5_shape_s), s5i_s.reshape(s5_shape_s))
```

```python
import functools

import jax
import jax.numpy as jnp
from jax import lax
from jax.experimental import pallas as pl
from jax.experimental.pallas import tpu as pltpu

F32 = jnp.float32
BF16 = jnp.bfloat16

D_MODEL = 1024
D_LRU = 512
D_S5 = 512
D_IN = 2 * D_LRU + D_S5
CONV_W = 4
LRU_C = 8.0
S5_GROUP = 16
S5_GROUPS = D_S5 // S5_GROUP
S5_STATE = 64
S5_LANES = S5_GROUPS * S5_STATE
D_FF = 2816
N_MOD = 6
EPS = 1e-6

SUBLANES = 8
MXU_DIM = 256
VMEM_LIMIT_BYTES = 58 * 1024 * 1024

TIME_CHUNK = 64
GATE_BLOCK = MXU_DIM
GATE_BLOCKS = D_LRU // GATE_BLOCK
S5_BLOCKS = 4
S5_BLOCK_CH = D_S5 // S5_BLOCKS
S5_BLOCK_LANES = S5_LANES // S5_BLOCKS
FFN_CHUNK = MXU_DIM
FFN_CHUNKS = D_FF // FFN_CHUNK
STAGE_ROWS = 1024
STAGE_COLS = 512


def _dot(a, b):
    return jnp.dot(a, b, preferred_element_type=F32)


def _rms(x, g):
    return x * lax.rsqrt(jnp.mean(x * x, axis=-1, keepdims=True) + EPS) * g


def _log_sigmoid(x):
    return jnp.minimum(x, 0.0) - jnp.log1p(jnp.exp(-jnp.abs(x)))


def _lru_gate_block(conv_j, wg_j, ba_j, bx_j, lam_j):
    gz = _dot(conv_j.astype(BF16), wg_j)
    r = jax.nn.sigmoid(gz[:, :GATE_BLOCK] + ba_j)
    ig = jax.nn.sigmoid(gz[:, GATE_BLOCK:] + bx_j)
    log_a = LRU_C * r * _log_sigmoid(lam_j)
    a = jnp.exp(log_a)
    mult = jnp.sqrt(-jnp.tanh(log_a) * (a * a + 1.0))
    return a, mult * (ig * conv_j)


def _s5_bu_block(ub, wb_ref, j):
    bu = _dot(ub[:, j * S5_BLOCK_CH:(j + 1) * S5_BLOCK_CH], wb_ref[j])
    return bu[:, :S5_BLOCK_LANES], bu[:, S5_BLOCK_LANES:]


def _s5_y_block(hr, hi, wc_ref, j):
    return (_dot(hr.astype(BF16), wc_ref[j, :S5_BLOCK_LANES, :])
            + _dot(hi.astype(BF16), wc_ref[j, S5_BLOCK_LANES:, :]))


def _s5_glu(y, u, d, wglu_ref):
    z = _dot(jax.nn.gelu(y + d * u).astype(BF16), wglu_ref[...])
    return z[:, :D_S5] * jax.nn.sigmoid(z[:, D_S5:])


def _stream_pieces(pieces, stage, sems):
    def dst(k):
        r, c = pieces[k][0].shape
        return stage.at[k % 2, pl.ds(0, r), pl.ds(0, c)]

    copies = [pltpu.make_async_copy(pieces[k][0], dst(k), sems.at[k % 2]) for k in range(len(pieces))]
    copies[0].start()
    for k, (_, consume) in enumerate(pieces):
        if k + 1 < len(pieces):
            copies[k + 1].start()
        copies[k].wait()
        consume(dst(k))


def _batch_major_copies(tm_bufs, slot, bm_hbm, sems, chunk, to_hbm):
    _, tl, nb, _ = tm_bufs.shape
    copies = []
    for b in range(nb):
        hbm = bm_hbm.at[b, pl.ds(chunk * tl, tl), :]
        vmem = tm_bufs.at[slot, :, b, :]
        src, dst = (vmem, hbm) if to_hbm else (hbm, vmem)
        copies.append(pltpu.make_async_copy(src, dst, sems.at[slot, b]))
    return copies


def _s5_prep_kernel(lr_ref, li_ref, ldt_ref, bre_ref, bim_ref, abr_ref, abi_ref, bbr_ref, bbi_ref):
    lr = lr_ref[...]
    li = li_ref[...]
    dt = jnp.exp(ldt_ref[...])
    mag = jnp.exp(lr * dt)
    ab_r = mag * jnp.cos(li * dt)
    ab_i = mag * jnp.sin(li * dt)
    den = lr * lr + li * li
    fr = ((ab_r - 1.0) * lr + ab_i * li) / den
    fi = (ab_i * lr - (ab_r - 1.0) * li) / den
    br = bre_ref[...]
    bi = bim_ref[...]
    abr_ref[...] = ab_r
    abi_ref[...] = ab_i
    bbr_ref[...] = fr * br - fi * bi
    bbi_ref[...] = fr * bi + fi * br


def _s5_prep(lam_re, lam_im, log_dt, b_re, b_im):
    lr = lam_re.reshape(1, S5_LANES)
    li = lam_im.reshape(1, S5_LANES)
    ldt = jnp.repeat(log_dt, S5_STATE).reshape(1, S5_LANES)
    bre_t = b_re.reshape(S5_LANES, S5_GROUP).T
    bim_t = b_im.reshape(S5_LANES, S5_GROUP).T
    row = jax.ShapeDtypeStruct((1, S5_LANES), F32)
    mat = jax.ShapeDtypeStruct((S5_GROUP, S5_LANES), F32)
    return pl.pallas_call(_s5_prep_kernel, out_shape=(row, row, mat, mat), name="s5_prep")(
        lr, li, ldt, bre_t, bim_t)


def _mixer_kernel(
        n_chunks,
        x_hbm, cp_ref, cs_ref, adaw_hbm, adab_ref,
        xs_ref, sconv_ref, slru_ref, ss5r_ref, ss5i_ref,
        n1g_ref, win_hbm, cw_ref, cb_ref, wg_ref, ba_ref, bx_ref, lam_ref,
        abr_ref, abi_ref, wb_ref, wc_ref, d_ref, wglu_hbm, glru_ref, gs5_ref, wout_hbm,
        o_ref, mod2p_o, mod2s_o, convp_o, lrup_o, s5rp_o, s5ip_o,
        xs_o, convs_o, lrus_o, s5rs_o, s5is_o,
        x_buf, x_sems, stage, stage_sems, win_b, wglu_b, wout_b, mod1p, mod1s,
        xc_buf, h_lru, h_s5r, h_s5i, p_buf, hr_buf, hi_buf, a_buf, b_buf, y_buf, m_buf):
    step = pl.program_id(0)
    slot = step % 2
    _, tl, nb, _ = x_buf.shape
    rows = tl * nb
    hist = (CONV_W - 1) * nb
    half_mod = (N_MOD // 2) * D_MODEL

    @pl.when(step == 0)
    def _prologue():
        for c in _batch_major_copies(x_buf, 0, x_hbm, x_sems, 0, to_hbm=False):
            c.start()
        xc_buf[0:hist, :] = jnp.zeros((hist, D_LRU), F32)
        h_lru[...] = jnp.zeros_like(h_lru)
        h_s5r[...] = jnp.zeros_like(h_s5r)
        h_s5i[...] = jnp.zeros_like(h_s5i)

        cp = cp_ref[...]
        cs = cs_ref[...]
        silu_p = (cp * jax.nn.sigmoid(cp)).astype(BF16)
        silu_s = (cs * jax.nn.sigmoid(cs)).astype(BF16)
        pieces = []

        def ada_piece(k):
            cols = slice(k * STAGE_COLS, (k + 1) * STAGE_COLS)
            first = k * STAGE_COLS < half_mod
            dst_p, dst_s = (mod1p, mod1s) if first else (mod2p_o, mod2s_o)
            dcols = cols if first else slice(cols.start - half_mod, cols.stop - half_mod)

            def consume(view):
                w = view[...].astype(BF16)
                dst_p[:, dcols] = _dot(silu_p, w) + adab_ref[:, cols]
                dst_s[:, dcols] = _dot(silu_s, w) + adab_ref[:, cols]
            return adaw_hbm.at[:, pl.ds(k * STAGE_COLS, STAGE_COLS)], consume

        def cast_piece(src_hbm, dst, r0, nr, c0, nc):
            def consume(view):
                dst[r0:r0 + nr, c0:c0 + nc] = view[...].astype(BF16)
            return src_hbm.at[pl.ds(r0, nr), pl.ds(c0, nc)], consume

        for k in range(N_MOD * D_MODEL // STAGE_COLS):
            pieces.append(ada_piece(k))
        for k in range(D_IN // STAGE_COLS):
            pieces.append(cast_piece(win_hbm, win_b, 0, D_MODEL, k * STAGE_COLS, STAGE_COLS))
        for k in range(2 * D_S5 // STAGE_COLS):
            pieces.append(cast_piece(wglu_hbm, wglu_b, 0, D_S5, k * STAGE_COLS, STAGE_COLS))
        for k in range(D_MODEL // STAGE_COLS):
            pieces.append(cast_piece(wout_hbm, wout_b, 0, D_LRU + D_S5, k * STAGE_COLS, STAGE_COLS))
        _stream_pieces(pieces, stage, stage_sems)

    @pl.when(step < n_chunks)
    def _prompt_step():
        @pl.when(step + 1 < n_chunks)
        def _():
            for c in _batch_major_copies(x_buf, 1 - slot, x_hbm, x_sems, step + 1, to_hbm=False):
                c.start()

        for c in _batch_major_copies(x_buf, slot, x_hbm, x_sems, step, to_hbm=False):
            c.wait()

        sh1 = mod1p[:, 0:D_MODEL]
        sc1 = mod1p[:, D_MODEL:2 * D_MODEL]
        g1 = mod1p[:, 2 * D_MODEL:3 * D_MODEL]

        x3 = x_buf[slot]
        hn = _rms(x3, n1g_ref[...]) * (1.0 + sc1)[None] + sh1[None]
        p_buf[...] = _dot(hn.reshape(rows, D_MODEL).astype(BF16), win_b[...])

        xc_buf[hist:hist + rows, :] = p_buf[:, 0:D_LRU]
        for j in range(GATE_BLOCKS):
            sl = slice(j * GATE_BLOCK, (j + 1) * GATE_BLOCK)
            conv_j = cb_ref[:, sl]
            for k in range(CONV_W):
                conv_j = conv_j + cw_ref[k:k + 1, sl] * xc_buf[k * nb:k * nb + rows, sl]
            a, b = _lru_gate_block(conv_j, wg_ref[j], ba_ref[:, sl], bx_ref[:, sl], lam_ref[:, sl])
            a_buf[:, sl] = a
            b_buf[:, sl] = b

        h = h_lru[...]
        for t in range(tl):
            h = a_buf[t * nb:(t + 1) * nb, :] * h + b_buf[t * nb:(t + 1) * nb, :]
            b_buf[t * nb:(t + 1) * nb, :] = h
        h_lru[...] = h

        lru_out = b_buf[...] * jax.nn.gelu(p_buf[:, D_LRU:2 * D_LRU])
        m_buf[:, 0:D_LRU] = _rms(lru_out, glru_ref[...]).astype(BF16)

        u = p_buf[:, 2 * D_LRU:]
        ub = u.astype(BF16)
        for j in range(S5_BLOCKS):
            sl = slice(j * S5_BLOCK_LANES, (j + 1) * S5_BLOCK_LANES)
            hr_j = hr_buf.at[j % 2]
            hi_j = hi_buf.at[j % 2]
            bu_r, bu_i = _s5_bu_block(ub, wb_ref, j)
            hr_j[...] = bu_r
            hi_j[...] = bu_i
            ar = jnp.broadcast_to(abr_ref[:, sl], (nb, S5_BLOCK_LANES))
            ai = jnp.broadcast_to(abi_ref[:, sl], (nb, S5_BLOCK_LANES))

            hr = h_s5r[:, sl]
            hi = h_s5i[:, sl]
            for t in range(tl):
                rs = slice(t * nb, (t + 1) * nb)
                hr, hi = ar * hr - ai * hi + hr_j[rs, :], ar * hi + ai * hr + hi_j[rs, :]
                hr_j[rs, :] = hr
                hi_j[rs, :] = hi
            h_s5r[:, sl] = hr
            h_s5i[:, sl] = hi
            y_buf[:, j * S5_BLOCK_CH:(j + 1) * S5_BLOCK_CH] = _s5_y_block(hr_j[...], hi_j[...], wc_ref, j)

        s5_out = _s5_glu(y_buf[...], u, d_ref[...], wglu_b)
        m_buf[:, D_LRU:] = _rms(s5_out, gs5_ref[...]).astype(BF16)

        mix = _dot(m_buf[...], wout_b[...]).reshape(tl, nb, D_MODEL)
        o_ref[...] = x3 + g1[None] * mix

        xc_buf[0:hist, :] = xc_buf[rows:rows + hist, :]

    @pl.when(step == n_chunks)
    def _sample_step():
        convp_o[...] = xc_buf[0:hist, :]
        lrup_o[...] = h_lru[...]
        s5rp_o[...] = h_s5r[...]
        s5ip_o[...] = h_s5i[...]

        ns = xs_ref.shape[0]
        sh1 = mod1s[:, 0:D_MODEL]
        sc1 = mod1s[:, D_MODEL:2 * D_MODEL]
        g1 = mod1s[:, 2 * D_MODEL:3 * D_MODEL]

        x = xs_ref[...]
        hn = _rms(x, n1g_ref[...]) * (1.0 + sc1) + sh1
        p = _dot(hn.astype(BF16), win_b[...])
        lru_x = p[:, 0:D_LRU]

        hist_w = (CONV_W - 1) * D_LRU
        convs_o[:, 0:hist_w - D_LRU] = sconv_ref[:, D_LRU:hist_w]
        convs_o[:, hist_w - D_LRU:hist_w] = lru_x
        for j in range(GATE_BLOCKS):
            sl = slice(j * GATE_BLOCK, (j + 1) * GATE_BLOCK)
            conv_j = cb_ref[:, sl] + cw_ref[CONV_W - 1:CONV_W, sl] * lru_x[:, sl]
            for k in range(CONV_W - 1):
                conv_j = conv_j + cw_ref[k:k + 1, sl] * sconv_ref[:, k * D_LRU + j * GATE_BLOCK:
                                                                   k * D_LRU + (j + 1) * GATE_BLOCK]
            a, b = _lru_gate_block(conv_j, wg_ref[j], ba_ref[:, sl], bx_ref[:, sl], lam_ref[:, sl])
            lrus_o[:, sl] = a * slru_ref[:, sl] + b
        lru_out = lrus_o[...] * jax.nn.gelu(p[:, D_LRU:2 * D_LRU])
        ms_buf = m_buf.at[pl.ds(0, ns), :]
        ms_buf[:, 0:D_LRU] = _rms(lru_out, glru_ref[...]).astype(BF16)

        u = p[:, 2 * D_LRU:]
        ub = u.astype(BF16)
        ys_buf = y_buf.at[pl.ds(0, ns), :]
        for j in range(S5_BLOCKS):
            sl = slice(j * S5_BLOCK_LANES, (j + 1) * S5_BLOCK_LANES)
            bu_r, bu_i = _s5_bu_block(ub, wb_ref, j)
            ar = abr_ref[:, sl]
            ai = abi_ref[:, sl]
            h0r = ss5r_ref[:, sl]
            h0i = ss5i_ref[:, sl]
            hr = ar * h0r - ai * h0i + bu_r
            hi = ar * h0i + ai * h0r + bu_i
            s5rs_o[:, sl] = hr
            s5is_o[:, sl] = hi
            ys_buf[:, j * S5_BLOCK_CH:(j + 1) * S5_BLOCK_CH] = _s5_y_block(hr, hi, wc_ref, j)
        s5_out = _s5_glu(ys_buf[...], u, d_ref[...], wglu_b)
        ms_buf[:, D_LRU:] = _rms(s5_out, gs5_ref[...]).astype(BF16)
        xs_o[...] = x + g1 * _dot(ms_buf[...], wout_b[...])


def _const_spec(shape):
    nd = len(shape)
    return pl.BlockSpec(shape, lambda i, _nd=nd: (0,) * _nd, pipeline_mode=pl.Buffered(1))


def _out_const_spec(shape):
    nd = len(shape)
    return pl.BlockSpec(shape, lambda i, _nd=nd: (0,) * _nd)


_HBM = pl.BlockSpec(memory_space=pl.ANY)


def _mixer(x_prompt, c_prompt, c_sample, ada_w, ada_b, x_s, sconv, slru, ss5r, ss5i, w):
    nb, t_len, _ = x_prompt.shape
    ns = x_s.shape[0]
    tl = TIME_CHUNK
    n_chunks = t_len // tl
    rows = tl * nb
    hist = (CONV_W - 1) * nb
    half_mod = (N_MOD // 2) * D_MODEL

    inputs = [
        (x_prompt, _HBM), (c_prompt, None), (c_sample, None), (ada_w, _HBM), (ada_b, None),
        (x_s, None), (sconv, None), (slru, None), (ss5r, None), (ss5i, None),
        (w["n1g"], None), (w["win"], _HBM), (w["cw"], None), (w["cb"], None), (w["wg"], None),
        (w["ba"], None), (w["bx"], None), (w["lam"], None),
        (w["abr"], None), (w["abi"], None), (w["wb"], None), (w["wc"], None), (w["d"], None),
        (w["wglu"], _HBM), (w["glru"], None), (w["gs5"], None), (w["wout"], _HBM),
    ]
    in_specs = [spec if spec is not None else _const_spec(a.shape) for a, spec in inputs]

    out_shapes = [
        ((t_len, nb, D_MODEL), pl.BlockSpec((tl, nb, D_MODEL), lambda i: (jnp.minimum(i, n_chunks - 1), 0, 0))),
        ((nb, half_mod), None), ((ns, half_mod), None),
        ((hist, D_LRU), None), ((nb, D_LRU), None), ((nb, S5_LANES), None), ((nb, S5_LANES), None),
        ((ns, D_MODEL), None), ((ns, (CONV_W - 1) * D_LRU), None), ((ns, D_LRU), None),
        ((ns, S5_LANES), None), ((ns, S5_LANES), None),
    ]
    out_specs = [spec if spec is not None else _out_const_spec(s) for s, spec in out_shapes]
    out_shape = [jax.ShapeDtypeStruct(s, F32) for s, _ in out_shapes]

    scratch = [
        pltpu.VMEM((2, tl, nb, D_MODEL), F32),
        pltpu.SemaphoreType.DMA((2, nb)),
        pltpu.VMEM((2, STAGE_ROWS, STAGE_COLS), F32),
        pltpu.SemaphoreType.DMA((2,)),
        pltpu.VMEM((D_MODEL, D_IN), BF16),
        pltpu.VMEM((D_S5, 2 * D_S5), BF16),
        pltpu.VMEM((D_LRU + D_S5, D_MODEL), BF16),
        pltpu.VMEM((nb, half_mod), F32),
        pltpu.VMEM((ns, half_mod), F32),
        pltpu.VMEM((hist + rows, D_LRU), F32),
        pltpu.VMEM((nb, D_LRU), F32),
        pltpu.VMEM((nb, S5_LANES), F32),
        pltpu.VMEM((nb, S5_LANES), F32),
        pltpu.VMEM((rows, D_IN), F32),
        pltpu.VMEM((2, rows, S5_BLOCK_LANES), F32),
        pltpu.VMEM((2, rows, S5_BLOCK_LANES), F32),
        pltpu.VMEM((rows, D_LRU), F32),
        pltpu.VMEM((rows, D_LRU), F32),
        pltpu.VMEM((rows, D_S5), F32),
        pltpu.VMEM((rows, D_LRU + D_S5), BF16),
    ]
    return pl.pallas_call(
        functools.partial(_mixer_kernel, n_chunks),
        grid=(n_chunks + 1,),
        in_specs=in_specs,
        out_specs=out_specs,
        out_shape=out_shape,
        scratch_shapes=scratch,
        compiler_params=pltpu.CompilerParams(
            dimension_semantics=("arbitrary",), vmem_limit_bytes=VMEM_LIMIT_BYTES),
        name="mixer",
    )(*[a for a, _ in inputs])


def _ffn_rows(x3, mod_ref, n2g_ref, wgate_b, wup_b, wdown_b, gf_ref, act_ref):
    groups, mrows, _ = x3.shape
    rows = groups * mrows
    sh2 = mod_ref[:, 0:D_MODEL]
    sc2 = mod_ref[:, D_MODEL:2 * D_MODEL]
    g2 = mod_ref[:, 2 * D_MODEL:3 * D_MODEL]

    hn = _rms(x3, n2g_ref[...]) * (1.0 + sc2)[None] + sh2[None]
    hb = hn.reshape(rows, D_MODEL).astype(BF16)
    for c in range(FFN_CHUNKS):
        g = _dot(hb, wgate_b[c])
        up = _dot(hb, wup_b[c])
        act_ref[:, c * FFN_CHUNK:(c + 1) * FFN_CHUNK] = (g * jax.nn.sigmoid(g) * up).astype(BF16)
    ffn = _dot(act_ref[...], wdown_b[...]).reshape(groups, mrows, D_MODEL)
    return _rms(x3 + g2[None] * ffn, gf_ref[...])


def _ffn_kernel(
        n_chunks,
        x_ref, mod2p_ref, mod2s_ref, xs_ref, n2g_ref, wgate_hbm, wup_hbm, wdown_hbm, gf_ref,
        y_hbm, ys_o,
        stage_a, stage_b, stage_a_sems, stage_b_sems, wgate_b, wup_b, wdown_b, act_buf, y_buf, y_sems):
    step = pl.program_id(0)
    slot = step % 2

    @pl.when(step == 0)
    def _prologue():
        def col_piece(src_hbm, dst, c):
            def consume(view):
                dst[c] = view[...].astype(BF16)
            return src_hbm.at[:, pl.ds(c * FFN_CHUNK, FFN_CHUNK)], consume

        def row_piece(c):
            def consume(view):
                wdown_b[c * FFN_CHUNK:(c + 1) * FFN_CHUNK, :] = view[...].astype(BF16)
            return wdown_hbm.at[pl.ds(c * FFN_CHUNK, FFN_CHUNK), :], consume

        pieces = []
        for c in range(FFN_CHUNKS):
            pieces.append(col_piece(wgate_hbm, wgate_b, c))
            pieces.append(col_piece(wup_hbm, wup_b, c))
        _stream_pieces(pieces, stage_a, stage_a_sems)
        _stream_pieces([row_piece(c) for c in range(FFN_CHUNKS)], stage_b, stage_b_sems)

    @pl.when(step < n_chunks)
    def _prompt_step():
        y3 = _ffn_rows(x_ref[...], mod2p_ref, n2g_ref, wgate_b, wup_b, wdown_b, gf_ref, act_buf)

        @pl.when(step >= 2)
        def _():
            for c in _batch_major_copies(y_buf, slot, y_hbm, y_sems, step - 2, to_hbm=True):
                c.wait()

        y_buf[slot] = y3
        for c in _batch_major_copies(y_buf, slot, y_hbm, y_sems, step, to_hbm=True):
            c.start()

    @pl.when(step == n_chunks)
    def _sample_step():
        ns = xs_ref.shape[0]
        ys_o[...] = _ffn_rows(xs_ref[...][None], mod2s_ref, n2g_ref, wgate_b, wup_b, wdown_b, gf_ref,
                              act_buf.at[pl.ds(0, ns), :])[0]
        for s in range(max(n_chunks - 2, 0), n_chunks):
            for c in _batch_major_copies(y_buf, s % 2, y_hbm, y_sems, s, to_hbm=True):
                c.wait()


def _ffn(x32_t, mod2p, mod2s, x32_s, w, out_shape_p):
    t_len, nb, _ = x32_t.shape
    ns = x32_s.shape[0]
    tl = TIME_CHUNK
    n_chunks = t_len // tl
    rows = tl * nb
    inputs = [
        (x32_t, pl.BlockSpec((tl, nb, D_MODEL), lambda i: (jnp.minimum(i, n_chunks - 1), 0, 0))),
        (mod2p, None), (mod2s, None), (x32_s, None), (w["n2g"], None),
        (w["wgate"], _HBM), (w["wup"], _HBM), (w["wdown"], _HBM), (w["gf"], None),
    ]
    in_specs = [spec if spec is not None else _const_spec(a.shape) for a, spec in inputs]
    scratch = [
        pltpu.VMEM((2, D_MODEL, FFN_CHUNK), F32),
        pltpu.VMEM((2, FFN_CHUNK, D_MODEL), F32),
        pltpu.SemaphoreType.DMA((2,)),
        pltpu.SemaphoreType.DMA((2,)),
        pltpu.VMEM((FFN_CHUNKS, D_MODEL, FFN_CHUNK), BF16),
        pltpu.VMEM((FFN_CHUNKS, D_MODEL, FFN_CHUNK), BF16),
        pltpu.VMEM((D_FF, D_MODEL), BF16),
        pltpu.VMEM((rows, D_FF), BF16),
        pltpu.VMEM((2, tl, nb, D_MODEL), F32),
        pltpu.SemaphoreType.DMA((2, nb)),
    ]
    return pl.pallas_call(
        functools.partial(_ffn_kernel, n_chunks),
        grid=(n_chunks + 1,),
        in_specs=in_specs,
        out_specs=[_HBM, _out_const_spec((ns, D_MODEL))],
        out_shape=[jax.ShapeDtypeStruct(out_shape_p, F32), jax.ShapeDtypeStruct((ns, D_MODEL), F32)],
        scratch_shapes=scratch,
        compiler_params=pltpu.CompilerParams(
            dimension_semantics=("arbitrary",), vmem_limit_bytes=VMEM_LIMIT_BYTES),
        name="ffn",
    )(*[a for a, _ in inputs])


def _block_diag(blocks):
    n, r, c = blocks.shape
    eye = jnp.eye(n, dtype=blocks.dtype)
    return (eye[:, None, :, None] * blocks[:, :, None, :]).reshape(n * r, n * c)


def _prep_weights(p, abr, abi, bbr_t, bbi_t):
    w = {}
    row = lambda v: v.reshape(1, -1)
    w["n1g"] = row(p["norm1_g"])
    w["win"] = p["w_in"]
    w["cw"] = p["conv_w"]
    w["cb"] = row(p["conv_b"])
    wa = _block_diag(p["lru_wa"])
    wx = _block_diag(p["lru_wx"])
    w["wg"] = jnp.stack([
        jnp.concatenate([wa[j * GATE_BLOCK:(j + 1) * GATE_BLOCK, j * GATE_BLOCK:(j + 1) * GATE_BLOCK],
                         wx[j * GATE_BLOCK:(j + 1) * GATE_BLOCK, j * GATE_BLOCK:(j + 1) * GATE_BLOCK]], axis=1)
        for j in range(GATE_BLOCKS)]).astype(BF16)
    w["ba"] = row(p["lru_ba"])
    w["bx"] = row(p["lru_bx"])
    w["lam"] = row(p["lru_lambda"])
    w["abr"] = abr
    w["abi"] = abi
    to_blocks = lambda m: m.reshape(S5_GROUP, S5_GROUPS, S5_STATE).transpose(1, 0, 2)
    wbr = _block_diag(to_blocks(bbr_t))
    wbi = _block_diag(to_blocks(bbi_t))
    w["wb"] = jnp.stack([
        jnp.concatenate([wbr[j * S5_BLOCK_CH:(j + 1) * S5_BLOCK_CH, j * S5_BLOCK_LANES:(j + 1) * S5_BLOCK_LANES],
                         wbi[j * S5_BLOCK_CH:(j + 1) * S5_BLOCK_CH, j * S5_BLOCK_LANES:(j + 1) * S5_BLOCK_LANES]],
                        axis=1)
        for j in range(S5_BLOCKS)]).astype(BF16)
    wcr = _block_diag(p["s5_c_re"].transpose(0, 2, 1))
    wci = _block_diag(-p["s5_c_im"].transpose(0, 2, 1))
    w["wc"] = jnp.stack([
        jnp.concatenate([wcr[j * S5_BLOCK_LANES:(j + 1) * S5_BLOCK_LANES, j * S5_BLOCK_CH:(j + 1) * S5_BLOCK_CH],
                         wci[j * S5_BLOCK_LANES:(j + 1) * S5_BLOCK_LANES, j * S5_BLOCK_CH:(j + 1) * S5_BLOCK_CH]],
                        axis=0)
        for j in range(S5_BLOCKS)]).astype(BF16)
    w["d"] = row(p["s5_d"])
    w["wglu"] = p["s5_w_glu"]
    w["glru"] = row(p["g_lru_out"])
    w["gs5"] = row(p["g_s5_out"])
    w["wout"] = p["w_out"]
    w["n2g"] = row(p["norm2_g"])
    w["wgate"] = p["ffn_w_gate"]
    w["wup"] = p["ffn_w_up"]
    w["wdown"] = p["ffn_w_down"]
    w["gf"] = row(p["final_norm_g"])
    return w


def kernel(x_prompt, x_sample, state_conv, state_lru, state_s5_re, state_s5_im, c_prompt, c_sample,
           ada_w, ada_b, norm1_g, w_in, conv_w, conv_b, lru_wa, lru_ba, lru_wx, lru_bx, lru_lambda,
           s5_lambda_re, s5_lambda_im, s5_log_dt, s5_b_re, s5_b_im, s5_c_re, s5_c_im, s5_d, s5_w_glu,
           g_lru_out, g_s5_out, w_out, norm2_g, ffn_w_gate, ffn_w_up, ffn_w_down, final_norm_g):
    depth = ada_w.shape[0]
    assert depth == 1, "single-layer decoder step"
    bp, t_len, _ = x_prompt.shape
    bs, dec_t, _ = x_sample.shape
    assert bp == SUBLANES and dec_t == 1 and t_len % TIME_CHUNK == 0

    p = dict(norm1_g=norm1_g[0], w_in=w_in[0], conv_w=conv_w[0], conv_b=conv_b[0], lru_wa=lru_wa[0],
             lru_ba=lru_ba[0], lru_wx=lru_wx[0], lru_bx=lru_bx[0], lru_lambda=lru_lambda[0],
             s5_c_re=s5_c_re[0], s5_c_im=s5_c_im[0], s5_d=s5_d[0], s5_w_glu=s5_w_glu[0],
             g_lru_out=g_lru_out[0], g_s5_out=g_s5_out[0], w_out=w_out[0], norm2_g=norm2_g[0],
             ffn_w_gate=ffn_w_gate[0], ffn_w_up=ffn_w_up[0], ffn_w_down=ffn_w_down[0],
             final_norm_g=final_norm_g)

    abr, abi, bbr_t, bbi_t = _s5_prep(s5_lambda_re[0], s5_lambda_im[0], s5_log_dt[0], s5_b_re[0], s5_b_im[0])
    w = _prep_weights(p, abr, abi, bbr_t, bbi_t)

    (x32_t, mod2p, mod2s, conv_p, lru_p, s5r_p, s5i_p,
     x32_s, conv_s, lru_s, s5r_s, s5i_s) = _mixer(
        x_prompt, c_prompt, c_sample, ada_w[0], ada_b[0].reshape(1, -1),
        x_sample.reshape(bs, D_MODEL), state_conv[0].reshape(bs, (CONV_W - 1) * D_LRU), state_lru[0],
        state_s5_re[0].reshape(bs, S5_LANES), state_s5_im[0].reshape(bs, S5_LANES), w)
    y_prompt, y_s = _ffn(x32_t, mod2p, mod2s, x32_s, w, x_prompt.shape)

    conv_prompt = conv_p.reshape(CONV_W - 1, bp, D_LRU).transpose(1, 0, 2)[None]
    s5_shape_p = (1, bp, S5_GROUPS, S5_STATE)
    s5_shape_s = (1, bs, S5_GROUPS, S5_STATE)
    return (y_prompt, y_s.reshape(bs, 1, D_MODEL),
            conv_prompt, lru_p[None], s5r_p.reshape(s5_shape_p), s5i_p.reshape(s5_shape_p),
            conv_s.reshape(1, bs, CONV_W - 1, D_LRU), lru_s[None],
            s5r_s.reshape(s5_shape_s), s5i_s.reshape(s5_shape_s))
```

```python
import functools

import jax
import jax.numpy as jnp
from jax import lax
from jax.experimental import pallas as pl
from jax.experimental.pallas import tpu as pltpu

F32 = jnp.float32
BF16 = jnp.bfloat16

D_MODEL = 1024
D_LRU = 512
D_S5 = 512
D_IN = 2 * D_LRU + D_S5
CONV_W = 4
LRU_C = 8.0
S5_GROUP = 16
S5_GROUPS = D_S5 // S5_GROUP
S5_STATE = 64
S5_LANES = S5_GROUPS * S5_STATE
D_FF = 2816
N_MOD = 6
EPS = 1e-6

SUBLANES = 8
MXU_DIM = 256
VMEM_LIMIT_BYTES = 58 * 1024 * 1024

TIME_CHUNK = 64
GATE_BLOCK = MXU_DIM
GATE_BLOCKS = D_LRU // GATE_BLOCK
S5_BLOCKS = 4
S5_BLOCK_CH = D_S5 // S5_BLOCKS
S5_BLOCK_LANES = S5_LANES // S5_BLOCKS
FFN_CHUNK = MXU_DIM
FFN_CHUNKS = D_FF // FFN_CHUNK
STAGE_ROWS = 1024
STAGE_COLS = 512
STAGE_DEPTH = 3
FFN_STAGE_DEPTH = 4
FFN_TIME_CHUNK = 128
FFN_SUB_ROWS = 512


def _dot(a, b):
    return jnp.dot(a, b, preferred_element_type=F32)


def _rms(x, g):
    return x * lax.rsqrt(jnp.mean(x * x, axis=-1, keepdims=True) + EPS) * g


def _log_sigmoid(x):
    return jnp.minimum(x, 0.0) - jnp.log1p(jnp.exp(-jnp.abs(x)))


def _lru_gate_block(conv_j, wg_j, ba_j, bx_j, lam_j):
    gz = _dot(conv_j.astype(BF16), wg_j)
    r = jax.nn.sigmoid(gz[:, :GATE_BLOCK] + ba_j)
    ig = jax.nn.sigmoid(gz[:, GATE_BLOCK:] + bx_j)
    log_a = LRU_C * r * _log_sigmoid(lam_j)
    a = jnp.exp(log_a)
    mult = jnp.sqrt(-jnp.tanh(log_a) * (a * a + 1.0))
    return a, mult * (ig * conv_j)


def _s5_bu_block(ub, wb_ref, j):
    bu = _dot(ub[:, j * S5_BLOCK_CH:(j + 1) * S5_BLOCK_CH], wb_ref[j])
    return bu[:, :S5_BLOCK_LANES], bu[:, S5_BLOCK_LANES:]


def _s5_y_block(hr, hi, wc_ref, j):
    return (_dot(hr.astype(BF16), wc_ref[j, :S5_BLOCK_LANES, :])
            + _dot(hi.astype(BF16), wc_ref[j, S5_BLOCK_LANES:, :]))


def _s5_glu(y, u, d, wglu_ref):
    z = _dot(jax.nn.gelu(y + d * u).astype(BF16), wglu_ref[...])
    return z[:, :D_S5] * jax.nn.sigmoid(z[:, D_S5:])


def _stream_pieces(pieces, stage, sems):
    depth = stage.shape[0]

    def dst(k):
        r, c = pieces[k][0].shape
        return stage.at[k % depth, pl.ds(0, r), pl.ds(0, c)]

    copies = [pltpu.make_async_copy(pieces[k][0], dst(k), sems.at[k % depth]) for k in range(len(pieces))]
    for k in range(min(depth - 1, len(pieces))):
        copies[k].start()
    for k, (_, consume) in enumerate(pieces):
        if k + depth - 1 < len(pieces):
            copies[k + depth - 1].start()
        copies[k].wait()
        consume(dst(k))


def _batch_major_copies(tm_bufs, slot, bm_hbm, sems, chunk, to_hbm):
    _, tl, nb, _ = tm_bufs.shape
    copies = []
    for b in range(nb):
        hbm = bm_hbm.at[b, pl.ds(chunk * tl, tl), :]
        vmem = tm_bufs.at[slot, :, b, :]
        src, dst = (vmem, hbm) if to_hbm else (hbm, vmem)
        copies.append(pltpu.make_async_copy(src, dst, sems.at[slot, b]))
    return copies


def _s5_prep_kernel(lr_ref, li_ref, ldt_ref, bre_ref, bim_ref, abr_ref, abi_ref, bbr_ref, bbi_ref):
    lr = lr_ref[...]
    li = li_ref[...]
    dt = jnp.exp(ldt_ref[...])
    mag = jnp.exp(lr * dt)
    ab_r = mag * jnp.cos(li * dt)
    ab_i = mag * jnp.sin(li * dt)
    den = lr * lr + li * li
    fr = ((ab_r - 1.0) * lr + ab_i * li) / den
    fi = (ab_i * lr - (ab_r - 1.0) * li) / den
    br = bre_ref[...]
    bi = bim_ref[...]
    abr_ref[...] = ab_r
    abi_ref[...] = ab_i
    bbr_ref[...] = fr * br - fi * bi
    bbi_ref[...] = fr * bi + fi * br


def _s5_prep(lam_re, lam_im, log_dt, b_re, b_im):
    lr = lam_re.reshape(1, S5_LANES)
    li = lam_im.reshape(1, S5_LANES)
    ldt = jnp.repeat(log_dt, S5_STATE).reshape(1, S5_LANES)
    bre_t = b_re.reshape(S5_LANES, S5_GROUP).T
    bim_t = b_im.reshape(S5_LANES, S5_GROUP).T
    row = jax.ShapeDtypeStruct((1, S5_LANES), F32)
    mat = jax.ShapeDtypeStruct((S5_GROUP, S5_LANES), F32)
    return pl.pallas_call(_s5_prep_kernel, out_shape=(row, row, mat, mat), name="s5_prep")(
        lr, li, ldt, bre_t, bim_t)


def _mixer_kernel(
        n_chunks,
        x_hbm, cp_ref, cs_ref, adaw_hbm, adab_ref,
        xs_ref, sconv_ref, slru_ref, ss5r_ref, ss5i_ref,
        n1g_ref, win_hbm, cw_ref, cb_ref, wg_ref, ba_ref, bx_ref, lam_ref,
        abr_ref, abi_ref, wb_ref, wc_ref, d_ref, wglu_hbm, glru_ref, gs5_ref, wout_hbm,
        o_ref, mod2p_o, mod2s_o, convp_o, lrup_o, s5rp_o, s5ip_o,
        xs_o, convs_o, lrus_o, s5rs_o, s5is_o,
        x_buf, x_sems, stage, stage_sems, win_b, wglu_b, wout_b, mod1p, mod1s,
        xc_buf, h_lru, h_s5r, h_s5i, p_buf, hr_buf, hi_buf, a_buf, b_buf, y_buf, m_buf):
    step = pl.program_id(0)
    slot = step % 2
    _, tl, nb, _ = x_buf.shape
    rows = tl * nb
    hist = (CONV_W - 1) * nb
    half_mod = (N_MOD // 2) * D_MODEL

    @pl.when(step == 0)
    def _prologue():
        for c in _batch_major_copies(x_buf, 0, x_hbm, x_sems, 0, to_hbm=False):
            c.start()
        xc_buf[0:hist, :] = jnp.zeros((hist, D_LRU), F32)
        h_lru[...] = jnp.zeros_like(h_lru)
        h_s5r[...] = jnp.zeros_like(h_s5r)
        h_s5i[...] = jnp.zeros_like(h_s5i)

        cp = cp_ref[...]
        cs = cs_ref[...]
        silu_p = (cp * jax.nn.sigmoid(cp)).astype(BF16)
        silu_s = (cs * jax.nn.sigmoid(cs)).astype(BF16)
        pieces = []

        def ada_piece(k):
            cols = slice(k * STAGE_COLS, (k + 1) * STAGE_COLS)
            first = k * STAGE_COLS < half_mod
            dst_p, dst_s = (mod1p, mod1s) if first else (mod2p_o, mod2s_o)
            dcols = cols if first else slice(cols.start - half_mod, cols.stop - half_mod)

            def consume(view):
                w = view[...].astype(BF16)
                dst_p[:, dcols] = _dot(silu_p, w) + adab_ref[:, cols]
                dst_s[:, dcols] = _dot(silu_s, w) + adab_ref[:, cols]
            return adaw_hbm.at[:, pl.ds(k * STAGE_COLS, STAGE_COLS)], consume

        def cast_piece(src_hbm, dst, r0, nr, c0, nc):
            def consume(view):
                dst[r0:r0 + nr, c0:c0 + nc] = view[...].astype(BF16)
            return src_hbm.at[pl.ds(r0, nr), pl.ds(c0, nc)], consume

        for k in range(N_MOD * D_MODEL // STAGE_COLS):
            pieces.append(ada_piece(k))
        for k in range(D_IN // STAGE_COLS):
            pieces.append(cast_piece(win_hbm, win_b, 0, D_MODEL, k * STAGE_COLS, STAGE_COLS))
        for k in range(2 * D_S5 // STAGE_COLS):
            pieces.append(cast_piece(wglu_hbm, wglu_b, 0, D_S5, k * STAGE_COLS, STAGE_COLS))
        for k in range(D_MODEL // STAGE_COLS):
            pieces.append(cast_piece(wout_hbm, wout_b, 0, D_LRU + D_S5, k * STAGE_COLS, STAGE_COLS))
        _stream_pieces(pieces, stage, stage_sems)

    @pl.when(step < n_chunks)
    def _prompt_step():
        @pl.when(step + 1 < n_chunks)
        def _():
            for c in _batch_major_copies(x_buf, 1 - slot, x_hbm, x_sems, step + 1, to_hbm=False):
                c.start()

        for c in _batch_major_copies(x_buf, slot, x_hbm, x_sems, step, to_hbm=False):
            c.wait()

        sh1 = mod1p[:, 0:D_MODEL]
        sc1 = mod1p[:, D_MODEL:2 * D_MODEL]
        g1 = mod1p[:, 2 * D_MODEL:3 * D_MODEL]

        x3 = x_buf[slot]
        hn = _rms(x3, n1g_ref[...]) * (1.0 + sc1)[None] + sh1[None]
        p_buf[...] = _dot(hn.reshape(rows, D_MODEL).astype(BF16), win_b[...])

        xc_buf[hist:hist + rows, :] = p_buf[:, 0:D_LRU]
        for j in range(GATE_BLOCKS):
            sl = slice(j * GATE_BLOCK, (j + 1) * GATE_BLOCK)
            conv_j = cb_ref[:, sl]
            for k in range(CONV_W):
                conv_j = conv_j + cw_ref[k:k + 1, sl] * xc_buf[k * nb:k * nb + rows, sl]
            a, b = _lru_gate_block(conv_j, wg_ref[j], ba_ref[:, sl], bx_ref[:, sl], lam_ref[:, sl])
            a_buf[:, sl] = a
            b_buf[:, sl] = b

        h = h_lru[...]
        for t in range(tl):
            h = a_buf[t * nb:(t + 1) * nb, :] * h + b_buf[t * nb:(t + 1) * nb, :]
            b_buf[t * nb:(t + 1) * nb, :] = h
        h_lru[...] = h

        lru_out = b_buf[...] * jax.nn.gelu(p_buf[:, D_LRU:2 * D_LRU])
        m_buf[:, 0:D_LRU] = _rms(lru_out, glru_ref[...]).astype(BF16)

        u = p_buf[:, 2 * D_LRU:]
        ub = u.astype(BF16)
        for j in range(S5_BLOCKS):
            sl = slice(j * S5_BLOCK_LANES, (j + 1) * S5_BLOCK_LANES)
            hr_j = hr_buf.at[j % 2]
            hi_j = hi_buf.at[j % 2]
            bu_r, bu_i = _s5_bu_block(ub, wb_ref, j)
            hr_j[...] = bu_r
            hi_j[...] = bu_i
            ar = jnp.broadcast_to(abr_ref[:, sl], (nb, S5_BLOCK_LANES))
            ai = jnp.broadcast_to(abi_ref[:, sl], (nb, S5_BLOCK_LANES))

            hr = h_s5r[:, sl]
            hi = h_s5i[:, sl]
            for t in range(tl):
                rs = slice(t * nb, (t + 1) * nb)
                hr, hi = ar * hr - ai * hi + hr_j[rs, :], ar * hi + ai * hr + hi_j[rs, :]
                hr_j[rs, :] = hr
                hi_j[rs, :] = hi
            h_s5r[:, sl] = hr
            h_s5i[:, sl] = hi
            y_buf[:, j * S5_BLOCK_CH:(j + 1) * S5_BLOCK_CH] = _s5_y_block(hr_j[...], hi_j[...], wc_ref, j)

        s5_out = _s5_glu(y_buf[...], u, d_ref[...], wglu_b)
        m_buf[:, D_LRU:] = _rms(s5_out, gs5_ref[...]).astype(BF16)

        mix = _dot(m_buf[...], wout_b[...]).reshape(tl, nb, D_MODEL)
        o_ref[...] = x3 + g1[None] * mix

        xc_buf[0:hist, :] = xc_buf[rows:rows + hist, :]

    @pl.when(step == n_chunks)
    def _sample_step():
        convp_o[...] = xc_buf[0:hist, :]
        lrup_o[...] = h_lru[...]
        s5rp_o[...] = h_s5r[...]
        s5ip_o[...] = h_s5i[...]

        ns = xs_ref.shape[0]
        sh1 = mod1s[:, 0:D_MODEL]
        sc1 = mod1s[:, D_MODEL:2 * D_MODEL]
        g1 = mod1s[:, 2 * D_MODEL:3 * D_MODEL]

        x = xs_ref[...]
        hn = _rms(x, n1g_ref[...]) * (1.0 + sc1) + sh1
        p = _dot(hn.astype(BF16), win_b[...])
        lru_x = p[:, 0:D_LRU]

        hist_w = (CONV_W - 1) * D_LRU
        convs_o[:, 0:hist_w - D_LRU] = sconv_ref[:, D_LRU:hist_w]
        convs_o[:, hist_w - D_LRU:hist_w] = lru_x
        for j in range(GATE_BLOCKS):
            sl = slice(j * GATE_BLOCK, (j + 1) * GATE_BLOCK)
            conv_j = cb_ref[:, sl] + cw_ref[CONV_W - 1:CONV_W, sl] * lru_x[:, sl]
            for k in range(CONV_W - 1):
                conv_j = conv_j + cw_ref[k:k + 1, sl] * sconv_ref[:, k * D_LRU + j * GATE_BLOCK:
                                                                   k * D_LRU + (j + 1) * GATE_BLOCK]
            a, b = _lru_gate_block(conv_j, wg_ref[j], ba_ref[:, sl], bx_ref[:, sl], lam_ref[:, sl])
            lrus_o[:, sl] = a * slru_ref[:, sl] + b
        lru_out = lrus_o[...] * jax.nn.gelu(p[:, D_LRU:2 * D_LRU])
        ms_buf = m_buf.at[pl.ds(0, ns), :]
        ms_buf[:, 0:D_LRU] = _rms(lru_out, glru_ref[...]).astype(BF16)

        u = p[:, 2 * D_LRU:]
        ub = u.astype(BF16)
        ys_buf = y_buf.at[pl.ds(0, ns), :]
        for j in range(S5_BLOCKS):
            sl = slice(j * S5_BLOCK_LANES, (j + 1) * S5_BLOCK_LANES)
            bu_r, bu_i = _s5_bu_block(ub, wb_ref, j)
            ar = abr_ref[:, sl]
            ai = abi_ref[:, sl]
            h0r = ss5r_ref[:, sl]
            h0i = ss5i_ref[:, sl]
            hr = ar * h0r - ai * h0i + bu_r
            hi = ar * h0i + ai * h0r + bu_i
            s5rs_o[:, sl] = hr
            s5is_o[:, sl] = hi
            ys_buf[:, j * S5_BLOCK_CH:(j + 1) * S5_BLOCK_CH] = _s5_y_block(hr, hi, wc_ref, j)
        s5_out = _s5_glu(ys_buf[...], u, d_ref[...], wglu_b)
        ms_buf[:, D_LRU:] = _rms(s5_out, gs5_ref[...]).astype(BF16)
        xs_o[...] = x + g1 * _dot(ms_buf[...], wout_b[...])


def _const_spec(shape):
    nd = len(shape)
    return pl.BlockSpec(shape, lambda i, _nd=nd: (0,) * _nd, pipeline_mode=pl.Buffered(1))


def _out_const_spec(shape):
    nd = len(shape)
    return pl.BlockSpec(shape, lambda i, _nd=nd: (0,) * _nd)


_HBM = pl.BlockSpec(memory_space=pl.ANY)


def _mixer(x_prompt, c_prompt, c_sample, ada_w, ada_b, x_s, sconv, slru, ss5r, ss5i, w):
    nb, t_len, _ = x_prompt.shape
    ns = x_s.shape[0]
    tl = TIME_CHUNK
    n_chunks = t_len // tl
    rows = tl * nb
    hist = (CONV_W - 1) * nb
    half_mod = (N_MOD // 2) * D_MODEL

    inputs = [
        (x_prompt, _HBM), (c_prompt, None), (c_sample, None), (ada_w, _HBM), (ada_b, None),
        (x_s, None), (sconv, None), (slru, None), (ss5r, None), (ss5i, None),
        (w["n1g"], None), (w["win"], _HBM), (w["cw"], None), (w["cb"], None), (w["wg"], None),
        (w["ba"], None), (w["bx"], None), (w["lam"], None),
        (w["abr"], None), (w["abi"], None), (w["wb"], None), (w["wc"], None), (w["d"], None),
        (w["wglu"], _HBM), (w["glru"], None), (w["gs5"], None), (w["wout"], _HBM),
    ]
    in_specs = [spec if spec is not None else _const_spec(a.shape) for a, spec in inputs]

    out_shapes = [
        ((t_len, nb, D_MODEL), pl.BlockSpec((tl, nb, D_MODEL), lambda i: (jnp.minimum(i, n_chunks - 1), 0, 0))),
        ((nb, half_mod), None), ((ns, half_mod), None),
        ((hist, D_LRU), None), ((nb, D_LRU), None), ((nb, S5_LANES), None), ((nb, S5_LANES), None),
        ((ns, D_MODEL), None), ((ns, (CONV_W - 1) * D_LRU), None), ((ns, D_LRU), None),
        ((ns, S5_LANES), None), ((ns, S5_LANES), None),
    ]
    out_specs = [spec if spec is not None else _out_const_spec(s) for s, spec in out_shapes]
    out_shape = [jax.ShapeDtypeStruct(s, F32) for s, _ in out_shapes]

    scratch = [
        pltpu.VMEM((2, tl, nb, D_MODEL), F32),
        pltpu.SemaphoreType.DMA((2, nb)),
        pltpu.VMEM((STAGE_DEPTH, STAGE_ROWS, STAGE_COLS), F32),
        pltpu.SemaphoreType.DMA((STAGE_DEPTH,)),
        pltpu.VMEM((D_MODEL, D_IN), BF16),
        pltpu.VMEM((D_S5, 2 * D_S5), BF16),
        pltpu.VMEM((D_LRU + D_S5, D_MODEL), BF16),
        pltpu.VMEM((nb, half_mod), F32),
        pltpu.VMEM((ns, half_mod), F32),
        pltpu.VMEM((hist + rows, D_LRU), F32),
        pltpu.VMEM((nb, D_LRU), F32),
        pltpu.VMEM((nb, S5_LANES), F32),
        pltpu.VMEM((nb, S5_LANES), F32),
        pltpu.VMEM((rows, D_IN), F32),
        pltpu.VMEM((2, rows, S5_BLOCK_LANES), F32),
        pltpu.VMEM((2, rows, S5_BLOCK_LANES), F32),
        pltpu.VMEM((rows, D_LRU), F32),
        pltpu.VMEM((rows, D_LRU), F32),
        pltpu.VMEM((rows, D_S5), F32),
        pltpu.VMEM((rows, D_LRU + D_S5), BF16),
    ]
    return pl.pallas_call(
        functools.partial(_mixer_kernel, n_chunks),
        grid=(n_chunks + 1,),
        in_specs=in_specs,
        out_specs=out_specs,
        out_shape=out_shape,
        scratch_shapes=scratch,
        compiler_params=pltpu.CompilerParams(
            dimension_semantics=("arbitrary",), vmem_limit_bytes=VMEM_LIMIT_BYTES),
        name="mixer",
    )(*[a for a, _ in inputs])


def _ffn_rows(x3, mod_ref, n2g_ref, wgate_b, wup_b, wdown_b, gf_ref, act_ref):
    groups, mrows, _ = x3.shape
    rows = groups * mrows
    sh2 = mod_ref[:, 0:D_MODEL]
    sc2 = mod_ref[:, D_MODEL:2 * D_MODEL]
    g2 = mod_ref[:, 2 * D_MODEL:3 * D_MODEL]

    hn = _rms(x3, n2g_ref[...]) * (1.0 + sc2)[None] + sh2[None]
    hb = hn.reshape(rows, D_MODEL).astype(BF16)
    for c in range(FFN_CHUNKS):
        g = _dot(hb, wgate_b[c])
        up = _dot(hb, wup_b[c])
        act_ref[:, c * FFN_CHUNK:(c + 1) * FFN_CHUNK] = (g * jax.nn.sigmoid(g) * up).astype(BF16)
    ffn = _dot(act_ref[...], wdown_b[...]).reshape(groups, mrows, D_MODEL)
    return _rms(x3 + g2[None] * ffn, gf_ref[...])


def _ffn_kernel(
        n_chunks,
        x_ref, mod2p_ref, mod2s_ref, xs_ref, n2g_ref, wgate_hbm, wup_hbm, wdown_hbm, gf_ref,
        y_hbm, ys_o,
        stage_a, stage_b, stage_a_sems, stage_b_sems, wgate_b, wup_b, wdown_b, act_buf, y_buf, y_sems):
    step = pl.program_id(0)
    slot = step % 2

    @pl.when(step == 0)
    def _prologue():
        def col_piece(src_hbm, dst, c):
            def consume(view):
                dst[c] = view[...].astype(BF16)
            return src_hbm.at[:, pl.ds(c * FFN_CHUNK, FFN_CHUNK)], consume

        def row_piece(c):
            def consume(view):
                wdown_b[c * FFN_CHUNK:(c + 1) * FFN_CHUNK, :] = view[...].astype(BF16)
            return wdown_hbm.at[pl.ds(c * FFN_CHUNK, FFN_CHUNK), :], consume

        pieces = []
        for c in range(FFN_CHUNKS):
            pieces.append(col_piece(wgate_hbm, wgate_b, c))
            pieces.append(col_piece(wup_hbm, wup_b, c))
        _stream_pieces(pieces, stage_a, stage_a_sems)
        _stream_pieces([row_piece(c) for c in range(FFN_CHUNKS)], stage_b, stage_b_sems)

    @pl.when(step < n_chunks)
    def _prompt_step():
        @pl.when(step >= 2)
        def _():
            for c in _batch_major_copies(y_buf, slot, y_hbm, y_sems, step - 2, to_hbm=True):
                c.wait()

        tl, nb, _ = x_ref.shape
        sub = FFN_SUB_ROWS // nb
        for h in range(tl // sub):
            y_buf[slot, pl.ds(h * sub, sub)] = _ffn_rows(
                x_ref[h * sub:(h + 1) * sub], mod2p_ref, n2g_ref, wgate_b, wup_b, wdown_b, gf_ref,
                act_buf.at[pl.ds(h * FFN_SUB_ROWS, FFN_SUB_ROWS), :])
        for c in _batch_major_copies(y_buf, slot, y_hbm, y_sems, step, to_hbm=True):
            c.start()

    @pl.when(step == n_chunks)
    def _sample_step():
        ns = xs_ref.shape[0]
        ys_o[...] = _ffn_rows(xs_ref[...][None], mod2s_ref, n2g_ref, wgate_b, wup_b, wdown_b, gf_ref,
                              act_buf.at[pl.ds(0, ns), :])[0]
        for s in range(max(n_chunks - 2, 0), n_chunks):
            for c in _batch_major_copies(y_buf, s % 2, y_hbm, y_sems, s, to_hbm=True):
                c.wait()


def _ffn(x32_t, mod2p, mod2s, x32_s, w, out_shape_p):
    t_len, nb, _ = x32_t.shape
    ns = x32_s.shape[0]
    tl = FFN_TIME_CHUNK
    n_chunks = t_len // tl
    rows = tl * nb
    inputs = [
        (x32_t, pl.BlockSpec((tl, nb, D_MODEL), lambda i: (jnp.minimum(i, n_chunks - 1), 0, 0))),
        (mod2p, None), (mod2s, None), (x32_s, None), (w["n2g"], None),
        (w["wgate"], _HBM), (w["wup"], _HBM), (w["wdown"], _HBM), (w["gf"], None),
    ]
    in_specs = [spec if spec is not None else _const_spec(a.shape) for a, spec in inputs]
    scratch = [
        pltpu.VMEM((FFN_STAGE_DEPTH, D_MODEL, FFN_CHUNK), F32),
        pltpu.VMEM((FFN_STAGE_DEPTH, FFN_CHUNK, D_MODEL), F32),
        pltpu.SemaphoreType.DMA((FFN_STAGE_DEPTH,)),
        pltpu.SemaphoreType.DMA((FFN_STAGE_DEPTH,)),
        pltpu.VMEM((FFN_CHUNKS, D_MODEL, FFN_CHUNK), BF16),
        pltpu.VMEM((FFN_CHUNKS, D_MODEL, FFN_CHUNK), BF16),
        pltpu.VMEM((D_FF, D_MODEL), BF16),
        pltpu.VMEM((rows, D_FF), BF16),
        pltpu.VMEM((2, tl, nb, D_MODEL), F32),
        pltpu.SemaphoreType.DMA((2, nb)),
    ]
    return pl.pallas_call(
        functools.partial(_ffn_kernel, n_chunks),
        grid=(n_chunks + 1,),
        in_specs=in_specs,
        out_specs=[_HBM, _out_const_spec((ns, D_MODEL))],
        out_shape=[jax.ShapeDtypeStruct(out_shape_p, F32), jax.ShapeDtypeStruct((ns, D_MODEL), F32)],
        scratch_shapes=scratch,
        compiler_params=pltpu.CompilerParams(
            dimension_semantics=("arbitrary",), vmem_limit_bytes=VMEM_LIMIT_BYTES),
        name="ffn",
    )(*[a for a, _ in inputs])


def _block_diag(blocks):
    n, r, c = blocks.shape
    eye = jnp.eye(n, dtype=blocks.dtype)
    return (eye[:, None, :, None] * blocks[:, :, None, :]).reshape(n * r, n * c)


def _prep_weights(p, abr, abi, bbr_t, bbi_t):
    w = {}
    row = lambda v: v.reshape(1, -1)
    w["n1g"] = row(p["norm1_g"])
    w["win"] = p["w_in"]
    w["cw"] = p["conv_w"]
    w["cb"] = row(p["conv_b"])
    wa = _block_diag(p["lru_wa"])
    wx = _block_diag(p["lru_wx"])
    w["wg"] = jnp.stack([
        jnp.concatenate([wa[j * GATE_BLOCK:(j + 1) * GATE_BLOCK, j * GATE_BLOCK:(j + 1) * GATE_BLOCK],
                         wx[j * GATE_BLOCK:(j + 1) * GATE_BLOCK, j * GATE_BLOCK:(j + 1) * GATE_BLOCK]], axis=1)
        for j in range(GATE_BLOCKS)]).astype(BF16)
    w["ba"] = row(p["lru_ba"])
    w["bx"] = row(p["lru_bx"])
    w["lam"] = row(p["lru_lambda"])
    w["abr"] = abr
    w["abi"] = abi
    to_blocks = lambda m: m.reshape(S5_GROUP, S5_GROUPS, S5_STATE).transpose(1, 0, 2)
    wbr = _block_diag(to_blocks(bbr_t))
    wbi = _block_diag(to_blocks(bbi_t))
    w["wb"] = jnp.stack([
        jnp.concatenate([wbr[j * S5_BLOCK_CH:(j + 1) * S5_BLOCK_CH, j * S5_BLOCK_LANES:(j + 1) * S5_BLOCK_LANES],
                         wbi[j * S5_BLOCK_CH:(j + 1) * S5_BLOCK_CH, j * S5_BLOCK_LANES:(j + 1) * S5_BLOCK_LANES]],
                        axis=1)
        for j in range(S5_BLOCKS)]).astype(BF16)
    wcr = _block_diag(p["s5_c_re"].transpose(0, 2, 1))
    wci = _block_diag(-p["s5_c_im"].transpose(0, 2, 1))
    w["wc"] = jnp.stack([
        jnp.concatenate([wcr[j * S5_BLOCK_LANES:(j + 1) * S5_BLOCK_LANES, j * S5_BLOCK_CH:(j + 1) * S5_BLOCK_CH],
                         wci[j * S5_BLOCK_LANES:(j + 1) * S5_BLOCK_LANES, j * S5_BLOCK_CH:(j + 1) * S5_BLOCK_CH]],
                        axis=0)
        for j in range(S5_BLOCKS)]).astype(BF16)
    w["d"] = row(p["s5_d"])
    w["wglu"] = p["s5_w_glu"]
    w["glru"] = row(p["g_lru_out"])
    w["gs5"] = row(p["g_s5_out"])
    w["wout"] = p["w_out"]
    w["n2g"] = row(p["norm2_g"])
    w["wgate"] = p["ffn_w_gate"]
    w["wup"] = p["ffn_w_up"]
    w["wdown"] = p["ffn_w_down"]
    w["gf"] = row(p["final_norm_g"])
    return w


def kernel(x_prompt, x_sample, state_conv, state_lru, state_s5_re, state_s5_im, c_prompt, c_sample,
           ada_w, ada_b, norm1_g, w_in, conv_w, conv_b, lru_wa, lru_ba, lru_wx, lru_bx, lru_lambda,
           s5_lambda_re, s5_lambda_im, s5_log_dt, s5_b_re, s5_b_im, s5_c_re, s5_c_im, s5_d, s5_w_glu,
           g_lru_out, g_s5_out, w_out, norm2_g, ffn_w_gate, ffn_w_up, ffn_w_down, final_norm_g):
    depth = ada_w.shape[0]
    assert depth == 1, "single-layer decoder step"
    bp, t_len, _ = x_prompt.shape
    bs, dec_t, _ = x_sample.shape
    assert bp == SUBLANES and dec_t == 1 and t_len % TIME_CHUNK == 0 and t_len % FFN_TIME_CHUNK == 0
    assert (FFN_TIME_CHUNK * SUBLANES) % FFN_SUB_ROWS == 0 and bs <= FFN_SUB_ROWS

    p = dict(norm1_g=norm1_g[0], w_in=w_in[0], conv_w=conv_w[0], conv_b=conv_b[0], lru_wa=lru_wa[0],
             lru_ba=lru_ba[0], lru_wx=lru_wx[0], lru_bx=lru_bx[0], lru_lambda=lru_lambda[0],
             s5_c_re=s5_c_re[0], s5_c_im=s5_c_im[0], s5_d=s5_d[0], s5_w_glu=s5_w_glu[0],
             g_lru_out=g_lru_out[0], g_s5_out=g_s5_out[0], w_out=w_out[0], norm2_g=norm2_g[0],
             ffn_w_gate=ffn_w_gate[0], ffn_w_up=ffn_w_up[0], ffn_w_down=ffn_w_down[0],
             final_norm_g=final_norm_g)

    abr, abi, bbr_t, bbi_t = _s5_prep(s5_lambda_re[0], s5_lambda_im[0], s5_log_dt[0], s5_b_re[0], s5_b_im[0])
    w = _prep_weights(p, abr, abi, bbr_t, bbi_t)

    (x32_t, mod2p, mod2s, conv_p, lru_p, s5r_p, s5i_p,
     x32_s, conv_s, lru_s, s5r_s, s5i_s) = _mixer(
        x_prompt, c_prompt, c_sample, ada_w[0], ada_b[0].reshape(1, -1),
        x_sample.reshape(bs, D_MODEL), state_conv[0].reshape(bs, (CONV_W - 1) * D_LRU), state_lru[0],
        state_s5_re[0].reshape(bs, S5_LANES), state_s5_im[0].reshape(bs, S5_LANES), w)
    y_prompt, y_s = _ffn(x32_t, mod2p, mod2s, x32_s, w, x_prompt.shape)

    conv_prompt = conv_p.reshape(CONV_W - 1, bp, D_LRU).transpose(1, 0, 2)[None]
    s5_shape_p = (1, bp, S5_GROUPS, S5_STATE)
    s5_shape_s = (1, bs, S5_GROUPS, S5_STATE)
    return (y_prompt, y_s.reshape(bs, 1, D_MODEL),
            conv_prompt, lru_p[None], s5r_p.reshape(s5_shape_p), s5i_p.reshape(s5_shape_p),
            conv_s.reshape(1, bs, CONV_W - 1, D_LRU), lru_s[None],
            s5r_s.reshape(s5_shape_s), s5i_s.reshape(s5_shape_s))
```

```python
import functools

import jax
import jax.numpy as jnp
from jax import lax
from jax.experimental import pallas as pl
from jax.experimental.pallas import tpu as pltpu

F32 = jnp.float32
BF16 = jnp.bfloat16

D_MODEL = 1024
D_LRU = 512
D_S5 = 512
D_IN = 2 * D_LRU + D_S5
CONV_W = 4
LRU_C = 8.0
S5_GROUP = 16
S5_GROUPS = D_S5 // S5_GROUP
S5_STATE = 64
S5_LANES = S5_GROUPS * S5_STATE
D_FF = 2816
N_MOD = 6
EPS = 1e-6

SUBLANES = 8
MXU_DIM = 256
VMEM_LIMIT_BYTES = 58 * 1024 * 1024

TIME_CHUNK = 64
GATE_BLOCK = MXU_DIM
GATE_BLOCKS = D_LRU // GATE_BLOCK
S5_BLOCKS = 4
S5_BLOCK_CH = D_S5 // S5_BLOCKS
S5_BLOCK_LANES = S5_LANES // S5_BLOCKS
FFN_CHUNK = MXU_DIM
FFN_CHUNKS = D_FF // FFN_CHUNK
STAGE_ROWS = 256
STAGE_COLS = D_IN
STAGE_DEPTH = 3
FFN_STAGE_DEPTH = 3
FFN_STAGE_ROWS = 128
FFN_TIME_CHUNK = 128
FFN_SUB_ROWS = 512


def _dot(a, b):
    return jnp.dot(a, b, preferred_element_type=F32)


def _rms(x, g):
    return x * lax.rsqrt(jnp.mean(x * x, axis=-1, keepdims=True) + EPS) * g


def _log_sigmoid(x):
    return jnp.minimum(x, 0.0) - jnp.log1p(jnp.exp(-jnp.abs(x)))


def _lru_gate_block(conv_j, wg_j, ba_j, bx_j, lam_j):
    gz = _dot(conv_j.astype(BF16), wg_j)
    r = jax.nn.sigmoid(gz[:, :GATE_BLOCK] + ba_j)
    ig = jax.nn.sigmoid(gz[:, GATE_BLOCK:] + bx_j)
    log_a = (LRU_C * _log_sigmoid(lam_j)) * r
    a = jnp.exp(log_a)
    mult = jnp.sqrt(-jnp.tanh(log_a) * (a * a + 1.0))
    return a, mult * (ig * conv_j)


def _s5_bu_block(ub, wb_ref, j):
    bu = _dot(ub[:, j * S5_BLOCK_CH:(j + 1) * S5_BLOCK_CH], wb_ref[j])
    return bu[:, :S5_BLOCK_LANES], bu[:, S5_BLOCK_LANES:]


def _s5_y_block(hr, hi, wc_ref, j):
    return (_dot(hr.astype(BF16), wc_ref[j, :S5_BLOCK_LANES, :])
            + _dot(hi.astype(BF16), wc_ref[j, S5_BLOCK_LANES:, :]))


def _s5_glu(y, u, d, wglu_ref):
    z = _dot(jax.nn.gelu(y + d * u).astype(BF16), wglu_ref[...])
    return z[:, :D_S5] * jax.nn.sigmoid(z[:, D_S5:])


def _stream_pieces(pieces, stage, sems):
    depth = stage.shape[0]

    def dst(k):
        r, c = pieces[k][0].shape
        return stage.at[k % depth, pl.ds(0, r), pl.ds(0, c)]

    copies = [pltpu.make_async_copy(pieces[k][0], dst(k), sems.at[k % depth]) for k in range(len(pieces))]
    for k in range(min(depth - 1, len(pieces))):
        copies[k].start()
    for k, (_, consume) in enumerate(pieces):
        if k + depth - 1 < len(pieces):
            copies[k + depth - 1].start()
        copies[k].wait()
        consume(dst(k))


def _batch_major_copies(tm_bufs, slot, bm_hbm, sems, chunk, to_hbm):
    _, tl, nb, _ = tm_bufs.shape
    copies = []
    for b in range(nb):
        hbm = bm_hbm.at[b, pl.ds(chunk * tl, tl), :]
        vmem = tm_bufs.at[slot, :, b, :]
        src, dst = (vmem, hbm) if to_hbm else (hbm, vmem)
        copies.append(pltpu.make_async_copy(src, dst, sems.at[slot, b]))
    return copies


def _s5_prep_kernel(lr_ref, li_ref, ldt_ref, bre_ref, bim_ref, abr_ref, abi_ref, bbr_ref, bbi_ref):
    lr = lr_ref[...]
    li = li_ref[...]
    dt = jnp.exp(ldt_ref[...])
    mag = jnp.exp(lr * dt)
    ab_r = mag * jnp.cos(li * dt)
    ab_i = mag * jnp.sin(li * dt)
    den = lr * lr + li * li
    fr = ((ab_r - 1.0) * lr + ab_i * li) / den
    fi = (ab_i * lr - (ab_r - 1.0) * li) / den
    br = bre_ref[...]
    bi = bim_ref[...]
    abr_ref[...] = ab_r
    abi_ref[...] = ab_i
    bbr_ref[...] = fr * br - fi * bi
    bbi_ref[...] = fr * bi + fi * br


def _s5_prep(lam_re, lam_im, log_dt, b_re, b_im):
    lr = lam_re.reshape(1, S5_LANES)
    li = lam_im.reshape(1, S5_LANES)
    ldt = jnp.repeat(log_dt, S5_STATE).reshape(1, S5_LANES)
    bre_t = b_re.reshape(S5_LANES, S5_GROUP).T
    bim_t = b_im.reshape(S5_LANES, S5_GROUP).T
    row = jax.ShapeDtypeStruct((1, S5_LANES), F32)
    mat = jax.ShapeDtypeStruct((S5_GROUP, S5_LANES), F32)
    return pl.pallas_call(_s5_prep_kernel, out_shape=(row, row, mat, mat), name="s5_prep")(
        lr, li, ldt, bre_t, bim_t)


def _mixer_kernel(
        n_chunks,
        x_hbm, cp_ref, cs_ref, adaw_hbm, adab_ref,
        xs_ref, sconv_ref, slru_ref, ss5r_ref, ss5i_ref,
        n1g_ref, win_hbm, cw_ref, cb_ref, wg_ref, ba_ref, bx_ref, lam_ref,
        abr_ref, abi_ref, wb_ref, wc_ref, d_ref, wglu_hbm, glru_ref, gs5_ref, wout_hbm,
        o_ref, mod2p_o, mod2s_o, convp_o, lrup_o, s5rp_o, s5ip_o,
        xs_o, convs_o, lrus_o, s5rs_o, s5is_o,
        x_buf, x_sems, stage, stage_sems, win_b, wglu_b, wout_b, mod1p, mod1s,
        xc_buf, h_lru, h_s5r, h_s5i, p_buf, hr_buf, hi_buf, a_buf, b_buf, y_buf, m_buf):
    step = pl.program_id(0)
    slot = step % 2
    _, tl, nb, _ = x_buf.shape
    rows = tl * nb
    hist = (CONV_W - 1) * nb
    half_mod = (N_MOD // 2) * D_MODEL

    @pl.when(step == 0)
    def _prologue():
        for c in _batch_major_copies(x_buf, 0, x_hbm, x_sems, 0, to_hbm=False):
            c.start()
        xc_buf[0:hist, :] = jnp.zeros((hist, D_LRU), F32)
        h_lru[...] = jnp.zeros_like(h_lru)
        h_s5r[...] = jnp.zeros_like(h_s5r)
        h_s5i[...] = jnp.zeros_like(h_s5i)

        cp = cp_ref[...]
        cs = cs_ref[...]
        silu_p = (cp * jax.nn.sigmoid(cp)).astype(BF16)
        silu_s = (cs * jax.nn.sigmoid(cs)).astype(BF16)
        pieces = []

        def ada_piece(cg, kb):
            cols = slice(cg * STAGE_COLS, (cg + 1) * STAGE_COLS)
            first = cols.start < half_mod
            dst_p, dst_s = (mod1p, mod1s) if first else (mod2p_o, mod2s_o)
            dcols = cols if first else slice(cols.start - half_mod, cols.stop - half_mod)
            krows = slice(kb * STAGE_ROWS, (kb + 1) * STAGE_ROWS)

            def consume(view):
                w = view[...].astype(BF16)
                part_p = _dot(silu_p[:, krows], w)
                part_s = _dot(silu_s[:, krows], w)
                if kb == 0:
                    dst_p[:, dcols] = part_p + adab_ref[:, cols]
                    dst_s[:, dcols] = part_s + adab_ref[:, cols]
                else:
                    dst_p[:, dcols] += part_p
                    dst_s[:, dcols] += part_s
            return adaw_hbm.at[pl.ds(kb * STAGE_ROWS, STAGE_ROWS), pl.ds(cg * STAGE_COLS, STAGE_COLS)], consume

        def cast_piece(src_hbm, dst, rb):
            ncols = dst.shape[1]

            def consume(view):
                dst[rb * STAGE_ROWS:(rb + 1) * STAGE_ROWS, :] = view[...].astype(BF16)
            return src_hbm.at[pl.ds(rb * STAGE_ROWS, STAGE_ROWS), pl.ds(0, ncols)], consume

        for src_hbm, dst in ((win_hbm, win_b),):
            for rb in range(dst.shape[0] // STAGE_ROWS):
                pieces.append(cast_piece(src_hbm, dst, rb))
        for cg in range(N_MOD * D_MODEL // STAGE_COLS):
            for kb in range(D_MODEL // STAGE_ROWS):
                pieces.append(ada_piece(cg, kb))
        for src_hbm, dst in ((wglu_hbm, wglu_b), (wout_hbm, wout_b)):
            for rb in range(dst.shape[0] // STAGE_ROWS):
                pieces.append(cast_piece(src_hbm, dst, rb))
        _stream_pieces(pieces, stage, stage_sems)

    @pl.when(step < n_chunks)
    def _prompt_step():
        @pl.when(step + 1 < n_chunks)
        def _():
            for c in _batch_major_copies(x_buf, 1 - slot, x_hbm, x_sems, step + 1, to_hbm=False):
                c.start()

        for c in _batch_major_copies(x_buf, slot, x_hbm, x_sems, step, to_hbm=False):
            c.wait()

        sh1 = mod1p[:, 0:D_MODEL]
        sc1 = mod1p[:, D_MODEL:2 * D_MODEL]
        g1 = mod1p[:, 2 * D_MODEL:3 * D_MODEL]

        x3 = x_buf[slot]
        hn = _rms(x3, (n1g_ref[...] * (1.0 + sc1))[None]) + sh1[None]
        p_buf[...] = _dot(hn.reshape(rows, D_MODEL).astype(BF16), win_b[...])

        xc_buf[hist:hist + rows, :] = p_buf[:, 0:D_LRU]
        for j in range(GATE_BLOCKS):
            sl = slice(j * GATE_BLOCK, (j + 1) * GATE_BLOCK)
            conv_j = cb_ref[:, sl]
            for k in range(CONV_W):
                conv_j = conv_j + cw_ref[k:k + 1, sl] * xc_buf[k * nb:k * nb + rows, sl]
            a, b = _lru_gate_block(conv_j, wg_ref[j], ba_ref[:, sl], bx_ref[:, sl], lam_ref[:, sl])
            a_buf[:, sl] = a
            b_buf[:, sl] = b

        h = h_lru[...]
        for t in range(tl):
            h = a_buf[t * nb:(t + 1) * nb, :] * h + b_buf[t * nb:(t + 1) * nb, :]
            b_buf[t * nb:(t + 1) * nb, :] = h
        h_lru[...] = h

        lru_out = b_buf[...] * jax.nn.gelu(p_buf[:, D_LRU:2 * D_LRU])
        m_buf[:, 0:D_LRU] = _rms(lru_out, glru_ref[...]).astype(BF16)

        u = p_buf[:, 2 * D_LRU:]
        ub = u.astype(BF16)
        for j in range(S5_BLOCKS):
            sl = slice(j * S5_BLOCK_LANES, (j + 1) * S5_BLOCK_LANES)
            hr_j = hr_buf.at[j % 2]
            hi_j = hi_buf.at[j % 2]
            bu_r, bu_i = _s5_bu_block(ub, wb_ref, j)
            hr_j[...] = bu_r
            hi_j[...] = bu_i
            ar = jnp.broadcast_to(abr_ref[:, sl], (nb, S5_BLOCK_LANES))
            ai = jnp.broadcast_to(abi_ref[:, sl], (nb, S5_BLOCK_LANES))

            hr = h_s5r[:, sl]
            hi = h_s5i[:, sl]
            for t in range(tl):
                rs = slice(t * nb, (t + 1) * nb)
                hr, hi = ar * hr - ai * hi + hr_j[rs, :], ar * hi + ai * hr + hi_j[rs, :]
                hr_j[rs, :] = hr
                hi_j[rs, :] = hi
            h_s5r[:, sl] = hr
            h_s5i[:, sl] = hi
            y_buf[:, j * S5_BLOCK_CH:(j + 1) * S5_BLOCK_CH] = _s5_y_block(hr_j[...], hi_j[...], wc_ref, j)

        s5_out = _s5_glu(y_buf[...], u, d_ref[...], wglu_b)
        m_buf[:, D_LRU:] = _rms(s5_out, gs5_ref[...]).astype(BF16)

        mix = _dot(m_buf[...], wout_b[...]).reshape(tl, nb, D_MODEL)
        o_ref[...] = x3 + g1[None] * mix

        xc_buf[0:hist, :] = xc_buf[rows:rows + hist, :]

    @pl.when(step == n_chunks)
    def _sample_step():
        convp_o[...] = xc_buf[0:hist, :]
        lrup_o[...] = h_lru[...]
        s5rp_o[...] = h_s5r[...]
        s5ip_o[...] = h_s5i[...]

        ns = xs_ref.shape[0]
        sh1 = mod1s[:, 0:D_MODEL]
        sc1 = mod1s[:, D_MODEL:2 * D_MODEL]
        g1 = mod1s[:, 2 * D_MODEL:3 * D_MODEL]

        x = xs_ref[...]
        hn = _rms(x, n1g_ref[...] * (1.0 + sc1)) + sh1
        p = _dot(hn.astype(BF16), win_b[...])
        lru_x = p[:, 0:D_LRU]

        hist_w = (CONV_W - 1) * D_LRU
        convs_o[:, 0:hist_w - D_LRU] = sconv_ref[:, D_LRU:hist_w]
        convs_o[:, hist_w - D_LRU:hist_w] = lru_x
        for j in range(GATE_BLOCKS):
            sl = slice(j * GATE_BLOCK, (j + 1) * GATE_BLOCK)
            conv_j = cb_ref[:, sl] + cw_ref[CONV_W - 1:CONV_W, sl] * lru_x[:, sl]
            for k in range(CONV_W - 1):
                conv_j = conv_j + cw_ref[k:k + 1, sl] * sconv_ref[:, k * D_LRU + j * GATE_BLOCK:
                                                                   k * D_LRU + (j + 1) * GATE_BLOCK]
            a, b = _lru_gate_block(conv_j, wg_ref[j], ba_ref[:, sl], bx_ref[:, sl], lam_ref[:, sl])
            lrus_o[:, sl] = a * slru_ref[:, sl] + b
        lru_out = lrus_o[...] * jax.nn.gelu(p[:, D_LRU:2 * D_LRU])
        ms_buf = m_buf.at[pl.ds(0, ns), :]
        ms_buf[:, 0:D_LRU] = _rms(lru_out, glru_ref[...]).astype(BF16)

        u = p[:, 2 * D_LRU:]
        ub = u.astype(BF16)
        ys_buf = y_buf.at[pl.ds(0, ns), :]
        for j in range(S5_BLOCKS):
            sl = slice(j * S5_BLOCK_LANES, (j + 1) * S5_BLOCK_LANES)
            bu_r, bu_i = _s5_bu_block(ub, wb_ref, j)
            ar = abr_ref[:, sl]
            ai = abi_ref[:, sl]
            h0r = ss5r_ref[:, sl]
            h0i = ss5i_ref[:, sl]
            hr = ar * h0r - ai * h0i + bu_r
            hi = ar * h0i + ai * h0r + bu_i
            s5rs_o[:, sl] = hr
            s5is_o[:, sl] = hi
            ys_buf[:, j * S5_BLOCK_CH:(j + 1) * S5_BLOCK_CH] = _s5_y_block(hr, hi, wc_ref, j)
        s5_out = _s5_glu(ys_buf[...], u, d_ref[...], wglu_b)
        ms_buf[:, D_LRU:] = _rms(s5_out, gs5_ref[...]).astype(BF16)
        xs_o[...] = x + g1 * _dot(ms_buf[...], wout_b[...])


def _const_spec(shape):
    nd = len(shape)
    return pl.BlockSpec(shape, lambda i, _nd=nd: (0,) * _nd, pipeline_mode=pl.Buffered(1))


def _out_const_spec(shape):
    nd = len(shape)
    return pl.BlockSpec(shape, lambda i, _nd=nd: (0,) * _nd)


_HBM = pl.BlockSpec(memory_space=pl.ANY)


def _mixer(x_prompt, c_prompt, c_sample, ada_w, ada_b, x_s, sconv, slru, ss5r, ss5i, w):
    nb, t_len, _ = x_prompt.shape
    ns = x_s.shape[0]
    tl = TIME_CHUNK
    n_chunks = t_len // tl
    rows = tl * nb
    hist = (CONV_W - 1) * nb
    half_mod = (N_MOD // 2) * D_MODEL

    inputs = [
        (x_prompt, _HBM), (c_prompt, None), (c_sample, None), (ada_w, _HBM), (ada_b, None),
        (x_s, None), (sconv, None), (slru, None), (ss5r, None), (ss5i, None),
        (w["n1g"], None), (w["win"], _HBM), (w["cw"], None), (w["cb"], None), (w["wg"], None),
        (w["ba"], None), (w["bx"], None), (w["lam"], None),
        (w["abr"], None), (w["abi"], None), (w["wb"], None), (w["wc"], None), (w["d"], None),
        (w["wglu"], _HBM), (w["glru"], None), (w["gs5"], None), (w["wout"], _HBM),
    ]
    in_specs = [spec if spec is not None else _const_spec(a.shape) for a, spec in inputs]

    out_shapes = [
        ((t_len, nb, D_MODEL), pl.BlockSpec((tl, nb, D_MODEL), lambda i: (jnp.minimum(i, n_chunks - 1), 0, 0))),
        ((nb, half_mod), None), ((ns, half_mod), None),
        ((hist, D_LRU), None), ((nb, D_LRU), None), ((nb, S5_LANES), None), ((nb, S5_LANES), None),
        ((ns, D_MODEL), None), ((ns, (CONV_W - 1) * D_LRU), None), ((ns, D_LRU), None),
        ((ns, S5_LANES), None), ((ns, S5_LANES), None),
    ]
    out_specs = [spec if spec is not None else _out_const_spec(s) for s, spec in out_shapes]
    out_shape = [jax.ShapeDtypeStruct(s, F32) for s, _ in out_shapes]

    scratch = [
        pltpu.VMEM((2, tl, nb, D_MODEL), F32),
        pltpu.SemaphoreType.DMA((2, nb)),
        pltpu.VMEM((STAGE_DEPTH, STAGE_ROWS, STAGE_COLS), F32),
        pltpu.SemaphoreType.DMA((STAGE_DEPTH,)),
        pltpu.VMEM((D_MODEL, D_IN), BF16),
        pltpu.VMEM((D_S5, 2 * D_S5), BF16),
        pltpu.VMEM((D_LRU + D_S5, D_MODEL), BF16),
        pltpu.VMEM((nb, half_mod), F32),
        pltpu.VMEM((ns, half_mod), F32),
        pltpu.VMEM((hist + rows, D_LRU), F32),
        pltpu.VMEM((nb, D_LRU), F32),
        pltpu.VMEM((nb, S5_LANES), F32),
        pltpu.VMEM((nb, S5_LANES), F32),
        pltpu.VMEM((rows, D_IN), F32),
        pltpu.VMEM((2, rows, S5_BLOCK_LANES), F32),
        pltpu.VMEM((2, rows, S5_BLOCK_LANES), F32),
        pltpu.VMEM((rows, D_LRU), F32),
        pltpu.VMEM((rows, D_LRU), F32),
        pltpu.VMEM((rows, D_S5), F32),
        pltpu.VMEM((rows, D_LRU + D_S5), BF16),
    ]
    return pl.pallas_call(
        functools.partial(_mixer_kernel, n_chunks),
        grid=(n_chunks + 1,),
        in_specs=in_specs,
        out_specs=out_specs,
        out_shape=out_shape,
        scratch_shapes=scratch,
        compiler_params=pltpu.CompilerParams(
            dimension_semantics=("arbitrary",), vmem_limit_bytes=VMEM_LIMIT_BYTES),
        name="mixer",
    )(*[a for a, _ in inputs])


def _ffn_rows(x3, mod_ref, n2g_ref, wgate_b, wup_b, wdown_b, gf_ref, act_ref):
    groups, mrows, _ = x3.shape
    rows = groups * mrows
    sh2 = mod_ref[:, 0:D_MODEL]
    sc2 = mod_ref[:, D_MODEL:2 * D_MODEL]
    g2 = mod_ref[:, 2 * D_MODEL:3 * D_MODEL]

    hn = _rms(x3, (n2g_ref[...] * (1.0 + sc2))[None]) + sh2[None]
    hb = hn.reshape(rows, D_MODEL).astype(BF16)
    for c in range(FFN_CHUNKS):
        g = _dot(hb, wgate_b[c])
        up = _dot(hb, wup_b[c])
        act_ref[:, c * FFN_CHUNK:(c + 1) * FFN_CHUNK] = (g * jax.nn.sigmoid(g) * up).astype(BF16)
    ffn = _dot(act_ref[...], wdown_b[...]).reshape(groups, mrows, D_MODEL)
    return _rms(x3 + g2[None] * ffn, gf_ref[...])


def _ffn_kernel(
        n_chunks,
        x_ref, mod2p_ref, mod2s_ref, xs_ref, n2g_ref, wgate_hbm, wup_hbm, wdown_hbm, gf_ref,
        y_hbm, ys_o,
        stage_a, stage_b, stage_a_sems, stage_b_sems, wgate_b, wup_b, wdown_b, act_buf, y_buf, y_sems):
    step = pl.program_id(0)
    slot = step % 2

    @pl.when(step == 0)
    def _prologue():
        def wide_piece(src_hbm, dst, rb):
            rows = slice(rb * FFN_STAGE_ROWS, (rb + 1) * FFN_STAGE_ROWS)

            def consume(view):
                for c in range(FFN_CHUNKS):
                    dst[c, rows, :] = view[:, c * FFN_CHUNK:(c + 1) * FFN_CHUNK].astype(BF16)
            return src_hbm.at[pl.ds(rb * FFN_STAGE_ROWS, FFN_STAGE_ROWS), :], consume

        def row_piece(c):
            def consume(view):
                wdown_b[c * FFN_CHUNK:(c + 1) * FFN_CHUNK, :] = view[...].astype(BF16)
            return wdown_hbm.at[pl.ds(c * FFN_CHUNK, FFN_CHUNK), :], consume

        pieces = []
        for rb in range(D_MODEL // FFN_STAGE_ROWS):
            pieces.append(wide_piece(wgate_hbm, wgate_b, rb))
            pieces.append(wide_piece(wup_hbm, wup_b, rb))
        _stream_pieces(pieces, stage_a, stage_a_sems)
        _stream_pieces([row_piece(c) for c in range(FFN_CHUNKS)], stage_b, stage_b_sems)

    @pl.when(step < n_chunks)
    def _prompt_step():
        @pl.when(step >= 2)
        def _():
            for c in _batch_major_copies(y_buf, slot, y_hbm, y_sems, step - 2, to_hbm=True):
                c.wait()

        tl, nb, _ = x_ref.shape
        sub = FFN_SUB_ROWS // nb
        for h in range(tl // sub):
            y_buf[slot, pl.ds(h * sub, sub)] = _ffn_rows(
                x_ref[h * sub:(h + 1) * sub], mod2p_ref, n2g_ref, wgate_b, wup_b, wdown_b, gf_ref,
                act_buf.at[pl.ds(h * FFN_SUB_ROWS, FFN_SUB_ROWS), :])
        for c in _batch_major_copies(y_buf, slot, y_hbm, y_sems, step, to_hbm=True):
            c.start()

    @pl.when(step == n_chunks)
    def _sample_step():
        ns = xs_ref.shape[0]
        ys_o[...] = _ffn_rows(xs_ref[...][None], mod2s_ref, n2g_ref, wgate_b, wup_b, wdown_b, gf_ref,
                              act_buf.at[pl.ds(0, ns), :])[0]
        for s in range(max(n_chunks - 2, 0), n_chunks):
            for c in _batch_major_copies(y_buf, s % 2, y_hbm, y_sems, s, to_hbm=True):
                c.wait()


def _ffn(x32_t, mod2p, mod2s, x32_s, w, out_shape_p):
    t_len, nb, _ = x32_t.shape
    ns = x32_s.shape[0]
    tl = FFN_TIME_CHUNK
    n_chunks = t_len // tl
    rows = tl * nb
    inputs = [
        (x32_t, pl.BlockSpec((tl, nb, D_MODEL), lambda i: (jnp.minimum(i, n_chunks - 1), 0, 0))),
        (mod2p, None), (mod2s, None), (x32_s, None), (w["n2g"], None),
        (w["wgate"], _HBM), (w["wup"], _HBM), (w["wdown"], _HBM), (w["gf"], None),
    ]
    in_specs = [spec if spec is not None else _const_spec(a.shape) for a, spec in inputs]
    scratch = [
        pltpu.VMEM((FFN_STAGE_DEPTH, FFN_STAGE_ROWS, D_FF), F32),
        pltpu.VMEM((FFN_STAGE_DEPTH, FFN_CHUNK, D_MODEL), F32),
        pltpu.SemaphoreType.DMA((FFN_STAGE_DEPTH,)),
        pltpu.SemaphoreType.DMA((FFN_STAGE_DEPTH,)),
        pltpu.VMEM((FFN_CHUNKS, D_MODEL, FFN_CHUNK), BF16),
        pltpu.VMEM((FFN_CHUNKS, D_MODEL, FFN_CHUNK), BF16),
        pltpu.VMEM((D_FF, D_MODEL), BF16),
        pltpu.VMEM((rows, D_FF), BF16),
        pltpu.VMEM((2, tl, nb, D_MODEL), F32),
        pltpu.SemaphoreType.DMA((2, nb)),
    ]
    return pl.pallas_call(
        functools.partial(_ffn_kernel, n_chunks),
        grid=(n_chunks + 1,),
        in_specs=in_specs,
        out_specs=[_HBM, _out_const_spec((ns, D_MODEL))],
        out_shape=[jax.ShapeDtypeStruct(out_shape_p, F32), jax.ShapeDtypeStruct((ns, D_MODEL), F32)],
        scratch_shapes=scratch,
        compiler_params=pltpu.CompilerParams(
            dimension_semantics=("arbitrary",), vmem_limit_bytes=VMEM_LIMIT_BYTES),
        name="ffn",
    )(*[a for a, _ in inputs])


def _block_diag(blocks):
    n, r, c = blocks.shape
    eye = jnp.eye(n, dtype=blocks.dtype)
    return (eye[:, None, :, None] * blocks[:, :, None, :]).reshape(n * r, n * c)


def _prep_weights(p, abr, abi, bbr_t, bbi_t):
    w = {}
    row = lambda v: v.reshape(1, -1)
    w["n1g"] = row(p["norm1_g"])
    w["win"] = p["w_in"]
    w["cw"] = p["conv_w"]
    w["cb"] = row(p["conv_b"])
    wa = _block_diag(p["lru_wa"])
    wx = _block_diag(p["lru_wx"])
    w["wg"] = jnp.stack([
        jnp.concatenate([wa[j * GATE_BLOCK:(j + 1) * GATE_BLOCK, j * GATE_BLOCK:(j + 1) * GATE_BLOCK],
                         wx[j * GATE_BLOCK:(j + 1) * GATE_BLOCK, j * GATE_BLOCK:(j + 1) * GATE_BLOCK]], axis=1)
        for j in range(GATE_BLOCKS)]).astype(BF16)
    w["ba"] = row(p["lru_ba"])
    w["bx"] = row(p["lru_bx"])
    w["lam"] = row(p["lru_lambda"])
    w["abr"] = abr
    w["abi"] = abi
    to_blocks = lambda m: m.reshape(S5_GROUP, S5_GROUPS, S5_STATE).transpose(1, 0, 2)
    wbr = _block_diag(to_blocks(bbr_t))
    wbi = _block_diag(to_blocks(bbi_t))
    w["wb"] = jnp.stack([
        jnp.concatenate([wbr[j * S5_BLOCK_CH:(j + 1) * S5_BLOCK_CH, j * S5_BLOCK_LANES:(j + 1) * S5_BLOCK_LANES],
                         wbi[j * S5_BLOCK_CH:(j + 1) * S5_BLOCK_CH, j * S5_BLOCK_LANES:(j + 1) * S5_BLOCK_LANES]],
                        axis=1)
        for j in range(S5_BLOCKS)]).astype(BF16)
    wcr = _block_diag(p["s5_c_re"].transpose(0, 2, 1))
    wci = _block_diag(-p["s5_c_im"].transpose(0, 2, 1))
    w["wc"] = jnp.stack([
        jnp.concatenate([wcr[j * S5_BLOCK_LANES:(j + 1) * S5_BLOCK_LANES, j * S5_BLOCK_CH:(j + 1) * S5_BLOCK_CH],
                         wci[j * S5_BLOCK_LANES:(j + 1) * S5_BLOCK_LANES, j * S5_BLOCK_CH:(j + 1) * S5_BLOCK_CH]],
                        axis=0)
        for j in range(S5_BLOCKS)]).astype(BF16)
    w["d"] = row(p["s5_d"])
    w["wglu"] = p["s5_w_glu"]
    w["glru"] = row(p["g_lru_out"])
    w["gs5"] = row(p["g_s5_out"])
    w["wout"] = p["w_out"]
    w["n2g"] = row(p["norm2_g"])
    w["wgate"] = p["ffn_w_gate"]
    w["wup"] = p["ffn_w_up"]
    w["wdown"] = p["ffn_w_down"]
    w["gf"] = row(p["final_norm_g"])
    return w


def kernel(x_prompt, x_sample, state_conv, state_lru, state_s5_re, state_s5_im, c_prompt, c_sample,
           ada_w, ada_b, norm1_g, w_in, conv_w, conv_b, lru_wa, lru_ba, lru_wx, lru_bx, lru_lambda,
           s5_lambda_re, s5_lambda_im, s5_log_dt, s5_b_re, s5_b_im, s5_c_re, s5_c_im, s5_d, s5_w_glu,
           g_lru_out, g_s5_out, w_out, norm2_g, ffn_w_gate, ffn_w_up, ffn_w_down, final_norm_g):
    depth = ada_w.shape[0]
    assert depth == 1, "single-layer decoder step"
    bp, t_len, _ = x_prompt.shape
    bs, dec_t, _ = x_sample.shape
    assert bp == SUBLANES and dec_t == 1 and t_len % TIME_CHUNK == 0 and t_len % FFN_TIME_CHUNK == 0
    assert (FFN_TIME_CHUNK * SUBLANES) % FFN_SUB_ROWS == 0 and bs <= FFN_SUB_ROWS

    p = dict(norm1_g=norm1_g[0], w_in=w_in[0], conv_w=conv_w[0], conv_b=conv_b[0], lru_wa=lru_wa[0],
             lru_ba=lru_ba[0], lru_wx=lru_wx[0], lru_bx=lru_bx[0], lru_lambda=lru_lambda[0],
             s5_c_re=s5_c_re[0], s5_c_im=s5_c_im[0], s5_d=s5_d[0], s5_w_glu=s5_w_glu[0],
             g_lru_out=g_lru_out[0], g_s5_out=g_s5_out[0], w_out=w_out[0], norm2_g=norm2_g[0],
             ffn_w_gate=ffn_w_gate[0], ffn_w_up=ffn_w_up[0], ffn_w_down=ffn_w_down[0],
             final_norm_g=final_norm_g)

    abr, abi, bbr_t, bbi_t = _s5_prep(s5_lambda_re[0], s5_lambda_im[0], s5_log_dt[0], s5_b_re[0], s5_b_im[0])
    w = _prep_weights(p, abr, abi, bbr_t, bbi_t)

    (x32_t, mod2p, mod2s, conv_p, lru_p, s5r_p, s5i_p,
     x32_s, conv_s, lru_s, s5r_s, s5i_s) = _mixer(
        x_prompt, c_prompt, c_sample, ada_w[0], ada_b[0].reshape(1, -1),
        x_sample.reshape(bs, D_MODEL), state_conv[0].reshape(bs, (CONV_W - 1) * D_LRU), state_lru[0],
        state_s5_re[0].reshape(bs, S5_LANES), state_s5_im[0].reshape(bs, S5_LANES), w)
    y_prompt, y_s = _ffn(x32_t, mod2p, mod2s, x32_s, w, x_prompt.shape)

    conv_prompt = conv_p.reshape(CONV_W - 1, bp, D_LRU).transpose(1, 0, 2)[None]
    s5_shape_p = (1, bp, S5_GROUPS, S5_STATE)
    s5_shape_s = (1, bs, S5_GROUPS, S5_STATE)
    return (y_prompt, y_s.reshape(bs, 1, D_MODEL),
            conv_prompt, lru_p[None], s5r_p.reshape(s5_shape_p), s5i_p.reshape(s5_shape_p),
            conv_s.reshape(1, bs, CONV_W - 1, D_LRU), lru_s[None],
            s5r_s.reshape(s5_shape_s), s5i_s.reshape(s5_shape_s))
```

```python
import functools

import jax
import jax.numpy as jnp
from jax import lax
from jax.experimental import pallas as pl
from jax.experimental.pallas import tpu as pltpu

F32 = jnp.float32
BF16 = jnp.bfloat16

D_MODEL = 1024
D_LRU = 512
D_S5 = 512
D_IN = 2 * D_LRU + D_S5
CONV_W = 4
LRU_C = 8.0
S5_GROUP = 16
S5_GROUPS = D_S5 // S5_GROUP
S5_STATE = 64
S5_LANES = S5_GROUPS * S5_STATE
D_FF = 2816
N_MOD = 6
EPS = 1e-6

SUBLANES = 8
MXU_DIM = 256
DMA_PRIORITIES = 2
VMEM_LIMIT_BYTES = 58 * 1024 * 1024

TIME_CHUNK = 64
GATE_BLOCK = MXU_DIM
GATE_BLOCKS = D_LRU // GATE_BLOCK
S5_BLOCKS = 4
S5_BLOCK_CH = D_S5 // S5_BLOCKS
S5_BLOCK_LANES = S5_LANES // S5_BLOCKS
FFN_CHUNK = MXU_DIM
FFN_CHUNKS = D_FF // FFN_CHUNK
STAGE_ROWS = 256
STAGE_COLS = D_IN
STAGE_DEPTH = 3
FFN_STAGE_DEPTH = 3
FFN_STAGE_ROWS = 128
FFN_TIME_CHUNK = 128
FFN_SUB_ROWS = 512


def _dot(a, b):
    return jnp.dot(a, b, preferred_element_type=F32)


def _rms(x, g):
    return x * lax.rsqrt(jnp.mean(x * x, axis=-1, keepdims=True) + EPS) * g


def _log_sigmoid(x):
    return jnp.minimum(x, 0.0) - jnp.log1p(jnp.exp(-jnp.abs(x)))


def _lru_gate_block(conv_j, wg_j, ba_j, bx_j, lam_j):
    gz = _dot(conv_j.astype(BF16), wg_j)
    r = jax.nn.sigmoid(gz[:, :GATE_BLOCK] + ba_j)
    ig = jax.nn.sigmoid(gz[:, GATE_BLOCK:] + bx_j)
    log_a = (LRU_C * _log_sigmoid(lam_j)) * r
    a = jnp.exp(log_a)
    mult = jnp.sqrt(-jnp.tanh(log_a) * (a * a + 1.0))
    return a, mult * (ig * conv_j)


def _s5_bu_block(ub, wb_ref, j):
    bu = _dot(ub[:, j * S5_BLOCK_CH:(j + 1) * S5_BLOCK_CH], wb_ref[j])
    return bu[:, :S5_BLOCK_LANES], bu[:, S5_BLOCK_LANES:]


def _s5_y_block(hr, hi, wc_ref, j):
    return (_dot(hr.astype(BF16), wc_ref[j, :S5_BLOCK_LANES, :])
            + _dot(hi.astype(BF16), wc_ref[j, S5_BLOCK_LANES:, :]))


def _s5_glu(y, u, d, wglu_ref):
    z = _dot(jax.nn.gelu(y + d * u).astype(BF16), wglu_ref[...])
    return z[:, :D_S5] * jax.nn.sigmoid(z[:, D_S5:])


def _stream_pieces(pieces, stage, sems):
    depth = stage.shape[0]

    def dst(k):
        r, c = pieces[k][0].shape
        return stage.at[k % depth, pl.ds(0, r), pl.ds(0, c)]

    copies = [pltpu.make_async_copy(pieces[k][0], dst(k), sems.at[k % depth]) for k in range(len(pieces))]
    for k in range(min(depth - 1, len(pieces))):
        copies[k].start(priority=k % DMA_PRIORITIES)
    for k, (_, consume) in enumerate(pieces):
        if k + depth - 1 < len(pieces):
            copies[k + depth - 1].start(priority=(k + depth - 1) % DMA_PRIORITIES)
        copies[k].wait()
        consume(dst(k))


def _batch_major_copies(tm_bufs, slot, bm_hbm, sems, chunk, to_hbm):
    _, tl, nb, _ = tm_bufs.shape
    copies = []
    for b in range(nb):
        hbm = bm_hbm.at[b, pl.ds(chunk * tl, tl), :]
        vmem = tm_bufs.at[slot, :, b, :]
        src, dst = (vmem, hbm) if to_hbm else (hbm, vmem)
        copies.append(pltpu.make_async_copy(src, dst, sems.at[slot, b]))
    return copies


def _s5_prep_kernel(lr_ref, li_ref, ldt_ref, bre_ref, bim_ref, abr_ref, abi_ref, bbr_ref, bbi_ref):
    lr = lr_ref[...]
    li = li_ref[...]
    dt = jnp.exp(ldt_ref[...])
    mag = jnp.exp(lr * dt)
    ab_r = mag * jnp.cos(li * dt)
    ab_i = mag * jnp.sin(li * dt)
    den = lr * lr + li * li
    fr = ((ab_r - 1.0) * lr + ab_i * li) / den
    fi = (ab_i * lr - (ab_r - 1.0) * li) / den
    br = bre_ref[...]
    bi = bim_ref[...]
    abr_ref[...] = ab_r
    abi_ref[...] = ab_i
    bbr_ref[...] = fr * br - fi * bi
    bbi_ref[...] = fr * bi + fi * br


def _s5_prep(lam_re, lam_im, log_dt, b_re, b_im):
    lr = lam_re.reshape(1, S5_LANES)
    li = lam_im.reshape(1, S5_LANES)
    ldt = jnp.repeat(log_dt, S5_STATE).reshape(1, S5_LANES)
    bre_t = b_re.reshape(S5_LANES, S5_GROUP).T
    bim_t = b_im.reshape(S5_LANES, S5_GROUP).T
    row = jax.ShapeDtypeStruct((1, S5_LANES), F32)
    mat = jax.ShapeDtypeStruct((S5_GROUP, S5_LANES), F32)
    return pl.pallas_call(_s5_prep_kernel, out_shape=(row, row, mat, mat), name="s5_prep")(
        lr, li, ldt, bre_t, bim_t)


def _mixer_kernel(
        n_chunks,
        x_hbm, cp_ref, cs_ref, adaw_hbm, adab_ref,
        xs_ref, sconv_ref, slru_ref, ss5r_ref, ss5i_ref,
        n1g_ref, win_hbm, cw_ref, cb_ref, wg_ref, ba_ref, bx_ref, lam_ref,
        abr_ref, abi_ref, wb_ref, wc_ref, d_ref, wglu_hbm, glru_ref, gs5_ref, wout_hbm,
        o_ref, mod2p_o, mod2s_o, convp_o, lrup_o, s5rp_o, s5ip_o,
        xs_o, convs_o, lrus_o, s5rs_o, s5is_o,
        x_buf, x_sems, stage, stage_sems, win_b, wglu_b, wout_b, mod1p, mod1s,
        xc_buf, h_lru, h_s5r, h_s5i, p_buf, hr_buf, hi_buf, a_buf, b_buf, y_buf, m_buf):
    step = pl.program_id(0)
    slot = step % 2
    _, tl, nb, _ = x_buf.shape
    rows = tl * nb
    hist = (CONV_W - 1) * nb
    half_mod = (N_MOD // 2) * D_MODEL

    @pl.when(step == 0)
    def _prologue():
        for c in _batch_major_copies(x_buf, 0, x_hbm, x_sems, 0, to_hbm=False):
            c.start()
        xc_buf[0:hist, :] = jnp.zeros((hist, D_LRU), F32)
        h_lru[...] = jnp.zeros_like(h_lru)
        h_s5r[...] = jnp.zeros_like(h_s5r)
        h_s5i[...] = jnp.zeros_like(h_s5i)

        cp = cp_ref[...]
        cs = cs_ref[...]
        silu_p = (cp * jax.nn.sigmoid(cp)).astype(BF16)
        silu_s = (cs * jax.nn.sigmoid(cs)).astype(BF16)
        pieces = []

        def ada_piece(cg, kb):
            cols = slice(cg * STAGE_COLS, (cg + 1) * STAGE_COLS)
            first = cols.start < half_mod
            dst_p, dst_s = (mod1p, mod1s) if first else (mod2p_o, mod2s_o)
            dcols = cols if first else slice(cols.start - half_mod, cols.stop - half_mod)
            krows = slice(kb * STAGE_ROWS, (kb + 1) * STAGE_ROWS)

            def consume(view):
                w = view[...].astype(BF16)
                part_p = _dot(silu_p[:, krows], w)
                part_s = _dot(silu_s[:, krows], w)
                if kb == 0:
                    dst_p[:, dcols] = part_p + adab_ref[:, cols]
                    dst_s[:, dcols] = part_s + adab_ref[:, cols]
                else:
                    dst_p[:, dcols] += part_p
                    dst_s[:, dcols] += part_s
            return adaw_hbm.at[pl.ds(kb * STAGE_ROWS, STAGE_ROWS), pl.ds(cg * STAGE_COLS, STAGE_COLS)], consume

        def cast_piece(src_hbm, dst, rb):
            ncols = dst.shape[1]

            def consume(view):
                dst[rb * STAGE_ROWS:(rb + 1) * STAGE_ROWS, :] = view[...].astype(BF16)
            return src_hbm.at[pl.ds(rb * STAGE_ROWS, STAGE_ROWS), pl.ds(0, ncols)], consume

        for src_hbm, dst in ((win_hbm, win_b),):
            for rb in range(dst.shape[0] // STAGE_ROWS):
                pieces.append(cast_piece(src_hbm, dst, rb))
        for cg in range(N_MOD * D_MODEL // STAGE_COLS):
            for kb in range(D_MODEL // STAGE_ROWS):
                pieces.append(ada_piece(cg, kb))
        for src_hbm, dst in ((wglu_hbm, wglu_b), (wout_hbm, wout_b)):
            for rb in range(dst.shape[0] // STAGE_ROWS):
                pieces.append(cast_piece(src_hbm, dst, rb))
        _stream_pieces(pieces, stage, stage_sems)

    @pl.when(step < n_chunks)
    def _prompt_step():
        @pl.when(step + 1 < n_chunks)
        def _():
            for c in _batch_major_copies(x_buf, 1 - slot, x_hbm, x_sems, step + 1, to_hbm=False):
                c.start()

        for c in _batch_major_copies(x_buf, slot, x_hbm, x_sems, step, to_hbm=False):
            c.wait()

        sh1 = mod1p[:, 0:D_MODEL]
        sc1 = mod1p[:, D_MODEL:2 * D_MODEL]
        g1 = mod1p[:, 2 * D_MODEL:3 * D_MODEL]

        x3 = x_buf[slot]
        hn = _rms(x3, (n1g_ref[...] * (1.0 + sc1))[None]) + sh1[None]
        p_buf[...] = _dot(hn.reshape(rows, D_MODEL).astype(BF16), win_b[...])

        xc_buf[hist:hist + rows, :] = p_buf[:, 0:D_LRU]
        for j in range(GATE_BLOCKS):
            sl = slice(j * GATE_BLOCK, (j + 1) * GATE_BLOCK)
            conv_j = cb_ref[:, sl]
            for k in range(CONV_W):
                conv_j = conv_j + cw_ref[k:k + 1, sl] * xc_buf[k * nb:k * nb + rows, sl]
            a, b = _lru_gate_block(conv_j, wg_ref[j], ba_ref[:, sl], bx_ref[:, sl], lam_ref[:, sl])
            a_buf[:, sl] = a
            b_buf[:, sl] = b

        h = h_lru[...]
        for t in range(tl):
            h = a_buf[t * nb:(t + 1) * nb, :] * h + b_buf[t * nb:(t + 1) * nb, :]
            b_buf[t * nb:(t + 1) * nb, :] = h
        h_lru[...] = h

        lru_out = b_buf[...] * jax.nn.gelu(p_buf[:, D_LRU:2 * D_LRU])
        m_buf[:, 0:D_LRU] = _rms(lru_out, glru_ref[...]).astype(BF16)

        u = p_buf[:, 2 * D_LRU:]
        ub = u.astype(BF16)
        for j in range(S5_BLOCKS):
            sl = slice(j * S5_BLOCK_LANES, (j + 1) * S5_BLOCK_LANES)
            hr_j = hr_buf.at[j % 2]
            hi_j = hi_buf.at[j % 2]
            bu_r, bu_i = _s5_bu_block(ub, wb_ref, j)
            hr_j[...] = bu_r
            hi_j[...] = bu_i
            ar = jnp.broadcast_to(abr_ref[:, sl], (nb, S5_BLOCK_LANES))
            ai = jnp.broadcast_to(abi_ref[:, sl], (nb, S5_BLOCK_LANES))

            hr = h_s5r[:, sl]
            hi = h_s5i[:, sl]
            for t in range(tl):
                rs = slice(t * nb, (t + 1) * nb)
                hr, hi = ar * hr - ai * hi + hr_j[rs, :], ar * hi + ai * hr + hi_j[rs, :]
                hr_j[rs, :] = hr
                hi_j[rs, :] = hi
            h_s5r[:, sl] = hr
            h_s5i[:, sl] = hi
            y_buf[:, j * S5_BLOCK_CH:(j + 1) * S5_BLOCK_CH] = _s5_y_block(hr_j[...], hi_j[...], wc_ref, j)

        s5_out = _s5_glu(y_buf[...], u, d_ref[...], wglu_b)
        m_buf[:, D_LRU:] = _rms(s5_out, gs5_ref[...]).astype(BF16)

        mix = _dot(m_buf[...], wout_b[...]).reshape(tl, nb, D_MODEL)
        o_ref[...] = x3 + g1[None] * mix

        xc_buf[0:hist, :] = xc_buf[rows:rows + hist, :]

    @pl.when(step == n_chunks)
    def _sample_step():
        convp_o[...] = xc_buf[0:hist, :]
        lrup_o[...] = h_lru[...]
        s5rp_o[...] = h_s5r[...]
        s5ip_o[...] = h_s5i[...]

        ns = xs_ref.shape[0]
        sh1 = mod1s[:, 0:D_MODEL]
        sc1 = mod1s[:, D_MODEL:2 * D_MODEL]
        g1 = mod1s[:, 2 * D_MODEL:3 * D_MODEL]

        x = xs_ref[...]
        hn = _rms(x, n1g_ref[...] * (1.0 + sc1)) + sh1
        p = _dot(hn.astype(BF16), win_b[...])
        lru_x = p[:, 0:D_LRU]

        hist_w = (CONV_W - 1) * D_LRU
        convs_o[:, 0:hist_w - D_LRU] = sconv_ref[:, D_LRU:hist_w]
        convs_o[:, hist_w - D_LRU:hist_w] = lru_x
        for j in range(GATE_BLOCKS):
            sl = slice(j * GATE_BLOCK, (j + 1) * GATE_BLOCK)
            conv_j = cb_ref[:, sl] + cw_ref[CONV_W - 1:CONV_W, sl] * lru_x[:, sl]
            for k in range(CONV_W - 1):
                conv_j = conv_j + cw_ref[k:k + 1, sl] * sconv_ref[:, k * D_LRU + j * GATE_BLOCK:
                                                                   k * D_LRU + (j + 1) * GATE_BLOCK]
            a, b = _lru_gate_block(conv_j, wg_ref[j], ba_ref[:, sl], bx_ref[:, sl], lam_ref[:, sl])
            lrus_o[:, sl] = a * slru_ref[:, sl] + b
        lru_out = lrus_o[...] * jax.nn.gelu(p[:, D_LRU:2 * D_LRU])
        ms_buf = m_buf.at[pl.ds(0, ns), :]
        ms_buf[:, 0:D_LRU] = _rms(lru_out, glru_ref[...]).astype(BF16)

        u = p[:, 2 * D_LRU:]
        ub = u.astype(BF16)
        ys_buf = y_buf.at[pl.ds(0, ns), :]
        for j in range(S5_BLOCKS):
            sl = slice(j * S5_BLOCK_LANES, (j + 1) * S5_BLOCK_LANES)
            bu_r, bu_i = _s5_bu_block(ub, wb_ref, j)
            ar = abr_ref[:, sl]
            ai = abi_ref[:, sl]
            h0r = ss5r_ref[:, sl]
            h0i = ss5i_ref[:, sl]
            hr = ar * h0r - ai * h0i + bu_r
            hi = ar * h0i + ai * h0r + bu_i
            s5rs_o[:, sl] = hr
            s5is_o[:, sl] = hi
            ys_buf[:, j * S5_BLOCK_CH:(j + 1) * S5_BLOCK_CH] = _s5_y_block(hr, hi, wc_ref, j)
        s5_out = _s5_glu(ys_buf[...], u, d_ref[...], wglu_b)
        ms_buf[:, D_LRU:] = _rms(s5_out, gs5_ref[...]).astype(BF16)
        xs_o[...] = x + g1 * _dot(ms_buf[...], wout_b[...])


def _const_spec(shape):
    nd = len(shape)
    return pl.BlockSpec(shape, lambda i, _nd=nd: (0,) * _nd, pipeline_mode=pl.Buffered(1))


def _out_const_spec(shape):
    nd = len(shape)
    return pl.BlockSpec(shape, lambda i, _nd=nd: (0,) * _nd)


_HBM = pl.BlockSpec(memory_space=pl.ANY)


def _mixer(x_prompt, c_prompt, c_sample, ada_w, ada_b, x_s, sconv, slru, ss5r, ss5i, w):
    nb, t_len, _ = x_prompt.shape
    ns = x_s.shape[0]
    tl = TIME_CHUNK
    n_chunks = t_len // tl
    rows = tl * nb
    hist = (CONV_W - 1) * nb
    half_mod = (N_MOD // 2) * D_MODEL

    inputs = [
        (x_prompt, _HBM), (c_prompt, None), (c_sample, None), (ada_w, _HBM), (ada_b, None),
        (x_s, None), (sconv, None), (slru, None), (ss5r, None), (ss5i, None),
        (w["n1g"], None), (w["win"], _HBM), (w["cw"], None), (w["cb"], None), (w["wg"], None),
        (w["ba"], None), (w["bx"], None), (w["lam"], None),
        (w["abr"], None), (w["abi"], None), (w["wb"], None), (w["wc"], None), (w["d"], None),
        (w["wglu"], _HBM), (w["glru"], None), (w["gs5"], None), (w["wout"], _HBM),
    ]
    in_specs = [spec if spec is not None else _const_spec(a.shape) for a, spec in inputs]

    out_shapes = [
        ((t_len, nb, D_MODEL), pl.BlockSpec((tl, nb, D_MODEL), lambda i: (jnp.minimum(i, n_chunks - 1), 0, 0))),
        ((nb, half_mod), None), ((ns, half_mod), None),
        ((hist, D_LRU), None), ((nb, D_LRU), None), ((nb, S5_LANES), None), ((nb, S5_LANES), None),
        ((ns, D_MODEL), None), ((ns, (CONV_W - 1) * D_LRU), None), ((ns, D_LRU), None),
        ((ns, S5_LANES), None), ((ns, S5_LANES), None),
    ]
    out_specs = [spec if spec is not None else _out_const_spec(s) for s, spec in out_shapes]
    out_shape = [jax.ShapeDtypeStruct(s, F32) for s, _ in out_shapes]

    scratch = [
        pltpu.VMEM((2, tl, nb, D_MODEL), F32),
        pltpu.SemaphoreType.DMA((2, nb)),
        pltpu.VMEM((STAGE_DEPTH, STAGE_ROWS, STAGE_COLS), F32),
        pltpu.SemaphoreType.DMA((STAGE_DEPTH,)),
        pltpu.VMEM((D_MODEL, D_IN), BF16),
        pltpu.VMEM((D_S5, 2 * D_S5), BF16),
        pltpu.VMEM((D_LRU + D_S5, D_MODEL), BF16),
        pltpu.VMEM((nb, half_mod), F32),
        pltpu.VMEM((ns, half_mod), F32),
        pltpu.VMEM((hist + rows, D_LRU), F32),
        pltpu.VMEM((nb, D_LRU), F32),
        pltpu.VMEM((nb, S5_LANES), F32),
        pltpu.VMEM((nb, S5_LANES), F32),
        pltpu.VMEM((rows, D_IN), F32),
        pltpu.VMEM((2, rows, S5_BLOCK_LANES), F32),
        pltpu.VMEM((2, rows, S5_BLOCK_LANES), F32),
        pltpu.VMEM((rows, D_LRU), F32),
        pltpu.VMEM((rows, D_LRU), F32),
        pltpu.VMEM((rows, D_S5), F32),
        pltpu.VMEM((rows, D_LRU + D_S5), BF16),
    ]
    return pl.pallas_call(
        functools.partial(_mixer_kernel, n_chunks),
        grid=(n_chunks + 1,),
        in_specs=in_specs,
        out_specs=out_specs,
        out_shape=out_shape,
        scratch_shapes=scratch,
        compiler_params=pltpu.CompilerParams(
            dimension_semantics=("arbitrary",), vmem_limit_bytes=VMEM_LIMIT_BYTES),
        name="mixer",
    )(*[a for a, _ in inputs])


def _ffn_rows(x3, mod_ref, n2g_ref, wgate_b, wup_b, wdown_b, gf_ref, act_ref):
    groups, mrows, _ = x3.shape
    rows = groups * mrows
    sh2 = mod_ref[:, 0:D_MODEL]
    sc2 = mod_ref[:, D_MODEL:2 * D_MODEL]
    g2 = mod_ref[:, 2 * D_MODEL:3 * D_MODEL]

    hn = _rms(x3, (n2g_ref[...] * (1.0 + sc2))[None]) + sh2[None]
    hb = hn.reshape(rows, D_MODEL).astype(BF16)
    for c in range(FFN_CHUNKS):
        g = _dot(hb, wgate_b[c])
        up = _dot(hb, wup_b[c])
        act_ref[:, c * FFN_CHUNK:(c + 1) * FFN_CHUNK] = (g * jax.nn.sigmoid(g) * up).astype(BF16)
    ffn = _dot(act_ref[...], wdown_b[...]).reshape(groups, mrows, D_MODEL)
    return _rms(x3 + g2[None] * ffn, gf_ref[...])


def _ffn_kernel(
        n_chunks,
        x_ref, mod2p_ref, mod2s_ref, xs_ref, n2g_ref, wgate_hbm, wup_hbm, wdown_hbm, gf_ref,
        y_hbm, ys_o,
        stage_a, stage_b, stage_a_sems, stage_b_sems, wgate_b, wup_b, wdown_b, act_buf, y_buf, y_sems):
    step = pl.program_id(0)
    slot = step % 2

    @pl.when(step == 0)
    def _prologue():
        def wide_piece(src_hbm, dst, rb):
            rows = slice(rb * FFN_STAGE_ROWS, (rb + 1) * FFN_STAGE_ROWS)

            def consume(view):
                for c in range(FFN_CHUNKS):
                    dst[c, rows, :] = view[:, c * FFN_CHUNK:(c + 1) * FFN_CHUNK].astype(BF16)
            return src_hbm.at[pl.ds(rb * FFN_STAGE_ROWS, FFN_STAGE_ROWS), :], consume

        def row_piece(c):
            def consume(view):
                wdown_b[c * FFN_CHUNK:(c + 1) * FFN_CHUNK, :] = view[...].astype(BF16)
            return wdown_hbm.at[pl.ds(c * FFN_CHUNK, FFN_CHUNK), :], consume

        pieces = []
        for rb in range(D_MODEL // FFN_STAGE_ROWS):
            pieces.append(wide_piece(wgate_hbm, wgate_b, rb))
            pieces.append(wide_piece(wup_hbm, wup_b, rb))
        _stream_pieces(pieces, stage_a, stage_a_sems)
        _stream_pieces([row_piece(c) for c in range(FFN_CHUNKS)], stage_b, stage_b_sems)

    @pl.when(step < n_chunks)
    def _prompt_step():
        @pl.when(step >= 2)
        def _():
            for c in _batch_major_copies(y_buf, slot, y_hbm, y_sems, step - 2, to_hbm=True):
                c.wait()

        tl, nb, _ = x_ref.shape
        sub = FFN_SUB_ROWS // nb
        for h in range(tl // sub):
            y_buf[slot, pl.ds(h * sub, sub)] = _ffn_rows(
                x_ref[h * sub:(h + 1) * sub], mod2p_ref, n2g_ref, wgate_b, wup_b, wdown_b, gf_ref,
                act_buf.at[pl.ds(h * FFN_SUB_ROWS, FFN_SUB_ROWS), :])
        for c in _batch_major_copies(y_buf, slot, y_hbm, y_sems, step, to_hbm=True):
            c.start()

    @pl.when(step == n_chunks)
    def _sample_step():
        ns = xs_ref.shape[0]
        ys_o[...] = _ffn_rows(xs_ref[...][None], mod2s_ref, n2g_ref, wgate_b, wup_b, wdown_b, gf_ref,
                              act_buf.at[pl.ds(0, ns), :])[0]
        for s in range(max(n_chunks - 2, 0), n_chunks):
            for c in _batch_major_copies(y_buf, s % 2, y_hbm, y_sems, s, to_hbm=True):
                c.wait()


def _ffn(x32_t, mod2p, mod2s, x32_s, w, out_shape_p):
    t_len, nb, _ = x32_t.shape
    ns = x32_s.shape[0]
    tl = FFN_TIME_CHUNK
    n_chunks = t_len // tl
    rows = tl * nb
    inputs = [
        (x32_t, pl.BlockSpec((tl, nb, D_MODEL), lambda i: (jnp.minimum(i, n_chunks - 1), 0, 0))),
        (mod2p, None), (mod2s, None), (x32_s, None), (w["n2g"], None),
        (w["wgate"], _HBM), (w["wup"], _HBM), (w["wdown"], _HBM), (w["gf"], None),
    ]
    in_specs = [spec if spec is not None else _const_spec(a.shape) for a, spec in inputs]
    scratch = [
        pltpu.VMEM((FFN_STAGE_DEPTH, FFN_STAGE_ROWS, D_FF), F32),
        pltpu.VMEM((FFN_STAGE_DEPTH, FFN_CHUNK, D_MODEL), F32),
        pltpu.SemaphoreType.DMA((FFN_STAGE_DEPTH,)),
        pltpu.SemaphoreType.DMA((FFN_STAGE_DEPTH,)),
        pltpu.VMEM((FFN_CHUNKS, D_MODEL, FFN_CHUNK), BF16),
        pltpu.VMEM((FFN_CHUNKS, D_MODEL, FFN_CHUNK), BF16),
        pltpu.VMEM((D_FF, D_MODEL), BF16),
        pltpu.VMEM((rows, D_FF), BF16),
        pltpu.VMEM((2, tl, nb, D_MODEL), F32),
        pltpu.SemaphoreType.DMA((2, nb)),
    ]
    return pl.pallas_call(
        functools.partial(_ffn_kernel, n_chunks),
        grid=(n_chunks + 1,),
        in_specs=in_specs,
        out_specs=[_HBM, _out_const_spec((ns, D_MODEL))],
        out_shape=[jax.ShapeDtypeStruct(out_shape_p, F32), jax.ShapeDtypeStruct((ns, D_MODEL), F32)],
        scratch_shapes=scratch,
        compiler_params=pltpu.CompilerParams(
            dimension_semantics=("arbitrary",), vmem_limit_bytes=VMEM_LIMIT_BYTES),
        name="ffn",
    )(*[a for a, _ in inputs])


def _block_diag(blocks):
    n, r, c = blocks.shape
    eye = jnp.eye(n, dtype=blocks.dtype)
    return (eye[:, None, :, None] * blocks[:, :, None, :]).reshape(n * r, n * c)


def _prep_weights(p, abr, abi, bbr_t, bbi_t):
    w = {}
    row = lambda v: v.reshape(1, -1)
    w["n1g"] = row(p["norm1_g"])
    w["win"] = p["w_in"]
    w["cw"] = p["conv_w"]
    w["cb"] = row(p["conv_b"])
    wa = _block_diag(p["lru_wa"])
    wx = _block_diag(p["lru_wx"])
    w["wg"] = jnp.stack([
        jnp.concatenate([wa[j * GATE_BLOCK:(j + 1) * GATE_BLOCK, j * GATE_BLOCK:(j + 1) * GATE_BLOCK],
                         wx[j * GATE_BLOCK:(j + 1) * GATE_BLOCK, j * GATE_BLOCK:(j + 1) * GATE_BLOCK]], axis=1)
        for j in range(GATE_BLOCKS)]).astype(BF16)
    w["ba"] = row(p["lru_ba"])
    w["bx"] = row(p["lru_bx"])
    w["lam"] = row(p["lru_lambda"])
    w["abr"] = abr
    w["abi"] = abi
    to_blocks = lambda m: m.reshape(S5_GROUP, S5_GROUPS, S5_STATE).transpose(1, 0, 2)
    wbr = _block_diag(to_blocks(bbr_t))
    wbi = _block_diag(to_blocks(bbi_t))
    w["wb"] = jnp.stack([
        jnp.concatenate([wbr[j * S5_BLOCK_CH:(j + 1) * S5_BLOCK_CH, j * S5_BLOCK_LANES:(j + 1) * S5_BLOCK_LANES],
                         wbi[j * S5_BLOCK_CH:(j + 1) * S5_BLOCK_CH, j * S5_BLOCK_LANES:(j + 1) * S5_BLOCK_LANES]],
                        axis=1)
        for j in range(S5_BLOCKS)]).astype(BF16)
    wcr = _block_diag(p["s5_c_re"].transpose(0, 2, 1))
    wci = _block_diag(-p["s5_c_im"].transpose(0, 2, 1))
    w["wc"] = jnp.stack([
        jnp.concatenate([wcr[j * S5_BLOCK_LANES:(j + 1) * S5_BLOCK_LANES, j * S5_BLOCK_CH:(j + 1) * S5_BLOCK_CH],
                         wci[j * S5_BLOCK_LANES:(j + 1) * S5_BLOCK_LANES, j * S5_BLOCK_CH:(j + 1) * S5_BLOCK_CH]],
                        axis=0)
        for j in range(S5_BLOCKS)]).astype(BF16)
    w["d"] = row(p["s5_d"])
    w["wglu"] = p["s5_w_glu"]
    w["glru"] = row(p["g_lru_out"])
    w["gs5"] = row(p["g_s5_out"])
    w["wout"] = p["w_out"]
    w["n2g"] = row(p["norm2_g"])
    w["wgate"] = p["ffn_w_gate"]
    w["wup"] = p["ffn_w_up"]
    w["wdown"] = p["ffn_w_down"]
    w["gf"] = row(p["final_norm_g"])
    return w


def kernel(x_prompt, x_sample, state_conv, state_lru, state_s5_re, state_s5_im, c_prompt, c_sample,
           ada_w, ada_b, norm1_g, w_in, conv_w, conv_b, lru_wa, lru_ba, lru_wx, lru_bx, lru_lambda,
           s5_lambda_re, s5_lambda_im, s5_log_dt, s5_b_re, s5_b_im, s5_c_re, s5_c_im, s5_d, s5_w_glu,
           g_lru_out, g_s5_out, w_out, norm2_g, ffn_w_gate, ffn_w_up, ffn_w_down, final_norm_g):
    depth = ada_w.shape[0]
    assert depth == 1, "single-layer decoder step"
    bp, t_len, _ = x_prompt.shape
    bs, dec_t, _ = x_sample.shape
    assert bp == SUBLANES and dec_t == 1 and t_len % TIME_CHUNK == 0 and t_len % FFN_TIME_CHUNK == 0
    assert (FFN_TIME_CHUNK * SUBLANES) % FFN_SUB_ROWS == 0 and bs <= FFN_SUB_ROWS

    p = dict(norm1_g=norm1_g[0], w_in=w_in[0], conv_w=conv_w[0], conv_b=conv_b[0], lru_wa=lru_wa[0],
             lru_ba=lru_ba[0], lru_wx=lru_wx[0], lru_bx=lru_bx[0], lru_lambda=lru_lambda[0],
             s5_c_re=s5_c_re[0], s5_c_im=s5_c_im[0], s5_d=s5_d[0], s5_w_glu=s5_w_glu[0],
             g_lru_out=g_lru_out[0], g_s5_out=g_s5_out[0], w_out=w_out[0], norm2_g=norm2_g[0],
             ffn_w_gate=ffn_w_gate[0], ffn_w_up=ffn_w_up[0], ffn_w_down=ffn_w_down[0],
             final_norm_g=final_norm_g)

    abr, abi, bbr_t, bbi_t = _s5_prep(s5_lambda_re[0], s5_lambda_im[0], s5_log_dt[0], s5_b_re[0], s5_b_im[0])
    w = _prep_weights(p, abr, abi, bbr_t, bbi_t)

    (x32_t, mod2p, mod2s, conv_p, lru_p, s5r_p, s5i_p,
     x32_s, conv_s, lru_s, s5r_s, s5i_s) = _mixer(
        x_prompt, c_prompt, c_sample, ada_w[0], ada_b[0].reshape(1, -1),
        x_sample.reshape(bs, D_MODEL), state_conv[0].reshape(bs, (CONV_W - 1) * D_LRU), state_lru[0],
        state_s5_re[0].reshape(bs, S5_LANES), state_s5_im[0].reshape(bs, S5_LANES), w)
    y_prompt, y_s = _ffn(x32_t, mod2p, mod2s, x32_s, w, x_prompt.shape)

    conv_prompt = conv_p.reshape(CONV_W - 1, bp, D_LRU).transpose(1, 0, 2)[None]
    s5_shape_p = (1, bp, S5_GROUPS, S5_STATE)
    s5_shape_s = (1, bs, S5_GROUPS, S5_STATE)
    return (y_prompt, y_s.reshape(bs, 1, D_MODEL),
            conv_prompt, lru_p[None], s5r_p.reshape(s5_shape_p), s5i_p.reshape(s5_shape_p),
            conv_s.reshape(1, bs, CONV_W - 1, D_LRU), lru_s[None],
            s5r_s.reshape(s5_shape_s), s5i_s.reshape(s5_shape_s))
```

```python
import functools

import jax
import jax.numpy as jnp
from jax import lax
from jax.experimental import pallas as pl
from jax.experimental.pallas import tpu as pltpu

F32 = jnp.float32
BF16 = jnp.bfloat16
I32 = jnp.int32

D_MODEL = 1024
D_LRU = 512
D_S5 = 512
D_IN = 2 * D_LRU + D_S5
LRU_HEAD_DIM = 64
CONV_W = 4
LRU_C = 8.0
S5_GROUP = 16
S5_GROUPS = D_S5 // S5_GROUP
S5_STATE = 64
S5_LANES = S5_GROUPS * S5_STATE
D_FF = 2816
N_MOD = 6
EPS = 1e-6

SUBLANES = 8
MXU_DIM = 256
VMEM_LIMIT_BYTES = 58 * 1024 * 1024

TIME_CHUNK = 64
GATE_BLOCK = MXU_DIM
GATE_BLOCKS = D_LRU // GATE_BLOCK
S5_BLOCKS = 4
S5_BLOCK_GROUPS = S5_GROUPS // S5_BLOCKS
S5_BLOCK_CH = D_S5 // S5_BLOCKS
S5_BLOCK_LANES = S5_LANES // S5_BLOCKS
FFN_CHUNK = MXU_DIM
FFN_CHUNKS = D_FF // FFN_CHUNK
STAGE_ROWS = 256
STAGE_COLS = D_IN
STAGE_DEPTH = 3
FFN_STAGE_DEPTH = 3
FFN_STAGE_ROWS = 128
FFN_TIME_CHUNK = 128
FFN_SUB_ROWS = 512


def _dot(a, b):
    return jnp.dot(a, b, preferred_element_type=F32)


def _rms(x, g):
    return x * lax.rsqrt(jnp.mean(x * x, axis=-1, keepdims=True) + EPS) * g


def _log_sigmoid(x):
    return jnp.minimum(x, 0.0) - jnp.log1p(jnp.exp(-jnp.abs(x)))


def _log2(n):
    assert n & (n - 1) == 0
    return n.bit_length() - 1


def _same_group(shape, row_group, col_group):
    r = lax.broadcasted_iota(I32, shape, 0) >> _log2(row_group)
    c = lax.broadcasted_iota(I32, shape, 1) >> _log2(col_group)
    return r == c


def _lane_tiled_block_diag(blk, reps, row_group):
    r, w = blk.shape
    n = w * reps
    sel = (lax.broadcasted_iota(I32, (w, n), 1) & (w - 1)) == lax.broadcasted_iota(I32, (w, n), 0)
    tiled = _dot(blk.astype(BF16), jnp.where(sel, 1.0, 0.0).astype(BF16))
    return jnp.where(_same_group((r, n), row_group, w), tiled, 0.0).astype(BF16)


def _sublane_tiled_block_diag(blk, reps, col_group):
    h, n = blk.shape
    tiled = jnp.broadcast_to(blk[None], (reps, h, n)).reshape(reps * h, n)
    return jnp.where(_same_group((reps * h, n), h, col_group), tiled, 0.0).astype(BF16)


def _lru_gate_block(conv_j, wg_j, ba_j, bx_j, lam_j):
    gz = _dot(conv_j.astype(BF16), wg_j)
    r = jax.nn.sigmoid(gz[:, :GATE_BLOCK] + ba_j)
    ig = jax.nn.sigmoid(gz[:, GATE_BLOCK:] + bx_j)
    log_a = (LRU_C * _log_sigmoid(lam_j)) * r
    a = jnp.exp(log_a)
    mult = jnp.sqrt(-jnp.tanh(log_a) * (a * a + 1.0))
    return a, mult * (ig * conv_j)


def _s5_bu_block(ub, wb_ref, j):
    bu = _dot(ub[:, j * S5_BLOCK_CH:(j + 1) * S5_BLOCK_CH], wb_ref[j])
    return bu[:, :S5_BLOCK_LANES], bu[:, S5_BLOCK_LANES:]


def _s5_y_block(hr, hi, wc_ref, j):
    return (_dot(hr.astype(BF16), wc_ref[j, :S5_BLOCK_LANES, :])
            + _dot(hi.astype(BF16), wc_ref[j, S5_BLOCK_LANES:, :]))


def _s5_glu(y, u, d, wglu_ref):
    z = _dot(jax.nn.gelu(y + d * u).astype(BF16), wglu_ref[...])
    return z[:, :D_S5] * jax.nn.sigmoid(z[:, D_S5:])


def _stream_pieces(pieces, stage, sems):
    depth = stage.shape[0]

    def dst(k):
        r, c = pieces[k][0].shape
        return stage.at[k % depth, pl.ds(0, r), pl.ds(0, c)]

    copies = [pltpu.make_async_copy(pieces[k][0], dst(k), sems.at[k % depth]) for k in range(len(pieces))]
    for k in range(min(depth - 1, len(pieces))):
        copies[k].start()
    for k, (_, consume) in enumerate(pieces):
        if k + depth - 1 < len(pieces):
            copies[k + depth - 1].start()
        copies[k].wait()
        consume(dst(k))


def _batch_major_copies(tm_bufs, slot, bm_hbm, sems, chunk, to_hbm):
    _, tl, nb, _ = tm_bufs.shape
    copies = []
    for b in range(nb):
        hbm = bm_hbm.at[b, pl.ds(chunk * tl, tl), :]
        vmem = tm_bufs.at[slot, :, b, :]
        src, dst = (vmem, hbm) if to_hbm else (hbm, vmem)
        copies.append(pltpu.make_async_copy(src, dst, sems.at[slot, b]))
    return copies


def _mixer_kernel(
        n_chunks,
        x_hbm, cp_ref, cs_ref, adaw_hbm, adab_ref,
        xs_ref, sconv_ref, slru_ref, ss5r_ref, ss5i_ref,
        n1g_ref, win_hbm, cw_ref, cb_ref, wa_ref, wx_ref, ba_ref, bx_ref, lam_ref,
        slr_ref, sli_ref, sldt_ref, bret_ref, bimt_ref, cre_ref, cim_ref,
        d_ref, wglu_hbm, glru_ref, gs5_ref, wout_hbm,
        o_ref, mod2p_o, mod2s_o, convp_o, lrup_o, s5rp_o, s5ip_o,
        xs_o, convs_o, lrus_o, s5rs_o, s5is_o,
        x_buf, x_sems, stage, stage_sems, win_b, wglu_b, wout_b, wg_b, wb_b, wc_b, abr_s, abi_s,
        mod1p, mod1s, xc_buf, h_lru, h_s5r, h_s5i, p_buf, hr_buf, hi_buf, a_buf, b_buf, y_buf, m_buf,
        s5s_r, s5s_i):
    step = pl.program_id(0)
    slot = step % 2
    _, tl, nb, _ = x_buf.shape
    rows = tl * nb
    hist = (CONV_W - 1) * nb
    half_mod = (N_MOD // 2) * D_MODEL

    @pl.when(step == 0)
    def _prologue():
        for c in _batch_major_copies(x_buf, 0, x_hbm, x_sems, 0, to_hbm=False):
            c.start()
        xc_buf[0:hist, :] = jnp.zeros((hist, D_LRU), F32)
        h_lru[...] = jnp.zeros_like(h_lru)
        h_s5r[...] = jnp.zeros_like(h_s5r)
        h_s5i[...] = jnp.zeros_like(h_s5i)

        cp = cp_ref[...]
        cs = cs_ref[...]
        silu_p = (cp * jax.nn.sigmoid(cp)).astype(BF16)
        silu_s = (cs * jax.nn.sigmoid(cs)).astype(BF16)
        pieces = []

        def ada_piece(cg, kb):
            cols = slice(cg * STAGE_COLS, (cg + 1) * STAGE_COLS)
            first = cols.start < half_mod
            dst_p, dst_s = (mod1p, mod1s) if first else (mod2p_o, mod2s_o)
            dcols = cols if first else slice(cols.start - half_mod, cols.stop - half_mod)
            krows = slice(kb * STAGE_ROWS, (kb + 1) * STAGE_ROWS)

            def consume(view):
                w = view[...].astype(BF16)
                part_p = _dot(silu_p[:, krows], w)
                part_s = _dot(silu_s[:, krows], w)
                if kb == 0:
                    dst_p[:, dcols] = part_p + adab_ref[:, cols]
                    dst_s[:, dcols] = part_s + adab_ref[:, cols]
                else:
                    dst_p[:, dcols] += part_p
                    dst_s[:, dcols] += part_s
            return adaw_hbm.at[pl.ds(kb * STAGE_ROWS, STAGE_ROWS), pl.ds(cg * STAGE_COLS, STAGE_COLS)], consume

        def cast_piece(src_hbm, dst, rb):
            ncols = dst.shape[1]

            def consume(view):
                dst[rb * STAGE_ROWS:(rb + 1) * STAGE_ROWS, :] = view[...].astype(BF16)
            return src_hbm.at[pl.ds(rb * STAGE_ROWS, STAGE_ROWS), pl.ds(0, ncols)], consume

        for rb in range(win_b.shape[0] // STAGE_ROWS):
            pieces.append(cast_piece(win_hbm, win_b, rb))
        for cg in range(N_MOD * D_MODEL // STAGE_COLS):
            for kb in range(D_MODEL // STAGE_ROWS):
                pieces.append(ada_piece(cg, kb))
        for src_hbm, dst in ((wglu_hbm, wglu_b), (wout_hbm, wout_b)):
            for rb in range(dst.shape[0] // STAGE_ROWS):
                pieces.append(cast_piece(src_hbm, dst, rb))
        _stream_pieces(pieces, stage, stage_sems)

        for j in range(GATE_BLOCKS):
            rs = slice(j * GATE_BLOCK, (j + 1) * GATE_BLOCK)
            reps = GATE_BLOCK // LRU_HEAD_DIM
            wg_b[j, :, :GATE_BLOCK] = _lane_tiled_block_diag(wa_ref[rs, :], reps, LRU_HEAD_DIM)
            wg_b[j, :, GATE_BLOCK:] = _lane_tiled_block_diag(wx_ref[rs, :], reps, LRU_HEAD_DIM)

        lr = slr_ref[...]
        li = sli_ref[...]
        dt = jnp.exp(sldt_ref[...])
        mag = jnp.exp(lr * dt)
        ab_r = mag * jnp.cos(li * dt)
        ab_i = mag * jnp.sin(li * dt)
        den = lr * lr + li * li
        fr = ((ab_r - 1.0) * lr + ab_i * li) / den
        fi = (ab_i * lr - (ab_r - 1.0) * li) / den
        abr_s[...] = ab_r
        abi_s[...] = ab_i
        br = bret_ref[...]
        bi = bimt_ref[...]
        bb_r = fr * br - fi * bi
        bb_i = fr * bi + fi * br
        for j in range(S5_BLOCKS):
            sl = slice(j * S5_BLOCK_LANES, (j + 1) * S5_BLOCK_LANES)
            wb_b[j, :, :S5_BLOCK_LANES] = _sublane_tiled_block_diag(bb_r[:, sl], S5_BLOCK_GROUPS, S5_STATE)
            wb_b[j, :, S5_BLOCK_LANES:] = _sublane_tiled_block_diag(bb_i[:, sl], S5_BLOCK_GROUPS, S5_STATE)
            cs_rows = slice(j * S5_BLOCK_CH, (j + 1) * S5_BLOCK_CH)
            ct_r = _lane_tiled_block_diag(cre_ref[cs_rows, :], S5_BLOCK_GROUPS, S5_GROUP)
            ct_i = _lane_tiled_block_diag(-cim_ref[cs_rows, :], S5_BLOCK_GROUPS, S5_GROUP)
            wc_b[j, :S5_BLOCK_LANES, :] = ct_r.astype(F32).T.astype(BF16)
            wc_b[j, S5_BLOCK_LANES:, :] = ct_i.astype(F32).T.astype(BF16)

    @pl.when(step < n_chunks)
    def _prompt_step():
        @pl.when(step + 1 < n_chunks)
        def _():
            for c in _batch_major_copies(x_buf, 1 - slot, x_hbm, x_sems, step + 1, to_hbm=False):
                c.start()

        for c in _batch_major_copies(x_buf, slot, x_hbm, x_sems, step, to_hbm=False):
            c.wait()

        sh1 = mod1p[:, 0:D_MODEL]
        sc1 = mod1p[:, D_MODEL:2 * D_MODEL]
        g1 = mod1p[:, 2 * D_MODEL:3 * D_MODEL]

        x3 = x_buf[slot]
        hn = _rms(x3, (n1g_ref[...] * (1.0 + sc1))[None]) + sh1[None]
        p_buf[...] = _dot(hn.reshape(rows, D_MODEL).astype(BF16), win_b[...])

        xc_buf[hist:hist + rows, :] = p_buf[:, 0:D_LRU]
        for j in range(GATE_BLOCKS):
            sl = slice(j * GATE_BLOCK, (j + 1) * GATE_BLOCK)
            conv_j = cb_ref[:, sl]
            for k in range(CONV_W):
                conv_j = conv_j + cw_ref[k:k + 1, sl] * xc_buf[k * nb:k * nb + rows, sl]
            a, b = _lru_gate_block(conv_j, wg_b[j], ba_ref[:, sl], bx_ref[:, sl], lam_ref[:, sl])
            a_buf[:, sl] = a
            b_buf[:, sl] = b

        h = h_lru[...]
        for t in range(tl):
            h = a_buf[t * nb:(t + 1) * nb, :] * h + b_buf[t * nb:(t + 1) * nb, :]
            b_buf[t * nb:(t + 1) * nb, :] = h
        h_lru[...] = h

        lru_out = b_buf[...] * jax.nn.gelu(p_buf[:, D_LRU:2 * D_LRU])
        m_buf[:, 0:D_LRU] = _rms(lru_out, glru_ref[...]).astype(BF16)

        u = p_buf[:, 2 * D_LRU:]
        ub = u.astype(BF16)
        for j in range(S5_BLOCKS):
            sl = slice(j * S5_BLOCK_LANES, (j + 1) * S5_BLOCK_LANES)
            hr_j = hr_buf.at[j % 2]
            hi_j = hi_buf.at[j % 2]
            bu_r, bu_i = _s5_bu_block(ub, wb_b, j)
            hr_j[...] = bu_r
            hi_j[...] = bu_i
            ar = jnp.broadcast_to(abr_s[:, sl], (nb, S5_BLOCK_LANES))
            ai = jnp.broadcast_to(abi_s[:, sl], (nb, S5_BLOCK_LANES))
            hr = h_s5r[:, sl]
            hi = h_s5i[:, sl]
            for t in range(tl):
                rs = slice(t * nb, (t + 1) * nb)
                hr, hi = ar * hr - ai * hi + hr_j[rs, :], ar * hi + ai * hr + hi_j[rs, :]
                hr_j[rs, :] = hr
                hi_j[rs, :] = hi
            h_s5r[:, sl] = hr
            h_s5i[:, sl] = hi
            y_buf[:, j * S5_BLOCK_CH:(j + 1) * S5_BLOCK_CH] = _s5_y_block(hr_j[...], hi_j[...], wc_b, j)

        s5_out = _s5_glu(y_buf[...], u, d_ref[...], wglu_b)
        m_buf[:, D_LRU:] = _rms(s5_out, gs5_ref[...]).astype(BF16)

        mix = _dot(m_buf[...], wout_b[...]).reshape(tl, nb, D_MODEL)
        o_ref[...] = x3 + g1[None] * mix

        xc_buf[0:hist, :] = xc_buf[rows:rows + hist, :]

    @pl.when(step == n_chunks)
    def _sample_step():
        for k in range(CONV_W - 1):
            convp_o[:, k, :] = xc_buf[k * nb:(k + 1) * nb, :]
        lrup_o[...] = h_lru[...]
        s5rp_o[...] = h_s5r[...].reshape(s5rp_o.shape)
        s5ip_o[...] = h_s5i[...].reshape(s5ip_o.shape)

        ns = xs_ref.shape[0]
        sh1 = mod1s[:, 0:D_MODEL]
        sc1 = mod1s[:, D_MODEL:2 * D_MODEL]
        g1 = mod1s[:, 2 * D_MODEL:3 * D_MODEL]

        x = xs_ref[:, 0, :]
        hn = _rms(x, n1g_ref[...] * (1.0 + sc1)) + sh1
        p = _dot(hn.astype(BF16), win_b[...])
        lru_x = p[:, 0:D_LRU]

        for k in range(CONV_W - 2):
            convs_o[:, k, :] = sconv_ref[:, k + 1, :]
        convs_o[:, CONV_W - 2, :] = lru_x
        for j in range(GATE_BLOCKS):
            sl = slice(j * GATE_BLOCK, (j + 1) * GATE_BLOCK)
            conv_j = cb_ref[:, sl] + cw_ref[CONV_W - 1:CONV_W, sl] * lru_x[:, sl]
            for k in range(CONV_W - 1):
                conv_j = conv_j + cw_ref[k:k + 1, sl] * sconv_ref[:, k, sl]
            a, b = _lru_gate_block(conv_j, wg_b[j], ba_ref[:, sl], bx_ref[:, sl], lam_ref[:, sl])
            lrus_o[:, sl] = a * slru_ref[:, sl] + b
        lru_out = lrus_o[...] * jax.nn.gelu(p[:, D_LRU:2 * D_LRU])
        ms_buf = m_buf.at[pl.ds(0, ns), :]
        ms_buf[:, 0:D_LRU] = _rms(lru_out, glru_ref[...]).astype(BF16)

        u = p[:, 2 * D_LRU:]
        ub = u.astype(BF16)
        ys_buf = y_buf.at[pl.ds(0, ns), :]
        h0r = ss5r_ref[...].reshape(ns, S5_LANES)
        h0i = ss5i_ref[...].reshape(ns, S5_LANES)
        for j in range(S5_BLOCKS):
            sl = slice(j * S5_BLOCK_LANES, (j + 1) * S5_BLOCK_LANES)
            bu_r, bu_i = _s5_bu_block(ub, wb_b, j)
            ar = abr_s[:, sl]
            ai = abi_s[:, sl]
            hr = ar * h0r[:, sl] - ai * h0i[:, sl] + bu_r
            hi = ar * h0i[:, sl] + ai * h0r[:, sl] + bu_i
            s5s_r[:, sl] = hr
            s5s_i[:, sl] = hi
            ys_buf[:, j * S5_BLOCK_CH:(j + 1) * S5_BLOCK_CH] = _s5_y_block(hr, hi, wc_b, j)
        s5rs_o[...] = s5s_r[...].reshape(s5rs_o.shape)
        s5is_o[...] = s5s_i[...].reshape(s5is_o.shape)
        s5_out = _s5_glu(ys_buf[...], u, d_ref[...], wglu_b)
        ms_buf[:, D_LRU:] = _rms(s5_out, gs5_ref[...]).astype(BF16)
        xs_o[...] = x + g1 * _dot(ms_buf[...], wout_b[...])


def _const_spec(shape):
    nd = len(shape)
    return pl.BlockSpec(shape, lambda i, _nd=nd: (0,) * _nd, pipeline_mode=pl.Buffered(1))


def _out_const_spec(shape):
    nd = len(shape)
    return pl.BlockSpec(shape, lambda i, _nd=nd: (0,) * _nd)


_HBM = pl.BlockSpec(memory_space=pl.ANY)


def _mixer(x_prompt, c_prompt, c_sample, ada_w, ada_b, x_s, sconv, slru, ss5r, ss5i, w):
    nb, t_len, _ = x_prompt.shape
    ns = x_s.shape[0]
    tl = TIME_CHUNK
    n_chunks = t_len // tl
    rows = tl * nb
    hist = (CONV_W - 1) * nb
    half_mod = (N_MOD // 2) * D_MODEL

    inputs = [
        (x_prompt, _HBM), (c_prompt, None), (c_sample, None), (ada_w, _HBM), (ada_b, None),
        (x_s, None), (sconv, None), (slru, None), (ss5r, None), (ss5i, None),
        (w["n1g"], None), (w["win"], _HBM), (w["cw"], None), (w["cb"], None), (w["wa"], None), (w["wx"], None),
        (w["ba"], None), (w["bx"], None), (w["lam"], None),
        (w["slr"], None), (w["sli"], None), (w["sldt"], None), (w["bret"], None), (w["bimt"], None),
        (w["cre"], None), (w["cim"], None),
        (w["d"], None), (w["wglu"], _HBM), (w["glru"], None), (w["gs5"], None), (w["wout"], _HBM),
    ]
    in_specs = [spec if spec is not None else _const_spec(a.shape) for a, spec in inputs]

    out_shapes = [
        ((t_len, nb, D_MODEL), pl.BlockSpec((tl, nb, D_MODEL), lambda i: (jnp.minimum(i, n_chunks - 1), 0, 0))),
        ((nb, half_mod), None), ((ns, half_mod), None),
        ((nb, CONV_W - 1, D_LRU), None), ((nb, D_LRU), None),
        ((nb, S5_GROUPS, S5_STATE), None), ((nb, S5_GROUPS, S5_STATE), None),
        ((ns, D_MODEL), None), ((ns, CONV_W - 1, D_LRU), None), ((ns, D_LRU), None),
        ((ns, S5_GROUPS, S5_STATE), None), ((ns, S5_GROUPS, S5_STATE), None),
    ]
    out_specs = [spec if spec is not None else _out_const_spec(s) for s, spec in out_shapes]
    out_shape = [jax.ShapeDtypeStruct(s, F32) for s, _ in out_shapes]

    scratch = [
        pltpu.VMEM((2, tl, nb, D_MODEL), F32),
        pltpu.SemaphoreType.DMA((2, nb)),
        pltpu.VMEM((STAGE_DEPTH, STAGE_ROWS, STAGE_COLS), F32),
        pltpu.SemaphoreType.DMA((STAGE_DEPTH,)),
        pltpu.VMEM((D_MODEL, D_IN), BF16),
        pltpu.VMEM((D_S5, 2 * D_S5), BF16),
        pltpu.VMEM((D_LRU + D_S5, D_MODEL), BF16),
        pltpu.VMEM((GATE_BLOCKS, GATE_BLOCK, 2 * GATE_BLOCK), BF16),
        pltpu.VMEM((S5_BLOCKS, S5_BLOCK_CH, 2 * S5_BLOCK_LANES), BF16),
        pltpu.VMEM((S5_BLOCKS, 2 * S5_BLOCK_LANES, S5_BLOCK_CH), BF16),
        pltpu.VMEM((1, S5_LANES), F32),
        pltpu.VMEM((1, S5_LANES), F32),
        pltpu.VMEM((nb, half_mod), F32),
        pltpu.VMEM((ns, half_mod), F32),
        pltpu.VMEM((hist + rows, D_LRU), F32),
        pltpu.VMEM((nb, D_LRU), F32),
        pltpu.VMEM((nb, S5_LANES), F32),
        pltpu.VMEM((nb, S5_LANES), F32),
        pltpu.VMEM((rows, D_IN), F32),
        pltpu.VMEM((2, rows, S5_BLOCK_LANES), F32),
        pltpu.VMEM((2, rows, S5_BLOCK_LANES), F32),
        pltpu.VMEM((rows, D_LRU), F32),
        pltpu.VMEM((rows, D_LRU), F32),
        pltpu.VMEM((rows, D_S5), F32),
        pltpu.VMEM((rows, D_LRU + D_S5), BF16),
        pltpu.VMEM((ns, S5_LANES), F32),
        pltpu.VMEM((ns, S5_LANES), F32),
    ]
    return pl.pallas_call(
        functools.partial(_mixer_kernel, n_chunks),
        grid=(n_chunks + 1,),
        in_specs=in_specs,
        out_specs=out_specs,
        out_shape=out_shape,
        scratch_shapes=scratch,
        compiler_params=pltpu.CompilerParams(
            dimension_semantics=("arbitrary",), vmem_limit_bytes=VMEM_LIMIT_BYTES),
        name="mixer",
    )(*[a for a, _ in inputs])


def _ffn_rows(x3, mod_ref, n2g_ref, wgate_b, wup_b, wdown_b, gf_ref, act_ref):
    groups, mrows, _ = x3.shape
    rows = groups * mrows
    sh2 = mod_ref[:, 0:D_MODEL]
    sc2 = mod_ref[:, D_MODEL:2 * D_MODEL]
    g2 = mod_ref[:, 2 * D_MODEL:3 * D_MODEL]

    hn = _rms(x3, (n2g_ref[...] * (1.0 + sc2))[None]) + sh2[None]
    hb = hn.reshape(rows, D_MODEL).astype(BF16)
    for c in range(FFN_CHUNKS):
        g = _dot(hb, wgate_b[c])
        up = _dot(hb, wup_b[c])
        act_ref[:, c * FFN_CHUNK:(c + 1) * FFN_CHUNK] = (g * jax.nn.sigmoid(g) * up).astype(BF16)
    ffn = _dot(act_ref[...], wdown_b[...]).reshape(groups, mrows, D_MODEL)
    return _rms(x3 + g2[None] * ffn, gf_ref[...])


def _ffn_kernel(
        n_chunks,
        x_ref, mod2p_ref, mod2s_ref, xs_ref, n2g_ref, wgate_hbm, wup_hbm, wdown_hbm, gf_ref,
        y_hbm, ys_o,
        stage_a, stage_b, stage_a_sems, stage_b_sems, wgate_b, wup_b, wdown_b, act_buf, y_buf, y_sems):
    step = pl.program_id(0)
    slot = step % 2

    @pl.when(step == 0)
    def _prologue():
        def wide_piece(src_hbm, dst, rb):
            rows = slice(rb * FFN_STAGE_ROWS, (rb + 1) * FFN_STAGE_ROWS)

            def consume(view):
                for c in range(FFN_CHUNKS):
                    dst[c, rows, :] = view[:, c * FFN_CHUNK:(c + 1) * FFN_CHUNK].astype(BF16)
            return src_hbm.at[pl.ds(rb * FFN_STAGE_ROWS, FFN_STAGE_ROWS), :], consume

        def row_piece(c):
            def consume(view):
                wdown_b[c * FFN_CHUNK:(c + 1) * FFN_CHUNK, :] = view[...].astype(BF16)
            return wdown_hbm.at[pl.ds(c * FFN_CHUNK, FFN_CHUNK), :], consume

        pieces = []
        for rb in range(D_MODEL // FFN_STAGE_ROWS):
            pieces.append(wide_piece(wgate_hbm, wgate_b, rb))
            pieces.append(wide_piece(wup_hbm, wup_b, rb))
        _stream_pieces(pieces, stage_a, stage_a_sems)
        _stream_pieces([row_piece(c) for c in range(FFN_CHUNKS)], stage_b, stage_b_sems)

    @pl.when(step < n_chunks)
    def _prompt_step():
        @pl.when(step >= 2)
        def _():
            for c in _batch_major_copies(y_buf, slot, y_hbm, y_sems, step - 2, to_hbm=True):
                c.wait()

        tl, nb, _ = x_ref.shape
        sub = FFN_SUB_ROWS // nb
        for h in range(tl // sub):
            y_buf[slot, pl.ds(h * sub, sub)] = _ffn_rows(
                x_ref[h * sub:(h + 1) * sub], mod2p_ref, n2g_ref, wgate_b, wup_b, wdown_b, gf_ref,
                act_buf.at[pl.ds(h * FFN_SUB_ROWS, FFN_SUB_ROWS), :])
        for c in _batch_major_copies(y_buf, slot, y_hbm, y_sems, step, to_hbm=True):
            c.start()

    @pl.when(step == n_chunks)
    def _sample_step():
        ns = xs_ref.shape[0]
        ys_o[:, 0, :] = _ffn_rows(xs_ref[...][None], mod2s_ref, n2g_ref, wgate_b, wup_b, wdown_b, gf_ref,
                                  act_buf.at[pl.ds(0, ns), :])[0]
        for s in range(max(n_chunks - 2, 0), n_chunks):
            for c in _batch_major_copies(y_buf, s % 2, y_hbm, y_sems, s, to_hbm=True):
                c.wait()


def _ffn(x32_t, mod2p, mod2s, x32_s, w, out_shape_p, out_shape_s):
    t_len, nb, _ = x32_t.shape
    tl = FFN_TIME_CHUNK
    n_chunks = t_len // tl
    rows = tl * nb
    inputs = [
        (x32_t, pl.BlockSpec((tl, nb, D_MODEL), lambda i: (jnp.minimum(i, n_chunks - 1), 0, 0))),
        (mod2p, None), (mod2s, None), (x32_s, None), (w["n2g"], None),
        (w["wgate"], _HBM), (w["wup"], _HBM), (w["wdown"], _HBM), (w["gf"], None),
    ]
    in_specs = [spec if spec is not None else _const_spec(a.shape) for a, spec in inputs]
    scratch = [
        pltpu.VMEM((FFN_STAGE_DEPTH, FFN_STAGE_ROWS, D_FF), F32),
        pltpu.VMEM((FFN_STAGE_DEPTH, FFN_CHUNK, D_MODEL), F32),
        pltpu.SemaphoreType.DMA((FFN_STAGE_DEPTH,)),
        pltpu.SemaphoreType.DMA((FFN_STAGE_DEPTH,)),
        pltpu.VMEM((FFN_CHUNKS, D_MODEL, FFN_CHUNK), BF16),
        pltpu.VMEM((FFN_CHUNKS, D_MODEL, FFN_CHUNK), BF16),
        pltpu.VMEM((D_FF, D_MODEL), BF16),
        pltpu.VMEM((rows, D_FF), BF16),
        pltpu.VMEM((2, tl, nb, D_MODEL), F32),
        pltpu.SemaphoreType.DMA((2, nb)),
    ]
    return pl.pallas_call(
        functools.partial(_ffn_kernel, n_chunks),
        grid=(n_chunks + 1,),
        in_specs=in_specs,
        out_specs=[_HBM, _out_const_spec(out_shape_s)],
        out_shape=[jax.ShapeDtypeStruct(out_shape_p, F32), jax.ShapeDtypeStruct(out_shape_s, F32)],
        scratch_shapes=scratch,
        compiler_params=pltpu.CompilerParams(
            dimension_semantics=("arbitrary",), vmem_limit_bytes=VMEM_LIMIT_BYTES),
        name="ffn",
    )(*[a for a, _ in inputs])


def kernel(x_prompt, x_sample, state_conv, state_lru, state_s5_re, state_s5_im, c_prompt, c_sample,
           ada_w, ada_b, norm1_g, w_in, conv_w, conv_b, lru_wa, lru_ba, lru_wx, lru_bx, lru_lambda,
           s5_lambda_re, s5_lambda_im, s5_log_dt, s5_b_re, s5_b_im, s5_c_re, s5_c_im, s5_d, s5_w_glu,
           g_lru_out, g_s5_out, w_out, norm2_g, ffn_w_gate, ffn_w_up, ffn_w_down, final_norm_g):
    depth = ada_w.shape[0]
    assert depth == 1, "single-layer decoder step"
    bp, t_len, _ = x_prompt.shape
    bs, dec_t, _ = x_sample.shape
    assert bp == SUBLANES and dec_t == 1 and t_len % TIME_CHUNK == 0 and t_len % FFN_TIME_CHUNK == 0
    assert (FFN_TIME_CHUNK * SUBLANES) % FFN_SUB_ROWS == 0 and bs <= FFN_SUB_ROWS

    w = dict(
        n1g=norm1_g, win=w_in[0], cw=conv_w[0], cb=conv_b,
        wa=lru_wa.reshape(D_LRU, LRU_HEAD_DIM), wx=lru_wx.reshape(D_LRU, LRU_HEAD_DIM),
        ba=lru_ba, bx=lru_bx, lam=lru_lambda,
        slr=s5_lambda_re.reshape(1, S5_LANES), sli=s5_lambda_im.reshape(1, S5_LANES),
        sldt=jnp.repeat(s5_log_dt[0], S5_STATE).reshape(1, S5_LANES),
        bret=s5_b_re.reshape(S5_LANES, S5_GROUP).T, bimt=s5_b_im.reshape(S5_LANES, S5_GROUP).T,
        cre=s5_c_re.reshape(D_S5, S5_STATE), cim=s5_c_im.reshape(D_S5, S5_STATE),
        d=s5_d, wglu=s5_w_glu[0], glru=g_lru_out, gs5=g_s5_out, wout=w_out[0],
        n2g=norm2_g, wgate=ffn_w_gate[0], wup=ffn_w_up[0], wdown=ffn_w_down[0],
        gf=final_norm_g.reshape(1, D_MODEL),
    )

    (x32_t, mod2p, mod2s, conv_p, lru_p, s5r_p, s5i_p,
     x32_s, conv_s, lru_s, s5r_s, s5i_s) = _mixer(
        x_prompt, c_prompt, c_sample, ada_w[0], ada_b, x_sample, state_conv[0], state_lru[0],
        state_s5_re[0], state_s5_im[0], w)
    y_prompt, y_sample = _ffn(x32_t, mod2p, mod2s, x32_s, w, x_prompt.shape, x_sample.shape)

    return (y_prompt, y_sample, conv_p[None], lru_p[None], s5r_p[None], s5i_p[None],
            conv_s[None], lru_s[None], s5r_s[None], s5i_s[None])
```

```python
import functools

import jax
import jax.numpy as jnp
from jax import lax
from jax.experimental import pallas as pl
from jax.experimental.pallas import tpu as pltpu

F32 = jnp.float32
BF16 = jnp.bfloat16
I32 = jnp.int32

D_MODEL = 1024
D_LRU = 512
D_S5 = 512
D_IN = 2 * D_LRU + D_S5
LRU_HEAD_DIM = 64
CONV_W = 4
LRU_C = 8.0
S5_GROUP = 16
S5_GROUPS = D_S5 // S5_GROUP
S5_STATE = 64
S5_LANES = S5_GROUPS * S5_STATE
D_FF = 2816
N_MOD = 6
EPS = 1e-6

SUBLANES = 8
MXU_DIM = 256
VMEM_LIMIT_BYTES = 58 * 1024 * 1024

TIME_CHUNK = 64
FRONT_SLICES = 2
GATE_BLOCK = MXU_DIM
GATE_BLOCKS = D_LRU // GATE_BLOCK
S5_BLOCKS = 4
S5_BLOCK_GROUPS = S5_GROUPS // S5_BLOCKS
S5_BLOCK_CH = D_S5 // S5_BLOCKS
S5_BLOCK_LANES = S5_LANES // S5_BLOCKS
FFN_CHUNK = MXU_DIM
FFN_CHUNKS = D_FF // FFN_CHUNK
STAGE_ROWS = 256
STAGE_COLS = D_IN
STAGE_DEPTH = 3
FFN_STAGE_DEPTH = 3
FFN_STAGE_ROWS = 128
FFN_TIME_CHUNK = 128
FFN_SUB_ROWS = 512


def _dot(a, b):
    return jnp.dot(a, b, preferred_element_type=F32)


def _rms(x, g):
    return x * lax.rsqrt(jnp.mean(x * x, axis=-1, keepdims=True) + EPS) * g


def _log_sigmoid(x):
    return jnp.minimum(x, 0.0) - jnp.log1p(jnp.exp(-jnp.abs(x)))


def _log2(n):
    assert n & (n - 1) == 0
    return n.bit_length() - 1


def _same_group(shape, row_group, col_group):
    r = lax.broadcasted_iota(I32, shape, 0) >> _log2(row_group)
    c = lax.broadcasted_iota(I32, shape, 1) >> _log2(col_group)
    return r == c


def _lane_tiled_block_diag(blk, reps, row_group):
    r, w = blk.shape
    n = w * reps
    sel = (lax.broadcasted_iota(I32, (w, n), 1) & (w - 1)) == lax.broadcasted_iota(I32, (w, n), 0)
    tiled = _dot(blk.astype(BF16), jnp.where(sel, 1.0, 0.0).astype(BF16))
    return jnp.where(_same_group((r, n), row_group, w), tiled, 0.0).astype(BF16)


def _sublane_tiled_block_diag(blk, reps, col_group):
    h, n = blk.shape
    tiled = jnp.broadcast_to(blk[None], (reps, h, n)).reshape(reps * h, n)
    return jnp.where(_same_group((reps * h, n), h, col_group), tiled, 0.0).astype(BF16)


def _lru_gate_block(conv_j, wg_j, ba_j, bx_j, lam_j):
    gz = _dot(conv_j.astype(BF16), wg_j)
    r = jax.nn.sigmoid(gz[:, :GATE_BLOCK] + ba_j)
    ig = jax.nn.sigmoid(gz[:, GATE_BLOCK:] + bx_j)
    log_a = (LRU_C * _log_sigmoid(lam_j)) * r
    a = jnp.exp(log_a)
    mult = jnp.sqrt(-jnp.tanh(log_a) * (a * a + 1.0))
    return a, mult * (ig * conv_j)


def _s5_bu_block(ub, wb_ref, j):
    bu = _dot(ub[:, j * S5_BLOCK_CH:(j + 1) * S5_BLOCK_CH], wb_ref[j])
    return bu[:, :S5_BLOCK_LANES], bu[:, S5_BLOCK_LANES:]


def _s5_y_block(hr, hi, wc_ref, j):
    return (_dot(hr.astype(BF16), wc_ref[j, :S5_BLOCK_LANES, :])
            + _dot(hi.astype(BF16), wc_ref[j, S5_BLOCK_LANES:, :]))


def _s5_glu(y, u, d, wglu_ref):
    z = _dot(jax.nn.gelu(y + d * u).astype(BF16), wglu_ref[...])
    return z[:, :D_S5] * jax.nn.sigmoid(z[:, D_S5:])


def _stream_pieces(pieces, stage, sems):
    depth = stage.shape[0]

    def dst(k):
        r, c = pieces[k][0].shape
        return stage.at[k % depth, pl.ds(0, r), pl.ds(0, c)]

    copies = [pltpu.make_async_copy(pieces[k][0], dst(k), sems.at[k % depth]) for k in range(len(pieces))]
    for k in range(min(depth - 1, len(pieces))):
        copies[k].start()
    for k, (_, consume) in enumerate(pieces):
        if k + depth - 1 < len(pieces):
            copies[k + depth - 1].start()
        copies[k].wait()
        consume(dst(k))


def _batch_major_copies(tm_bufs, slot, bm_hbm, sems, chunk, to_hbm):
    _, tl, nb, _ = tm_bufs.shape
    copies = []
    for b in range(nb):
        hbm = bm_hbm.at[b, pl.ds(chunk * tl, tl), :]
        vmem = tm_bufs.at[slot, :, b, :]
        src, dst = (vmem, hbm) if to_hbm else (hbm, vmem)
        copies.append(pltpu.make_async_copy(src, dst, sems.at[slot, b]))
    return copies


def _mixer_kernel(
        n_chunks,
        x_hbm, cp_ref, cs_ref, adaw_hbm, adab_ref,
        xs_ref, sconv_ref, slru_ref, ss5r_ref, ss5i_ref,
        n1g_ref, win_hbm, cw_ref, cb_ref, wa_ref, wx_ref, ba_ref, bx_ref, lam_ref,
        slr_ref, sli_ref, sldt_ref, bret_ref, bimt_ref, cre_ref, cim_ref,
        d_ref, wglu_hbm, glru_ref, gs5_ref, wout_hbm,
        o_ref, mod2p_o, mod2s_o, convp_o, lrup_o, s5rp_o, s5ip_o,
        xs_o, convs_o, lrus_o, s5rs_o, s5is_o,
        x_buf, x_sems, stage, stage_sems, win_b, wglu_b, wout_b, wg_b, wb_b, wc_b, abr_s, abi_s,
        mod1p, mod1s, xc_buf, h_lru, h_s5r, h_s5i, p_buf, hr_buf, hi_buf, a_buf, b_buf, y_buf, m_buf,
        s5s_r, s5s_i):
    step = pl.program_id(0)
    slot = step % 2
    _, tl, nb, _ = x_buf.shape
    rows = tl * nb
    hist = (CONV_W - 1) * nb
    half_mod = (N_MOD // 2) * D_MODEL

    @pl.when(step == 0)
    def _prologue():
        for c in _batch_major_copies(x_buf, 0, x_hbm, x_sems, 0, to_hbm=False):
            c.start()
        xc_buf[0:hist, :] = jnp.zeros((hist, D_LRU), F32)
        h_lru[...] = jnp.zeros_like(h_lru)
        h_s5r[...] = jnp.zeros_like(h_s5r)
        h_s5i[...] = jnp.zeros_like(h_s5i)

        cp = cp_ref[...]
        cs = cs_ref[...]
        silu_p = (cp * jax.nn.sigmoid(cp)).astype(BF16)
        silu_s = (cs * jax.nn.sigmoid(cs)).astype(BF16)
        pieces = []

        def ada_piece(cg, kb):
            cols = slice(cg * STAGE_COLS, (cg + 1) * STAGE_COLS)
            first = cols.start < half_mod
            dst_p, dst_s = (mod1p, mod1s) if first else (mod2p_o, mod2s_o)
            dcols = cols if first else slice(cols.start - half_mod, cols.stop - half_mod)
            krows = slice(kb * STAGE_ROWS, (kb + 1) * STAGE_ROWS)

            def consume(view):
                w = view[...].astype(BF16)
                part_p = _dot(silu_p[:, krows], w)
                part_s = _dot(silu_s[:, krows], w)
                if kb == 0:
                    dst_p[:, dcols] = part_p + adab_ref[:, cols]
                    dst_s[:, dcols] = part_s + adab_ref[:, cols]
                else:
                    dst_p[:, dcols] += part_p
                    dst_s[:, dcols] += part_s
            return adaw_hbm.at[pl.ds(kb * STAGE_ROWS, STAGE_ROWS), pl.ds(cg * STAGE_COLS, STAGE_COLS)], consume

        def cast_piece(src_hbm, dst, rb):
            ncols = dst.shape[1]

            def consume(view):
                dst[rb * STAGE_ROWS:(rb + 1) * STAGE_ROWS, :] = view[...].astype(BF16)
            return src_hbm.at[pl.ds(rb * STAGE_ROWS, STAGE_ROWS), pl.ds(0, ncols)], consume

        for rb in range(win_b.shape[0] // STAGE_ROWS):
            pieces.append(cast_piece(win_hbm, win_b, rb))
        for cg in range(N_MOD * D_MODEL // STAGE_COLS):
            for kb in range(D_MODEL // STAGE_ROWS):
                pieces.append(ada_piece(cg, kb))
        for src_hbm, dst in ((wglu_hbm, wglu_b), (wout_hbm, wout_b)):
            for rb in range(dst.shape[0] // STAGE_ROWS):
                pieces.append(cast_piece(src_hbm, dst, rb))
        _stream_pieces(pieces, stage, stage_sems)

        for j in range(GATE_BLOCKS):
            rs = slice(j * GATE_BLOCK, (j + 1) * GATE_BLOCK)
            reps = GATE_BLOCK // LRU_HEAD_DIM
            wg_b[j, :, :GATE_BLOCK] = _lane_tiled_block_diag(wa_ref[rs, :], reps, LRU_HEAD_DIM)
            wg_b[j, :, GATE_BLOCK:] = _lane_tiled_block_diag(wx_ref[rs, :], reps, LRU_HEAD_DIM)

        lr = slr_ref[...]
        li = sli_ref[...]
        dt = jnp.exp(sldt_ref[...])
        mag = jnp.exp(lr * dt)
        ab_r = mag * jnp.cos(li * dt)
        ab_i = mag * jnp.sin(li * dt)
        den = lr * lr + li * li
        fr = ((ab_r - 1.0) * lr + ab_i * li) / den
        fi = (ab_i * lr - (ab_r - 1.0) * li) / den
        abr_s[...] = ab_r
        abi_s[...] = ab_i
        br = bret_ref[...]
        bi = bimt_ref[...]
        bb_r = fr * br - fi * bi
        bb_i = fr * bi + fi * br
        for j in range(S5_BLOCKS):
            sl = slice(j * S5_BLOCK_LANES, (j + 1) * S5_BLOCK_LANES)
            wb_b[j, :, :S5_BLOCK_LANES] = _sublane_tiled_block_diag(bb_r[:, sl], S5_BLOCK_GROUPS, S5_STATE)
            wb_b[j, :, S5_BLOCK_LANES:] = _sublane_tiled_block_diag(bb_i[:, sl], S5_BLOCK_GROUPS, S5_STATE)
            cs_rows = slice(j * S5_BLOCK_CH, (j + 1) * S5_BLOCK_CH)
            ct_r = _lane_tiled_block_diag(cre_ref[cs_rows, :], S5_BLOCK_GROUPS, S5_GROUP)
            ct_i = _lane_tiled_block_diag(-cim_ref[cs_rows, :], S5_BLOCK_GROUPS, S5_GROUP)
            wc_b[j, :S5_BLOCK_LANES, :] = ct_r.astype(F32).T.astype(BF16)
            wc_b[j, S5_BLOCK_LANES:, :] = ct_i.astype(F32).T.astype(BF16)

    @pl.when(step < n_chunks)
    def _prompt_step():
        @pl.when(step + 1 < n_chunks)
        def _():
            for c in _batch_major_copies(x_buf, 1 - slot, x_hbm, x_sems, step + 1, to_hbm=False):
                c.start()

        for c in _batch_major_copies(x_buf, slot, x_hbm, x_sems, step, to_hbm=False):
            c.wait()

        sh1 = mod1p[:, 0:D_MODEL]
        sc1 = mod1p[:, D_MODEL:2 * D_MODEL]
        g1 = mod1p[:, 2 * D_MODEL:3 * D_MODEL]

        scale1 = (n1g_ref[...] * (1.0 + sc1))[None]
        tsub = tl // FRONT_SLICES

        def project(s):
            r0, r1 = s * tsub * nb, (s + 1) * tsub * nb
            xs3 = x_buf[slot, pl.ds(s * tsub, tsub)]
            hn = _rms(xs3, scale1) + sh1[None]
            p_buf[r0:r1, :] = _dot(hn.reshape(r1 - r0, D_MODEL).astype(BF16), win_b[...])
            xc_buf[hist + r0:hist + r1, :] = p_buf[r0:r1, 0:D_LRU]

        def lru_gates(s):
            r0, r1 = s * tsub * nb, (s + 1) * tsub * nb
            for j in range(GATE_BLOCKS):
                sl = slice(j * GATE_BLOCK, (j + 1) * GATE_BLOCK)
                conv_j = cb_ref[:, sl]
                for k in range(CONV_W):
                    conv_j = conv_j + cw_ref[k:k + 1, sl] * xc_buf[k * nb + r0:k * nb + r1, sl]
                a, b = _lru_gate_block(conv_j, wg_b[j], ba_ref[:, sl], bx_ref[:, sl], lam_ref[:, sl])
                a_buf[r0:r1, sl] = a
                b_buf[r0:r1, sl] = b

        def lru_scan():
            h = h_lru[...]
            for t in range(tl):
                h = a_buf[t * nb:(t + 1) * nb, :] * h + b_buf[t * nb:(t + 1) * nb, :]
                b_buf[t * nb:(t + 1) * nb, :] = h
            h_lru[...] = h
            lru_out = b_buf[...] * jax.nn.gelu(p_buf[:, D_LRU:2 * D_LRU])
            m_buf[:, 0:D_LRU] = _rms(lru_out, glru_ref[...]).astype(BF16)

        def s5_input(j):
            ub = p_buf[:, 2 * D_LRU + j * S5_BLOCK_CH:2 * D_LRU + (j + 1) * S5_BLOCK_CH].astype(BF16)
            bu = _dot(ub, wb_b[j])
            hr_buf[j % 2] = bu[:, :S5_BLOCK_LANES]
            hi_buf[j % 2] = bu[:, S5_BLOCK_LANES:]

        def s5_scan(j):
            sl = slice(j * S5_BLOCK_LANES, (j + 1) * S5_BLOCK_LANES)
            hr_j = hr_buf.at[j % 2]
            hi_j = hi_buf.at[j % 2]
            ar = jnp.broadcast_to(abr_s[:, sl], (nb, S5_BLOCK_LANES))
            ai = jnp.broadcast_to(abi_s[:, sl], (nb, S5_BLOCK_LANES))
            hr = h_s5r[:, sl]
            hi = h_s5i[:, sl]
            for t in range(tl):
                rs = slice(t * nb, (t + 1) * nb)
                hr, hi = ar * hr - ai * hi + hr_j[rs, :], ar * hi + ai * hr + hi_j[rs, :]
                hr_j[rs, :] = hr
                hi_j[rs, :] = hi
            h_s5r[:, sl] = hr
            h_s5i[:, sl] = hi

        def s5_readout(j):
            y_buf[:, j * S5_BLOCK_CH:(j + 1) * S5_BLOCK_CH] = _s5_y_block(
                hr_buf[j % 2], hi_buf[j % 2], wc_b, j)

        for s in range(FRONT_SLICES):
            project(s)
        lru_gates(0)
        s5_input(0)
        s5_input(1)
        for s in range(1, FRONT_SLICES):
            lru_gates(s)
        for j in range(S5_BLOCKS):
            s5_scan(j)
            s5_readout(j)
            if j + 2 < S5_BLOCKS:
                s5_input(j + 2)
            if j == 1:
                lru_scan()

        s5_out = _s5_glu(y_buf[...], p_buf[:, 2 * D_LRU:], d_ref[...], wglu_b)
        m_buf[:, D_LRU:] = _rms(s5_out, gs5_ref[...]).astype(BF16)

        mix = _dot(m_buf[...], wout_b[...]).reshape(tl, nb, D_MODEL)
        o_ref[...] = x_buf[slot] + g1[None] * mix

        xc_buf[0:hist, :] = xc_buf[rows:rows + hist, :]

    @pl.when(step == n_chunks)
    def _sample_step():
        for k in range(CONV_W - 1):
            convp_o[:, k, :] = xc_buf[k * nb:(k + 1) * nb, :]
        lrup_o[...] = h_lru[...]
        s5rp_o[...] = h_s5r[...].reshape(s5rp_o.shape)
        s5ip_o[...] = h_s5i[...].reshape(s5ip_o.shape)

        ns = xs_ref.shape[0]
        sh1 = mod1s[:, 0:D_MODEL]
        sc1 = mod1s[:, D_MODEL:2 * D_MODEL]
        g1 = mod1s[:, 2 * D_MODEL:3 * D_MODEL]

        x = xs_ref[:, 0, :]
        hn = _rms(x, n1g_ref[...] * (1.0 + sc1)) + sh1
        p = _dot(hn.astype(BF16), win_b[...])
        lru_x = p[:, 0:D_LRU]

        for k in range(CONV_W - 2):
            convs_o[:, k, :] = sconv_ref[:, k + 1, :]
        convs_o[:, CONV_W - 2, :] = lru_x
        for j in range(GATE_BLOCKS):
            sl = slice(j * GATE_BLOCK, (j + 1) * GATE_BLOCK)
            conv_j = cb_ref[:, sl] + cw_ref[CONV_W - 1:CONV_W, sl] * lru_x[:, sl]
            for k in range(CONV_W - 1):
                conv_j = conv_j + cw_ref[k:k + 1, sl] * sconv_ref[:, k, sl]
            a, b = _lru_gate_block(conv_j, wg_b[j], ba_ref[:, sl], bx_ref[:, sl], lam_ref[:, sl])
            lrus_o[:, sl] = a * slru_ref[:, sl] + b
        lru_out = lrus_o[...] * jax.nn.gelu(p[:, D_LRU:2 * D_LRU])
        ms_buf = m_buf.at[pl.ds(0, ns), :]
        ms_buf[:, 0:D_LRU] = _rms(lru_out, glru_ref[...]).astype(BF16)

        u = p[:, 2 * D_LRU:]
        ub = u.astype(BF16)
        ys_buf = y_buf.at[pl.ds(0, ns), :]
        h0r = ss5r_ref[...].reshape(ns, S5_LANES)
        h0i = ss5i_ref[...].reshape(ns, S5_LANES)
        for j in range(S5_BLOCKS):
            sl = slice(j * S5_BLOCK_LANES, (j + 1) * S5_BLOCK_LANES)
            bu_r, bu_i = _s5_bu_block(ub, wb_b, j)
            ar = abr_s[:, sl]
            ai = abi_s[:, sl]
            hr = ar * h0r[:, sl] - ai * h0i[:, sl] + bu_r
            hi = ar * h0i[:, sl] + ai * h0r[:, sl] + bu_i
            s5s_r[:, sl] = hr
            s5s_i[:, sl] = hi
            ys_buf[:, j * S5_BLOCK_CH:(j + 1) * S5_BLOCK_CH] = _s5_y_block(hr, hi, wc_b, j)
        s5rs_o[...] = s5s_r[...].reshape(s5rs_o.shape)
        s5is_o[...] = s5s_i[...].reshape(s5is_o.shape)
        s5_out = _s5_glu(ys_buf[...], u, d_ref[...], wglu_b)
        ms_buf[:, D_LRU:] = _rms(s5_out, gs5_ref[...]).astype(BF16)
        xs_o[...] = x + g1 * _dot(ms_buf[...], wout_b[...])


def _const_spec(shape):
    nd = len(shape)
    return pl.BlockSpec(shape, lambda i, _nd=nd: (0,) * _nd, pipeline_mode=pl.Buffered(1))


def _out_const_spec(shape):
    nd = len(shape)
    return pl.BlockSpec(shape, lambda i, _nd=nd: (0,) * _nd)


_HBM = pl.BlockSpec(memory_space=pl.ANY)


def _mixer(x_prompt, c_prompt, c_sample, ada_w, ada_b, x_s, sconv, slru, ss5r, ss5i, w):
    nb, t_len, _ = x_prompt.shape
    ns = x_s.shape[0]
    tl = TIME_CHUNK
    n_chunks = t_len // tl
    rows = tl * nb
    hist = (CONV_W - 1) * nb
    half_mod = (N_MOD // 2) * D_MODEL

    inputs = [
        (x_prompt, _HBM), (c_prompt, None), (c_sample, None), (ada_w, _HBM), (ada_b, None),
        (x_s, None), (sconv, None), (slru, None), (ss5r, None), (ss5i, None),
        (w["n1g"], None), (w["win"], _HBM), (w["cw"], None), (w["cb"], None), (w["wa"], None), (w["wx"], None),
        (w["ba"], None), (w["bx"], None), (w["lam"], None),
        (w["slr"], None), (w["sli"], None), (w["sldt"], None), (w["bret"], None), (w["bimt"], None),
        (w["cre"], None), (w["cim"], None),
        (w["d"], None), (w["wglu"], _HBM), (w["glru"], None), (w["gs5"], None), (w["wout"], _HBM),
    ]
    in_specs = [spec if spec is not None else _const_spec(a.shape) for a, spec in inputs]

    out_shapes = [
        ((t_len, nb, D_MODEL), pl.BlockSpec((tl, nb, D_MODEL), lambda i: (jnp.minimum(i, n_chunks - 1), 0, 0))),
        ((nb, half_mod), None), ((ns, half_mod), None),
        ((nb, CONV_W - 1, D_LRU), None), ((nb, D_LRU), None),
        ((nb, S5_GROUPS, S5_STATE), None), ((nb, S5_GROUPS, S5_STATE), None),
        ((ns, D_MODEL), None), ((ns, CONV_W - 1, D_LRU), None), ((ns, D_LRU), None),
        ((ns, S5_GROUPS, S5_STATE), None), ((ns, S5_GROUPS, S5_STATE), None),
    ]
    out_specs = [spec if spec is not None else _out_const_spec(s) for s, spec in out_shapes]
    out_shape = [jax.ShapeDtypeStruct(s, F32) for s, _ in out_shapes]

    scratch = [
        pltpu.VMEM((2, tl, nb, D_MODEL), F32),
        pltpu.SemaphoreType.DMA((2, nb)),
        pltpu.VMEM((STAGE_DEPTH, STAGE_ROWS, STAGE_COLS), F32),
        pltpu.SemaphoreType.DMA((STAGE_DEPTH,)),
        pltpu.VMEM((D_MODEL, D_IN), BF16),
        pltpu.VMEM((D_S5, 2 * D_S5), BF16),
        pltpu.VMEM((D_LRU + D_S5, D_MODEL), BF16),
        pltpu.VMEM((GATE_BLOCKS, GATE_BLOCK, 2 * GATE_BLOCK), BF16),
        pltpu.VMEM((S5_BLOCKS, S5_BLOCK_CH, 2 * S5_BLOCK_LANES), BF16),
        pltpu.VMEM((S5_BLOCKS, 2 * S5_BLOCK_LANES, S5_BLOCK_CH), BF16),
        pltpu.VMEM((1, S5_LANES), F32),
        pltpu.VMEM((1, S5_LANES), F32),
        pltpu.VMEM((nb, half_mod), F32),
        pltpu.VMEM((ns, half_mod), F32),
        pltpu.VMEM((hist + rows, D_LRU), F32),
        pltpu.VMEM((nb, D_LRU), F32),
        pltpu.VMEM((nb, S5_LANES), F32),
        pltpu.VMEM((nb, S5_LANES), F32),
        pltpu.VMEM((rows, D_IN), F32),
        pltpu.VMEM((2, rows, S5_BLOCK_LANES), F32),
        pltpu.VMEM((2, rows, S5_BLOCK_LANES), F32),
        pltpu.VMEM((rows, D_LRU), F32),
        pltpu.VMEM((rows, D_LRU), F32),
        pltpu.VMEM((rows, D_S5), F32),
        pltpu.VMEM((rows, D_LRU + D_S5), BF16),
        pltpu.VMEM((ns, S5_LANES), F32),
        pltpu.VMEM((ns, S5_LANES), F32),
    ]
    return pl.pallas_call(
        functools.partial(_mixer_kernel, n_chunks),
        grid=(n_chunks + 1,),
        in_specs=in_specs,
        out_specs=out_specs,
        out_shape=out_shape,
        scratch_shapes=scratch,
        compiler_params=pltpu.CompilerParams(
            dimension_semantics=("arbitrary",), vmem_limit_bytes=VMEM_LIMIT_BYTES),
        name="mixer",
    )(*[a for a, _ in inputs])


def _ffn_rows(x3, mod_ref, n2g_ref, wgate_b, wup_b, wdown_b, gf_ref, act_ref):
    groups, mrows, _ = x3.shape
    rows = groups * mrows
    sh2 = mod_ref[:, 0:D_MODEL]
    sc2 = mod_ref[:, D_MODEL:2 * D_MODEL]
    g2 = mod_ref[:, 2 * D_MODEL:3 * D_MODEL]

    hn = _rms(x3, (n2g_ref[...] * (1.0 + sc2))[None]) + sh2[None]
    hb = hn.reshape(rows, D_MODEL).astype(BF16)
    for c in range(FFN_CHUNKS):
        g = _dot(hb, wgate_b[c])
        up = _dot(hb, wup_b[c])
        act_ref[:, c * FFN_CHUNK:(c + 1) * FFN_CHUNK] = (g * jax.nn.sigmoid(g) * up).astype(BF16)
    ffn = _dot(act_ref[...], wdown_b[...]).reshape(groups, mrows, D_MODEL)
    return _rms(x3 + g2[None] * ffn, gf_ref[...])


def _ffn_kernel(
        n_chunks,
        x_ref, mod2p_ref, mod2s_ref, xs_ref, n2g_ref, wgate_hbm, wup_hbm, wdown_hbm, gf_ref,
        y_hbm, ys_o,
        stage_a, stage_b, stage_a_sems, stage_b_sems, wgate_b, wup_b, wdown_b, act_buf, y_buf, y_sems):
    step = pl.program_id(0)
    slot = step % 2

    @pl.when(step == 0)
    def _prologue():
        def wide_piece(src_hbm, dst, rb):
            rows = slice(rb * FFN_STAGE_ROWS, (rb + 1) * FFN_STAGE_ROWS)

            def consume(view):
                for c in range(FFN_CHUNKS):
                    dst[c, rows, :] = view[:, c * FFN_CHUNK:(c + 1) * FFN_CHUNK].astype(BF16)
            return src_hbm.at[pl.ds(rb * FFN_STAGE_ROWS, FFN_STAGE_ROWS), :], consume

        def row_piece(c):
            def consume(view):
                wdown_b[c * FFN_CHUNK:(c + 1) * FFN_CHUNK, :] = view[...].astype(BF16)
            return wdown_hbm.at[pl.ds(c * FFN_CHUNK, FFN_CHUNK), :], consume

        pieces = []
        for rb in range(D_MODEL // FFN_STAGE_ROWS):
            pieces.append(wide_piece(wgate_hbm, wgate_b, rb))
            pieces.append(wide_piece(wup_hbm, wup_b, rb))
        _stream_pieces(pieces, stage_a, stage_a_sems)
        _stream_pieces([row_piece(c) for c in range(FFN_CHUNKS)], stage_b, stage_b_sems)

    @pl.when(step < n_chunks)
    def _prompt_step():
        @pl.when(step >= 2)
        def _():
            for c in _batch_major_copies(y_buf, slot, y_hbm, y_sems, step - 2, to_hbm=True):
                c.wait()

        tl, nb, _ = x_ref.shape
        sub = FFN_SUB_ROWS // nb
        for h in range(tl // sub):
            y_buf[slot, pl.ds(h * sub, sub)] = _ffn_rows(
                x_ref[h * sub:(h + 1) * sub], mod2p_ref, n2g_ref, wgate_b, wup_b, wdown_b, gf_ref,
                act_buf.at[pl.ds(h * FFN_SUB_ROWS, FFN_SUB_ROWS), :])
        for c in _batch_major_copies(y_buf, slot, y_hbm, y_sems, step, to_hbm=True):
            c.start()

    @pl.when(step == n_chunks)
    def _sample_step():
        ns = xs_ref.shape[0]
        ys_o[:, 0, :] = _ffn_rows(xs_ref[...][None], mod2s_ref, n2g_ref, wgate_b, wup_b, wdown_b, gf_ref,
                                  act_buf.at[pl.ds(0, ns), :])[0]
        for s in range(max(n_chunks - 2, 0), n_chunks):
            for c in _batch_major_copies(y_buf, s % 2, y_hbm, y_sems, s, to_hbm=True):
                c.wait()


def _ffn(x32_t, mod2p, mod2s, x32_s, w, out_shape_p, out_shape_s):
    t_len, nb, _ = x32_t.shape
    tl = FFN_TIME_CHUNK
    n_chunks = t_len // tl
    rows = tl * nb
    inputs = [
        (x32_t, pl.BlockSpec((tl, nb, D_MODEL), lambda i: (jnp.minimum(i, n_chunks - 1), 0, 0))),
        (mod2p, None), (mod2s, None), (x32_s, None), (w["n2g"], None),
        (w["wgate"], _HBM), (w["wup"], _HBM), (w["wdown"], _HBM), (w["gf"], None),
    ]
    in_specs = [spec if spec is not None else _const_spec(a.shape) for a, spec in inputs]
    scratch = [
        pltpu.VMEM((FFN_STAGE_DEPTH, FFN_STAGE_ROWS, D_FF), F32),
        pltpu.VMEM((FFN_STAGE_DEPTH, FFN_CHUNK, D_MODEL), F32),
        pltpu.SemaphoreType.DMA((FFN_STAGE_DEPTH,)),
        pltpu.SemaphoreType.DMA((FFN_STAGE_DEPTH,)),
        pltpu.VMEM((FFN_CHUNKS, D_MODEL, FFN_CHUNK), BF16),
        pltpu.VMEM((FFN_CHUNKS, D_MODEL, FFN_CHUNK), BF16),
        pltpu.VMEM((D_FF, D_MODEL), BF16),
        pltpu.VMEM((rows, D_FF), BF16),
        pltpu.VMEM((2, tl, nb, D_MODEL), F32),
        pltpu.SemaphoreType.DMA((2, nb)),
    ]
    return pl.pallas_call(
        functools.partial(_ffn_kernel, n_chunks),
        grid=(n_chunks + 1,),
        in_specs=in_specs,
        out_specs=[_HBM, _out_const_spec(out_shape_s)],
        out_shape=[jax.ShapeDtypeStruct(out_shape_p, F32), jax.ShapeDtypeStruct(out_shape_s, F32)],
        scratch_shapes=scratch,
        compiler_params=pltpu.CompilerParams(
            dimension_semantics=("arbitrary",), vmem_limit_bytes=VMEM_LIMIT_BYTES),
        name="ffn",
    )(*[a for a, _ in inputs])


def kernel(x_prompt, x_sample, state_conv, state_lru, state_s5_re, state_s5_im, c_prompt, c_sample,
           ada_w, ada_b, norm1_g, w_in, conv_w, conv_b, lru_wa, lru_ba, lru_wx, lru_bx, lru_lambda,
           s5_lambda_re, s5_lambda_im, s5_log_dt, s5_b_re, s5_b_im, s5_c_re, s5_c_im, s5_d, s5_w_glu,
           g_lru_out, g_s5_out, w_out, norm2_g, ffn_w_gate, ffn_w_up, ffn_w_down, final_norm_g):
    depth = ada_w.shape[0]
    assert depth == 1, "single-layer decoder step"
    bp, t_len, _ = x_prompt.shape
    bs, dec_t, _ = x_sample.shape
    assert bp == SUBLANES and dec_t == 1 and t_len % TIME_CHUNK == 0 and t_len % FFN_TIME_CHUNK == 0
    assert (FFN_TIME_CHUNK * SUBLANES) % FFN_SUB_ROWS == 0 and bs <= FFN_SUB_ROWS

    w = dict(
        n1g=norm1_g, win=w_in[0], cw=conv_w[0], cb=conv_b,
        wa=lru_wa.reshape(D_LRU, LRU_HEAD_DIM), wx=lru_wx.reshape(D_LRU, LRU_HEAD_DIM),
        ba=lru_ba, bx=lru_bx, lam=lru_lambda,
        slr=s5_lambda_re.reshape(1, S5_LANES), sli=s5_lambda_im.reshape(1, S5_LANES),
        sldt=jnp.repeat(s5_log_dt[0], S5_STATE).reshape(1, S5_LANES),
        bret=s5_b_re.reshape(S5_LANES, S5_GROUP).T, bimt=s5_b_im.reshape(S5_LANES, S5_GROUP).T,
        cre=s5_c_re.reshape(D_S5, S5_STATE), cim=s5_c_im.reshape(D_S5, S5_STATE),
        d=s5_d, wglu=s5_w_glu[0], glru=g_lru_out, gs5=g_s5_out, wout=w_out[0],
        n2g=norm2_g, wgate=ffn_w_gate[0], wup=ffn_w_up[0], wdown=ffn_w_down[0],
        gf=final_norm_g.reshape(1, D_MODEL),
    )

    (x32_t, mod2p, mod2s, conv_p, lru_p, s5r_p, s5i_p,
     x32_s, conv_s, lru_s, s5r_s, s5i_s) = _mixer(
        x_prompt, c_prompt, c_sample, ada_w[0], ada_b, x_sample, state_conv[0], state_lru[0],
        state_s5_re[0], state_s5_im[0], w)
    y_prompt, y_sample = _ffn(x32_t, mod2p, mod2s, x32_s, w, x_prompt.shape, x_sample.shape)

    return (y_prompt, y_sample, conv_p[None], lru_p[None], s5r_p[None], s5i_p[None],
            conv_s[None], lru_s[None], s5r_s[None], s5i_s[None])
```

```python
import functools

import jax
import jax.numpy as jnp
from jax import lax
from jax.experimental import pallas as pl
from jax.experimental.pallas import tpu as pltpu

F32 = jnp.float32
BF16 = jnp.bfloat16
I32 = jnp.int32

D_MODEL = 1024
D_LRU = 512
D_S5 = 512
D_IN = 2 * D_LRU + D_S5
LRU_HEAD_DIM = 64
CONV_W = 4
LRU_C = 8.0
S5_GROUP = 16
S5_GROUPS = D_S5 // S5_GROUP
S5_STATE = 64
S5_LANES = S5_GROUPS * S5_STATE
D_FF = 2816
N_MOD = 6
EPS = 1e-6

SUBLANES = 8
MXU_DIM = 256
VMEM_LIMIT_BYTES = 58 * 1024 * 1024

TIME_CHUNK = 64
FRONT_SLICES = 2
GATE_BLOCK = MXU_DIM
GATE_BLOCKS = D_LRU // GATE_BLOCK
S5_BLOCKS = 4
S5_BLOCK_GROUPS = S5_GROUPS // S5_BLOCKS
S5_BLOCK_CH = D_S5 // S5_BLOCKS
S5_BLOCK_LANES = S5_LANES // S5_BLOCKS
FFN_CHUNK = MXU_DIM
FFN_CHUNKS = D_FF // FFN_CHUNK
STAGE_ROWS = 256
STAGE_COLS = D_IN
STAGE_DEPTH = 3
FFN_STAGE_DEPTH = 3
FFN_STAGE_ROWS = 128
FFN_TIME_CHUNK = 128
FFN_SUB_ROWS = 512


def _dot(a, b):
    return jnp.dot(a, b, preferred_element_type=F32)


def _rms(x, g):
    return x * lax.rsqrt(jnp.mean(x * x, axis=-1, keepdims=True) + EPS) * g


def _log_sigmoid(x):
    return jnp.minimum(x, 0.0) - jnp.log1p(jnp.exp(-jnp.abs(x)))


def _sigmoid(x):
    return 0.5 * jnp.tanh(0.5 * x) + 0.5


def _sqrt_nonneg(x):
    return jnp.where(x > 0.0, x * lax.rsqrt(x), 0.0)


def _log2(n):
    assert n & (n - 1) == 0
    return n.bit_length() - 1


def _same_group(shape, row_group, col_group):
    r = lax.broadcasted_iota(I32, shape, 0) >> _log2(row_group)
    c = lax.broadcasted_iota(I32, shape, 1) >> _log2(col_group)
    return r == c


def _lane_tiled_block_diag(blk, reps, row_group):
    r, w = blk.shape
    n = w * reps
    sel = (lax.broadcasted_iota(I32, (w, n), 1) & (w - 1)) == lax.broadcasted_iota(I32, (w, n), 0)
    tiled = _dot(blk.astype(BF16), jnp.where(sel, 1.0, 0.0).astype(BF16))
    return jnp.where(_same_group((r, n), row_group, w), tiled, 0.0).astype(BF16)


def _sublane_tiled_block_diag(blk, reps, col_group):
    h, n = blk.shape
    tiled = jnp.broadcast_to(blk[None], (reps, h, n)).reshape(reps * h, n)
    return jnp.where(_same_group((reps * h, n), h, col_group), tiled, 0.0).astype(BF16)


def _lru_gate_block(conv_j, wg_j, ba_j, bx_j, lam_j):
    gz = _dot(conv_j.astype(BF16), wg_j)
    r = _sigmoid(gz[:, :GATE_BLOCK] + ba_j)
    ig = _sigmoid(gz[:, GATE_BLOCK:] + bx_j)
    log_a = (LRU_C * _log_sigmoid(lam_j)) * r
    a = jnp.exp(log_a)
    mult = _sqrt_nonneg(-jnp.tanh(log_a) * (a * a + 1.0))
    return a, mult * (ig * conv_j)


def _s5_bu_block(ub, wb_ref, j):
    bu = _dot(ub[:, j * S5_BLOCK_CH:(j + 1) * S5_BLOCK_CH], wb_ref[j])
    return bu[:, :S5_BLOCK_LANES], bu[:, S5_BLOCK_LANES:]


def _s5_y_block(hr, hi, wc_ref, j):
    return (_dot(hr.astype(BF16), wc_ref[j, :S5_BLOCK_LANES, :])
            + _dot(hi.astype(BF16), wc_ref[j, S5_BLOCK_LANES:, :]))


def _s5_glu(y, u, d, wglu_ref):
    z = _dot(jax.nn.gelu(y + d * u).astype(BF16), wglu_ref[...])
    return z[:, :D_S5] * _sigmoid(z[:, D_S5:])


def _stream_pieces(pieces, stage, sems):
    depth = stage.shape[0]

    def dst(k):
        r, c = pieces[k][0].shape
        return stage.at[k % depth, pl.ds(0, r), pl.ds(0, c)]

    copies = [pltpu.make_async_copy(pieces[k][0], dst(k), sems.at[k % depth]) for k in range(len(pieces))]
    for k in range(min(depth - 1, len(pieces))):
        copies[k].start()
    for k, (_, consume) in enumerate(pieces):
        if k + depth - 1 < len(pieces):
            copies[k + depth - 1].start()
        copies[k].wait()
        consume(dst(k))


def _batch_major_copies(tm_bufs, slot, bm_hbm, sems, chunk, to_hbm):
    _, tl, nb, _ = tm_bufs.shape
    copies = []
    for b in range(nb):
        hbm = bm_hbm.at[b, pl.ds(chunk * tl, tl), :]
        vmem = tm_bufs.at[slot, :, b, :]
        src, dst = (vmem, hbm) if to_hbm else (hbm, vmem)
        copies.append(pltpu.make_async_copy(src, dst, sems.at[slot, b]))
    return copies


def _mixer_kernel(
        n_chunks,
        x_hbm, cp_ref, cs_ref, adaw_hbm, adab_ref,
        xs_ref, sconv_ref, slru_ref, ss5r_ref, ss5i_ref,
        n1g_ref, win_hbm, cw_ref, cb_ref, wa_ref, wx_ref, ba_ref, bx_ref, lam_ref,
        slr_ref, sli_ref, sldt_ref, bret_ref, bimt_ref, cre_ref, cim_ref,
        d_ref, wglu_hbm, glru_ref, gs5_ref, wout_hbm,
        o_ref, mod2p_o, mod2s_o, convp_o, lrup_o, s5rp_o, s5ip_o,
        xs_o, convs_o, lrus_o, s5rs_o, s5is_o,
        x_buf, x_sems, stage, stage_sems, win_b, wglu_b, wout_b, wg_b, wb_b, wc_b, abr_s, abi_s,
        mod1p, mod1s, xc_buf, h_lru, h_s5r, h_s5i, p_buf, hr_buf, hi_buf, a_buf, b_buf, y_buf, m_buf,
        s5s_r, s5s_i):
    step = pl.program_id(0)
    slot = step % 2
    _, tl, nb, _ = x_buf.shape
    rows = tl * nb
    hist = (CONV_W - 1) * nb
    half_mod = (N_MOD // 2) * D_MODEL

    @pl.when(step == 0)
    def _prologue():
        for c in _batch_major_copies(x_buf, 0, x_hbm, x_sems, 0, to_hbm=False):
            c.start()
        xc_buf[0:hist, :] = jnp.zeros((hist, D_LRU), F32)
        h_lru[...] = jnp.zeros_like(h_lru)
        h_s5r[...] = jnp.zeros_like(h_s5r)
        h_s5i[...] = jnp.zeros_like(h_s5i)

        cp = cp_ref[...]
        cs = cs_ref[...]
        silu_p = (cp * _sigmoid(cp)).astype(BF16)
        silu_s = (cs * _sigmoid(cs)).astype(BF16)
        pieces = []

        def ada_piece(cg, kb):
            cols = slice(cg * STAGE_COLS, (cg + 1) * STAGE_COLS)
            first = cols.start < half_mod
            dst_p, dst_s = (mod1p, mod1s) if first else (mod2p_o, mod2s_o)
            dcols = cols if first else slice(cols.start - half_mod, cols.stop - half_mod)
            krows = slice(kb * STAGE_ROWS, (kb + 1) * STAGE_ROWS)

            def consume(view):
                w = view[...].astype(BF16)
                part_p = _dot(silu_p[:, krows], w)
                part_s = _dot(silu_s[:, krows], w)
                if kb == 0:
                    dst_p[:, dcols] = part_p + adab_ref[:, cols]
                    dst_s[:, dcols] = part_s + adab_ref[:, cols]
                else:
                    dst_p[:, dcols] += part_p
                    dst_s[:, dcols] += part_s
            return adaw_hbm.at[pl.ds(kb * STAGE_ROWS, STAGE_ROWS), pl.ds(cg * STAGE_COLS, STAGE_COLS)], consume

        def cast_piece(src_hbm, dst, rb):
            ncols = dst.shape[1]

            def consume(view):
                dst[rb * STAGE_ROWS:(rb + 1) * STAGE_ROWS, :] = view[...].astype(BF16)
            return src_hbm.at[pl.ds(rb * STAGE_ROWS, STAGE_ROWS), pl.ds(0, ncols)], consume

        for rb in range(win_b.shape[0] // STAGE_ROWS):
            pieces.append(cast_piece(win_hbm, win_b, rb))
        for cg in range(N_MOD * D_MODEL // STAGE_COLS):
            for kb in range(D_MODEL // STAGE_ROWS):
                pieces.append(ada_piece(cg, kb))
        for src_hbm, dst in ((wglu_hbm, wglu_b), (wout_hbm, wout_b)):
            for rb in range(dst.shape[0] // STAGE_ROWS):
                pieces.append(cast_piece(src_hbm, dst, rb))
        _stream_pieces(pieces, stage, stage_sems)

        for j in range(GATE_BLOCKS):
            rs = slice(j * GATE_BLOCK, (j + 1) * GATE_BLOCK)
            reps = GATE_BLOCK // LRU_HEAD_DIM
            wg_b[j, :, :GATE_BLOCK] = _lane_tiled_block_diag(wa_ref[rs, :], reps, LRU_HEAD_DIM)
            wg_b[j, :, GATE_BLOCK:] = _lane_tiled_block_diag(wx_ref[rs, :], reps, LRU_HEAD_DIM)

        lr = slr_ref[...]
        li = sli_ref[...]
        dt = jnp.exp(sldt_ref[...])
        mag = jnp.exp(lr * dt)
        ab_r = mag * jnp.cos(li * dt)
        ab_i = mag * jnp.sin(li * dt)
        den = lr * lr + li * li
        fr = ((ab_r - 1.0) * lr + ab_i * li) / den
        fi = (ab_i * lr - (ab_r - 1.0) * li) / den
        abr_s[...] = ab_r
        abi_s[...] = ab_i
        br = bret_ref[...]
        bi = bimt_ref[...]
        bb_r = fr * br - fi * bi
        bb_i = fr * bi + fi * br
        for j in range(S5_BLOCKS):
            sl = slice(j * S5_BLOCK_LANES, (j + 1) * S5_BLOCK_LANES)
            wb_b[j, :, :S5_BLOCK_LANES] = _sublane_tiled_block_diag(bb_r[:, sl], S5_BLOCK_GROUPS, S5_STATE)
            wb_b[j, :, S5_BLOCK_LANES:] = _sublane_tiled_block_diag(bb_i[:, sl], S5_BLOCK_GROUPS, S5_STATE)
            cs_rows = slice(j * S5_BLOCK_CH, (j + 1) * S5_BLOCK_CH)
            ct_r = _lane_tiled_block_diag(cre_ref[cs_rows, :], S5_BLOCK_GROUPS, S5_GROUP)
            ct_i = _lane_tiled_block_diag(-cim_ref[cs_rows, :], S5_BLOCK_GROUPS, S5_GROUP)
            wc_b[j, :S5_BLOCK_LANES, :] = ct_r.astype(F32).T.astype(BF16)
            wc_b[j, S5_BLOCK_LANES:, :] = ct_i.astype(F32).T.astype(BF16)

    @pl.when(step < n_chunks)
    def _prompt_step():
        @pl.when(step + 1 < n_chunks)
        def _():
            for c in _batch_major_copies(x_buf, 1 - slot, x_hbm, x_sems, step + 1, to_hbm=False):
                c.start()

        for c in _batch_major_copies(x_buf, slot, x_hbm, x_sems, step, to_hbm=False):
            c.wait()

        sh1 = mod1p[:, 0:D_MODEL]
        sc1 = mod1p[:, D_MODEL:2 * D_MODEL]
        g1 = mod1p[:, 2 * D_MODEL:3 * D_MODEL]

        scale1 = (n1g_ref[...] * (1.0 + sc1))[None]
        tsub = tl // FRONT_SLICES

        def project(s):
            r0, r1 = s * tsub * nb, (s + 1) * tsub * nb
            xs3 = x_buf[slot, pl.ds(s * tsub, tsub)]
            hn = _rms(xs3, scale1) + sh1[None]
            p_buf[r0:r1, :] = _dot(hn.reshape(r1 - r0, D_MODEL).astype(BF16), win_b[...])
            xc_buf[hist + r0:hist + r1, :] = p_buf[r0:r1, 0:D_LRU]

        def lru_gates(s):
            r0, r1 = s * tsub * nb, (s + 1) * tsub * nb
            for j in range(GATE_BLOCKS):
                sl = slice(j * GATE_BLOCK, (j + 1) * GATE_BLOCK)
                conv_j = cb_ref[:, sl]
                for k in range(CONV_W):
                    conv_j = conv_j + cw_ref[k:k + 1, sl] * xc_buf[k * nb + r0:k * nb + r1, sl]
                a, b = _lru_gate_block(conv_j, wg_b[j], ba_ref[:, sl], bx_ref[:, sl], lam_ref[:, sl])
                a_buf[r0:r1, sl] = a
                b_buf[r0:r1, sl] = b

        def lru_scan():
            h = h_lru[...]
            for t in range(tl):
                h = a_buf[t * nb:(t + 1) * nb, :] * h + b_buf[t * nb:(t + 1) * nb, :]
                b_buf[t * nb:(t + 1) * nb, :] = h
            h_lru[...] = h
            lru_out = b_buf[...] * jax.nn.gelu(p_buf[:, D_LRU:2 * D_LRU])
            m_buf[:, 0:D_LRU] = _rms(lru_out, glru_ref[...]).astype(BF16)

        def s5_input(j):
            ub = p_buf[:, 2 * D_LRU + j * S5_BLOCK_CH:2 * D_LRU + (j + 1) * S5_BLOCK_CH].astype(BF16)
            bu = _dot(ub, wb_b[j])
            hr_buf[j % 2] = bu[:, :S5_BLOCK_LANES]
            hi_buf[j % 2] = bu[:, S5_BLOCK_LANES:]

        def s5_scan(j):
            sl = slice(j * S5_BLOCK_LANES, (j + 1) * S5_BLOCK_LANES)
            hr_j = hr_buf.at[j % 2]
            hi_j = hi_buf.at[j % 2]
            ar = jnp.broadcast_to(abr_s[:, sl], (nb, S5_BLOCK_LANES))
            ai = jnp.broadcast_to(abi_s[:, sl], (nb, S5_BLOCK_LANES))
            hr = h_s5r[:, sl]
            hi = h_s5i[:, sl]
            for t in range(tl):
                rs = slice(t * nb, (t + 1) * nb)
                hr, hi = ar * hr - ai * hi + hr_j[rs, :], ar * hi + ai * hr + hi_j[rs, :]
                hr_j[rs, :] = hr
                hi_j[rs, :] = hi
            h_s5r[:, sl] = hr
            h_s5i[:, sl] = hi

        def s5_readout(j):
            y_buf[:, j * S5_BLOCK_CH:(j + 1) * S5_BLOCK_CH] = _s5_y_block(
                hr_buf[j % 2], hi_buf[j % 2], wc_b, j)

        for s in range(FRONT_SLICES):
            project(s)
        lru_gates(0)
        s5_input(0)
        s5_input(1)
        for s in range(1, FRONT_SLICES):
            lru_gates(s)
        for j in range(S5_BLOCKS):
            s5_scan(j)
            s5_readout(j)
            if j + 2 < S5_BLOCKS:
                s5_input(j + 2)
            if j == 1:
                lru_scan()

        s5_out = _s5_glu(y_buf[...], p_buf[:, 2 * D_LRU:], d_ref[...], wglu_b)
        m_buf[:, D_LRU:] = _rms(s5_out, gs5_ref[...]).astype(BF16)

        mix = _dot(m_buf[...], wout_b[...]).reshape(tl, nb, D_MODEL)
        o_ref[...] = x_buf[slot] + g1[None] * mix

        xc_buf[0:hist, :] = xc_buf[rows:rows + hist, :]

    @pl.when(step == n_chunks)
    def _sample_step():
        for k in range(CONV_W - 1):
            convp_o[:, k, :] = xc_buf[k * nb:(k + 1) * nb, :]
        lrup_o[...] = h_lru[...]
        s5rp_o[...] = h_s5r[...].reshape(s5rp_o.shape)
        s5ip_o[...] = h_s5i[...].reshape(s5ip_o.shape)

        ns = xs_ref.shape[0]
        sh1 = mod1s[:, 0:D_MODEL]
        sc1 = mod1s[:, D_MODEL:2 * D_MODEL]
        g1 = mod1s[:, 2 * D_MODEL:3 * D_MODEL]

        x = xs_ref[:, 0, :]
        hn = _rms(x, n1g_ref[...] * (1.0 + sc1)) + sh1
        p = _dot(hn.astype(BF16), win_b[...])
        lru_x = p[:, 0:D_LRU]

        for k in range(CONV_W - 2):
            convs_o[:, k, :] = sconv_ref[:, k + 1, :]
        convs_o[:, CONV_W - 2, :] = lru_x
        for j in range(GATE_BLOCKS):
            sl = slice(j * GATE_BLOCK, (j + 1) * GATE_BLOCK)
            conv_j = cb_ref[:, sl] + cw_ref[CONV_W - 1:CONV_W, sl] * lru_x[:, sl]
            for k in range(CONV_W - 1):
                conv_j = conv_j + cw_ref[k:k + 1, sl] * sconv_ref[:, k, sl]
            a, b = _lru_gate_block(conv_j, wg_b[j], ba_ref[:, sl], bx_ref[:, sl], lam_ref[:, sl])
            lrus_o[:, sl] = a * slru_ref[:, sl] + b
        lru_out = lrus_o[...] * jax.nn.gelu(p[:, D_LRU:2 * D_LRU])
        ms_buf = m_buf.at[pl.ds(0, ns), :]
        ms_buf[:, 0:D_LRU] = _rms(lru_out, glru_ref[...]).astype(BF16)

        u = p[:, 2 * D_LRU:]
        ub = u.astype(BF16)
        ys_buf = y_buf.at[pl.ds(0, ns), :]
        h0r = ss5r_ref[...].reshape(ns, S5_LANES)
        h0i = ss5i_ref[...].reshape(ns, S5_LANES)
        for j in range(S5_BLOCKS):
            sl = slice(j * S5_BLOCK_LANES, (j + 1) * S5_BLOCK_LANES)
            bu_r, bu_i = _s5_bu_block(ub, wb_b, j)
            ar = abr_s[:, sl]
            ai = abi_s[:, sl]
            hr = ar * h0r[:, sl] - ai * h0i[:, sl] + bu_r
            hi = ar * h0i[:, sl] + ai * h0r[:, sl] + bu_i
            s5s_r[:, sl] = hr
            s5s_i[:, sl] = hi
            ys_buf[:, j * S5_BLOCK_CH:(j + 1) * S5_BLOCK_CH] = _s5_y_block(hr, hi, wc_b, j)
        s5rs_o[...] = s5s_r[...].reshape(s5rs_o.shape)
        s5is_o[...] = s5s_i[...].reshape(s5is_o.shape)
        s5_out = _s5_glu(ys_buf[...], u, d_ref[...], wglu_b)
        ms_buf[:, D_LRU:] = _rms(s5_out, gs5_ref[...]).astype(BF16)
        xs_o[...] = x + g1 * _dot(ms_buf[...], wout_b[...])


def _const_spec(shape):
    nd = len(shape)
    return pl.BlockSpec(shape, lambda i, _nd=nd: (0,) * _nd, pipeline_mode=pl.Buffered(1))


def _out_const_spec(shape):
    nd = len(shape)
    return pl.BlockSpec(shape, lambda i, _nd=nd: (0,) * _nd)


_HBM = pl.BlockSpec(memory_space=pl.ANY)


def _mixer(x_prompt, c_prompt, c_sample, ada_w, ada_b, x_s, sconv, slru, ss5r, ss5i, w):
    nb, t_len, _ = x_prompt.shape
    ns = x_s.shape[0]
    tl = TIME_CHUNK
    n_chunks = t_len // tl
    rows = tl * nb
    hist = (CONV_W - 1) * nb
    half_mod = (N_MOD // 2) * D_MODEL

    inputs = [
        (x_prompt, _HBM), (c_prompt, None), (c_sample, None), (ada_w, _HBM), (ada_b, None),
        (x_s, None), (sconv, None), (slru, None), (ss5r, None), (ss5i, None),
        (w["n1g"], None), (w["win"], _HBM), (w["cw"], None), (w["cb"], None), (w["wa"], None), (w["wx"], None),
        (w["ba"], None), (w["bx"], None), (w["lam"], None),
        (w["slr"], None), (w["sli"], None), (w["sldt"], None), (w["bret"], None), (w["bimt"], None),
        (w["cre"], None), (w["cim"], None),
        (w["d"], None), (w["wglu"], _HBM), (w["glru"], None), (w["gs5"], None), (w["wout"], _HBM),
    ]
    in_specs = [spec if spec is not None else _const_spec(a.shape) for a, spec in inputs]

    out_shapes = [
        ((t_len, nb, D_MODEL), pl.BlockSpec((tl, nb, D_MODEL), lambda i: (jnp.minimum(i, n_chunks - 1), 0, 0))),
        ((nb, half_mod), None), ((ns, half_mod), None),
        ((nb, CONV_W - 1, D_LRU), None), ((nb, D_LRU), None),
        ((nb, S5_GROUPS, S5_STATE), None), ((nb, S5_GROUPS, S5_STATE), None),
        ((ns, D_MODEL), None), ((ns, CONV_W - 1, D_LRU), None), ((ns, D_LRU), None),
        ((ns, S5_GROUPS, S5_STATE), None), ((ns, S5_GROUPS, S5_STATE), None),
    ]
    out_specs = [spec if spec is not None else _out_const_spec(s) for s, spec in out_shapes]
    out_shape = [jax.ShapeDtypeStruct(s, F32) for s, _ in out_shapes]

    scratch = [
        pltpu.VMEM((2, tl, nb, D_MODEL), F32),
        pltpu.SemaphoreType.DMA((2, nb)),
        pltpu.VMEM((STAGE_DEPTH, STAGE_ROWS, STAGE_COLS), F32),
        pltpu.SemaphoreType.DMA((STAGE_DEPTH,)),
        pltpu.VMEM((D_MODEL, D_IN), BF16),
        pltpu.VMEM((D_S5, 2 * D_S5), BF16),
        pltpu.VMEM((D_LRU + D_S5, D_MODEL), BF16),
        pltpu.VMEM((GATE_BLOCKS, GATE_BLOCK, 2 * GATE_BLOCK), BF16),
        pltpu.VMEM((S5_BLOCKS, S5_BLOCK_CH, 2 * S5_BLOCK_LANES), BF16),
        pltpu.VMEM((S5_BLOCKS, 2 * S5_BLOCK_LANES, S5_BLOCK_CH), BF16),
        pltpu.VMEM((1, S5_LANES), F32),
        pltpu.VMEM((1, S5_LANES), F32),
        pltpu.VMEM((nb, half_mod), F32),
        pltpu.VMEM((ns, half_mod), F32),
        pltpu.VMEM((hist + rows, D_LRU), F32),
        pltpu.VMEM((nb, D_LRU), F32),
        pltpu.VMEM((nb, S5_LANES), F32),
        pltpu.VMEM((nb, S5_LANES), F32),
        pltpu.VMEM((rows, D_IN), F32),
        pltpu.VMEM((2, rows, S5_BLOCK_LANES), F32),
        pltpu.VMEM((2, rows, S5_BLOCK_LANES), F32),
        pltpu.VMEM((rows, D_LRU), F32),
        pltpu.VMEM((rows, D_LRU), F32),
        pltpu.VMEM((rows, D_S5), F32),
        pltpu.VMEM((rows, D_LRU + D_S5), BF16),
        pltpu.VMEM((ns, S5_LANES), F32),
        pltpu.VMEM((ns, S5_LANES), F32),
    ]
    return pl.pallas_call(
        functools.partial(_mixer_kernel, n_chunks),
        grid=(n_chunks + 1,),
        in_specs=in_specs,
        out_specs=out_specs,
        out_shape=out_shape,
        scratch_shapes=scratch,
        compiler_params=pltpu.CompilerParams(
            dimension_semantics=("arbitrary",), vmem_limit_bytes=VMEM_LIMIT_BYTES),
        name="mixer",
    )(*[a for a, _ in inputs])


def _ffn_rows(x3, mod_ref, n2g_ref, wgate_b, wup_b, wdown_b, gf_ref, act_ref):
    groups, mrows, _ = x3.shape
    rows = groups * mrows
    sh2 = mod_ref[:, 0:D_MODEL]
    sc2 = mod_ref[:, D_MODEL:2 * D_MODEL]
    g2 = mod_ref[:, 2 * D_MODEL:3 * D_MODEL]

    hn = _rms(x3, (n2g_ref[...] * (1.0 + sc2))[None]) + sh2[None]
    hb = hn.reshape(rows, D_MODEL).astype(BF16)
    for c in range(FFN_CHUNKS):
        g = _dot(hb, wgate_b[c])
        up = _dot(hb, wup_b[c])
        act_ref[:, c * FFN_CHUNK:(c + 1) * FFN_CHUNK] = (g * _sigmoid(g) * up).astype(BF16)
    ffn = _dot(act_ref[...], wdown_b[...]).reshape(groups, mrows, D_MODEL)
    return _rms(x3 + g2[None] * ffn, gf_ref[...])


def _ffn_kernel(
        n_chunks,
        x_ref, mod2p_ref, mod2s_ref, xs_ref, n2g_ref, wgate_hbm, wup_hbm, wdown_hbm, gf_ref,
        y_hbm, ys_o,
        stage_a, stage_b, stage_a_sems, stage_b_sems, wgate_b, wup_b, wdown_b, act_buf, y_buf, y_sems):
    step = pl.program_id(0)
    slot = step % 2

    @pl.when(step == 0)
    def _prologue():
        def wide_piece(src_hbm, dst, rb):
            rows = slice(rb * FFN_STAGE_ROWS, (rb + 1) * FFN_STAGE_ROWS)

            def consume(view):
                for c in range(FFN_CHUNKS):
                    dst[c, rows, :] = view[:, c * FFN_CHUNK:(c + 1) * FFN_CHUNK].astype(BF16)
            return src_hbm.at[pl.ds(rb * FFN_STAGE_ROWS, FFN_STAGE_ROWS), :], consume

        def row_piece(c):
            def consume(view):
                wdown_b[c * FFN_CHUNK:(c + 1) * FFN_CHUNK, :] = view[...].astype(BF16)
            return wdown_hbm.at[pl.ds(c * FFN_CHUNK, FFN_CHUNK), :], consume

        pieces = []
        for rb in range(D_MODEL // FFN_STAGE_ROWS):
            pieces.append(wide_piece(wgate_hbm, wgate_b, rb))
            pieces.append(wide_piece(wup_hbm, wup_b, rb))
        _stream_pieces(pieces, stage_a, stage_a_sems)
        _stream_pieces([row_piece(c) for c in range(FFN_CHUNKS)], stage_b, stage_b_sems)

    @pl.when(step < n_chunks)
    def _prompt_step():
        @pl.when(step >= 2)
        def _():
            for c in _batch_major_copies(y_buf, slot, y_hbm, y_sems, step - 2, to_hbm=True):
                c.wait()

        tl, nb, _ = x_ref.shape
        sub = FFN_SUB_ROWS // nb
        for h in range(tl // sub):
            y_buf[slot, pl.ds(h * sub, sub)] = _ffn_rows(
                x_ref[h * sub:(h + 1) * sub], mod2p_ref, n2g_ref, wgate_b, wup_b, wdown_b, gf_ref,
                act_buf.at[pl.ds(h * FFN_SUB_ROWS, FFN_SUB_ROWS), :])
        for c in _batch_major_copies(y_buf, slot, y_hbm, y_sems, step, to_hbm=True):
            c.start()

    @pl.when(step == n_chunks)
    def _sample_step():
        ns = xs_ref.shape[0]
        ys_o[:, 0, :] = _ffn_rows(xs_ref[...][None], mod2s_ref, n2g_ref, wgate_b, wup_b, wdown_b, gf_ref,
                                  act_buf.at[pl.ds(0, ns), :])[0]
        for s in range(max(n_chunks - 2, 0), n_chunks):
            for c in _batch_major_copies(y_buf, s % 2, y_hbm, y_sems, s, to_hbm=True):
                c.wait()


def _ffn(x32_t, mod2p, mod2s, x32_s, w, out_shape_p, out_shape_s):
    t_len, nb, _ = x32_t.shape
    tl = FFN_TIME_CHUNK
    n_chunks = t_len // tl
    rows = tl * nb
    inputs = [
        (x32_t, pl.BlockSpec((tl, nb, D_MODEL), lambda i: (jnp.minimum(i, n_chunks - 1), 0, 0))),
        (mod2p, None), (mod2s, None), (x32_s, None), (w["n2g"], None),
        (w["wgate"], _HBM), (w["wup"], _HBM), (w["wdown"], _HBM), (w["gf"], None),
    ]
    in_specs = [spec if spec is not None else _const_spec(a.shape) for a, spec in inputs]
    scratch = [
        pltpu.VMEM((FFN_STAGE_DEPTH, FFN_STAGE_ROWS, D_FF), F32),
        pltpu.VMEM((FFN_STAGE_DEPTH, FFN_CHUNK, D_MODEL), F32),
        pltpu.SemaphoreType.DMA((FFN_STAGE_DEPTH,)),
        pltpu.SemaphoreType.DMA((FFN_STAGE_DEPTH,)),
        pltpu.VMEM((FFN_CHUNKS, D_MODEL, FFN_CHUNK), BF16),
        pltpu.VMEM((FFN_CHUNKS, D_MODEL, FFN_CHUNK), BF16),
        pltpu.VMEM((D_FF, D_MODEL), BF16),
        pltpu.VMEM((rows, D_FF), BF16),
        pltpu.VMEM((2, tl, nb, D_MODEL), F32),
        pltpu.SemaphoreType.DMA((2, nb)),
    ]
    return pl.pallas_call(
        functools.partial(_ffn_kernel, n_chunks),
        grid=(n_chunks + 1,),
        in_specs=in_specs,
        out_specs=[_HBM, _out_const_spec(out_shape_s)],
        out_shape=[jax.ShapeDtypeStruct(out_shape_p, F32), jax.ShapeDtypeStruct(out_shape_s, F32)],
        scratch_shapes=scratch,
        compiler_params=pltpu.CompilerParams(
            dimension_semantics=("arbitrary",), vmem_limit_bytes=VMEM_LIMIT_BYTES),
        name="ffn",
    )(*[a for a, _ in inputs])


def kernel(x_prompt, x_sample, state_conv, state_lru, state_s5_re, state_s5_im, c_prompt, c_sample,
           ada_w, ada_b, norm1_g, w_in, conv_w, conv_b, lru_wa, lru_ba, lru_wx, lru_bx, lru_lambda,
           s5_lambda_re, s5_lambda_im, s5_log_dt, s5_b_re, s5_b_im, s5_c_re, s5_c_im, s5_d, s5_w_glu,
           g_lru_out, g_s5_out, w_out, norm2_g, ffn_w_gate, ffn_w_up, ffn_w_down, final_norm_g):
    depth = ada_w.shape[0]
    assert depth == 1, "single-layer decoder step"
    bp, t_len, _ = x_prompt.shape
    bs, dec_t, _ = x_sample.shape
    assert bp == SUBLANES and dec_t == 1 and t_len % TIME_CHUNK == 0 and t_len % FFN_TIME_CHUNK == 0
    assert (FFN_TIME_CHUNK * SUBLANES) % FFN_SUB_ROWS == 0 and bs <= FFN_SUB_ROWS

    w = dict(
        n1g=norm1_g, win=w_in[0], cw=conv_w[0], cb=conv_b,
        wa=lru_wa.reshape(D_LRU, LRU_HEAD_DIM), wx=lru_wx.reshape(D_LRU, LRU_HEAD_DIM),
        ba=lru_ba, bx=lru_bx, lam=lru_lambda,
        slr=s5_lambda_re.reshape(1, S5_LANES), sli=s5_lambda_im.reshape(1, S5_LANES),
        sldt=jnp.repeat(s5_log_dt[0], S5_STATE).reshape(1, S5_LANES),
        bret=s5_b_re.reshape(S5_LANES, S5_GROUP).T, bimt=s5_b_im.reshape(S5_LANES, S5_GROUP).T,
        cre=s5_c_re.reshape(D_S5, S5_STATE), cim=s5_c_im.reshape(D_S5, S5_STATE),
        d=s5_d, wglu=s5_w_glu[0], glru=g_lru_out, gs5=g_s5_out, wout=w_out[0],
        n2g=norm2_g, wgate=ffn_w_gate[0], wup=ffn_w_up[0], wdown=ffn_w_down[0],
        gf=final_norm_g.reshape(1, D_MODEL),
    )

    (x32_t, mod2p, mod2s, conv_p, lru_p, s5r_p, s5i_p,
     x32_s, conv_s, lru_s, s5r_s, s5i_s) = _mixer(
        x_prompt, c_prompt, c_sample, ada_w[0], ada_b, x_sample, state_conv[0], state_lru[0],
        state_s5_re[0], state_s5_im[0], w)
    y_prompt, y_sample = _ffn(x32_t, mod2p, mod2s, x32_s, w, x_prompt.shape, x_sample.shape)

    return (y_prompt, y_sample, conv_p[None], lru_p[None], s5r_p[None], s5i_p[None],
            conv_s[None], lru_s[None], s5r_s[None], s5i_s[None])
```

```python
import functools

import jax
import jax.numpy as jnp
from jax import lax
from jax.experimental import pallas as pl
from jax.experimental.pallas import tpu as pltpu

F32 = jnp.float32
BF16 = jnp.bfloat16
I32 = jnp.int32

D_MODEL = 1024
D_LRU = 512
D_S5 = 512
D_IN = 2 * D_LRU + D_S5
LRU_HEAD_DIM = 64
CONV_W = 4
LRU_C = 8.0
S5_GROUP = 16
S5_GROUPS = D_S5 // S5_GROUP
S5_STATE = 64
S5_LANES = S5_GROUPS * S5_STATE
D_FF = 2816
N_MOD = 6
EPS = 1e-6

SUBLANES = 8
MXU_DIM = 256
VMEM_LIMIT_BYTES = 58 * 1024 * 1024

TIME_CHUNK = 64
FRONT_SLICES = 2
GATE_BLOCK = MXU_DIM
GATE_BLOCKS = D_LRU // GATE_BLOCK
S5_BLOCKS = 4
S5_BLOCK_GROUPS = S5_GROUPS // S5_BLOCKS
S5_BLOCK_CH = D_S5 // S5_BLOCKS
S5_BLOCK_LANES = S5_LANES // S5_BLOCKS
FFN_CHUNK = MXU_DIM
FFN_CHUNKS = D_FF // FFN_CHUNK
STAGE_ROWS = 256
STAGE_COLS = D_IN
STAGE_DEPTH = 3
BG_COLS = D_FF // 2
FFN_TIME_CHUNK = 128
FFN_SUB_ROWS = 512


def _dot(a, b):
    return jnp.dot(a, b, preferred_element_type=F32)


def _rms(x, g):
    return x * lax.rsqrt(jnp.mean(x * x, axis=-1, keepdims=True) + EPS) * g


def _log_sigmoid(x):
    return jnp.minimum(x, 0.0) - jnp.log1p(jnp.exp(-jnp.abs(x)))


def _sigmoid(x):
    return 0.5 * jnp.tanh(0.5 * x) + 0.5


def _sqrt_nonneg(x):
    return jnp.where(x > 0.0, x * lax.rsqrt(x), 0.0)


def _log2(n):
    assert n & (n - 1) == 0
    return n.bit_length() - 1


def _same_group(shape, row_group, col_group):
    r = lax.broadcasted_iota(I32, shape, 0) >> _log2(row_group)
    c = lax.broadcasted_iota(I32, shape, 1) >> _log2(col_group)
    return r == c


def _lane_tiled_block_diag(blk, reps, row_group):
    r, w = blk.shape
    n = w * reps
    sel = (lax.broadcasted_iota(I32, (w, n), 1) & (w - 1)) == lax.broadcasted_iota(I32, (w, n), 0)
    tiled = _dot(blk.astype(BF16), jnp.where(sel, 1.0, 0.0).astype(BF16))
    return jnp.where(_same_group((r, n), row_group, w), tiled, 0.0).astype(BF16)


def _sublane_tiled_block_diag(blk, reps, col_group):
    h, n = blk.shape
    tiled = jnp.broadcast_to(blk[None], (reps, h, n)).reshape(reps * h, n)
    return jnp.where(_same_group((reps * h, n), h, col_group), tiled, 0.0).astype(BF16)


def _lru_gate_block(conv_j, wg_j, ba_j, bx_j, lam_j):
    gz = _dot(conv_j.astype(BF16), wg_j)
    r = _sigmoid(gz[:, :GATE_BLOCK] + ba_j)
    ig = _sigmoid(gz[:, GATE_BLOCK:] + bx_j)
    log_a = (LRU_C * _log_sigmoid(lam_j)) * r
    a = jnp.exp(log_a)
    mult = _sqrt_nonneg(-jnp.tanh(log_a) * (a * a + 1.0))
    return a, mult * (ig * conv_j)


def _s5_bu_block(ub, wb_ref, j):
    bu = _dot(ub[:, j * S5_BLOCK_CH:(j + 1) * S5_BLOCK_CH], wb_ref[j])
    return bu[:, :S5_BLOCK_LANES], bu[:, S5_BLOCK_LANES:]


def _s5_y_block(hr, hi, wc_ref, j):
    return (_dot(hr.astype(BF16), wc_ref[j, :S5_BLOCK_LANES, :])
            + _dot(hi.astype(BF16), wc_ref[j, S5_BLOCK_LANES:, :]))


def _s5_glu(y, u, d, wglu_ref):
    z = _dot(jax.nn.gelu(y + d * u).astype(BF16), wglu_ref[...])
    return z[:, :D_S5] * _sigmoid(z[:, D_S5:])


def _stream_pieces(pieces, stage, sems):
    depth = stage.shape[0]

    def dst(k):
        r, c = pieces[k][0].shape
        return stage.at[k % depth, pl.ds(0, r), pl.ds(0, c)]

    copies = [pltpu.make_async_copy(pieces[k][0], dst(k), sems.at[k % depth]) for k in range(len(pieces))]
    for k in range(min(depth - 1, len(pieces))):
        copies[k].start()
    for k, (_, consume) in enumerate(pieces):
        if k + depth - 1 < len(pieces):
            copies[k + depth - 1].start()
        copies[k].wait()
        consume(dst(k))


def _batch_major_copies(tm_bufs, slot, bm_hbm, sems, chunk, to_hbm):
    _, tl, nb, _ = tm_bufs.shape
    copies = []
    for b in range(nb):
        hbm = bm_hbm.at[b, pl.ds(chunk * tl, tl), :]
        vmem = tm_bufs.at[slot, :, b, :]
        src, dst = (vmem, hbm) if to_hbm else (hbm, vmem)
        copies.append(pltpu.make_async_copy(src, dst, sems.at[slot, b]))
    return copies


def _mixer_kernel(
        n_chunks,
        x_hbm, cp_ref, cs_ref, adaw_hbm, adab_ref,
        xs_ref, sconv_ref, slru_ref, ss5r_ref, ss5i_ref,
        n1g_ref, win_hbm, cw_ref, cb_ref, wa_ref, wx_ref, ba_ref, bx_ref, lam_ref,
        slr_ref, sli_ref, sldt_ref, bret_ref, bimt_ref, cre_ref, cim_ref,
        d_ref, wglu_hbm, glru_ref, gs5_ref, wout_hbm, wgate_hbm, wup_hbm, wdown_hbm,
        o_ref, mod2p_o, mod2s_o, convp_o, lrup_o, s5rp_o, s5ip_o,
        xs_o, convs_o, lrus_o, s5rs_o, s5is_o, wgate_o, wup_o, wdown_o,
        x_buf, x_sems, stage, stage_sems, win_b, wglu_b, wout_b, wg_b, wb_b, wc_b, abr_s, abi_s,
        mod1p, mod1s, xc_buf, h_lru, h_s5r, h_s5i, p_buf, hr_buf, hi_buf, a_buf, b_buf, y_buf, m_buf,
        s5s_r, s5s_i, bg_buf, bg_sems):
    step = pl.program_id(0)
    slot = step % 2
    _, tl, nb, _ = x_buf.shape
    rows = tl * nb
    hist = (CONV_W - 1) * nb
    half_mod = (N_MOD // 2) * D_MODEL

    bg_pieces = []
    for src, dst in ((wgate_hbm, wgate_o), (wup_hbm, wup_o)):
        for rb in range(D_MODEL // STAGE_ROWS):
            for ch in range(D_FF // BG_COLS):
                win = (pl.ds(rb * STAGE_ROWS, STAGE_ROWS), pl.ds(ch * BG_COLS, BG_COLS))
                bg_pieces.append((src.at[win], dst.at[win]))
    for rb in range(D_FF // STAGE_ROWS):
        win = (pl.ds(rb * STAGE_ROWS, STAGE_ROWS), pl.ds(0, D_MODEL))
        bg_pieces.append((wdown_hbm.at[win], wdown_o.at[win]))
    assert len(bg_pieces) < n_chunks

    def bg_in(k):
        r, c = bg_pieces[k][0].shape
        return pltpu.make_async_copy(bg_pieces[k][0], stage.at[k % STAGE_DEPTH, pl.ds(0, r), pl.ds(0, c)],
                                     stage_sems.at[k % STAGE_DEPTH])

    def bg_out(k):
        r, c = bg_pieces[k][0].shape
        return pltpu.make_async_copy(bg_buf.at[k % 2, pl.ds(0, r), pl.ds(0, c)], bg_pieces[k][1],
                                     bg_sems.at[k % 2])

    @pl.when(step == 0)
    def _prologue():
        for c in _batch_major_copies(x_buf, 0, x_hbm, x_sems, 0, to_hbm=False):
            c.start()
        xc_buf[0:hist, :] = jnp.zeros((hist, D_LRU), F32)
        h_lru[...] = jnp.zeros_like(h_lru)
        h_s5r[...] = jnp.zeros_like(h_s5r)
        h_s5i[...] = jnp.zeros_like(h_s5i)

        cp = cp_ref[...]
        cs = cs_ref[...]
        silu_p = (cp * _sigmoid(cp)).astype(BF16)
        silu_s = (cs * _sigmoid(cs)).astype(BF16)
        pieces = []

        def ada_piece(cg, kb):
            cols = slice(cg * STAGE_COLS, (cg + 1) * STAGE_COLS)
            first = cols.start < half_mod
            dst_p, dst_s = (mod1p, mod1s) if first else (mod2p_o, mod2s_o)
            dcols = cols if first else slice(cols.start - half_mod, cols.stop - half_mod)
            krows = slice(kb * STAGE_ROWS, (kb + 1) * STAGE_ROWS)

            def consume(view):
                w = view[...].astype(BF16)
                part_p = _dot(silu_p[:, krows], w)
                part_s = _dot(silu_s[:, krows], w)
                if kb == 0:
                    dst_p[:, dcols] = part_p + adab_ref[:, cols]
                    dst_s[:, dcols] = part_s + adab_ref[:, cols]
                else:
                    dst_p[:, dcols] += part_p
                    dst_s[:, dcols] += part_s
            return adaw_hbm.at[pl.ds(kb * STAGE_ROWS, STAGE_ROWS), pl.ds(cg * STAGE_COLS, STAGE_COLS)], consume

        def cast_piece(src_hbm, dst, rb):
            ncols = dst.shape[1]

            def consume(view):
                dst[rb * STAGE_ROWS:(rb + 1) * STAGE_ROWS, :] = view[...].astype(BF16)
            return src_hbm.at[pl.ds(rb * STAGE_ROWS, STAGE_ROWS), pl.ds(0, ncols)], consume

        for rb in range(win_b.shape[0] // STAGE_ROWS):
            pieces.append(cast_piece(win_hbm, win_b, rb))
        for cg in range(N_MOD * D_MODEL // STAGE_COLS):
            for kb in range(D_MODEL // STAGE_ROWS):
                pieces.append(ada_piece(cg, kb))
        for src_hbm, dst in ((wglu_hbm, wglu_b), (wout_hbm, wout_b)):
            for rb in range(dst.shape[0] // STAGE_ROWS):
                pieces.append(cast_piece(src_hbm, dst, rb))
        _stream_pieces(pieces, stage, stage_sems)
        bg_in(0).start()

        for j in range(GATE_BLOCKS):
            rs = slice(j * GATE_BLOCK, (j + 1) * GATE_BLOCK)
            reps = GATE_BLOCK // LRU_HEAD_DIM
            wg_b[j, :, :GATE_BLOCK] = _lane_tiled_block_diag(wa_ref[rs, :], reps, LRU_HEAD_DIM)
            wg_b[j, :, GATE_BLOCK:] = _lane_tiled_block_diag(wx_ref[rs, :], reps, LRU_HEAD_DIM)

        lr = slr_ref[...]
        li = sli_ref[...]
        dt = jnp.exp(sldt_ref[...])
        mag = jnp.exp(lr * dt)
        ab_r = mag * jnp.cos(li * dt)
        ab_i = mag * jnp.sin(li * dt)
        den = lr * lr + li * li
        fr = ((ab_r - 1.0) * lr + ab_i * li) / den
        fi = (ab_i * lr - (ab_r - 1.0) * li) / den
        abr_s[...] = ab_r
        abi_s[...] = ab_i
        br = bret_ref[...]
        bi = bimt_ref[...]
        bb_r = fr * br - fi * bi
        bb_i = fr * bi + fi * br
        for j in range(S5_BLOCKS):
            sl = slice(j * S5_BLOCK_LANES, (j + 1) * S5_BLOCK_LANES)
            wb_b[j, :, :S5_BLOCK_LANES] = _sublane_tiled_block_diag(bb_r[:, sl], S5_BLOCK_GROUPS, S5_STATE)
            wb_b[j, :, S5_BLOCK_LANES:] = _sublane_tiled_block_diag(bb_i[:, sl], S5_BLOCK_GROUPS, S5_STATE)
            cs_rows = slice(j * S5_BLOCK_CH, (j + 1) * S5_BLOCK_CH)
            ct_r = _lane_tiled_block_diag(cre_ref[cs_rows, :], S5_BLOCK_GROUPS, S5_GROUP)
            ct_i = _lane_tiled_block_diag(-cim_ref[cs_rows, :], S5_BLOCK_GROUPS, S5_GROUP)
            wc_b[j, :S5_BLOCK_LANES, :] = ct_r.astype(F32).T.astype(BF16)
            wc_b[j, S5_BLOCK_LANES:, :] = ct_i.astype(F32).T.astype(BF16)

    for k in range(len(bg_pieces)):
        @pl.when(step == k + 1)
        def _bg_task(k=k):
            r, c = bg_pieces[k][0].shape
            bg_in(k).wait()
            if k >= 2:
                bg_out(k - 2).wait()
            bg_buf[k % 2, 0:r, 0:c] = stage[k % STAGE_DEPTH, 0:r, 0:c].astype(BF16)
            bg_out(k).start()
            if k + 1 < len(bg_pieces):
                bg_in(k + 1).start()

    @pl.when(step < n_chunks)
    def _prompt_step():
        @pl.when(step + 1 < n_chunks)
        def _():
            for c in _batch_major_copies(x_buf, 1 - slot, x_hbm, x_sems, step + 1, to_hbm=False):
                c.start()

        for c in _batch_major_copies(x_buf, slot, x_hbm, x_sems, step, to_hbm=False):
            c.wait()

        sh1 = mod1p[:, 0:D_MODEL]
        sc1 = mod1p[:, D_MODEL:2 * D_MODEL]
        g1 = mod1p[:, 2 * D_MODEL:3 * D_MODEL]

        scale1 = (n1g_ref[...] * (1.0 + sc1))[None]
        tsub = tl // FRONT_SLICES

        def project(s):
            r0, r1 = s * tsub * nb, (s + 1) * tsub * nb
            xs3 = x_buf[slot, pl.ds(s * tsub, tsub)]
            hn = _rms(xs3, scale1) + sh1[None]
            p_buf[r0:r1, :] = _dot(hn.reshape(r1 - r0, D_MODEL).astype(BF16), win_b[...])
            xc_buf[hist + r0:hist + r1, :] = p_buf[r0:r1, 0:D_LRU]

        def lru_gates(s):
            r0, r1 = s * tsub * nb, (s + 1) * tsub * nb
            for j in range(GATE_BLOCKS):
                sl = slice(j * GATE_BLOCK, (j + 1) * GATE_BLOCK)
                conv_j = cb_ref[:, sl]
                for k in range(CONV_W):
                    conv_j = conv_j + cw_ref[k:k + 1, sl] * xc_buf[k * nb + r0:k * nb + r1, sl]
                a, b = _lru_gate_block(conv_j, wg_b[j], ba_ref[:, sl], bx_ref[:, sl], lam_ref[:, sl])
                a_buf[r0:r1, sl] = a
                b_buf[r0:r1, sl] = b

        def lru_scan():
            h = h_lru[...]
            for t in range(tl):
                h = a_buf[t * nb:(t + 1) * nb, :] * h + b_buf[t * nb:(t + 1) * nb, :]
                b_buf[t * nb:(t + 1) * nb, :] = h
            h_lru[...] = h
            lru_out = b_buf[...] * jax.nn.gelu(p_buf[:, D_LRU:2 * D_LRU])
            m_buf[:, 0:D_LRU] = _rms(lru_out, glru_ref[...]).astype(BF16)

        def s5_input(j):
            ub = p_buf[:, 2 * D_LRU + j * S5_BLOCK_CH:2 * D_LRU + (j + 1) * S5_BLOCK_CH].astype(BF16)
            bu = _dot(ub, wb_b[j])
            hr_buf[j % 2] = bu[:, :S5_BLOCK_LANES]
            hi_buf[j % 2] = bu[:, S5_BLOCK_LANES:]

        def s5_scan(j):
            sl = slice(j * S5_BLOCK_LANES, (j + 1) * S5_BLOCK_LANES)
            hr_j = hr_buf.at[j % 2]
            hi_j = hi_buf.at[j % 2]
            ar = jnp.broadcast_to(abr_s[:, sl], (nb, S5_BLOCK_LANES))
            ai = jnp.broadcast_to(abi_s[:, sl], (nb, S5_BLOCK_LANES))
            hr = h_s5r[:, sl]
            hi = h_s5i[:, sl]
            for t in range(tl):
                rs = slice(t * nb, (t + 1) * nb)
                hr, hi = ar * hr - ai * hi + hr_j[rs, :], ar * hi + ai * hr + hi_j[rs, :]
                hr_j[rs, :] = hr
                hi_j[rs, :] = hi
            h_s5r[:, sl] = hr
            h_s5i[:, sl] = hi

        def s5_readout(j):
            y_buf[:, j * S5_BLOCK_CH:(j + 1) * S5_BLOCK_CH] = _s5_y_block(
                hr_buf[j % 2], hi_buf[j % 2], wc_b, j)

        for s in range(FRONT_SLICES):
            project(s)
        lru_gates(0)
        s5_input(0)
        s5_input(1)
        for s in range(1, FRONT_SLICES):
            lru_gates(s)
        for j in range(S5_BLOCKS):
            s5_scan(j)
            s5_readout(j)
            if j + 2 < S5_BLOCKS:
                s5_input(j + 2)
            if j == 1:
                lru_scan()

        s5_out = _s5_glu(y_buf[...], p_buf[:, 2 * D_LRU:], d_ref[...], wglu_b)
        m_buf[:, D_LRU:] = _rms(s5_out, gs5_ref[...]).astype(BF16)

        mix = _dot(m_buf[...], wout_b[...]).reshape(tl, nb, D_MODEL)
        o_ref[...] = x_buf[slot] + g1[None] * mix

        xc_buf[0:hist, :] = xc_buf[rows:rows + hist, :]

    @pl.when(step == n_chunks)
    def _sample_step():
        for k in range(max(len(bg_pieces) - 2, 0), len(bg_pieces)):
            bg_out(k).wait()
        for k in range(CONV_W - 1):
            convp_o[:, k, :] = xc_buf[k * nb:(k + 1) * nb, :]
        lrup_o[...] = h_lru[...]
        s5rp_o[...] = h_s5r[...].reshape(s5rp_o.shape)
        s5ip_o[...] = h_s5i[...].reshape(s5ip_o.shape)

        ns = xs_ref.shape[0]
        sh1 = mod1s[:, 0:D_MODEL]
        sc1 = mod1s[:, D_MODEL:2 * D_MODEL]
        g1 = mod1s[:, 2 * D_MODEL:3 * D_MODEL]

        x = xs_ref[:, 0, :]
        hn = _rms(x, n1g_ref[...] * (1.0 + sc1)) + sh1
        p = _dot(hn.astype(BF16), win_b[...])
        lru_x = p[:, 0:D_LRU]

        for k in range(CONV_W - 2):
            convs_o[:, k, :] = sconv_ref[:, k + 1, :]
        convs_o[:, CONV_W - 2, :] = lru_x
        for j in range(GATE_BLOCKS):
            sl = slice(j * GATE_BLOCK, (j + 1) * GATE_BLOCK)
            conv_j = cb_ref[:, sl] + cw_ref[CONV_W - 1:CONV_W, sl] * lru_x[:, sl]
            for k in range(CONV_W - 1):
                conv_j = conv_j + cw_ref[k:k + 1, sl] * sconv_ref[:, k, sl]
            a, b = _lru_gate_block(conv_j, wg_b[j], ba_ref[:, sl], bx_ref[:, sl], lam_ref[:, sl])
            lrus_o[:, sl] = a * slru_ref[:, sl] + b
        lru_out = lrus_o[...] * jax.nn.gelu(p[:, D_LRU:2 * D_LRU])
        ms_buf = m_buf.at[pl.ds(0, ns), :]
        ms_buf[:, 0:D_LRU] = _rms(lru_out, glru_ref[...]).astype(BF16)

        u = p[:, 2 * D_LRU:]
        ub = u.astype(BF16)
        ys_buf = y_buf.at[pl.ds(0, ns), :]
        h0r = ss5r_ref[...].reshape(ns, S5_LANES)
        h0i = ss5i_ref[...].reshape(ns, S5_LANES)
        for j in range(S5_BLOCKS):
            sl = slice(j * S5_BLOCK_LANES, (j + 1) * S5_BLOCK_LANES)
            bu_r, bu_i = _s5_bu_block(ub, wb_b, j)
            ar = abr_s[:, sl]
            ai = abi_s[:, sl]
            hr = ar * h0r[:, sl] - ai * h0i[:, sl] + bu_r
            hi = ar * h0i[:, sl] + ai * h0r[:, sl] + bu_i
            s5s_r[:, sl] = hr
            s5s_i[:, sl] = hi
            ys_buf[:, j * S5_BLOCK_CH:(j + 1) * S5_BLOCK_CH] = _s5_y_block(hr, hi, wc_b, j)
        s5rs_o[...] = s5s_r[...].reshape(s5rs_o.shape)
        s5is_o[...] = s5s_i[...].reshape(s5is_o.shape)
        s5_out = _s5_glu(ys_buf[...], u, d_ref[...], wglu_b)
        ms_buf[:, D_LRU:] = _rms(s5_out, gs5_ref[...]).astype(BF16)
        xs_o[...] = x + g1 * _dot(ms_buf[...], wout_b[...])


def _const_spec(shape):
    nd = len(shape)
    return pl.BlockSpec(shape, lambda i, _nd=nd: (0,) * _nd, pipeline_mode=pl.Buffered(1))


def _out_const_spec(shape):
    nd = len(shape)
    return pl.BlockSpec(shape, lambda i, _nd=nd: (0,) * _nd)


_HBM = pl.BlockSpec(memory_space=pl.ANY)


def _mixer(x_prompt, c_prompt, c_sample, ada_w, ada_b, x_s, sconv, slru, ss5r, ss5i, w):
    nb, t_len, _ = x_prompt.shape
    ns = x_s.shape[0]
    tl = TIME_CHUNK
    n_chunks = t_len // tl
    rows = tl * nb
    hist = (CONV_W - 1) * nb
    half_mod = (N_MOD // 2) * D_MODEL

    inputs = [
        (x_prompt, _HBM), (c_prompt, None), (c_sample, None), (ada_w, _HBM), (ada_b, None),
        (x_s, None), (sconv, None), (slru, None), (ss5r, None), (ss5i, None),
        (w["n1g"], None), (w["win"], _HBM), (w["cw"], None), (w["cb"], None), (w["wa"], None), (w["wx"], None),
        (w["ba"], None), (w["bx"], None), (w["lam"], None),
        (w["slr"], None), (w["sli"], None), (w["sldt"], None), (w["bret"], None), (w["bimt"], None),
        (w["cre"], None), (w["cim"], None),
        (w["d"], None), (w["wglu"], _HBM), (w["glru"], None), (w["gs5"], None), (w["wout"], _HBM),
        (w["wgate"], _HBM), (w["wup"], _HBM), (w["wdown"], _HBM),
    ]
    in_specs = [spec if spec is not None else _const_spec(a.shape) for a, spec in inputs]

    out_shapes = [
        ((t_len, nb, D_MODEL), pl.BlockSpec((tl, nb, D_MODEL), lambda i: (jnp.minimum(i, n_chunks - 1), 0, 0))),
        ((nb, half_mod), None), ((ns, half_mod), None),
        ((nb, CONV_W - 1, D_LRU), None), ((nb, D_LRU), None),
        ((nb, S5_GROUPS, S5_STATE), None), ((nb, S5_GROUPS, S5_STATE), None),
        ((ns, D_MODEL), None), ((ns, CONV_W - 1, D_LRU), None), ((ns, D_LRU), None),
        ((ns, S5_GROUPS, S5_STATE), None), ((ns, S5_GROUPS, S5_STATE), None),
    ]
    out_specs = [spec if spec is not None else _out_const_spec(s) for s, spec in out_shapes]
    out_shape = [jax.ShapeDtypeStruct(s, F32) for s, _ in out_shapes]
    for wkey in ("wgate", "wup", "wdown"):
        out_specs.append(_HBM)
        out_shape.append(jax.ShapeDtypeStruct(w[wkey].shape, BF16))

    scratch = [
        pltpu.VMEM((2, tl, nb, D_MODEL), F32),
        pltpu.SemaphoreType.DMA((2, nb)),
        pltpu.VMEM((STAGE_DEPTH, STAGE_ROWS, STAGE_COLS), F32),
        pltpu.SemaphoreType.DMA((STAGE_DEPTH,)),
        pltpu.VMEM((D_MODEL, D_IN), BF16),
        pltpu.VMEM((D_S5, 2 * D_S5), BF16),
        pltpu.VMEM((D_LRU + D_S5, D_MODEL), BF16),
        pltpu.VMEM((GATE_BLOCKS, GATE_BLOCK, 2 * GATE_BLOCK), BF16),
        pltpu.VMEM((S5_BLOCKS, S5_BLOCK_CH, 2 * S5_BLOCK_LANES), BF16),
        pltpu.VMEM((S5_BLOCKS, 2 * S5_BLOCK_LANES, S5_BLOCK_CH), BF16),
        pltpu.VMEM((1, S5_LANES), F32),
        pltpu.VMEM((1, S5_LANES), F32),
        pltpu.VMEM((nb, half_mod), F32),
        pltpu.VMEM((ns, half_mod), F32),
        pltpu.VMEM((hist + rows, D_LRU), F32),
        pltpu.VMEM((nb, D_LRU), F32),
        pltpu.VMEM((nb, S5_LANES), F32),
        pltpu.VMEM((nb, S5_LANES), F32),
        pltpu.VMEM((rows, D_IN), F32),
        pltpu.VMEM((2, rows, S5_BLOCK_LANES), F32),
        pltpu.VMEM((2, rows, S5_BLOCK_LANES), F32),
        pltpu.VMEM((rows, D_LRU), F32),
        pltpu.VMEM((rows, D_LRU), F32),
        pltpu.VMEM((rows, D_S5), F32),
        pltpu.VMEM((rows, D_LRU + D_S5), BF16),
        pltpu.VMEM((ns, S5_LANES), F32),
        pltpu.VMEM((ns, S5_LANES), F32),
        pltpu.VMEM((2, STAGE_ROWS, BG_COLS), BF16),
        pltpu.SemaphoreType.DMA((2,)),
    ]
    return pl.pallas_call(
        functools.partial(_mixer_kernel, n_chunks),
        grid=(n_chunks + 1,),
        in_specs=in_specs,
        out_specs=out_specs,
        out_shape=out_shape,
        scratch_shapes=scratch,
        compiler_params=pltpu.CompilerParams(
            dimension_semantics=("arbitrary",), vmem_limit_bytes=VMEM_LIMIT_BYTES),
        name="mixer",
    )(*[a for a, _ in inputs])


def _ffn_rows(x3, mod_ref, n2g_ref, wgate_ref, wup_ref, wdown_ref, gf_ref, act_ref):
    groups, mrows, _ = x3.shape
    rows = groups * mrows
    sh2 = mod_ref[:, 0:D_MODEL]
    sc2 = mod_ref[:, D_MODEL:2 * D_MODEL]
    g2 = mod_ref[:, 2 * D_MODEL:3 * D_MODEL]

    hn = _rms(x3, (n2g_ref[...] * (1.0 + sc2))[None]) + sh2[None]
    hb = hn.reshape(rows, D_MODEL).astype(BF16)
    for c in range(FFN_CHUNKS):
        cols = slice(c * FFN_CHUNK, (c + 1) * FFN_CHUNK)
        g = _dot(hb, wgate_ref[:, cols])
        up = _dot(hb, wup_ref[:, cols])
        act_ref[:, cols] = (g * jax.nn.sigmoid(g) * up).astype(BF16)
    ffn = _dot(act_ref[...], wdown_ref[...]).reshape(groups, mrows, D_MODEL)
    return _rms(x3 + g2[None] * ffn, gf_ref[...])


def _ffn_kernel(
        n_chunks,
        x_ref, mod2p_ref, mod2s_ref, xs_ref, n2g_ref, wgate_ref, wup_ref, wdown_ref, gf_ref,
        y_hbm, ys_o,
        act_buf, y_buf, y_sems):
    step = pl.program_id(0)
    slot = step % 2

    @pl.when(step < n_chunks)
    def _prompt_step():
        @pl.when(step >= 2)
        def _():
            for c in _batch_major_copies(y_buf, slot, y_hbm, y_sems, step - 2, to_hbm=True):
                c.wait()

        tl, nb, _ = x_ref.shape
        sub = FFN_SUB_ROWS // nb
        for h in range(tl // sub):
            y_buf[slot, pl.ds(h * sub, sub)] = _ffn_rows(
                x_ref[h * sub:(h + 1) * sub], mod2p_ref, n2g_ref, wgate_ref, wup_ref, wdown_ref, gf_ref,
                act_buf.at[pl.ds(h * FFN_SUB_ROWS, FFN_SUB_ROWS), :])
        for c in _batch_major_copies(y_buf, slot, y_hbm, y_sems, step, to_hbm=True):
            c.start()

    @pl.when(step == n_chunks)
    def _sample_step():
        ns = xs_ref.shape[0]
        ys_o[:, 0, :] = _ffn_rows(xs_ref[...][None], mod2s_ref, n2g_ref, wgate_ref, wup_ref, wdown_ref, gf_ref,
                                  act_buf.at[pl.ds(0, ns), :])[0]
        for s in range(max(n_chunks - 2, 0), n_chunks):
            for c in _batch_major_copies(y_buf, s % 2, y_hbm, y_sems, s, to_hbm=True):
                c.wait()


def _ffn(x32_t, mod2p, mod2s, x32_s, wgate_b, wup_b, wdown_b, w, out_shape_p, out_shape_s):
    t_len, nb, _ = x32_t.shape
    tl = FFN_TIME_CHUNK
    n_chunks = t_len // tl
    rows = tl * nb
    inputs = [
        (x32_t, pl.BlockSpec((tl, nb, D_MODEL), lambda i: (jnp.minimum(i, n_chunks - 1), 0, 0))),
        (mod2p, None), (mod2s, None), (x32_s, None), (w["n2g"], None),
        (wgate_b, None), (wup_b, None), (wdown_b, None), (w["gf"], None),
    ]
    in_specs = [spec if spec is not None else _const_spec(a.shape) for a, spec in inputs]
    scratch = [
        pltpu.VMEM((rows, D_FF), BF16),
        pltpu.VMEM((2, tl, nb, D_MODEL), F32),
        pltpu.SemaphoreType.DMA((2, nb)),
    ]
    return pl.pallas_call(
        functools.partial(_ffn_kernel, n_chunks),
        grid=(n_chunks + 1,),
        in_specs=in_specs,
        out_specs=[_HBM, _out_const_spec(out_shape_s)],
        out_shape=[jax.ShapeDtypeStruct(out_shape_p, F32), jax.ShapeDtypeStruct(out_shape_s, F32)],
        scratch_shapes=scratch,
        compiler_params=pltpu.CompilerParams(
            dimension_semantics=("arbitrary",), vmem_limit_bytes=VMEM_LIMIT_BYTES),
        name="ffn",
    )(*[a for a, _ in inputs])


def kernel(x_prompt, x_sample, state_conv, state_lru, state_s5_re, state_s5_im, c_prompt, c_sample,
           ada_w, ada_b, norm1_g, w_in, conv_w, conv_b, lru_wa, lru_ba, lru_wx, lru_bx, lru_lambda,
           s5_lambda_re, s5_lambda_im, s5_log_dt, s5_b_re, s5_b_im, s5_c_re, s5_c_im, s5_d, s5_w_glu,
           g_lru_out, g_s5_out, w_out, norm2_g, ffn_w_gate, ffn_w_up, ffn_w_down, final_norm_g):
    depth = ada_w.shape[0]
    assert depth == 1, "single-layer decoder step"
    bp, t_len, _ = x_prompt.shape
    bs, dec_t, _ = x_sample.shape
    assert bp == SUBLANES and dec_t == 1 and t_len % TIME_CHUNK == 0 and t_len % FFN_TIME_CHUNK == 0
    assert (FFN_TIME_CHUNK * SUBLANES) % FFN_SUB_ROWS == 0 and bs <= FFN_SUB_ROWS

    w = dict(
        n1g=norm1_g, win=w_in[0], cw=conv_w[0], cb=conv_b,
        wa=lru_wa.reshape(D_LRU, LRU_HEAD_DIM), wx=lru_wx.reshape(D_LRU, LRU_HEAD_DIM),
        ba=lru_ba, bx=lru_bx, lam=lru_lambda,
        slr=s5_lambda_re.reshape(1, S5_LANES), sli=s5_lambda_im.reshape(1, S5_LANES),
        sldt=jnp.repeat(s5_log_dt[0], S5_STATE).reshape(1, S5_LANES),
        bret=s5_b_re.reshape(S5_LANES, S5_GROUP).T, bimt=s5_b_im.reshape(S5_LANES, S5_GROUP).T,
        cre=s5_c_re.reshape(D_S5, S5_STATE), cim=s5_c_im.reshape(D_S5, S5_STATE),
        d=s5_d, wglu=s5_w_glu[0], glru=g_lru_out, gs5=g_s5_out, wout=w_out[0],
        n2g=norm2_g, wgate=ffn_w_gate[0], wup=ffn_w_up[0], wdown=ffn_w_down[0],
        gf=final_norm_g.reshape(1, D_MODEL),
    )

    (x32_t, mod2p, mod2s, conv_p, lru_p, s5r_p, s5i_p,
     x32_s, conv_s, lru_s, s5r_s, s5i_s, wgate_b, wup_b, wdown_b) = _mixer(
        x_prompt, c_prompt, c_sample, ada_w[0], ada_b, x_sample, state_conv[0], state_lru[0],
        state_s5_re[0], state_s5_im[0], w)
    y_prompt, y_sample = _ffn(x32_t, mod2p, mod2s, x32_s, wgate_b, wup_b, wdown_b, w,
                              x_prompt.shape, x_sample.shape)

    return (y_prompt, y_sample, conv_p[None], lru_p[None], s5r_p[None], s5i_p[None],
            conv_s[None], lru_s[None], s5r_s[None], s5i_s[None])
```

```python
import functools

import jax
import jax.numpy as jnp
from jax import lax
from jax.experimental import pallas as pl
from jax.experimental.pallas import tpu as pltpu

F32 = jnp.float32
BF16 = jnp.bfloat16
I32 = jnp.int32

D_MODEL = 1024
D_LRU = 512
D_S5 = 512
D_IN = 2 * D_LRU + D_S5
LRU_HEAD_DIM = 64
CONV_W = 4
LRU_C = 8.0
S5_GROUP = 16
S5_GROUPS = D_S5 // S5_GROUP
S5_STATE = 64
S5_LANES = S5_GROUPS * S5_STATE
D_FF = 2816
N_MOD = 6
EPS = 1e-6

SUBLANES = 8
MXU_DIM = 256
VMEM_LIMIT_BYTES = 58 * 1024 * 1024

TIME_CHUNK = 64
FRONT_SLICES = 2
GATE_BLOCK = MXU_DIM
GATE_BLOCKS = D_LRU // GATE_BLOCK
S5_BLOCKS = 4
S5_BLOCK_GROUPS = S5_GROUPS // S5_BLOCKS
S5_BLOCK_CH = D_S5 // S5_BLOCKS
S5_BLOCK_LANES = S5_LANES // S5_BLOCKS
FFN_CHUNK = MXU_DIM
FFN_CHUNKS = D_FF // FFN_CHUNK
STAGE_ROWS = 256
STAGE_COLS = D_IN
STAGE_DEPTH = 3
BG_COLS = D_FF // 2
FFN_TIME_CHUNK = 128
FFN_SUB_ROWS = 512


def _dot(a, b):
    return jnp.dot(a, b, preferred_element_type=F32)


def _rms(x, g):
    return x * lax.rsqrt(jnp.mean(x * x, axis=-1, keepdims=True) + EPS) * g


def _log_sigmoid(x):
    return jnp.minimum(x, 0.0) - jnp.log1p(jnp.exp(-jnp.abs(x)))


def _sigmoid(x):
    return 0.5 * jnp.tanh(0.5 * x) + 0.5


def _sqrt_nonneg(x):
    return jnp.where(x > 0.0, x * lax.rsqrt(x), 0.0)


def _log2(n):
    assert n & (n - 1) == 0
    return n.bit_length() - 1


def _same_group(shape, row_group, col_group):
    r = lax.broadcasted_iota(I32, shape, 0) >> _log2(row_group)
    c = lax.broadcasted_iota(I32, shape, 1) >> _log2(col_group)
    return r == c


def _lane_tiled_block_diag(blk, reps, row_group):
    r, w = blk.shape
    n = w * reps
    sel = (lax.broadcasted_iota(I32, (w, n), 1) & (w - 1)) == lax.broadcasted_iota(I32, (w, n), 0)
    tiled = _dot(blk.astype(BF16), jnp.where(sel, 1.0, 0.0).astype(BF16))
    return jnp.where(_same_group((r, n), row_group, w), tiled, 0.0).astype(BF16)


def _sublane_tiled_block_diag(blk, reps, col_group):
    h, n = blk.shape
    tiled = jnp.broadcast_to(blk[None], (reps, h, n)).reshape(reps * h, n)
    return jnp.where(_same_group((reps * h, n), h, col_group), tiled, 0.0).astype(BF16)


def _lru_gate_block(conv_j, wg_j, ba_j, bx_j, lam_j):
    gz = _dot(conv_j.astype(BF16), wg_j)
    r = _sigmoid(gz[:, :GATE_BLOCK] + ba_j)
    ig = _sigmoid(gz[:, GATE_BLOCK:] + bx_j)
    log_a = (LRU_C * _log_sigmoid(lam_j)) * r
    a = jnp.exp(log_a)
    mult = _sqrt_nonneg(-jnp.tanh(log_a) * (a * a + 1.0))
    return a, mult * (ig * conv_j)


def _s5_bu_block(ub, wb_ref, j):
    bu = _dot(ub[:, j * S5_BLOCK_CH:(j + 1) * S5_BLOCK_CH], wb_ref[j])
    return bu[:, :S5_BLOCK_LANES], bu[:, S5_BLOCK_LANES:]


def _s5_y_block(hr, hi, wc_ref, j):
    return (_dot(hr.astype(BF16), wc_ref[j, :S5_BLOCK_LANES, :])
            + _dot(hi.astype(BF16), wc_ref[j, S5_BLOCK_LANES:, :]))


def _s5_glu(y, u, d, wglu_ref):
    z = _dot(jax.nn.gelu(y + d * u).astype(BF16), wglu_ref[...])
    return z[:, :D_S5] * _sigmoid(z[:, D_S5:])


def _stream_pieces(pieces, stage, sems):
    depth = stage.shape[0]

    def dst(k):
        r, c = pieces[k][0].shape
        return stage.at[k % depth, pl.ds(0, r), pl.ds(0, c)]

    copies = [pltpu.make_async_copy(pieces[k][0], dst(k), sems.at[k % depth]) for k in range(len(pieces))]
    for k in range(min(depth - 1, len(pieces))):
        copies[k].start()
    for k, (_, consume) in enumerate(pieces):
        if k + depth - 1 < len(pieces):
            copies[k + depth - 1].start()
        copies[k].wait()
        consume(dst(k))


def _batch_major_copies(tm_bufs, slot, bm_hbm, sems, chunk, to_hbm):
    _, tl, nb, _ = tm_bufs.shape
    copies = []
    for b in range(nb):
        hbm = bm_hbm.at[b, pl.ds(chunk * tl, tl), :]
        vmem = tm_bufs.at[slot, :, b, :]
        src, dst = (vmem, hbm) if to_hbm else (hbm, vmem)
        copies.append(pltpu.make_async_copy(src, dst, sems.at[slot, b]))
    return copies


def _mixer_kernel(
        n_chunks,
        x_hbm, cp_ref, cs_ref, adaw_hbm, adab_ref,
        xs_ref, sconv_ref, slru_ref, ss5r_ref, ss5i_ref,
        n1g_ref, win_hbm, cw_ref, cb_ref, wa_ref, wx_ref, ba_ref, bx_ref, lam_ref,
        slr_ref, sli_ref, sldt_ref, bret_ref, bimt_ref, cre_ref, cim_ref,
        d_ref, wglu_hbm, glru_ref, gs5_ref, wout_hbm, wgate_hbm, wup_hbm, wdown_hbm,
        o_ref, mod2p_o, mod2s_o, convp_o, lrup_o, s5rp_o, s5ip_o,
        xs_o, convs_o, lrus_o, s5rs_o, s5is_o, wgate_o, wup_o, wdown_o,
        x_buf, x_sems, stage, stage_sems, win_b, wglu_b, wout_b, wg_b, wb_b, wc_b, abr_s, abi_s,
        mod1p, mod1s, xc_buf, h_lru, h_s5r, h_s5i, p_buf, hr_buf, hi_buf, a_buf, b_buf, y_buf, m_buf,
        s5s_r, s5s_i, bg_buf, bg_sems):
    step = pl.program_id(0)
    slot = step % 2
    _, tl, nb, _ = x_buf.shape
    rows = tl * nb
    hist = (CONV_W - 1) * nb
    half_mod = (N_MOD // 2) * D_MODEL

    bg_pieces = []
    for src, dst in ((wgate_hbm, wgate_o), (wup_hbm, wup_o)):
        for rb in range(D_MODEL // STAGE_ROWS):
            for ch in range(D_FF // BG_COLS):
                win = (pl.ds(rb * STAGE_ROWS, STAGE_ROWS), pl.ds(ch * BG_COLS, BG_COLS))
                bg_pieces.append((src.at[win], dst.at[win]))
    for rb in range(D_FF // STAGE_ROWS):
        win = (pl.ds(rb * STAGE_ROWS, STAGE_ROWS), pl.ds(0, D_MODEL))
        bg_pieces.append((wdown_hbm.at[win], wdown_o.at[win]))
    assert len(bg_pieces) < n_chunks

    def bg_in(k):
        r, c = bg_pieces[k][0].shape
        return pltpu.make_async_copy(bg_pieces[k][0], stage.at[k % STAGE_DEPTH, pl.ds(0, r), pl.ds(0, c)],
                                     stage_sems.at[k % STAGE_DEPTH])

    def bg_out(k):
        r, c = bg_pieces[k][0].shape
        return pltpu.make_async_copy(bg_buf.at[k % 2, pl.ds(0, r), pl.ds(0, c)], bg_pieces[k][1],
                                     bg_sems.at[k % 2])

    @pl.when(step == 0)
    def _prologue():
        for c in _batch_major_copies(x_buf, 0, x_hbm, x_sems, 0, to_hbm=False):
            c.start()
        xc_buf[0:hist, :] = jnp.zeros((hist, D_LRU), F32)
        h_lru[...] = jnp.zeros_like(h_lru)
        h_s5r[...] = jnp.zeros_like(h_s5r)
        h_s5i[...] = jnp.zeros_like(h_s5i)

        cp = cp_ref[...]
        cs = cs_ref[...]
        silu_p = (cp * _sigmoid(cp)).astype(BF16)
        silu_s = (cs * _sigmoid(cs)).astype(BF16)
        pieces = []

        def ada_piece(cg, kb):
            cols = slice(cg * STAGE_COLS, (cg + 1) * STAGE_COLS)
            first = cols.start < half_mod
            dst_p, dst_s = (mod1p, mod1s) if first else (mod2p_o, mod2s_o)
            dcols = cols if first else slice(cols.start - half_mod, cols.stop - half_mod)
            krows = slice(kb * STAGE_ROWS, (kb + 1) * STAGE_ROWS)

            def consume(view):
                w = view[...].astype(BF16)
                part_p = _dot(silu_p[:, krows], w)
                part_s = _dot(silu_s[:, krows], w)
                if kb == 0:
                    dst_p[:, dcols] = part_p + adab_ref[:, cols]
                    dst_s[:, dcols] = part_s + adab_ref[:, cols]
                else:
                    dst_p[:, dcols] += part_p
                    dst_s[:, dcols] += part_s
            return adaw_hbm.at[pl.ds(kb * STAGE_ROWS, STAGE_ROWS), pl.ds(cg * STAGE_COLS, STAGE_COLS)], consume

        def cast_piece(src_hbm, dst, rb):
            ncols = dst.shape[1]

            def consume(view):
                dst[rb * STAGE_ROWS:(rb + 1) * STAGE_ROWS, :] = view[...].astype(BF16)
            return src_hbm.at[pl.ds(rb * STAGE_ROWS, STAGE_ROWS), pl.ds(0, ncols)], consume

        for rb in range(win_b.shape[0] // STAGE_ROWS):
            pieces.append(cast_piece(win_hbm, win_b, rb))
        for cg in range(N_MOD * D_MODEL // STAGE_COLS):
            for kb in range(D_MODEL // STAGE_ROWS):
                pieces.append(ada_piece(cg, kb))
        for src_hbm, dst in ((wglu_hbm, wglu_b), (wout_hbm, wout_b)):
            for rb in range(dst.shape[0] // STAGE_ROWS):
                pieces.append(cast_piece(src_hbm, dst, rb))
        _stream_pieces(pieces, stage, stage_sems)
        bg_in(0).start()

        for j in range(GATE_BLOCKS):
            rs = slice(j * GATE_BLOCK, (j + 1) * GATE_BLOCK)
            reps = GATE_BLOCK // LRU_HEAD_DIM
            wg_b[j, :, :GATE_BLOCK] = _lane_tiled_block_diag(wa_ref[rs, :], reps, LRU_HEAD_DIM)
            wg_b[j, :, GATE_BLOCK:] = _lane_tiled_block_diag(wx_ref[rs, :], reps, LRU_HEAD_DIM)

        lr = slr_ref[...]
        li = sli_ref[...]
        dt = jnp.exp(sldt_ref[...])
        mag = jnp.exp(lr * dt)
        ab_r = mag * jnp.cos(li * dt)
        ab_i = mag * jnp.sin(li * dt)
        den = lr * lr + li * li
        fr = ((ab_r - 1.0) * lr + ab_i * li) / den
        fi = (ab_i * lr - (ab_r - 1.0) * li) / den
        abr_s[...] = ab_r
        abi_s[...] = ab_i
        br = bret_ref[...]
        bi = bimt_ref[...]
        bb_r = fr * br - fi * bi
        bb_i = fr * bi + fi * br
        for j in range(S5_BLOCKS):
            sl = slice(j * S5_BLOCK_LANES, (j + 1) * S5_BLOCK_LANES)
            wb_b[j, :, :S5_BLOCK_LANES] = _sublane_tiled_block_diag(bb_r[:, sl], S5_BLOCK_GROUPS, S5_STATE)
            wb_b[j, :, S5_BLOCK_LANES:] = _sublane_tiled_block_diag(bb_i[:, sl], S5_BLOCK_GROUPS, S5_STATE)
            cs_rows = slice(j * S5_BLOCK_CH, (j + 1) * S5_BLOCK_CH)
            ct_r = _lane_tiled_block_diag(cre_ref[cs_rows, :], S5_BLOCK_GROUPS, S5_GROUP)
            ct_i = _lane_tiled_block_diag(-cim_ref[cs_rows, :], S5_BLOCK_GROUPS, S5_GROUP)
            wc_b[j, :S5_BLOCK_LANES, :] = ct_r.astype(F32).T.astype(BF16)
            wc_b[j, S5_BLOCK_LANES:, :] = ct_i.astype(F32).T.astype(BF16)

    @pl.when(step < n_chunks)
    def _prompt_step():
        @pl.when(step + 1 < n_chunks)
        def _():
            for c in _batch_major_copies(x_buf, 1 - slot, x_hbm, x_sems, step + 1, to_hbm=False):
                c.start()

        for c in _batch_major_copies(x_buf, slot, x_hbm, x_sems, step, to_hbm=False):
            c.wait()

        sh1 = mod1p[:, 0:D_MODEL]
        sc1 = mod1p[:, D_MODEL:2 * D_MODEL]
        g1 = mod1p[:, 2 * D_MODEL:3 * D_MODEL]

        scale1 = (n1g_ref[...] * (1.0 + sc1))[None]
        tsub = tl // FRONT_SLICES

        def project(s):
            r0, r1 = s * tsub * nb, (s + 1) * tsub * nb
            xs3 = x_buf[slot, pl.ds(s * tsub, tsub)]
            hn = _rms(xs3, scale1) + sh1[None]
            p_buf[r0:r1, :] = _dot(hn.reshape(r1 - r0, D_MODEL).astype(BF16), win_b[...])
            xc_buf[hist + r0:hist + r1, :] = p_buf[r0:r1, 0:D_LRU]

        def lru_gates(s):
            r0, r1 = s * tsub * nb, (s + 1) * tsub * nb
            for j in range(GATE_BLOCKS):
                sl = slice(j * GATE_BLOCK, (j + 1) * GATE_BLOCK)
                conv_j = cb_ref[:, sl]
                for k in range(CONV_W):
                    conv_j = conv_j + cw_ref[k:k + 1, sl] * xc_buf[k * nb + r0:k * nb + r1, sl]
                a, b = _lru_gate_block(conv_j, wg_b[j], ba_ref[:, sl], bx_ref[:, sl], lam_ref[:, sl])
                a_buf[r0:r1, sl] = a
                b_buf[r0:r1, sl] = b

        def lru_scan():
            h = h_lru[...]
            for t in range(tl):
                h = a_buf[t * nb:(t + 1) * nb, :] * h + b_buf[t * nb:(t + 1) * nb, :]
                b_buf[t * nb:(t + 1) * nb, :] = h
            h_lru[...] = h
            lru_out = b_buf[...] * jax.nn.gelu(p_buf[:, D_LRU:2 * D_LRU])
            m_buf[:, 0:D_LRU] = _rms(lru_out, glru_ref[...]).astype(BF16)

        def s5_input(j):
            ub = p_buf[:, 2 * D_LRU + j * S5_BLOCK_CH:2 * D_LRU + (j + 1) * S5_BLOCK_CH].astype(BF16)
            bu = _dot(ub, wb_b[j])
            hr_buf[j % 2] = bu[:, :S5_BLOCK_LANES]
            hi_buf[j % 2] = bu[:, S5_BLOCK_LANES:]

        def s5_scan(j):
            sl = slice(j * S5_BLOCK_LANES, (j + 1) * S5_BLOCK_LANES)
            hr_j = hr_buf.at[j % 2]
            hi_j = hi_buf.at[j % 2]
            ar = jnp.broadcast_to(abr_s[:, sl], (nb, S5_BLOCK_LANES))
            ai = jnp.broadcast_to(abi_s[:, sl], (nb, S5_BLOCK_LANES))
            hr = h_s5r[:, sl]
            hi = h_s5i[:, sl]
            for t in range(tl):
                rs = slice(t * nb, (t + 1) * nb)
                hr, hi = ar * hr - ai * hi + hr_j[rs, :], ar * hi + ai * hr + hi_j[rs, :]
                hr_j[rs, :] = hr
                hi_j[rs, :] = hi
            h_s5r[:, sl] = hr
            h_s5i[:, sl] = hi

        def s5_readout(j):
            y_buf[:, j * S5_BLOCK_CH:(j + 1) * S5_BLOCK_CH] = _s5_y_block(
                hr_buf[j % 2], hi_buf[j % 2], wc_b, j)

        for s in range(FRONT_SLICES):
            project(s)
        lru_gates(0)
        s5_input(0)
        s5_input(1)
        for s in range(1, FRONT_SLICES):
            lru_gates(s)
        for j in range(S5_BLOCKS):
            s5_scan(j)
            s5_readout(j)
            if j + 2 < S5_BLOCKS:
                s5_input(j + 2)
            if j == 1:
                lru_scan()

        s5_out = _s5_glu(y_buf[...], p_buf[:, 2 * D_LRU:], d_ref[...], wglu_b)
        m_buf[:, D_LRU:] = _rms(s5_out, gs5_ref[...]).astype(BF16)

        mix = _dot(m_buf[...], wout_b[...]).reshape(tl, nb, D_MODEL)
        o_ref[...] = x_buf[slot] + g1[None] * mix

        xc_buf[0:hist, :] = xc_buf[rows:rows + hist, :]

    for k in range(len(bg_pieces)):
        @pl.when(step == k + 1)
        def _bg_task(k=k):
            r, c = bg_pieces[k][0].shape
            bg_in(k).wait()
            if k >= 2:
                bg_out(k - 2).wait()
            bg_buf[k % 2, 0:r, 0:c] = stage[k % STAGE_DEPTH, 0:r, 0:c].astype(BF16)
            bg_out(k).start()
            if k + 1 < len(bg_pieces):
                bg_in(k + 1).start()

    @pl.when(step == n_chunks)
    def _sample_step():
        for k in range(max(len(bg_pieces) - 2, 0), len(bg_pieces)):
            bg_out(k).wait()
        for k in range(CONV_W - 1):
            convp_o[:, k, :] = xc_buf[k * nb:(k + 1) * nb, :]
        lrup_o[...] = h_lru[...]
        s5rp_o[...] = h_s5r[...].reshape(s5rp_o.shape)
        s5ip_o[...] = h_s5i[...].reshape(s5ip_o.shape)

        ns = xs_ref.shape[0]
        sh1 = mod1s[:, 0:D_MODEL]
        sc1 = mod1s[:, D_MODEL:2 * D_MODEL]
        g1 = mod1s[:, 2 * D_MODEL:3 * D_MODEL]

        x = xs_ref[:, 0, :]
        hn = _rms(x, n1g_ref[...] * (1.0 + sc1)) + sh1
        p = _dot(hn.astype(BF16), win_b[...])
        lru_x = p[:, 0:D_LRU]

        for k in range(CONV_W - 2):
            convs_o[:, k, :] = sconv_ref[:, k + 1, :]
        convs_o[:, CONV_W - 2, :] = lru_x
        for j in range(GATE_BLOCKS):
            sl = slice(j * GATE_BLOCK, (j + 1) * GATE_BLOCK)
            conv_j = cb_ref[:, sl] + cw_ref[CONV_W - 1:CONV_W, sl] * lru_x[:, sl]
            for k in range(CONV_W - 1):
                conv_j = conv_j + cw_ref[k:k + 1, sl] * sconv_ref[:, k, sl]
            a, b = _lru_gate_block(conv_j, wg_b[j], ba_ref[:, sl], bx_ref[:, sl], lam_ref[:, sl])
            lrus_o[:, sl] = a * slru_ref[:, sl] + b
        lru_out = lrus_o[...] * jax.nn.gelu(p[:, D_LRU:2 * D_LRU])
        ms_buf = m_buf.at[pl.ds(0, ns), :]
        ms_buf[:, 0:D_LRU] = _rms(lru_out, glru_ref[...]).astype(BF16)

        u = p[:, 2 * D_LRU:]
        ub = u.astype(BF16)
        ys_buf = y_buf.at[pl.ds(0, ns), :]
        h0r = ss5r_ref[...].reshape(ns, S5_LANES)
        h0i = ss5i_ref[...].reshape(ns, S5_LANES)
        for j in range(S5_BLOCKS):
            sl = slice(j * S5_BLOCK_LANES, (j + 1) * S5_BLOCK_LANES)
            bu_r, bu_i = _s5_bu_block(ub, wb_b, j)
            ar = abr_s[:, sl]
            ai = abi_s[:, sl]
            hr = ar * h0r[:, sl] - ai * h0i[:, sl] + bu_r
            hi = ar * h0i[:, sl] + ai * h0r[:, sl] + bu_i
            s5s_r[:, sl] = hr
            s5s_i[:, sl] = hi
            ys_buf[:, j * S5_BLOCK_CH:(j + 1) * S5_BLOCK_CH] = _s5_y_block(hr, hi, wc_b, j)
        s5rs_o[...] = s5s_r[...].reshape(s5rs_o.shape)
        s5is_o[...] = s5s_i[...].reshape(s5is_o.shape)
        s5_out = _s5_glu(ys_buf[...], u, d_ref[...], wglu_b)
        ms_buf[:, D_LRU:] = _rms(s5_out, gs5_ref[...]).astype(BF16)
        xs_o[...] = x + g1 * _dot(ms_buf[...], wout_b[...])


def _const_spec(shape):
    nd = len(shape)
    return pl.BlockSpec(shape, lambda i, _nd=nd: (0,) * _nd, pipeline_mode=pl.Buffered(1))


def _out_const_spec(shape):
    nd = len(shape)
    return pl.BlockSpec(shape, lambda i, _nd=nd: (0,) * _nd)


_HBM = pl.BlockSpec(memory_space=pl.ANY)


def _mixer(x_prompt, c_prompt, c_sample, ada_w, ada_b, x_s, sconv, slru, ss5r, ss5i, w):
    nb, t_len, _ = x_prompt.shape
    ns = x_s.shape[0]
    tl = TIME_CHUNK
    n_chunks = t_len // tl
    rows = tl * nb
    hist = (CONV_W - 1) * nb
    half_mod = (N_MOD // 2) * D_MODEL

    inputs = [
        (x_prompt, _HBM), (c_prompt, None), (c_sample, None), (ada_w, _HBM), (ada_b, None),
        (x_s, None), (sconv, None), (slru, None), (ss5r, None), (ss5i, None),
        (w["n1g"], None), (w["win"], _HBM), (w["cw"], None), (w["cb"], None), (w["wa"], None), (w["wx"], None),
        (w["ba"], None), (w["bx"], None), (w["lam"], None),
        (w["slr"], None), (w["sli"], None), (w["sldt"], None), (w["bret"], None), (w["bimt"], None),
        (w["cre"], None), (w["cim"], None),
        (w["d"], None), (w["wglu"], _HBM), (w["glru"], None), (w["gs5"], None), (w["wout"], _HBM),
        (w["wgate"], _HBM), (w["wup"], _HBM), (w["wdown"], _HBM),
    ]
    in_specs = [spec if spec is not None else _const_spec(a.shape) for a, spec in inputs]

    out_shapes = [
        ((t_len, nb, D_MODEL), pl.BlockSpec((tl, nb, D_MODEL), lambda i: (jnp.minimum(i, n_chunks - 1), 0, 0))),
        ((nb, half_mod), None), ((ns, half_mod), None),
        ((nb, CONV_W - 1, D_LRU), None), ((nb, D_LRU), None),
        ((nb, S5_GROUPS, S5_STATE), None), ((nb, S5_GROUPS, S5_STATE), None),
        ((ns, D_MODEL), None), ((ns, CONV_W - 1, D_LRU), None), ((ns, D_LRU), None),
        ((ns, S5_GROUPS, S5_STATE), None), ((ns, S5_GROUPS, S5_STATE), None),
    ]
    out_specs = [spec if spec is not None else _out_const_spec(s) for s, spec in out_shapes]
    out_shape = [jax.ShapeDtypeStruct(s, F32) for s, _ in out_shapes]
    for wkey in ("wgate", "wup", "wdown"):
        out_specs.append(_HBM)
        out_shape.append(jax.ShapeDtypeStruct(w[wkey].shape, BF16))

    scratch = [
        pltpu.VMEM((2, tl, nb, D_MODEL), F32),
        pltpu.SemaphoreType.DMA((2, nb)),
        pltpu.VMEM((STAGE_DEPTH, STAGE_ROWS, STAGE_COLS), F32),
        pltpu.SemaphoreType.DMA((STAGE_DEPTH,)),
        pltpu.VMEM((D_MODEL, D_IN), BF16),
        pltpu.VMEM((D_S5, 2 * D_S5), BF16),
        pltpu.VMEM((D_LRU + D_S5, D_MODEL), BF16),
        pltpu.VMEM((GATE_BLOCKS, GATE_BLOCK, 2 * GATE_BLOCK), BF16),
        pltpu.VMEM((S5_BLOCKS, S5_BLOCK_CH, 2 * S5_BLOCK_LANES), BF16),
        pltpu.VMEM((S5_BLOCKS, 2 * S5_BLOCK_LANES, S5_BLOCK_CH), BF16),
        pltpu.VMEM((1, S5_LANES), F32),
        pltpu.VMEM((1, S5_LANES), F32),
        pltpu.VMEM((nb, half_mod), F32),
        pltpu.VMEM((ns, half_mod), F32),
        pltpu.VMEM((hist + rows, D_LRU), F32),
        pltpu.VMEM((nb, D_LRU), F32),
        pltpu.VMEM((nb, S5_LANES), F32),
        pltpu.VMEM((nb, S5_LANES), F32),
        pltpu.VMEM((rows, D_IN), F32),
        pltpu.VMEM((2, rows, S5_BLOCK_LANES), F32),
        pltpu.VMEM((2, rows, S5_BLOCK_LANES), F32),
        pltpu.VMEM((rows, D_LRU), F32),
        pltpu.VMEM((rows, D_LRU), F32),
        pltpu.VMEM((rows, D_S5), F32),
        pltpu.VMEM((rows, D_LRU + D_S5), BF16),
        pltpu.VMEM((ns, S5_LANES), F32),
        pltpu.VMEM((ns, S5_LANES), F32),
        pltpu.VMEM((2, STAGE_ROWS, BG_COLS), BF16),
        pltpu.SemaphoreType.DMA((2,)),
    ]
    return pl.pallas_call(
        functools.partial(_mixer_kernel, n_chunks),
        grid=(n_chunks + 1,),
        in_specs=in_specs,
        out_specs=out_specs,
        out_shape=out_shape,
        scratch_shapes=scratch,
        compiler_params=pltpu.CompilerParams(
            dimension_semantics=("arbitrary",), vmem_limit_bytes=VMEM_LIMIT_BYTES),
        name="mixer",
    )(*[a for a, _ in inputs])


def _ffn_rows(x3, mod_ref, n2g_ref, wgate_ref, wup_ref, wdown_ref, gf_ref, act_ref):
    groups, mrows, _ = x3.shape
    rows = groups * mrows
    sh2 = mod_ref[:, 0:D_MODEL]
    sc2 = mod_ref[:, D_MODEL:2 * D_MODEL]
    g2 = mod_ref[:, 2 * D_MODEL:3 * D_MODEL]

    hn = _rms(x3, (n2g_ref[...] * (1.0 + sc2))[None]) + sh2[None]
    hb = hn.reshape(rows, D_MODEL).astype(BF16)
    for c in range(FFN_CHUNKS):
        cols = slice(c * FFN_CHUNK, (c + 1) * FFN_CHUNK)
        g = _dot(hb, wgate_ref[:, cols])
        up = _dot(hb, wup_ref[:, cols])
        act_ref[:, cols] = (g * jax.nn.sigmoid(g) * up).astype(BF16)
    ffn = _dot(act_ref[...], wdown_ref[...]).reshape(groups, mrows, D_MODEL)
    return _rms(x3 + g2[None] * ffn, gf_ref[...])


def _ffn_kernel(
        n_chunks,
        x_ref, mod2p_ref, mod2s_ref, xs_ref, n2g_ref, wgate_ref, wup_ref, wdown_ref, gf_ref,
        y_hbm, ys_o,
        act_buf, y_buf, y_sems):
    step = pl.program_id(0)
    slot = step % 2

    @pl.when(step < n_chunks)
    def _prompt_step():
        @pl.when(step >= 2)
        def _():
            for c in _batch_major_copies(y_buf, slot, y_hbm, y_sems, step - 2, to_hbm=True):
                c.wait()

        tl, nb, _ = x_ref.shape
        sub = FFN_SUB_ROWS // nb
        for h in range(tl // sub):
            y_buf[slot, pl.ds(h * sub, sub)] = _ffn_rows(
                x_ref[h * sub:(h + 1) * sub], mod2p_ref, n2g_ref, wgate_ref, wup_ref, wdown_ref, gf_ref,
                act_buf.at[pl.ds(h * FFN_SUB_ROWS, FFN_SUB_ROWS), :])
        for c in _batch_major_copies(y_buf, slot, y_hbm, y_sems, step, to_hbm=True):
            c.start()

    @pl.when(step == n_chunks)
    def _sample_step():
        ns = xs_ref.shape[0]
        ys_o[:, 0, :] = _ffn_rows(xs_ref[...][None], mod2s_ref, n2g_ref, wgate_ref, wup_ref, wdown_ref, gf_ref,
                                  act_buf.at[pl.ds(0, ns), :])[0]
        for s in range(max(n_chunks - 2, 0), n_chunks):
            for c in _batch_major_copies(y_buf, s % 2, y_hbm, y_sems, s, to_hbm=True):
                c.wait()


def _ffn(x32_t, mod2p, mod2s, x32_s, wgate_b, wup_b, wdown_b, w, out_shape_p, out_shape_s):
    t_len, nb, _ = x32_t.shape
    tl = FFN_TIME_CHUNK
    n_chunks = t_len // tl
    rows = tl * nb
    inputs = [
        (x32_t, pl.BlockSpec((tl, nb, D_MODEL), lambda i: (jnp.minimum(i, n_chunks - 1), 0, 0))),
        (mod2p, None), (mod2s, None), (x32_s, None), (w["n2g"], None),
        (wgate_b, None), (wup_b, None), (wdown_b, None), (w["gf"], None),
    ]
    in_specs = [spec if spec is not None else _const_spec(a.shape) for a, spec in inputs]
    scratch = [
        pltpu.VMEM((rows, D_FF), BF16),
        pltpu.VMEM((2, tl, nb, D_MODEL), F32),
        pltpu.SemaphoreType.DMA((2, nb)),
    ]
    return pl.pallas_call(
        functools.partial(_ffn_kernel, n_chunks),
        grid=(n_chunks + 1,),
        in_specs=in_specs,
        out_specs=[_HBM, _out_const_spec(out_shape_s)],
        out_shape=[jax.ShapeDtypeStruct(out_shape_p, F32), jax.ShapeDtypeStruct(out_shape_s, F32)],
        scratch_shapes=scratch,
        compiler_params=pltpu.CompilerParams(
            dimension_semantics=("arbitrary",), vmem_limit_bytes=VMEM_LIMIT_BYTES),
        name="ffn",
    )(*[a for a, _ in inputs])


def kernel(x_prompt, x_sample, state_conv, state_lru, state_s5_re, state_s5_im, c_prompt, c_sample,
           ada_w, ada_b, norm1_g, w_in, conv_w, conv_b, lru_wa, lru_ba, lru_wx, lru_bx, lru_lambda,
           s5_lambda_re, s5_lambda_im, s5_log_dt, s5_b_re, s5_b_im, s5_c_re, s5_c_im, s5_d, s5_w_glu,
           g_lru_out, g_s5_out, w_out, norm2_g, ffn_w_gate, ffn_w_up, ffn_w_down, final_norm_g):
    depth = ada_w.shape[0]
    assert depth == 1, "single-layer decoder step"
    bp, t_len, _ = x_prompt.shape
    bs, dec_t, _ = x_sample.shape
    assert bp == SUBLANES and dec_t == 1 and t_len % TIME_CHUNK == 0 and t_len % FFN_TIME_CHUNK == 0
    assert (FFN_TIME_CHUNK * SUBLANES) % FFN_SUB_ROWS == 0 and bs <= FFN_SUB_ROWS

    w = dict(
        n1g=norm1_g, win=w_in[0], cw=conv_w[0], cb=conv_b,
        wa=lru_wa.reshape(D_LRU, LRU_HEAD_DIM), wx=lru_wx.reshape(D_LRU, LRU_HEAD_DIM),
        ba=lru_ba, bx=lru_bx, lam=lru_lambda,
        slr=s5_lambda_re.reshape(1, S5_LANES), sli=s5_lambda_im.reshape(1, S5_LANES),
        sldt=jnp.repeat(s5_log_dt[0], S5_STATE).reshape(1, S5_LANES),
        bret=s5_b_re.reshape(S5_LANES, S5_GROUP).T, bimt=s5_b_im.reshape(S5_LANES, S5_GROUP).T,
        cre=s5_c_re.reshape(D_S5, S5_STATE), cim=s5_c_im.reshape(D_S5, S5_STATE),
        d=s5_d, wglu=s5_w_glu[0], glru=g_lru_out, gs5=g_s5_out, wout=w_out[0],
        n2g=norm2_g, wgate=ffn_w_gate[0], wup=ffn_w_up[0], wdown=ffn_w_down[0],
        gf=final_norm_g.reshape(1, D_MODEL),
    )

    (x32_t, mod2p, mod2s, conv_p, lru_p, s5r_p, s5i_p,
     x32_s, conv_s, lru_s, s5r_s, s5i_s, wgate_b, wup_b, wdown_b) = _mixer(
        x_prompt, c_prompt, c_sample, ada_w[0], ada_b, x_sample, state_conv[0], state_lru[0],
        state_s5_re[0], state_s5_im[0], w)
    y_prompt, y_sample = _ffn(x32_t, mod2p, mod2s, x32_s, wgate_b, wup_b, wdown_b, w,
                              x_prompt.shape, x_sample.shape)

    return (y_prompt, y_sample, conv_p[None], lru_p[None], s5r_p[None], s5i_p[None],
            conv_s[None], lru_s[None], s5r_s[None], s5i_s[None])
```

```python
import functools

import jax
import jax.numpy as jnp
from jax import lax
from jax.experimental import pallas as pl
from jax.experimental.pallas import tpu as pltpu

F32 = jnp.float32
BF16 = jnp.bfloat16
I32 = jnp.int32

D_MODEL = 1024
D_LRU = 512
D_S5 = 512
D_IN = 2 * D_LRU + D_S5
LRU_HEAD_DIM = 64
CONV_W = 4
LRU_C = 8.0
S5_GROUP = 16
S5_GROUPS = D_S5 // S5_GROUP
S5_STATE = 64
S5_LANES = S5_GROUPS * S5_STATE
D_FF = 2816
N_MOD = 6
EPS = 1e-6

SUBLANES = 8
MXU_DIM = 256
VMEM_LIMIT_BYTES = 60 * 1024 * 1024

TIME_CHUNK = 64
FRONT_SLICES = 2
GATE_BLOCK = MXU_DIM
GATE_BLOCKS = D_LRU // GATE_BLOCK
S5_BLOCKS = 4
S5_BLOCK_GROUPS = S5_GROUPS // S5_BLOCKS
S5_BLOCK_CH = D_S5 // S5_BLOCKS
S5_BLOCK_LANES = S5_LANES // S5_BLOCKS
FFN_CHUNK = MXU_DIM
FFN_CHUNKS = D_FF // FFN_CHUNK
STAGE_ROWS = 256
STAGE_COLS = D_IN
STAGE_DEPTH = 3
BG_COLS = D_FF // 2
FFN_TIME_CHUNK = 128
FFN_SUB_ROWS = 512


def _dot(a, b):
    return jnp.dot(a, b, preferred_element_type=F32)


def _rms(x, g):
    return x * lax.rsqrt(jnp.mean(x * x, axis=-1, keepdims=True) + EPS) * g


def _log_sigmoid(x):
    return jnp.minimum(x, 0.0) - jnp.log1p(jnp.exp(-jnp.abs(x)))


def _sigmoid(x):
    return 0.5 * jnp.tanh(0.5 * x) + 0.5


def _sqrt_nonneg(x):
    return jnp.where(x > 0.0, x * lax.rsqrt(x), 0.0)


def _log2(n):
    assert n & (n - 1) == 0
    return n.bit_length() - 1


def _same_group(shape, row_group, col_group):
    r = lax.broadcasted_iota(I32, shape, 0) >> _log2(row_group)
    c = lax.broadcasted_iota(I32, shape, 1) >> _log2(col_group)
    return r == c


def _lane_tiled_block_diag(blk, reps, row_group):
    r, w = blk.shape
    n = w * reps
    sel = (lax.broadcasted_iota(I32, (w, n), 1) & (w - 1)) == lax.broadcasted_iota(I32, (w, n), 0)
    tiled = _dot(blk.astype(BF16), jnp.where(sel, 1.0, 0.0).astype(BF16))
    return jnp.where(_same_group((r, n), row_group, w), tiled, 0.0).astype(BF16)


def _sublane_tiled_block_diag(blk, reps, col_group):
    h, n = blk.shape
    tiled = jnp.broadcast_to(blk[None], (reps, h, n)).reshape(reps * h, n)
    return jnp.where(_same_group((reps * h, n), h, col_group), tiled, 0.0).astype(BF16)


def _lru_gate_block(conv_j, wg_j, ba_j, bx_j, lam_j):
    gz = _dot(conv_j.astype(BF16), wg_j)
    r = _sigmoid(gz[:, :GATE_BLOCK] + ba_j)
    ig = _sigmoid(gz[:, GATE_BLOCK:] + bx_j)
    log_a = (LRU_C * _log_sigmoid(lam_j)) * r
    a = jnp.exp(log_a)
    mult = _sqrt_nonneg(-jnp.tanh(log_a) * (a * a + 1.0))
    return a, mult * (ig * conv_j)


def _s5_bu_block(ub, wb_ref, j):
    bu = _dot(ub[:, j * S5_BLOCK_CH:(j + 1) * S5_BLOCK_CH], wb_ref[j])
    return bu[:, :S5_BLOCK_LANES], bu[:, S5_BLOCK_LANES:]


def _s5_y_block(hr, hi, wc_ref, j):
    return (_dot(hr.astype(BF16), wc_ref[j, :S5_BLOCK_LANES, :])
            + _dot(hi.astype(BF16), wc_ref[j, S5_BLOCK_LANES:, :]))


def _s5_glu(y, u, d, wglu_ref):
    z = _dot(jax.nn.gelu(y + d * u).astype(BF16), wglu_ref[...])
    return z[:, :D_S5] * _sigmoid(z[:, D_S5:])


def _stream_pieces(pieces, stage, sems):
    depth = stage.shape[0]

    def dst(k):
        r, c = pieces[k][0].shape
        return stage.at[k % depth, pl.ds(0, r), pl.ds(0, c)]

    copies = [pltpu.make_async_copy(pieces[k][0], dst(k), sems.at[k % depth]) for k in range(len(pieces))]
    for k in range(min(depth - 1, len(pieces))):
        copies[k].start()
    for k, (_, consume) in enumerate(pieces):
        if k + depth - 1 < len(pieces):
            copies[k + depth - 1].start()
        copies[k].wait()
        consume(dst(k))


def _batch_major_copies(tm_bufs, slot, bm_hbm, sems, chunk, to_hbm):
    _, tl, nb, _ = tm_bufs.shape
    copies = []
    for b in range(nb):
        hbm = bm_hbm.at[b, pl.ds(chunk * tl, tl), :]
        vmem = tm_bufs.at[slot, :, b, :]
        src, dst = (vmem, hbm) if to_hbm else (hbm, vmem)
        copies.append(pltpu.make_async_copy(src, dst, sems.at[slot, b]))
    return copies


def _mixer_kernel(
        n_chunks,
        x_hbm, cp_ref, cs_ref, adaw_hbm, adab_ref,
        xs_ref, sconv_ref, slru_ref, ss5r_ref, ss5i_ref,
        n1g_ref, win_hbm, cw_ref, cb_ref, wa_ref, wx_ref, ba_ref, bx_ref, lam_ref,
        slr_ref, sli_ref, sldt_ref, bret_ref, bimt_ref, cre_ref, cim_ref,
        d_ref, wglu_hbm, glru_ref, gs5_ref, wout_hbm, wgate_hbm, wup_hbm, wdown_hbm,
        o_ref, mod2p_o, mod2s_o, convp_o, lrup_o, s5rp_o, s5ip_o,
        xs_o, convs_o, lrus_o, s5rs_o, s5is_o, wgate_o, wup_o, wdown_o,
        x_buf, x_sems, stage, stage_sems, win_b, wglu_b, wout_b, wg_b, wb_b, wc_b, abr_s, abi_s,
        mod1p, mod1s, xc_buf, h_lru, h_s5r, h_s5i, p_buf, hr_buf, hi_buf, a_buf, b_buf, y_buf, m_buf,
        bg_buf, bg_sems):
    step = pl.program_id(0)
    slot = step % 2
    _, tl, nb, _ = x_buf.shape
    rows = tl * nb
    hist = (CONV_W - 1) * nb
    half_mod = (N_MOD // 2) * D_MODEL

    bg_pieces = []
    for src, dst in ((wgate_hbm, wgate_o), (wup_hbm, wup_o)):
        for rb in range(D_MODEL // STAGE_ROWS):
            for ch in range(D_FF // BG_COLS):
                win = (pl.ds(rb * STAGE_ROWS, STAGE_ROWS), pl.ds(ch * BG_COLS, BG_COLS))
                bg_pieces.append((src.at[win], dst.at[win]))
    for rb in range(D_FF // STAGE_ROWS):
        win = (pl.ds(rb * STAGE_ROWS, STAGE_ROWS), pl.ds(0, D_MODEL))
        bg_pieces.append((wdown_hbm.at[win], wdown_o.at[win]))
    assert len(bg_pieces) < n_chunks

    def bg_in(k):
        r, c = bg_pieces[k][0].shape
        return pltpu.make_async_copy(bg_pieces[k][0], stage.at[k % STAGE_DEPTH, pl.ds(0, r), pl.ds(0, c)],
                                     stage_sems.at[k % STAGE_DEPTH])

    def bg_out(k):
        r, c = bg_pieces[k][0].shape
        return pltpu.make_async_copy(bg_buf.at[k % 2, pl.ds(0, r), pl.ds(0, c)], bg_pieces[k][1],
                                     bg_sems.at[k % 2])

    @pl.when(step == 0)
    def _prologue():
        for c in _batch_major_copies(x_buf, 0, x_hbm, x_sems, 0, to_hbm=False):
            c.start()
        xc_buf[0:hist, :] = jnp.zeros((hist, D_LRU), F32)
        h_lru[...] = jnp.zeros_like(h_lru)
        h_s5r[...] = jnp.zeros_like(h_s5r)
        h_s5i[...] = jnp.zeros_like(h_s5i)

        cp = cp_ref[...]
        cs = cs_ref[...]
        silu_p = (cp * _sigmoid(cp)).astype(BF16)
        silu_s = (cs * _sigmoid(cs)).astype(BF16)
        pieces = []

        def ada_piece(cg, kb):
            cols = slice(cg * STAGE_COLS, (cg + 1) * STAGE_COLS)
            first = cols.start < half_mod
            dst_p, dst_s = (mod1p, mod1s) if first else (mod2p_o, mod2s_o)
            dcols = cols if first else slice(cols.start - half_mod, cols.stop - half_mod)
            krows = slice(kb * STAGE_ROWS, (kb + 1) * STAGE_ROWS)

            def consume(view):
                w = view[...].astype(BF16)
                part_p = _dot(silu_p[:, krows], w)
                part_s = _dot(silu_s[:, krows], w)
                if kb == 0:
                    dst_p[:, dcols] = part_p + adab_ref[:, cols]
                    dst_s[:, dcols] = part_s + adab_ref[:, cols]
                else:
                    dst_p[:, dcols] += part_p
                    dst_s[:, dcols] += part_s
            return adaw_hbm.at[pl.ds(kb * STAGE_ROWS, STAGE_ROWS), pl.ds(cg * STAGE_COLS, STAGE_COLS)], consume

        def cast_piece(src_hbm, dst, rb):
            ncols = dst.shape[1]

            def consume(view):
                dst[rb * STAGE_ROWS:(rb + 1) * STAGE_ROWS, :] = view[...].astype(BF16)
            return src_hbm.at[pl.ds(rb * STAGE_ROWS, STAGE_ROWS), pl.ds(0, ncols)], consume

        for rb in range(win_b.shape[0] // STAGE_ROWS):
            pieces.append(cast_piece(win_hbm, win_b, rb))
        for cg in range(N_MOD * D_MODEL // STAGE_COLS):
            for kb in range(D_MODEL // STAGE_ROWS):
                pieces.append(ada_piece(cg, kb))
        for src_hbm, dst in ((wglu_hbm, wglu_b), (wout_hbm, wout_b)):
            for rb in range(dst.shape[0] // STAGE_ROWS):
                pieces.append(cast_piece(src_hbm, dst, rb))
        _stream_pieces(pieces, stage, stage_sems)
        bg_in(0).start()

        for j in range(GATE_BLOCKS):
            rs = slice(j * GATE_BLOCK, (j + 1) * GATE_BLOCK)
            reps = GATE_BLOCK // LRU_HEAD_DIM
            wg_b[j, :, :GATE_BLOCK] = _lane_tiled_block_diag(wa_ref[rs, :], reps, LRU_HEAD_DIM)
            wg_b[j, :, GATE_BLOCK:] = _lane_tiled_block_diag(wx_ref[rs, :], reps, LRU_HEAD_DIM)

        lr = slr_ref[...]
        li = sli_ref[...]
        dt = jnp.exp(sldt_ref[...])
        mag = jnp.exp(lr * dt)
        ab_r = mag * jnp.cos(li * dt)
        ab_i = mag * jnp.sin(li * dt)
        den = lr * lr + li * li
        fr = ((ab_r - 1.0) * lr + ab_i * li) / den
        fi = (ab_i * lr - (ab_r - 1.0) * li) / den
        abr_s[...] = ab_r
        abi_s[...] = ab_i
        br = bret_ref[...]
        bi = bimt_ref[...]
        bb_r = fr * br - fi * bi
        bb_i = fr * bi + fi * br
        for j in range(S5_BLOCKS):
            sl = slice(j * S5_BLOCK_LANES, (j + 1) * S5_BLOCK_LANES)
            wb_b[j, :, :S5_BLOCK_LANES] = _sublane_tiled_block_diag(bb_r[:, sl], S5_BLOCK_GROUPS, S5_STATE)
            wb_b[j, :, S5_BLOCK_LANES:] = _sublane_tiled_block_diag(bb_i[:, sl], S5_BLOCK_GROUPS, S5_STATE)
            cs_rows = slice(j * S5_BLOCK_CH, (j + 1) * S5_BLOCK_CH)
            ct_r = _lane_tiled_block_diag(cre_ref[cs_rows, :], S5_BLOCK_GROUPS, S5_GROUP)
            ct_i = _lane_tiled_block_diag(-cim_ref[cs_rows, :], S5_BLOCK_GROUPS, S5_GROUP)
            wc_b[j, :S5_BLOCK_LANES, :] = ct_r.astype(F32).T.astype(BF16)
            wc_b[j, S5_BLOCK_LANES:, :] = ct_i.astype(F32).T.astype(BF16)

    @pl.when(step < n_chunks)
    def _prompt_step():
        @pl.when(step + 1 < n_chunks)
        def _():
            for c in _batch_major_copies(x_buf, 1 - slot, x_hbm, x_sems, step + 1, to_hbm=False):
                c.start()

        for c in _batch_major_copies(x_buf, slot, x_hbm, x_sems, step, to_hbm=False):
            c.wait()

        sh1 = mod1p[:, 0:D_MODEL]
        sc1 = mod1p[:, D_MODEL:2 * D_MODEL]
        g1 = mod1p[:, 2 * D_MODEL:3 * D_MODEL]

        scale1 = (n1g_ref[...] * (1.0 + sc1))[None]
        tsub = tl // FRONT_SLICES

        def project(s):
            r0, r1 = s * tsub * nb, (s + 1) * tsub * nb
            xs3 = x_buf[slot, pl.ds(s * tsub, tsub)]
            hn = _rms(xs3, scale1) + sh1[None]
            p_buf[r0:r1, :] = _dot(hn.reshape(r1 - r0, D_MODEL).astype(BF16), win_b[...])
            xc_buf[hist + r0:hist + r1, :] = p_buf[r0:r1, 0:D_LRU]

        def lru_gates(s):
            r0, r1 = s * tsub * nb, (s + 1) * tsub * nb
            for j in range(GATE_BLOCKS):
                sl = slice(j * GATE_BLOCK, (j + 1) * GATE_BLOCK)
                conv_j = cb_ref[:, sl]
                for k in range(CONV_W):
                    conv_j = conv_j + cw_ref[k:k + 1, sl] * xc_buf[k * nb + r0:k * nb + r1, sl]
                a, b = _lru_gate_block(conv_j, wg_b[j], ba_ref[:, sl], bx_ref[:, sl], lam_ref[:, sl])
                a_buf[r0:r1, sl] = a
                b_buf[r0:r1, sl] = b

        def lru_scan():
            h = h_lru[...]
            for t in range(tl):
                h = a_buf[t * nb:(t + 1) * nb, :] * h + b_buf[t * nb:(t + 1) * nb, :]
                b_buf[t * nb:(t + 1) * nb, :] = h
            h_lru[...] = h
            lru_out = b_buf[...] * jax.nn.gelu(p_buf[:, D_LRU:2 * D_LRU])
            m_buf[:, 0:D_LRU] = _rms(lru_out, glru_ref[...]).astype(BF16)

        def s5_input(j):
            ub = p_buf[:, 2 * D_LRU + j * S5_BLOCK_CH:2 * D_LRU + (j + 1) * S5_BLOCK_CH].astype(BF16)
            bu = _dot(ub, wb_b[j])
            hr_buf[j % 2] = bu[:, :S5_BLOCK_LANES]
            hi_buf[j % 2] = bu[:, S5_BLOCK_LANES:]

        def s5_scan(j):
            sl = slice(j * S5_BLOCK_LANES, (j + 1) * S5_BLOCK_LANES)
            hr_j = hr_buf.at[j % 2]
            hi_j = hi_buf.at[j % 2]
            ar = jnp.broadcast_to(abr_s[:, sl], (nb, S5_BLOCK_LANES))
            ai = jnp.broadcast_to(abi_s[:, sl], (nb, S5_BLOCK_LANES))
            hr = h_s5r[:, sl]
            hi = h_s5i[:, sl]
            for t in range(tl):
                rs = slice(t * nb, (t + 1) * nb)
                hr, hi = ar * hr - ai * hi + hr_j[rs, :], ar * hi + ai * hr + hi_j[rs, :]
                hr_j[rs, :] = hr
                hi_j[rs, :] = hi
            h_s5r[:, sl] = hr
            h_s5i[:, sl] = hi

        def s5_readout(j):
            y_buf[:, j * S5_BLOCK_CH:(j + 1) * S5_BLOCK_CH] = _s5_y_block(
                hr_buf[j % 2], hi_buf[j % 2], wc_b, j)

        for s in range(FRONT_SLICES):
            project(s)
        lru_gates(0)
        s5_input(0)
        s5_input(1)
        for s in range(1, FRONT_SLICES):
            lru_gates(s)
        for j in range(S5_BLOCKS):
            s5_scan(j)
            s5_readout(j)
            if j + 2 < S5_BLOCKS:
                s5_input(j + 2)
            if j == 1:
                lru_scan()

        s5_out = _s5_glu(y_buf[...], p_buf[:, 2 * D_LRU:], d_ref[...], wglu_b)
        m_buf[:, D_LRU:] = _rms(s5_out, gs5_ref[...]).astype(BF16)

        mix = _dot(m_buf[...], wout_b[...]).reshape(tl, nb, D_MODEL)
        o_ref[...] = x_buf[slot] + g1[None] * mix

        xc_buf[0:hist, :] = xc_buf[rows:rows + hist, :]

    for k in range(len(bg_pieces)):
        @pl.when(step == k + 1)
        def _bg_task(k=k):
            r, c = bg_pieces[k][0].shape
            bg_in(k).wait()
            if k >= 2:
                bg_out(k - 2).wait()
            bg_buf[k % 2, 0:r, 0:c] = stage[k % STAGE_DEPTH, 0:r, 0:c].astype(BF16)
            bg_out(k).start()
            if k + 1 < len(bg_pieces):
                bg_in(k + 1).start()

    @pl.when(step == n_chunks)
    def _sample_step():
        for k in range(max(len(bg_pieces) - 2, 0), len(bg_pieces)):
            bg_out(k).wait()
        for k in range(CONV_W - 1):
            convp_o[:, k, :] = xc_buf[k * nb:(k + 1) * nb, :]
        lrup_o[...] = h_lru[...]
        s5rp_o[...] = h_s5r[...].reshape(s5rp_o.shape)
        s5ip_o[...] = h_s5i[...].reshape(s5ip_o.shape)

        ns = xs_ref.shape[0]
        sh1 = mod1s[:, 0:D_MODEL]
        sc1 = mod1s[:, D_MODEL:2 * D_MODEL]
        g1 = mod1s[:, 2 * D_MODEL:3 * D_MODEL]

        x = xs_ref[...]
        hn = _rms(x, n1g_ref[...] * (1.0 + sc1)) + sh1
        p = _dot(hn.astype(BF16), win_b[...])
        lru_x = p[:, 0:D_LRU]

        for k in range(CONV_W - 2):
            convs_o[:, k, :] = sconv_ref[:, k + 1, :]
        convs_o[:, CONV_W - 2, :] = lru_x
        for j in range(GATE_BLOCKS):
            sl = slice(j * GATE_BLOCK, (j + 1) * GATE_BLOCK)
            conv_j = cb_ref[:, sl] + cw_ref[CONV_W - 1:CONV_W, sl] * lru_x[:, sl]
            for k in range(CONV_W - 1):
                conv_j = conv_j + cw_ref[k:k + 1, sl] * sconv_ref[:, k, sl]
            a, b = _lru_gate_block(conv_j, wg_b[j], ba_ref[:, sl], bx_ref[:, sl], lam_ref[:, sl])
            lrus_o[:, sl] = a * slru_ref[:, sl] + b
        lru_out = lrus_o[...] * jax.nn.gelu(p[:, D_LRU:2 * D_LRU])
        ms_buf = m_buf.at[pl.ds(0, ns), :]
        ms_buf[:, 0:D_LRU] = _rms(lru_out, glru_ref[...]).astype(BF16)

        u = p[:, 2 * D_LRU:]
        ub = u.astype(BF16)
        ys_buf = y_buf.at[pl.ds(0, ns), :]
        h0r = ss5r_ref[...].reshape(ns, S5_LANES)
        h0i = ss5i_ref[...].reshape(ns, S5_LANES)
        for j in range(S5_BLOCKS):
            sl = slice(j * S5_BLOCK_LANES, (j + 1) * S5_BLOCK_LANES)
            bu_r, bu_i = _s5_bu_block(ub, wb_b, j)
            ar = abr_s[:, sl]
            ai = abi_s[:, sl]
            hr = ar * h0r[:, sl] - ai * h0i[:, sl] + bu_r
            hi = ar * h0i[:, sl] + ai * h0r[:, sl] + bu_i
            gs = slice(j * S5_BLOCK_GROUPS, (j + 1) * S5_BLOCK_GROUPS)
            s5rs_o[:, gs, :] = hr.reshape(ns, S5_BLOCK_GROUPS, S5_STATE)
            s5is_o[:, gs, :] = hi.reshape(ns, S5_BLOCK_GROUPS, S5_STATE)
            ys_buf[:, j * S5_BLOCK_CH:(j + 1) * S5_BLOCK_CH] = _s5_y_block(hr, hi, wc_b, j)
        s5_out = _s5_glu(ys_buf[...], u, d_ref[...], wglu_b)
        ms_buf[:, D_LRU:] = _rms(s5_out, gs5_ref[...]).astype(BF16)
        xs_o[...] = x + g1 * _dot(ms_buf[...], wout_b[...])


def _const_spec(shape):
    nd = len(shape)
    return pl.BlockSpec(shape, lambda i, _nd=nd: (0,) * _nd, pipeline_mode=pl.Buffered(1))


def _out_const_spec(shape):
    nd = len(shape)
    return pl.BlockSpec(shape, lambda i, _nd=nd: (0,) * _nd)


_HBM = pl.BlockSpec(memory_space=pl.ANY)


def _mixer(x_prompt, c_prompt, c_sample, ada_w, ada_b, x_s, sconv, slru, ss5r, ss5i, w):
    nb, t_len, _ = x_prompt.shape
    ns = x_s.shape[0]
    tl = TIME_CHUNK
    n_chunks = t_len // tl
    rows = tl * nb
    hist = (CONV_W - 1) * nb
    half_mod = (N_MOD // 2) * D_MODEL

    inputs = [
        (x_prompt, _HBM), (c_prompt, None), (c_sample, None), (ada_w, _HBM), (ada_b, None),
        (x_s, None), (sconv, None), (slru, None), (ss5r, None), (ss5i, None),
        (w["n1g"], None), (w["win"], _HBM), (w["cw"], None), (w["cb"], None), (w["wa"], None), (w["wx"], None),
        (w["ba"], None), (w["bx"], None), (w["lam"], None),
        (w["slr"], None), (w["sli"], None), (w["sldt"], None), (w["bret"], None), (w["bimt"], None),
        (w["cre"], None), (w["cim"], None),
        (w["d"], None), (w["wglu"], _HBM), (w["glru"], None), (w["gs5"], None), (w["wout"], _HBM),
        (w["wgate"], _HBM), (w["wup"], _HBM), (w["wdown"], _HBM),
    ]
    in_specs = [spec if spec is not None else _const_spec(a.shape) for a, spec in inputs]

    out_shapes = [
        ((t_len, nb, D_MODEL), pl.BlockSpec((tl, nb, D_MODEL), lambda i: (jnp.minimum(i, n_chunks - 1), 0, 0))),
        ((nb, half_mod), None), ((ns, half_mod), None),
        ((nb, CONV_W - 1, D_LRU), None), ((nb, D_LRU), None),
        ((nb, S5_GROUPS, S5_STATE), None), ((nb, S5_GROUPS, S5_STATE), None),
        ((ns, D_MODEL), None), ((ns, CONV_W - 1, D_LRU), None), ((ns, D_LRU), None),
        ((ns, S5_GROUPS, S5_STATE), None), ((ns, S5_GROUPS, S5_STATE), None),
    ]
    out_specs = [spec if spec is not None else _out_const_spec(s) for s, spec in out_shapes]
    out_shape = [jax.ShapeDtypeStruct(s, F32) for s, _ in out_shapes]
    for wkey in ("wgate", "wup", "wdown"):
        out_specs.append(_HBM)
        out_shape.append(jax.ShapeDtypeStruct(w[wkey].shape, BF16))

    scratch = [
        pltpu.VMEM((2, tl, nb, D_MODEL), F32),
        pltpu.SemaphoreType.DMA((2, nb)),
        pltpu.VMEM((STAGE_DEPTH, STAGE_ROWS, STAGE_COLS), F32),
        pltpu.SemaphoreType.DMA((STAGE_DEPTH,)),
        pltpu.VMEM((D_MODEL, D_IN), BF16),
        pltpu.VMEM((D_S5, 2 * D_S5), BF16),
        pltpu.VMEM((D_LRU + D_S5, D_MODEL), BF16),
        pltpu.VMEM((GATE_BLOCKS, GATE_BLOCK, 2 * GATE_BLOCK), BF16),
        pltpu.VMEM((S5_BLOCKS, S5_BLOCK_CH, 2 * S5_BLOCK_LANES), BF16),
        pltpu.VMEM((S5_BLOCKS, 2 * S5_BLOCK_LANES, S5_BLOCK_CH), BF16),
        pltpu.VMEM((1, S5_LANES), F32),
        pltpu.VMEM((1, S5_LANES), F32),
        pltpu.VMEM((nb, half_mod), F32),
        pltpu.VMEM((ns, half_mod), F32),
        pltpu.VMEM((hist + rows, D_LRU), F32),
        pltpu.VMEM((nb, D_LRU), F32),
        pltpu.VMEM((nb, S5_LANES), F32),
        pltpu.VMEM((nb, S5_LANES), F32),
        pltpu.VMEM((rows, D_IN), F32),
        pltpu.VMEM((2, rows, S5_BLOCK_LANES), F32),
        pltpu.VMEM((2, rows, S5_BLOCK_LANES), F32),
        pltpu.VMEM((rows, D_LRU), F32),
        pltpu.VMEM((rows, D_LRU), F32),
        pltpu.VMEM((rows, D_S5), F32),
        pltpu.VMEM((rows, D_LRU + D_S5), BF16),
        pltpu.VMEM((2, STAGE_ROWS, BG_COLS), BF16),
        pltpu.SemaphoreType.DMA((2,)),
    ]
    return pl.pallas_call(
        functools.partial(_mixer_kernel, n_chunks),
        grid=(n_chunks + 1,),
        in_specs=in_specs,
        out_specs=out_specs,
        out_shape=out_shape,
        scratch_shapes=scratch,
        compiler_params=pltpu.CompilerParams(
            dimension_semantics=("arbitrary",), vmem_limit_bytes=VMEM_LIMIT_BYTES),
        name="mixer",
    )(*[a for a, _ in inputs])


def _ffn_rows(x3, mod_ref, n2g_ref, wgate_ref, wup_ref, wdown_ref, gf_ref, act_ref):
    groups, mrows, _ = x3.shape
    rows = groups * mrows
    sh2 = mod_ref[:, 0:D_MODEL]
    sc2 = mod_ref[:, D_MODEL:2 * D_MODEL]
    g2 = mod_ref[:, 2 * D_MODEL:3 * D_MODEL]

    hn = _rms(x3, (n2g_ref[...] * (1.0 + sc2))[None]) + sh2[None]
    hb = hn.reshape(rows, D_MODEL).astype(BF16)
    for c in range(FFN_CHUNKS):
        cols = slice(c * FFN_CHUNK, (c + 1) * FFN_CHUNK)
        g = _dot(hb, wgate_ref[:, cols])
        up = _dot(hb, wup_ref[:, cols])
        act_ref[:, cols] = (g * jax.nn.sigmoid(g) * up).astype(BF16)
    ffn = _dot(act_ref[...], wdown_ref[...]).reshape(groups, mrows, D_MODEL)
    return _rms(x3 + g2[None] * ffn, gf_ref[...])


def _ffn_kernel(
        n_chunks,
        x_ref, mod2p_ref, mod2s_ref, xs_ref, n2g_ref, wgate_ref, wup_ref, wdown_ref, gf_ref,
        y_hbm, ys_o,
        act_buf, y_buf, y_sems):
    step = pl.program_id(0)
    slot = step % 2

    @pl.when(step < n_chunks)
    def _prompt_step():
        @pl.when(step >= 2)
        def _():
            for c in _batch_major_copies(y_buf, slot, y_hbm, y_sems, step - 2, to_hbm=True):
                c.wait()

        tl, nb, _ = x_ref.shape
        sub = FFN_SUB_ROWS // nb
        for h in range(tl // sub):
            y_buf[slot, pl.ds(h * sub, sub)] = _ffn_rows(
                x_ref[h * sub:(h + 1) * sub], mod2p_ref, n2g_ref, wgate_ref, wup_ref, wdown_ref, gf_ref,
                act_buf.at[pl.ds(h * FFN_SUB_ROWS, FFN_SUB_ROWS), :])
        for c in _batch_major_copies(y_buf, slot, y_hbm, y_sems, step, to_hbm=True):
            c.start()

    @pl.when(step == n_chunks)
    def _sample_step():
        ns = xs_ref.shape[0]
        ys_o[...] = _ffn_rows(xs_ref[...][None], mod2s_ref, n2g_ref, wgate_ref, wup_ref, wdown_ref, gf_ref,
                              act_buf.at[pl.ds(0, ns), :])[0]
        for s in range(max(n_chunks - 2, 0), n_chunks):
            for c in _batch_major_copies(y_buf, s % 2, y_hbm, y_sems, s, to_hbm=True):
                c.wait()


def _ffn(x32_t, mod2p, mod2s, x32_s, wgate_b, wup_b, wdown_b, w, out_shape_p, out_shape_s):
    t_len, nb, _ = x32_t.shape
    tl = FFN_TIME_CHUNK
    n_chunks = t_len // tl
    rows = tl * nb
    inputs = [
        (x32_t, pl.BlockSpec((tl, nb, D_MODEL), lambda i: (jnp.minimum(i, n_chunks - 1), 0, 0))),
        (mod2p, None), (mod2s, None), (x32_s, None), (w["n2g"], None),
        (wgate_b, None), (wup_b, None), (wdown_b, None), (w["gf"], None),
    ]
    in_specs = [spec if spec is not None else _const_spec(a.shape) for a, spec in inputs]
    scratch = [
        pltpu.VMEM((rows, D_FF), BF16),
        pltpu.VMEM((2, tl, nb, D_MODEL), F32),
        pltpu.SemaphoreType.DMA((2, nb)),
    ]
    return pl.pallas_call(
        functools.partial(_ffn_kernel, n_chunks),
        grid=(n_chunks + 1,),
        in_specs=in_specs,
        out_specs=[_HBM, _out_const_spec(out_shape_s)],
        out_shape=[jax.ShapeDtypeStruct(out_shape_p, F32), jax.ShapeDtypeStruct(out_shape_s, F32)],
        scratch_shapes=scratch,
        compiler_params=pltpu.CompilerParams(
            dimension_semantics=("arbitrary",), vmem_limit_bytes=VMEM_LIMIT_BYTES),
        name="ffn",
    )(*[a for a, _ in inputs])


def kernel(x_prompt, x_sample, state_conv, state_lru, state_s5_re, state_s5_im, c_prompt, c_sample,
           ada_w, ada_b, norm1_g, w_in, conv_w, conv_b, lru_wa, lru_ba, lru_wx, lru_bx, lru_lambda,
           s5_lambda_re, s5_lambda_im, s5_log_dt, s5_b_re, s5_b_im, s5_c_re, s5_c_im, s5_d, s5_w_glu,
           g_lru_out, g_s5_out, w_out, norm2_g, ffn_w_gate, ffn_w_up, ffn_w_down, final_norm_g):
    depth = ada_w.shape[0]
    assert depth == 1, "single-layer decoder step"
    bp, t_len, _ = x_prompt.shape
    bs, dec_t, _ = x_sample.shape
    assert bp == SUBLANES and dec_t == 1 and t_len % TIME_CHUNK == 0 and t_len % FFN_TIME_CHUNK == 0
    assert (FFN_TIME_CHUNK * SUBLANES) % FFN_SUB_ROWS == 0 and bs <= FFN_SUB_ROWS

    w = dict(
        n1g=norm1_g, win=w_in[0], cw=conv_w[0], cb=conv_b,
        wa=lru_wa.reshape(D_LRU, LRU_HEAD_DIM), wx=lru_wx.reshape(D_LRU, LRU_HEAD_DIM),
        ba=lru_ba, bx=lru_bx, lam=lru_lambda,
        slr=s5_lambda_re.reshape(1, S5_LANES), sli=s5_lambda_im.reshape(1, S5_LANES),
        sldt=jnp.repeat(s5_log_dt[0], S5_STATE).reshape(1, S5_LANES),
        bret=s5_b_re.reshape(S5_LANES, S5_GROUP).T, bimt=s5_b_im.reshape(S5_LANES, S5_GROUP).T,
        cre=s5_c_re.reshape(D_S5, S5_STATE), cim=s5_c_im.reshape(D_S5, S5_STATE),
        d=s5_d, wglu=s5_w_glu[0], glru=g_lru_out, gs5=g_s5_out, wout=w_out[0],
        n2g=norm2_g, wgate=ffn_w_gate[0], wup=ffn_w_up[0], wdown=ffn_w_down[0],
        gf=final_norm_g.reshape(1, D_MODEL),
    )

    (x32_t, mod2p, mod2s, conv_p, lru_p, s5r_p, s5i_p,
     x32_s, conv_s, lru_s, s5r_s, s5i_s, wgate_b, wup_b, wdown_b) = _mixer(
        x_prompt, c_prompt, c_sample, ada_w[0], ada_b, x_sample.reshape(bs, D_MODEL), state_conv[0], state_lru[0],
        state_s5_re[0], state_s5_im[0], w)
    y_prompt, y_s = _ffn(x32_t, mod2p, mod2s, x32_s, wgate_b, wup_b, wdown_b, w, x_prompt.shape, x32_s.shape)

    return (y_prompt, y_s.reshape(x_sample.shape), conv_p[None], lru_p[None], s5r_p[None], s5i_p[None],
            conv_s[None], lru_s[None], s5r_s[None], s5i_s[None])
```

```python
import functools

import jax
import jax.numpy as jnp
from jax import lax
from jax.experimental import pallas as pl
from jax.experimental.pallas import tpu as pltpu

F32 = jnp.float32
BF16 = jnp.bfloat16
I32 = jnp.int32

D_MODEL = 1024
D_LRU = 512
D_S5 = 512
D_IN = 2 * D_LRU + D_S5
LRU_HEAD_DIM = 64
CONV_W = 4
LRU_C = 8.0
S5_GROUP = 16
S5_GROUPS = D_S5 // S5_GROUP
S5_STATE = 64
S5_LANES = S5_GROUPS * S5_STATE
D_FF = 2816
N_MOD = 6
EPS = 1e-6

SUBLANES = 8
MXU_DIM = 256
VMEM_LIMIT_BYTES = 60 * 1024 * 1024

TIME_CHUNK = 64
FRONT_SLICES = 2
GATE_BLOCK = MXU_DIM
GATE_BLOCKS = D_LRU // GATE_BLOCK
S5_BLOCKS = 4
S5_BLOCK_GROUPS = S5_GROUPS // S5_BLOCKS
S5_BLOCK_CH = D_S5 // S5_BLOCKS
S5_BLOCK_LANES = S5_LANES // S5_BLOCKS
FFN_CHUNK = MXU_DIM
FFN_CHUNKS = D_FF // FFN_CHUNK
STAGE_ROWS = 256
STAGE_COLS = D_IN
STAGE_DEPTH = 3
BG_COLS = D_FF // 2
FFN_TIME_CHUNK = 128
FFN_SUB_ROWS = 512


def _dot(a, b):
    return jnp.dot(a, b, preferred_element_type=F32)


def _rms(x, g):
    return x * lax.rsqrt(jnp.mean(x * x, axis=-1, keepdims=True) + EPS) * g


def _log_sigmoid(x):
    return jnp.minimum(x, 0.0) - jnp.log1p(jnp.exp(-jnp.abs(x)))


def _sigmoid(x):
    return 0.5 * jnp.tanh(0.5 * x) + 0.5


def _sqrt_nonneg(x):
    return jnp.where(x > 0.0, x * lax.rsqrt(x), 0.0)


def _log2(n):
    assert n & (n - 1) == 0
    return n.bit_length() - 1


def _same_group(shape, row_group, col_group):
    r = lax.broadcasted_iota(I32, shape, 0) >> _log2(row_group)
    c = lax.broadcasted_iota(I32, shape, 1) >> _log2(col_group)
    return r == c


def _lane_tiled_block_diag(blk, reps, row_group):
    r, w = blk.shape
    n = w * reps
    sel = (lax.broadcasted_iota(I32, (w, n), 1) & (w - 1)) == lax.broadcasted_iota(I32, (w, n), 0)
    tiled = _dot(blk.astype(BF16), jnp.where(sel, 1.0, 0.0).astype(BF16))
    return jnp.where(_same_group((r, n), row_group, w), tiled, 0.0).astype(BF16)


def _sublane_tiled_block_diag(blk, reps, col_group):
    h, n = blk.shape
    tiled = jnp.broadcast_to(blk[None], (reps, h, n)).reshape(reps * h, n)
    return jnp.where(_same_group((reps * h, n), h, col_group), tiled, 0.0).astype(BF16)


def _lru_gate_block(conv_j, wg_j, ba_j, bx_j, lam_j):
    gz = _dot(conv_j.astype(BF16), wg_j)
    r = _sigmoid(gz[:, :GATE_BLOCK] + ba_j)
    ig = _sigmoid(gz[:, GATE_BLOCK:] + bx_j)
    log_a = (LRU_C * _log_sigmoid(lam_j)) * r
    a = jnp.exp(log_a)
    mult = _sqrt_nonneg(-jnp.tanh(log_a) * (a * a + 1.0))
    return a, mult * (ig * conv_j)


def _s5_bu_block(ub, wb_ref, j):
    bu = _dot(ub[:, j * S5_BLOCK_CH:(j + 1) * S5_BLOCK_CH], wb_ref[j])
    return bu[:, :S5_BLOCK_LANES], bu[:, S5_BLOCK_LANES:]


def _s5_y_block(hr, hi, wc_ref, j):
    return (_dot(hr.astype(BF16), wc_ref[j, :S5_BLOCK_LANES, :])
            + _dot(hi.astype(BF16), wc_ref[j, S5_BLOCK_LANES:, :]))


def _s5_glu(y, u, d, wglu_ref):
    z = _dot(jax.nn.gelu(y + d * u).astype(BF16), wglu_ref[...])
    return z[:, :D_S5] * _sigmoid(z[:, D_S5:])


def _stream_pieces(pieces, stage, sems):
    depth = stage.shape[0]

    def dst(k):
        r, c = pieces[k][0].shape
        return stage.at[k % depth, pl.ds(0, r), pl.ds(0, c)]

    copies = [pltpu.make_async_copy(pieces[k][0], dst(k), sems.at[k % depth]) for k in range(len(pieces))]
    for k in range(min(depth - 1, len(pieces))):
        copies[k].start()
    for k, (_, consume) in enumerate(pieces):
        if k + depth - 1 < len(pieces):
            copies[k + depth - 1].start()
        copies[k].wait()
        consume(dst(k))


def _batch_major_copies(tm_bufs, slot, bm_hbm, sems, chunk, to_hbm):
    _, tl, nb, _ = tm_bufs.shape
    copies = []
    for b in range(nb):
        hbm = bm_hbm.at[b, pl.ds(chunk * tl, tl), :]
        vmem = tm_bufs.at[slot, :, b, :]
        src, dst = (vmem, hbm) if to_hbm else (hbm, vmem)
        copies.append(pltpu.make_async_copy(src, dst, sems.at[slot, b]))
    return copies


def _mixer_kernel(
        n_chunks,
        x_hbm, cp_ref, cs_ref, adaw_hbm, adab_ref,
        xs_ref, sconv_ref, slru_ref, ss5r_ref, ss5i_ref,
        n1g_ref, win_hbm, cw_ref, cb_ref, wa_ref, wx_ref, ba_ref, bx_ref, lam_ref,
        slr_ref, sli_ref, sldt_ref, bret_ref, bimt_ref, cre_ref, cim_ref,
        d_ref, wglu_hbm, glru_ref, gs5_ref, wout_hbm, wgate_hbm, wup_hbm, wdown_hbm,
        o_ref, mod2p_o, mod2s_o, convp_o, lrup_o, s5rp_o, s5ip_o,
        xs_o, convs_o, lrus_o, s5rs_o, s5is_o, wgate_o, wup_o, wdown_o,
        x_buf, x_sems, stage, stage_sems, win_b, wglu_b, wout_b, wg_b, wb_b, wc_b, abr_s, abi_s,
        mod1p, mod1s, xc_buf, h_lru, h_s5r, h_s5i, p_buf, hr_buf, hi_buf, a_buf, b_buf, y_buf, m_buf,
        bg_buf, bg_sems):
    step = pl.program_id(0)
    slot = step % 2
    _, tl, nb, _ = x_buf.shape
    rows = tl * nb
    hist = (CONV_W - 1) * nb
    half_mod = (N_MOD // 2) * D_MODEL

    bg_pieces = []
    for src, dst in ((wgate_hbm, wgate_o), (wup_hbm, wup_o)):
        for rb in range(D_MODEL // STAGE_ROWS):
            for ch in range(D_FF // BG_COLS):
                win = (pl.ds(rb * STAGE_ROWS, STAGE_ROWS), pl.ds(ch * BG_COLS, BG_COLS))
                bg_pieces.append((src.at[win], dst.at[win]))
    for rb in range(D_FF // STAGE_ROWS):
        win = (pl.ds(rb * STAGE_ROWS, STAGE_ROWS), pl.ds(0, D_MODEL))
        bg_pieces.append((wdown_hbm.at[win], wdown_o.at[win]))
    assert len(bg_pieces) < n_chunks

    def bg_in(k):
        r, c = bg_pieces[k][0].shape
        return pltpu.make_async_copy(bg_pieces[k][0], stage.at[k % STAGE_DEPTH, pl.ds(0, r), pl.ds(0, c)],
                                     stage_sems.at[k % STAGE_DEPTH])

    def bg_out(k):
        r, c = bg_pieces[k][0].shape
        return pltpu.make_async_copy(bg_buf.at[k % 2, pl.ds(0, r), pl.ds(0, c)], bg_pieces[k][1],
                                     bg_sems.at[k % 2])

    @pl.when(step == 0)
    def _prologue():
        for c in _batch_major_copies(x_buf, 0, x_hbm, x_sems, 0, to_hbm=False):
            c.start()
        xc_buf[0:hist, :] = jnp.zeros((hist, D_LRU), F32)
        h_lru[...] = jnp.zeros_like(h_lru)
        h_s5r[...] = jnp.zeros_like(h_s5r)
        h_s5i[...] = jnp.zeros_like(h_s5i)

        cp = cp_ref[...]
        cs = cs_ref[...]
        silu_p = (cp * _sigmoid(cp)).astype(BF16)
        silu_s = (cs * _sigmoid(cs)).astype(BF16)
        pieces = []

        def ada_piece(cg, kb):
            cols = slice(cg * STAGE_COLS, (cg + 1) * STAGE_COLS)
            first = cols.start < half_mod
            dst_p, dst_s = (mod1p, mod1s) if first else (mod2p_o, mod2s_o)
            dcols = cols if first else slice(cols.start - half_mod, cols.stop - half_mod)
            krows = slice(kb * STAGE_ROWS, (kb + 1) * STAGE_ROWS)

            def consume(view):
                w = view[...].astype(BF16)
                part_p = _dot(silu_p[:, krows], w)
                part_s = _dot(silu_s[:, krows], w)
                if kb == 0:
                    dst_p[:, dcols] = part_p + adab_ref[:, cols]
                    dst_s[:, dcols] = part_s + adab_ref[:, cols]
                else:
                    dst_p[:, dcols] += part_p
                    dst_s[:, dcols] += part_s
            return adaw_hbm.at[pl.ds(kb * STAGE_ROWS, STAGE_ROWS), pl.ds(cg * STAGE_COLS, STAGE_COLS)], consume

        def cast_piece(src_hbm, dst, rb):
            ncols = dst.shape[1]

            def consume(view):
                dst[rb * STAGE_ROWS:(rb + 1) * STAGE_ROWS, :] = view[...].astype(BF16)
            return src_hbm.at[pl.ds(rb * STAGE_ROWS, STAGE_ROWS), pl.ds(0, ncols)], consume

        for rb in range(win_b.shape[0] // STAGE_ROWS):
            pieces.append(cast_piece(win_hbm, win_b, rb))
        for cg in range(N_MOD * D_MODEL // STAGE_COLS):
            for kb in range(D_MODEL // STAGE_ROWS):
                pieces.append(ada_piece(cg, kb))
        for src_hbm, dst in ((wglu_hbm, wglu_b), (wout_hbm, wout_b)):
            for rb in range(dst.shape[0] // STAGE_ROWS):
                pieces.append(cast_piece(src_hbm, dst, rb))
        _stream_pieces(pieces, stage, stage_sems)
        bg_in(0).start()

        for j in range(GATE_BLOCKS):
            rs = slice(j * GATE_BLOCK, (j + 1) * GATE_BLOCK)
            reps = GATE_BLOCK // LRU_HEAD_DIM
            wg_b[j, :, :GATE_BLOCK] = _lane_tiled_block_diag(wa_ref[rs, :], reps, LRU_HEAD_DIM)
            wg_b[j, :, GATE_BLOCK:] = _lane_tiled_block_diag(wx_ref[rs, :], reps, LRU_HEAD_DIM)

        lr = slr_ref[...]
        li = sli_ref[...]
        dt = jnp.exp(sldt_ref[...])
        mag = jnp.exp(lr * dt)
        ab_r = mag * jnp.cos(li * dt)
        ab_i = mag * jnp.sin(li * dt)
        den = lr * lr + li * li
        fr = ((ab_r - 1.0) * lr + ab_i * li) / den
        fi = (ab_i * lr - (ab_r - 1.0) * li) / den
        abr_s[...] = ab_r
        abi_s[...] = ab_i
        br = bret_ref[...]
        bi = bimt_ref[...]
        bb_r = fr * br - fi * bi
        bb_i = fr * bi + fi * br
        for j in range(S5_BLOCKS):
            sl = slice(j * S5_BLOCK_LANES, (j + 1) * S5_BLOCK_LANES)
            wb_b[j, :, :S5_BLOCK_LANES] = _sublane_tiled_block_diag(bb_r[:, sl], S5_BLOCK_GROUPS, S5_STATE)
            wb_b[j, :, S5_BLOCK_LANES:] = _sublane_tiled_block_diag(bb_i[:, sl], S5_BLOCK_GROUPS, S5_STATE)
            cs_rows = slice(j * S5_BLOCK_CH, (j + 1) * S5_BLOCK_CH)
            ct_r = _lane_tiled_block_diag(cre_ref[cs_rows, :], S5_BLOCK_GROUPS, S5_GROUP)
            ct_i = _lane_tiled_block_diag(-cim_ref[cs_rows, :], S5_BLOCK_GROUPS, S5_GROUP)
            wc_b[j, :S5_BLOCK_LANES, :] = ct_r.astype(F32).T.astype(BF16)
            wc_b[j, S5_BLOCK_LANES:, :] = ct_i.astype(F32).T.astype(BF16)

    @pl.when(step < n_chunks)
    def _prompt_step():
        @pl.when(step + 1 < n_chunks)
        def _():
            for c in _batch_major_copies(x_buf, 1 - slot, x_hbm, x_sems, step + 1, to_hbm=False):
                c.start()

        for c in _batch_major_copies(x_buf, slot, x_hbm, x_sems, step, to_hbm=False):
            c.wait()

        sh1 = mod1p[:, 0:D_MODEL]
        sc1 = mod1p[:, D_MODEL:2 * D_MODEL]
        g1 = mod1p[:, 2 * D_MODEL:3 * D_MODEL]

        scale1 = (n1g_ref[...] * (1.0 + sc1))[None]
        tsub = tl // FRONT_SLICES

        def project(s):
            r0, r1 = s * tsub * nb, (s + 1) * tsub * nb
            xs3 = x_buf[slot, pl.ds(s * tsub, tsub)]
            hn = _rms(xs3, scale1) + sh1[None]
            p_buf[r0:r1, :] = _dot(hn.reshape(r1 - r0, D_MODEL).astype(BF16), win_b[...])
            xc_buf[hist + r0:hist + r1, :] = p_buf[r0:r1, 0:D_LRU]

        def lru_gates(s):
            r0, r1 = s * tsub * nb, (s + 1) * tsub * nb
            for j in range(GATE_BLOCKS):
                sl = slice(j * GATE_BLOCK, (j + 1) * GATE_BLOCK)
                conv_j = cb_ref[:, sl]
                for k in range(CONV_W):
                    conv_j = conv_j + cw_ref[k:k + 1, sl] * xc_buf[k * nb + r0:k * nb + r1, sl]
                a, b = _lru_gate_block(conv_j, wg_b[j], ba_ref[:, sl], bx_ref[:, sl], lam_ref[:, sl])
                a_buf[r0:r1, sl] = a
                b_buf[r0:r1, sl] = b

        def lru_scan():
            h = h_lru[...]
            for t in range(tl):
                h = a_buf[t * nb:(t + 1) * nb, :] * h + b_buf[t * nb:(t + 1) * nb, :]
                b_buf[t * nb:(t + 1) * nb, :] = h
            h_lru[...] = h
            lru_out = b_buf[...] * jax.nn.gelu(p_buf[:, D_LRU:2 * D_LRU])
            m_buf[:, 0:D_LRU] = _rms(lru_out, glru_ref[...]).astype(BF16)

        def s5_input(j):
            ub = p_buf[:, 2 * D_LRU + j * S5_BLOCK_CH:2 * D_LRU + (j + 1) * S5_BLOCK_CH].astype(BF16)
            bu = _dot(ub, wb_b[j])
            hr_buf[j % 2] = bu[:, :S5_BLOCK_LANES]
            hi_buf[j % 2] = bu[:, S5_BLOCK_LANES:]

        def s5_scan(j):
            sl = slice(j * S5_BLOCK_LANES, (j + 1) * S5_BLOCK_LANES)
            hr_j = hr_buf.at[j % 2]
            hi_j = hi_buf.at[j % 2]
            ar = jnp.broadcast_to(abr_s[:, sl], (nb, S5_BLOCK_LANES))
            ai = jnp.broadcast_to(abi_s[:, sl], (nb, S5_BLOCK_LANES))
            hr = h_s5r[:, sl]
            hi = h_s5i[:, sl]
            for t in range(tl):
                rs = slice(t * nb, (t + 1) * nb)
                hr, hi = ar * hr - ai * hi + hr_j[rs, :], ar * hi + ai * hr + hi_j[rs, :]
                hr_j[rs, :] = hr
                hi_j[rs, :] = hi
            h_s5r[:, sl] = hr
            h_s5i[:, sl] = hi

        def s5_readout(j):
            y_buf[:, j * S5_BLOCK_CH:(j + 1) * S5_BLOCK_CH] = _s5_y_block(
                hr_buf[j % 2], hi_buf[j % 2], wc_b, j)

        for s in range(FRONT_SLICES):
            project(s)
        lru_gates(0)
        s5_input(0)
        s5_input(1)
        for s in range(1, FRONT_SLICES):
            lru_gates(s)
        for j in range(S5_BLOCKS):
            s5_scan(j)
            s5_readout(j)
            if j + 2 < S5_BLOCKS:
                s5_input(j + 2)
            if j == 1:
                lru_scan()

        s5_out = _s5_glu(y_buf[...], p_buf[:, 2 * D_LRU:], d_ref[...], wglu_b)
        m_buf[:, D_LRU:] = _rms(s5_out, gs5_ref[...]).astype(BF16)

        mix = _dot(m_buf[...], wout_b[...]).reshape(tl, nb, D_MODEL)
        o_ref[...] = x_buf[slot] + g1[None] * mix

        xc_buf[0:hist, :] = xc_buf[rows:rows + hist, :]

    for k in range(len(bg_pieces)):
        @pl.when(step == k + 1)
        def _bg_task(k=k):
            r, c = bg_pieces[k][0].shape
            bg_in(k).wait()
            if k >= 2:
                bg_out(k - 2).wait()
            bg_buf[k % 2, 0:r, 0:c] = stage[k % STAGE_DEPTH, 0:r, 0:c].astype(BF16)
            bg_out(k).start()
            if k + 1 < len(bg_pieces):
                bg_in(k + 1).start()

    @pl.when(step == n_chunks)
    def _sample_step():
        for k in range(max(len(bg_pieces) - 2, 0), len(bg_pieces)):
            bg_out(k).wait()
        for k in range(CONV_W - 1):
            convp_o[k] = xc_buf[k * nb:(k + 1) * nb, :]
        lrup_o[...] = h_lru[...]
        s5rp_o[...] = h_s5r[...].reshape(s5rp_o.shape)
        s5ip_o[...] = h_s5i[...].reshape(s5ip_o.shape)

        ns = xs_ref.shape[0]
        sh1 = mod1s[:, 0:D_MODEL]
        sc1 = mod1s[:, D_MODEL:2 * D_MODEL]
        g1 = mod1s[:, 2 * D_MODEL:3 * D_MODEL]

        x = xs_ref[...]
        hn = _rms(x, n1g_ref[...] * (1.0 + sc1)) + sh1
        p = _dot(hn.astype(BF16), win_b[...])
        lru_x = p[:, 0:D_LRU]

        for k in range(CONV_W - 2):
            convs_o[k] = sconv_ref[k + 1]
        convs_o[CONV_W - 2] = lru_x
        for j in range(GATE_BLOCKS):
            sl = slice(j * GATE_BLOCK, (j + 1) * GATE_BLOCK)
            conv_j = cb_ref[:, sl] + cw_ref[CONV_W - 1:CONV_W, sl] * lru_x[:, sl]
            for k in range(CONV_W - 1):
                conv_j = conv_j + cw_ref[k:k + 1, sl] * sconv_ref[k, :, sl]
            a, b = _lru_gate_block(conv_j, wg_b[j], ba_ref[:, sl], bx_ref[:, sl], lam_ref[:, sl])
            lrus_o[:, sl] = a * slru_ref[:, sl] + b
        lru_out = lrus_o[...] * jax.nn.gelu(p[:, D_LRU:2 * D_LRU])
        ms_buf = m_buf.at[pl.ds(0, ns), :]
        ms_buf[:, 0:D_LRU] = _rms(lru_out, glru_ref[...]).astype(BF16)

        u = p[:, 2 * D_LRU:]
        ub = u.astype(BF16)
        ys_buf = y_buf.at[pl.ds(0, ns), :]
        h0r = ss5r_ref[...].reshape(S5_LANES, ns).T
        h0i = ss5i_ref[...].reshape(S5_LANES, ns).T
        for j in range(S5_BLOCKS):
            sl = slice(j * S5_BLOCK_LANES, (j + 1) * S5_BLOCK_LANES)
            bu_r, bu_i = _s5_bu_block(ub, wb_b, j)
            ar = abr_s[:, sl]
            ai = abi_s[:, sl]
            hr = ar * h0r[:, sl] - ai * h0i[:, sl] + bu_r
            hi = ar * h0i[:, sl] + ai * h0r[:, sl] + bu_i
            gs = slice(j * S5_BLOCK_GROUPS, (j + 1) * S5_BLOCK_GROUPS)
            s5rs_o[gs] = hr.T.reshape(S5_BLOCK_GROUPS, S5_STATE, ns)
            s5is_o[gs] = hi.T.reshape(S5_BLOCK_GROUPS, S5_STATE, ns)
            ys_buf[:, j * S5_BLOCK_CH:(j + 1) * S5_BLOCK_CH] = _s5_y_block(hr, hi, wc_b, j)
        s5_out = _s5_glu(ys_buf[...], u, d_ref[...], wglu_b)
        ms_buf[:, D_LRU:] = _rms(s5_out, gs5_ref[...]).astype(BF16)
        xs_o[...] = x + g1 * _dot(ms_buf[...], wout_b[...])


def _const_spec(shape):
    nd = len(shape)
    return pl.BlockSpec(shape, lambda i, _nd=nd: (0,) * _nd, pipeline_mode=pl.Buffered(1))


def _out_const_spec(shape):
    nd = len(shape)
    return pl.BlockSpec(shape, lambda i, _nd=nd: (0,) * _nd)


_HBM = pl.BlockSpec(memory_space=pl.ANY)


def _mixer(x_prompt, c_prompt, c_sample, ada_w, ada_b, x_s, sconv, slru, ss5r, ss5i, w):
    nb, t_len, _ = x_prompt.shape
    ns = x_s.shape[0]
    tl = TIME_CHUNK
    n_chunks = t_len // tl
    rows = tl * nb
    hist = (CONV_W - 1) * nb
    half_mod = (N_MOD // 2) * D_MODEL

    inputs = [
        (x_prompt, _HBM), (c_prompt, None), (c_sample, None), (ada_w, _HBM), (ada_b, None),
        (x_s, None), (sconv, None), (slru, None), (ss5r, None), (ss5i, None),
        (w["n1g"], None), (w["win"], _HBM), (w["cw"], None), (w["cb"], None), (w["wa"], None), (w["wx"], None),
        (w["ba"], None), (w["bx"], None), (w["lam"], None),
        (w["slr"], None), (w["sli"], None), (w["sldt"], None), (w["bret"], None), (w["bimt"], None),
        (w["cre"], None), (w["cim"], None),
        (w["d"], None), (w["wglu"], _HBM), (w["glru"], None), (w["gs5"], None), (w["wout"], _HBM),
        (w["wgate"], _HBM), (w["wup"], _HBM), (w["wdown"], _HBM),
    ]
    in_specs = [spec if spec is not None else _const_spec(a.shape) for a, spec in inputs]

    out_shapes = [
        ((t_len, nb, D_MODEL), pl.BlockSpec((tl, nb, D_MODEL), lambda i: (jnp.minimum(i, n_chunks - 1), 0, 0))),
        ((nb, half_mod), None), ((ns, half_mod), None),
        ((CONV_W - 1, nb, D_LRU), None), ((nb, D_LRU), None),
        ((nb, S5_GROUPS, S5_STATE), None), ((nb, S5_GROUPS, S5_STATE), None),
        ((ns, D_MODEL), None), ((CONV_W - 1, ns, D_LRU), None), ((ns, D_LRU), None),
        ((S5_GROUPS, S5_STATE, ns), None), ((S5_GROUPS, S5_STATE, ns), None),
    ]
    out_specs = [spec if spec is not None else _out_const_spec(s) for s, spec in out_shapes]
    out_shape = [jax.ShapeDtypeStruct(s, F32) for s, _ in out_shapes]
    for wkey in ("wgate", "wup", "wdown"):
        out_specs.append(_HBM)
        out_shape.append(jax.ShapeDtypeStruct(w[wkey].shape, BF16))

    scratch = [
        pltpu.VMEM((2, tl, nb, D_MODEL), F32),
        pltpu.SemaphoreType.DMA((2, nb)),
        pltpu.VMEM((STAGE_DEPTH, STAGE_ROWS, STAGE_COLS), F32),
        pltpu.SemaphoreType.DMA((STAGE_DEPTH,)),
        pltpu.VMEM((D_MODEL, D_IN), BF16),
        pltpu.VMEM((D_S5, 2 * D_S5), BF16),
        pltpu.VMEM((D_LRU + D_S5, D_MODEL), BF16),
        pltpu.VMEM((GATE_BLOCKS, GATE_BLOCK, 2 * GATE_BLOCK), BF16),
        pltpu.VMEM((S5_BLOCKS, S5_BLOCK_CH, 2 * S5_BLOCK_LANES), BF16),
        pltpu.VMEM((S5_BLOCKS, 2 * S5_BLOCK_LANES, S5_BLOCK_CH), BF16),
        pltpu.VMEM((1, S5_LANES), F32),
        pltpu.VMEM((1, S5_LANES), F32),
        pltpu.VMEM((nb, half_mod), F32),
        pltpu.VMEM((ns, half_mod), F32),
        pltpu.VMEM((hist + rows, D_LRU), F32),
        pltpu.VMEM((nb, D_LRU), F32),
        pltpu.VMEM((nb, S5_LANES), F32),
        pltpu.VMEM((nb, S5_LANES), F32),
        pltpu.VMEM((rows, D_IN), F32),
        pltpu.VMEM((2, rows, S5_BLOCK_LANES), F32),
        pltpu.VMEM((2, rows, S5_BLOCK_LANES), F32),
        pltpu.VMEM((rows, D_LRU), F32),
        pltpu.VMEM((rows, D_LRU), F32),
        pltpu.VMEM((rows, D_S5), F32),
        pltpu.VMEM((rows, D_LRU + D_S5), BF16),
        pltpu.VMEM((2, STAGE_ROWS, BG_COLS), BF16),
        pltpu.SemaphoreType.DMA((2,)),
    ]
    return pl.pallas_call(
        functools.partial(_mixer_kernel, n_chunks),
        grid=(n_chunks + 1,),
        in_specs=in_specs,
        out_specs=out_specs,
        out_shape=out_shape,
        scratch_shapes=scratch,
        compiler_params=pltpu.CompilerParams(
            dimension_semantics=("arbitrary",), vmem_limit_bytes=VMEM_LIMIT_BYTES),
        name="mixer",
    )(*[a for a, _ in inputs])


def _ffn_rows(x3, mod_ref, n2g_ref, wgate_ref, wup_ref, wdown_ref, gf_ref, act_ref):
    groups, mrows, _ = x3.shape
    rows = groups * mrows
    sh2 = mod_ref[:, 0:D_MODEL]
    sc2 = mod_ref[:, D_MODEL:2 * D_MODEL]
    g2 = mod_ref[:, 2 * D_MODEL:3 * D_MODEL]

    hn = _rms(x3, (n2g_ref[...] * (1.0 + sc2))[None]) + sh2[None]
    hb = hn.reshape(rows, D_MODEL).astype(BF16)
    for c in range(FFN_CHUNKS):
        cols = slice(c * FFN_CHUNK, (c + 1) * FFN_CHUNK)
        g = _dot(hb, wgate_ref[:, cols])
        up = _dot(hb, wup_ref[:, cols])
        act_ref[:, cols] = (g * jax.nn.sigmoid(g) * up).astype(BF16)
    ffn = _dot(act_ref[...], wdown_ref[...]).reshape(groups, mrows, D_MODEL)
    return _rms(x3 + g2[None] * ffn, gf_ref[...])


def _ffn_kernel(
        n_chunks,
        x_ref, mod2p_ref, mod2s_ref, xs_ref, n2g_ref, wgate_ref, wup_ref, wdown_ref, gf_ref,
        y_hbm, ys_o,
        act_buf, y_buf, y_sems):
    step = pl.program_id(0)
    slot = step % 2

    @pl.when(step < n_chunks)
    def _prompt_step():
        @pl.when(step >= 2)
        def _():
            for c in _batch_major_copies(y_buf, slot, y_hbm, y_sems, step - 2, to_hbm=True):
                c.wait()

        tl, nb, _ = x_ref.shape
        sub = FFN_SUB_ROWS // nb
        for h in range(tl // sub):
            y_buf[slot, pl.ds(h * sub, sub)] = _ffn_rows(
                x_ref[h * sub:(h + 1) * sub], mod2p_ref, n2g_ref, wgate_ref, wup_ref, wdown_ref, gf_ref,
                act_buf.at[pl.ds(h * FFN_SUB_ROWS, FFN_SUB_ROWS), :])
        for c in _batch_major_copies(y_buf, slot, y_hbm, y_sems, step, to_hbm=True):
            c.start()

    @pl.when(step == n_chunks)
    def _sample_step():
        ns = xs_ref.shape[0]
        ys_o[...] = _ffn_rows(xs_ref[...][None], mod2s_ref, n2g_ref, wgate_ref, wup_ref, wdown_ref, gf_ref,
                              act_buf.at[pl.ds(0, ns), :])[0]
        for s in range(max(n_chunks - 2, 0), n_chunks):
            for c in _batch_major_copies(y_buf, s % 2, y_hbm, y_sems, s, to_hbm=True):
                c.wait()


def _ffn(x32_t, mod2p, mod2s, x32_s, wgate_b, wup_b, wdown_b, w, out_shape_p, out_shape_s):
    t_len, nb, _ = x32_t.shape
    tl = FFN_TIME_CHUNK
    n_chunks = t_len // tl
    rows = tl * nb
    inputs = [
        (x32_t, pl.BlockSpec((tl, nb, D_MODEL), lambda i: (jnp.minimum(i, n_chunks - 1), 0, 0))),
        (mod2p, None), (mod2s, None), (x32_s, None), (w["n2g"], None),
        (wgate_b, None), (wup_b, None), (wdown_b, None), (w["gf"], None),
    ]
    in_specs = [spec if spec is not None else _const_spec(a.shape) for a, spec in inputs]
    scratch = [
        pltpu.VMEM((rows, D_FF), BF16),
        pltpu.VMEM((2, tl, nb, D_MODEL), F32),
        pltpu.SemaphoreType.DMA((2, nb)),
    ]
    return pl.pallas_call(
        functools.partial(_ffn_kernel, n_chunks),
        grid=(n_chunks + 1,),
        in_specs=in_specs,
        out_specs=[_HBM, _out_const_spec(out_shape_s)],
        out_shape=[jax.ShapeDtypeStruct(out_shape_p, F32), jax.ShapeDtypeStruct(out_shape_s, F32)],
        scratch_shapes=scratch,
        compiler_params=pltpu.CompilerParams(
            dimension_semantics=("arbitrary",), vmem_limit_bytes=VMEM_LIMIT_BYTES),
        name="ffn",
    )(*[a for a, _ in inputs])


def kernel(x_prompt, x_sample, state_conv, state_lru, state_s5_re, state_s5_im, c_prompt, c_sample,
           ada_w, ada_b, norm1_g, w_in, conv_w, conv_b, lru_wa, lru_ba, lru_wx, lru_bx, lru_lambda,
           s5_lambda_re, s5_lambda_im, s5_log_dt, s5_b_re, s5_b_im, s5_c_re, s5_c_im, s5_d, s5_w_glu,
           g_lru_out, g_s5_out, w_out, norm2_g, ffn_w_gate, ffn_w_up, ffn_w_down, final_norm_g):
    depth = ada_w.shape[0]
    assert depth == 1, "single-layer decoder step"
    bp, t_len, _ = x_prompt.shape
    bs, dec_t, _ = x_sample.shape
    assert bp == SUBLANES and dec_t == 1 and t_len % TIME_CHUNK == 0 and t_len % FFN_TIME_CHUNK == 0
    assert (FFN_TIME_CHUNK * SUBLANES) % FFN_SUB_ROWS == 0 and bs <= FFN_SUB_ROWS

    w = dict(
        n1g=norm1_g, win=w_in[0], cw=conv_w[0], cb=conv_b,
        wa=lru_wa.reshape(D_LRU, LRU_HEAD_DIM), wx=lru_wx.reshape(D_LRU, LRU_HEAD_DIM),
        ba=lru_ba, bx=lru_bx, lam=lru_lambda,
        slr=s5_lambda_re.reshape(1, S5_LANES), sli=s5_lambda_im.reshape(1, S5_LANES),
        sldt=jnp.repeat(s5_log_dt[0], S5_STATE).reshape(1, S5_LANES),
        bret=s5_b_re.reshape(S5_LANES, S5_GROUP).T, bimt=s5_b_im.reshape(S5_LANES, S5_GROUP).T,
        cre=s5_c_re.reshape(D_S5, S5_STATE), cim=s5_c_im.reshape(D_S5, S5_STATE),
        d=s5_d, wglu=s5_w_glu[0], glru=g_lru_out, gs5=g_s5_out, wout=w_out[0],
        n2g=norm2_g, wgate=ffn_w_gate[0], wup=ffn_w_up[0], wdown=ffn_w_down[0],
        gf=final_norm_g.reshape(1, D_MODEL),
    )

    (x32_t, mod2p, mod2s, conv_p, lru_p, s5r_p, s5i_p,
     x32_s, conv_s, lru_s, s5r_s, s5i_s, wgate_b, wup_b, wdown_b) = _mixer(
        x_prompt, c_prompt, c_sample, ada_w[0], ada_b, x_sample.reshape(bs, D_MODEL),
        state_conv[0].transpose(1, 0, 2), state_lru[0],
        state_s5_re[0].transpose(1, 2, 0), state_s5_im[0].transpose(1, 2, 0), w)
    y_prompt, y_s = _ffn(x32_t, mod2p, mod2s, x32_s, wgate_b, wup_b, wdown_b, w, x_prompt.shape, x32_s.shape)

    return (y_prompt, y_s.reshape(x_sample.shape), conv_p.transpose(1, 0, 2)[None], lru_p[None],
            s5r_p[None], s5i_p[None], conv_s.transpose(1, 0, 2)[None], lru_s[None],
            s5r_s.transpose(2, 0, 1)[None], s5i_s.transpose(2, 0, 1)[None])
```

```python
import functools

import jax
import jax.numpy as jnp
from jax import lax
from jax.experimental import pallas as pl
from jax.experimental.pallas import tpu as pltpu

F32 = jnp.float32
BF16 = jnp.bfloat16
I32 = jnp.int32

D_MODEL = 1024
D_LRU = 512
D_S5 = 512
D_IN = 2 * D_LRU + D_S5
LRU_HEAD_DIM = 64
CONV_W = 4
LRU_C = 8.0
S5_GROUP = 16
S5_GROUPS = D_S5 // S5_GROUP
S5_STATE = 64
S5_LANES = S5_GROUPS * S5_STATE
D_FF = 2816
N_MOD = 6
EPS = 1e-6

SUBLANES = 8
MXU_DIM = 256
VMEM_LIMIT_BYTES = 60 * 1024 * 1024

TIME_CHUNK = 64
FRONT_SLICES = 2
GATE_BLOCK = MXU_DIM
GATE_BLOCKS = D_LRU // GATE_BLOCK
S5_BLOCKS = 4
S5_BLOCK_GROUPS = S5_GROUPS // S5_BLOCKS
S5_BLOCK_CH = D_S5 // S5_BLOCKS
S5_BLOCK_LANES = S5_LANES // S5_BLOCKS
FFN_CHUNK = MXU_DIM
FFN_CHUNKS = D_FF // FFN_CHUNK
STAGE_ROWS = 256
STAGE_COLS = D_IN
STAGE_DEPTH = 3
BG_COLS = D_FF // 2
FFN_TIME_CHUNK = 128
FFN_SUB_ROWS = 512


def _dot(a, b):
    return jnp.dot(a, b, preferred_element_type=F32)


def _rms(x, g):
    return x * lax.rsqrt(jnp.mean(x * x, axis=-1, keepdims=True) + EPS) * g


def _log_sigmoid(x):
    return jnp.minimum(x, 0.0) - jnp.log1p(jnp.exp(-jnp.abs(x)))


def _sigmoid(x):
    return 0.5 * jnp.tanh(0.5 * x) + 0.5


def _sqrt_nonneg(x):
    return jnp.where(x > 0.0, x * lax.rsqrt(x), 0.0)


def _log2(n):
    assert n & (n - 1) == 0
    return n.bit_length() - 1


def _same_group(shape, row_group, col_group):
    r = lax.broadcasted_iota(I32, shape, 0) >> _log2(row_group)
    c = lax.broadcasted_iota(I32, shape, 1) >> _log2(col_group)
    return r == c


def _lane_tiled_block_diag(blk, reps, row_group):
    r, w = blk.shape
    n = w * reps
    sel = (lax.broadcasted_iota(I32, (w, n), 1) & (w - 1)) == lax.broadcasted_iota(I32, (w, n), 0)
    tiled = _dot(blk.astype(BF16), jnp.where(sel, 1.0, 0.0).astype(BF16))
    return jnp.where(_same_group((r, n), row_group, w), tiled, 0.0).astype(BF16)


def _split3(v):
    hi = v.astype(BF16)
    r1 = v - hi.astype(F32)
    mid = r1.astype(BF16)
    lo = (r1 - mid.astype(F32)).astype(BF16)
    return hi, mid, lo


def _flatten_rows(v):
    g, n = v.shape
    sel = (lax.broadcasted_iota(I32, (n, g * n), 1) & (n - 1)) == lax.broadcasted_iota(I32, (n, g * n), 0)
    sel = jnp.where(sel, 1.0, 0.0).astype(BF16)
    hi, mid, lo = (_dot(p, sel) for p in _split3(v))
    tiled = (hi + mid) + lo
    return jnp.sum(jnp.where(_same_group((g, g * n), 1, n), tiled, 0.0), axis=0, keepdims=True)


def _repeat_rows(v, reps):
    g, n = v.shape
    sel = (lax.broadcasted_iota(I32, (g * reps, g), 0) >> _log2(reps)) == lax.broadcasted_iota(I32, (g * reps, g), 1)
    sel = jnp.where(sel, 1.0, 0.0).astype(BF16)
    hi, mid, lo = (_dot(sel, p) for p in _split3(v))
    return (hi + mid) + lo


def _lru_gate_block(conv_j, wg_j, ba_j, bx_j, lam_j):
    gz = _dot(conv_j.astype(BF16), wg_j)
    r = _sigmoid(gz[:, :GATE_BLOCK] + ba_j)
    ig = _sigmoid(gz[:, GATE_BLOCK:] + bx_j)
    log_a = (LRU_C * _log_sigmoid(lam_j)) * r
    a = jnp.exp(log_a)
    mult = _sqrt_nonneg(-jnp.tanh(log_a) * (a * a + 1.0))
    return a, mult * (ig * conv_j)


def _s5_bu_block(ub, wb_ref, j):
    bu = _dot(ub[:, j * S5_BLOCK_CH:(j + 1) * S5_BLOCK_CH], wb_ref[j])
    return bu[:, :S5_BLOCK_LANES], bu[:, S5_BLOCK_LANES:]


def _s5_y_block(hr, hi, wc_ref, j):
    return (_dot(hr.astype(BF16), wc_ref[j, :S5_BLOCK_LANES, :])
            + _dot(hi.astype(BF16), wc_ref[j, S5_BLOCK_LANES:, :]))


def _s5_glu(y, u, d, wglu_ref):
    z = _dot(jax.nn.gelu(y + d * u).astype(BF16), wglu_ref[...])
    return z[:, :D_S5] * _sigmoid(z[:, D_S5:])


def _stream_pieces(pieces, stage, sems):
    depth = stage.shape[0]

    def dst(k):
        r, c = pieces[k][0].shape
        return stage.at[k % depth, pl.ds(0, r), pl.ds(0, c)]

    copies = [pltpu.make_async_copy(pieces[k][0], dst(k), sems.at[k % depth]) for k in range(len(pieces))]
    for k in range(min(depth - 1, len(pieces))):
        copies[k].start()
    for k, (_, consume) in enumerate(pieces):
        if k + depth - 1 < len(pieces):
            copies[k + depth - 1].start()
        copies[k].wait()
        consume(dst(k))


def _batch_major_copies(tm_bufs, slot, bm_hbm, sems, chunk, to_hbm):
    _, tl, nb, _ = tm_bufs.shape
    copies = []
    for b in range(nb):
        hbm = bm_hbm.at[b, pl.ds(chunk * tl, tl), :]
        vmem = tm_bufs.at[slot, :, b, :]
        src, dst = (vmem, hbm) if to_hbm else (hbm, vmem)
        copies.append(pltpu.make_async_copy(src, dst, sems.at[slot, b]))
    return copies


def _mixer_kernel(
        n_chunks,
        x_hbm, cp_ref, cs_ref, adaw_hbm, adab_ref,
        xs_ref, sconv_ref, slru_ref, ss5r_ref, ss5i_ref,
        n1g_ref, win_hbm, cw_ref, cb_ref, wa_ref, wx_ref, ba_ref, bx_ref, lam_ref,
        slr_ref, sli_ref, sldt_ref, bre_ref, bim_ref, cre_ref, cim_ref,
        d_ref, wglu_hbm, glru_ref, gs5_ref, wout_hbm, wgate_hbm, wup_hbm, wdown_hbm,
        o_ref, mod2p_o, mod2s_o, convp_o, lrup_o, s5rp_o, s5ip_o,
        xs_o, convs_o, lrus_o, s5rs_o, s5is_o, wgate_o, wup_o, wdown_o,
        x_buf, x_sems, stage, stage_sems, win_b, wglu_b, wout_b, wg_b, wb_b, wc_b, abr_s, abi_s,
        mod1p, mod1s, xc_buf, h_lru, h_s5r, h_s5i, p_buf, hr_buf, hi_buf, a_buf, b_buf, y_buf, m_buf,
        bg_buf, bg_sems):
    step = pl.program_id(0)
    slot = step % 2
    _, tl, nb, _ = x_buf.shape
    rows = tl * nb
    hist = (CONV_W - 1) * nb
    half_mod = (N_MOD // 2) * D_MODEL

    bg_pieces = []
    for src, dst in ((wgate_hbm, wgate_o), (wup_hbm, wup_o)):
        for rb in range(D_MODEL // STAGE_ROWS):
            for ch in range(D_FF // BG_COLS):
                win = (pl.ds(rb * STAGE_ROWS, STAGE_ROWS), pl.ds(ch * BG_COLS, BG_COLS))
                bg_pieces.append((src.at[win], dst.at[win]))
    for rb in range(D_FF // STAGE_ROWS):
        win = (pl.ds(rb * STAGE_ROWS, STAGE_ROWS), pl.ds(0, D_MODEL))
        bg_pieces.append((wdown_hbm.at[win], wdown_o.at[win]))
    assert len(bg_pieces) < n_chunks

    def bg_in(k):
        r, c = bg_pieces[k][0].shape
        return pltpu.make_async_copy(bg_pieces[k][0], stage.at[k % STAGE_DEPTH, pl.ds(0, r), pl.ds(0, c)],
                                     stage_sems.at[k % STAGE_DEPTH])

    def bg_out(k):
        r, c = bg_pieces[k][0].shape
        return pltpu.make_async_copy(bg_buf.at[k % 2, pl.ds(0, r), pl.ds(0, c)], bg_pieces[k][1],
                                     bg_sems.at[k % 2])

    @pl.when(step == 0)
    def _prologue():
        for c in _batch_major_copies(x_buf, 0, x_hbm, x_sems, 0, to_hbm=False):
            c.start()
        xc_buf[0:hist, :] = jnp.zeros((hist, D_LRU), F32)
        h_lru[...] = jnp.zeros_like(h_lru)
        h_s5r[...] = jnp.zeros_like(h_s5r)
        h_s5i[...] = jnp.zeros_like(h_s5i)

        cp = cp_ref[...]
        cs = cs_ref[...]
        silu_p = (cp * _sigmoid(cp)).astype(BF16)
        silu_s = (cs * _sigmoid(cs)).astype(BF16)
        pieces = []

        def ada_piece(cg, kb):
            cols = slice(cg * STAGE_COLS, (cg + 1) * STAGE_COLS)
            first = cols.start < half_mod
            dst_p, dst_s = (mod1p, mod1s) if first else (mod2p_o, mod2s_o)
            dcols = cols if first else slice(cols.start - half_mod, cols.stop - half_mod)
            krows = slice(kb * STAGE_ROWS, (kb + 1) * STAGE_ROWS)

            def consume(view):
                w = view[...].astype(BF16)
                part_p = _dot(silu_p[:, krows], w)
                part_s = _dot(silu_s[:, krows], w)
                if kb == 0:
                    dst_p[:, dcols] = part_p + adab_ref[:, cols]
                    dst_s[:, dcols] = part_s + adab_ref[:, cols]
                else:
                    dst_p[:, dcols] += part_p
                    dst_s[:, dcols] += part_s
            return adaw_hbm.at[pl.ds(kb * STAGE_ROWS, STAGE_ROWS), pl.ds(cg * STAGE_COLS, STAGE_COLS)], consume

        def cast_piece(src_hbm, dst, rb):
            ncols = dst.shape[1]

            def consume(view):
                dst[rb * STAGE_ROWS:(rb + 1) * STAGE_ROWS, :] = view[...].astype(BF16)
            return src_hbm.at[pl.ds(rb * STAGE_ROWS, STAGE_ROWS), pl.ds(0, ncols)], consume

        for rb in range(win_b.shape[0] // STAGE_ROWS):
            pieces.append(cast_piece(win_hbm, win_b, rb))
        for cg in range(N_MOD * D_MODEL // STAGE_COLS):
            for kb in range(D_MODEL // STAGE_ROWS):
                pieces.append(ada_piece(cg, kb))
        for src_hbm, dst in ((wglu_hbm, wglu_b), (wout_hbm, wout_b)):
            for rb in range(dst.shape[0] // STAGE_ROWS):
                pieces.append(cast_piece(src_hbm, dst, rb))
        _stream_pieces(pieces, stage, stage_sems)
        bg_in(0).start()

        for j in range(GATE_BLOCKS):
            rs = slice(j * GATE_BLOCK, (j + 1) * GATE_BLOCK)
            reps = GATE_BLOCK // LRU_HEAD_DIM
            wg_b[j, :, :GATE_BLOCK] = _lane_tiled_block_diag(wa_ref[rs, :], reps, LRU_HEAD_DIM)
            wg_b[j, :, GATE_BLOCK:] = _lane_tiled_block_diag(wx_ref[rs, :], reps, LRU_HEAD_DIM)

        lr = slr_ref[...]
        li = sli_ref[...]
        diag = (lax.broadcasted_iota(I32, (S5_GROUPS, S5_GROUPS), 0)
                == lax.broadcasted_iota(I32, (S5_GROUPS, S5_GROUPS), 1))
        log_dt = jnp.sum(jnp.where(diag, sldt_ref[...], 0.0), axis=1, keepdims=True)
        dt = jnp.exp(log_dt)
        mag = jnp.exp(lr * dt)
        ab_r = mag * jnp.cos(li * dt)
        ab_i = mag * jnp.sin(li * dt)
        den = lr * lr + li * li
        fr = ((ab_r - 1.0) * lr + ab_i * li) / den
        fi = (ab_i * lr - (ab_r - 1.0) * li) / den
        abr_s[...] = _flatten_rows(ab_r)
        abi_s[...] = _flatten_rows(ab_i)
        br = bre_ref[...]
        bi = bim_ref[...]
        fr_c = _repeat_rows(fr, S5_GROUP)
        fi_c = _repeat_rows(fi, S5_GROUP)
        bb_r = fr_c * br - fi_c * bi
        bb_i = fr_c * bi + fi_c * br
        for j in range(S5_BLOCKS):
            cs_rows = slice(j * S5_BLOCK_CH, (j + 1) * S5_BLOCK_CH)
            bt_r = _lane_tiled_block_diag(bb_r[cs_rows, :], S5_BLOCK_GROUPS, S5_GROUP)
            bt_i = _lane_tiled_block_diag(bb_i[cs_rows, :], S5_BLOCK_GROUPS, S5_GROUP)
            wb_b[j, :, :S5_BLOCK_LANES] = bt_r
            wb_b[j, :, S5_BLOCK_LANES:] = bt_i
            ct_r = _lane_tiled_block_diag(cre_ref[cs_rows, :], S5_BLOCK_GROUPS, S5_GROUP)
            ct_i = _lane_tiled_block_diag(-cim_ref[cs_rows, :], S5_BLOCK_GROUPS, S5_GROUP)
            wc_b[j, :S5_BLOCK_LANES, :] = ct_r.astype(F32).T.astype(BF16)
            wc_b[j, S5_BLOCK_LANES:, :] = ct_i.astype(F32).T.astype(BF16)

    @pl.when(step < n_chunks)
    def _prompt_step():
        @pl.when(step + 1 < n_chunks)
        def _():
            for c in _batch_major_copies(x_buf, 1 - slot, x_hbm, x_sems, step + 1, to_hbm=False):
                c.start()

        for c in _batch_major_copies(x_buf, slot, x_hbm, x_sems, step, to_hbm=False):
            c.wait()

        sh1 = mod1p[:, 0:D_MODEL]
        sc1 = mod1p[:, D_MODEL:2 * D_MODEL]
        g1 = mod1p[:, 2 * D_MODEL:3 * D_MODEL]

        scale1 = (n1g_ref[...] * (1.0 + sc1))[None]
        tsub = tl // FRONT_SLICES

        def project(s):
            r0, r1 = s * tsub * nb, (s + 1) * tsub * nb
            xs3 = x_buf[slot, pl.ds(s * tsub, tsub)]
            hn = _rms(xs3, scale1) + sh1[None]
            p_buf[r0:r1, :] = _dot(hn.reshape(r1 - r0, D_MODEL).astype(BF16), win_b[...])
            xc_buf[hist + r0:hist + r1, :] = p_buf[r0:r1, 0:D_LRU]

        def lru_gates(s):
            r0, r1 = s * tsub * nb, (s + 1) * tsub * nb
            for j in range(GATE_BLOCKS):
                sl = slice(j * GATE_BLOCK, (j + 1) * GATE_BLOCK)
                conv_j = cb_ref[:, sl]
                for k in range(CONV_W):
                    conv_j = conv_j + cw_ref[k:k + 1, sl] * xc_buf[k * nb + r0:k * nb + r1, sl]
                a, b = _lru_gate_block(conv_j, wg_b[j], ba_ref[:, sl], bx_ref[:, sl], lam_ref[:, sl])
                a_buf[r0:r1, sl] = a
                b_buf[r0:r1, sl] = b

        def lru_scan():
            h = h_lru[...]
            for t in range(tl):
                h = a_buf[t * nb:(t + 1) * nb, :] * h + b_buf[t * nb:(t + 1) * nb, :]
                b_buf[t * nb:(t + 1) * nb, :] = h
            h_lru[...] = h
            lru_out = b_buf[...] * jax.nn.gelu(p_buf[:, D_LRU:2 * D_LRU])
            m_buf[:, 0:D_LRU] = _rms(lru_out, glru_ref[...]).astype(BF16)

        def s5_input(j):
            ub = p_buf[:, 2 * D_LRU + j * S5_BLOCK_CH:2 * D_LRU + (j + 1) * S5_BLOCK_CH].astype(BF16)
            bu = _dot(ub, wb_b[j])
            hr_buf[j % 2] = bu[:, :S5_BLOCK_LANES]
            hi_buf[j % 2] = bu[:, S5_BLOCK_LANES:]

        def s5_scan(j):
            sl = slice(j * S5_BLOCK_LANES, (j + 1) * S5_BLOCK_LANES)
            hr_j = hr_buf.at[j % 2]
            hi_j = hi_buf.at[j % 2]
            ar = jnp.broadcast_to(abr_s[:, sl], (nb, S5_BLOCK_LANES))
            ai = jnp.broadcast_to(abi_s[:, sl], (nb, S5_BLOCK_LANES))
            hr = h_s5r[:, sl]
            hi = h_s5i[:, sl]
            for t in range(tl):
                rs = slice(t * nb, (t + 1) * nb)
                hr, hi = ar * hr - ai * hi + hr_j[rs, :], ar * hi + ai * hr + hi_j[rs, :]
                hr_j[rs, :] = hr
                hi_j[rs, :] = hi
            h_s5r[:, sl] = hr
            h_s5i[:, sl] = hi

        def s5_readout(j):
            y_buf[:, j * S5_BLOCK_CH:(j + 1) * S5_BLOCK_CH] = _s5_y_block(
                hr_buf[j % 2], hi_buf[j % 2], wc_b, j)

        for s in range(FRONT_SLICES):
            project(s)
        lru_gates(0)
        s5_input(0)
        s5_input(1)
        for s in range(1, FRONT_SLICES):
            lru_gates(s)
        for j in range(S5_BLOCKS):
            s5_scan(j)
            s5_readout(j)
            if j + 2 < S5_BLOCKS:
                s5_input(j + 2)
            if j == 1:
                lru_scan()

        s5_out = _s5_glu(y_buf[...], p_buf[:, 2 * D_LRU:], d_ref[...], wglu_b)
        m_buf[:, D_LRU:] = _rms(s5_out, gs5_ref[...]).astype(BF16)

        mix = _dot(m_buf[...], wout_b[...]).reshape(tl, nb, D_MODEL)
        o_ref[...] = x_buf[slot] + g1[None] * mix

        xc_buf[0:hist, :] = xc_buf[rows:rows + hist, :]

    for k in range(len(bg_pieces)):
        @pl.when(step == k + 1)
        def _bg_task(k=k):
            r, c = bg_pieces[k][0].shape
            bg_in(k).wait()
            if k >= 2:
                bg_out(k - 2).wait()
            bg_buf[k % 2, 0:r, 0:c] = stage[k % STAGE_DEPTH, 0:r, 0:c].astype(BF16)
            bg_out(k).start()
            if k + 1 < len(bg_pieces):
                bg_in(k + 1).start()

    @pl.when(step == n_chunks)
    def _sample_step():
        for k in range(max(len(bg_pieces) - 2, 0), len(bg_pieces)):
            bg_out(k).wait()
        for k in range(CONV_W - 1):
            convp_o[k] = xc_buf[k * nb:(k + 1) * nb, :]
        lrup_o[...] = h_lru[...]
        s5rp_o[...] = h_s5r[...].reshape(s5rp_o.shape)
        s5ip_o[...] = h_s5i[...].reshape(s5ip_o.shape)

        ns = xs_ref.shape[0]
        sh1 = mod1s[:, 0:D_MODEL]
        sc1 = mod1s[:, D_MODEL:2 * D_MODEL]
        g1 = mod1s[:, 2 * D_MODEL:3 * D_MODEL]

        x = xs_ref[...]
        hn = _rms(x, n1g_ref[...] * (1.0 + sc1)) + sh1
        p = _dot(hn.astype(BF16), win_b[...])
        lru_x = p[:, 0:D_LRU]

        for k in range(CONV_W - 2):
            convs_o[k] = sconv_ref[k + 1]
        convs_o[CONV_W - 2] = lru_x
        for j in range(GATE_BLOCKS):
            sl = slice(j * GATE_BLOCK, (j + 1) * GATE_BLOCK)
            conv_j = cb_ref[:, sl] + cw_ref[CONV_W - 1:CONV_W, sl] * lru_x[:, sl]
            for k in range(CONV_W - 1):
                conv_j = conv_j + cw_ref[k:k + 1, sl] * sconv_ref[k, :, sl]
            a, b = _lru_gate_block(conv_j, wg_b[j], ba_ref[:, sl], bx_ref[:, sl], lam_ref[:, sl])
            lrus_o[:, sl] = a * slru_ref[:, sl] + b
        lru_out = lrus_o[...] * jax.nn.gelu(p[:, D_LRU:2 * D_LRU])
        ms_buf = m_buf.at[pl.ds(0, ns), :]
        ms_buf[:, 0:D_LRU] = _rms(lru_out, glru_ref[...]).astype(BF16)

        u = p[:, 2 * D_LRU:]
        ub = u.astype(BF16)
        ys_buf = y_buf.at[pl.ds(0, ns), :]
        h0r = ss5r_ref[...].reshape(S5_LANES, ns).T
        h0i = ss5i_ref[...].reshape(S5_LANES, ns).T
        for j in range(S5_BLOCKS):
            sl = slice(j * S5_BLOCK_LANES, (j + 1) * S5_BLOCK_LANES)
            bu_r, bu_i = _s5_bu_block(ub, wb_b, j)
            ar = abr_s[:, sl]
            ai = abi_s[:, sl]
            hr = ar * h0r[:, sl] - ai * h0i[:, sl] + bu_r
            hi = ar * h0i[:, sl] + ai * h0r[:, sl] + bu_i
            gs = slice(j * S5_BLOCK_GROUPS, (j + 1) * S5_BLOCK_GROUPS)
            s5rs_o[gs] = hr.T.reshape(S5_BLOCK_GROUPS, S5_STATE, ns)
            s5is_o[gs] = hi.T.reshape(S5_BLOCK_GROUPS, S5_STATE, ns)
            ys_buf[:, j * S5_BLOCK_CH:(j + 1) * S5_BLOCK_CH] = _s5_y_block(hr, hi, wc_b, j)
        s5_out = _s5_glu(ys_buf[...], u, d_ref[...], wglu_b)
        ms_buf[:, D_LRU:] = _rms(s5_out, gs5_ref[...]).astype(BF16)
        xs_o[...] = x + g1 * _dot(ms_buf[...], wout_b[...])


def _const_spec(shape):
    nd = len(shape)
    return pl.BlockSpec(shape, lambda i, _nd=nd: (0,) * _nd, pipeline_mode=pl.Buffered(1))


def _out_const_spec(shape):
    nd = len(shape)
    return pl.BlockSpec(shape, lambda i, _nd=nd: (0,) * _nd)


_HBM = pl.BlockSpec(memory_space=pl.ANY)


def _mixer(x_prompt, c_prompt, c_sample, ada_w, ada_b, x_s, sconv, slru, ss5r, ss5i, w):
    nb, t_len, _ = x_prompt.shape
    ns = x_s.shape[0]
    tl = TIME_CHUNK
    n_chunks = t_len // tl
    rows = tl * nb
    hist = (CONV_W - 1) * nb
    half_mod = (N_MOD // 2) * D_MODEL

    inputs = [
        (x_prompt, _HBM), (c_prompt, None), (c_sample, None), (ada_w, _HBM), (ada_b, None),
        (x_s, None), (sconv, None), (slru, None), (ss5r, None), (ss5i, None),
        (w["n1g"], None), (w["win"], _HBM), (w["cw"], None), (w["cb"], None), (w["wa"], None), (w["wx"], None),
        (w["ba"], None), (w["bx"], None), (w["lam"], None),
        (w["slr"], None), (w["sli"], None), (w["sldt"], None), (w["bre"], None), (w["bim"], None),
        (w["cre"], None), (w["cim"], None),
        (w["d"], None), (w["wglu"], _HBM), (w["glru"], None), (w["gs5"], None), (w["wout"], _HBM),
        (w["wgate"], _HBM), (w["wup"], _HBM), (w["wdown"], _HBM),
    ]
    in_specs = [spec if spec is not None else _const_spec(a.shape) for a, spec in inputs]

    out_shapes = [
        ((t_len, nb, D_MODEL), pl.BlockSpec((tl, nb, D_MODEL), lambda i: (jnp.minimum(i, n_chunks - 1), 0, 0))),
        ((nb, half_mod), None), ((ns, half_mod), None),
        ((CONV_W - 1, nb, D_LRU), None), ((nb, D_LRU), None),
        ((nb, S5_GROUPS, S5_STATE), None), ((nb, S5_GROUPS, S5_STATE), None),
        ((ns, D_MODEL), None), ((CONV_W - 1, ns, D_LRU), None), ((ns, D_LRU), None),
        ((S5_GROUPS, S5_STATE, ns), None), ((S5_GROUPS, S5_STATE, ns), None),
    ]
    out_specs = [spec if spec is not None else _out_const_spec(s) for s, spec in out_shapes]
    out_shape = [jax.ShapeDtypeStruct(s, F32) for s, _ in out_shapes]
    for wkey in ("wgate", "wup", "wdown"):
        out_specs.append(_HBM)
        out_shape.append(jax.ShapeDtypeStruct(w[wkey].shape, BF16))

    scratch = [
        pltpu.VMEM((2, tl, nb, D_MODEL), F32),
        pltpu.SemaphoreType.DMA((2, nb)),
        pltpu.VMEM((STAGE_DEPTH, STAGE_ROWS, STAGE_COLS), F32),
        pltpu.SemaphoreType.DMA((STAGE_DEPTH,)),
        pltpu.VMEM((D_MODEL, D_IN), BF16),
        pltpu.VMEM((D_S5, 2 * D_S5), BF16),
        pltpu.VMEM((D_LRU + D_S5, D_MODEL), BF16),
        pltpu.VMEM((GATE_BLOCKS, GATE_BLOCK, 2 * GATE_BLOCK), BF16),
        pltpu.VMEM((S5_BLOCKS, S5_BLOCK_CH, 2 * S5_BLOCK_LANES), BF16),
        pltpu.VMEM((S5_BLOCKS, 2 * S5_BLOCK_LANES, S5_BLOCK_CH), BF16),
        pltpu.VMEM((1, S5_LANES), F32),
        pltpu.VMEM((1, S5_LANES), F32),
        pltpu.VMEM((nb, half_mod), F32),
        pltpu.VMEM((ns, half_mod), F32),
        pltpu.VMEM((hist + rows, D_LRU), F32),
        pltpu.VMEM((nb, D_LRU), F32),
        pltpu.VMEM((nb, S5_LANES), F32),
        pltpu.VMEM((nb, S5_LANES), F32),
        pltpu.VMEM((rows, D_IN), F32),
        pltpu.VMEM((2, rows, S5_BLOCK_LANES), F32),
        pltpu.VMEM((2, rows, S5_BLOCK_LANES), F32),
        pltpu.VMEM((rows, D_LRU), F32),
        pltpu.VMEM((rows, D_LRU), F32),
        pltpu.VMEM((rows, D_S5), F32),
        pltpu.VMEM((rows, D_LRU + D_S5), BF16),
        pltpu.VMEM((2, STAGE_ROWS, BG_COLS), BF16),
        pltpu.SemaphoreType.DMA((2,)),
    ]
    return pl.pallas_call(
        functools.partial(_mixer_kernel, n_chunks),
        grid=(n_chunks + 1,),
        in_specs=in_specs,
        out_specs=out_specs,
        out_shape=out_shape,
        scratch_shapes=scratch,
        compiler_params=pltpu.CompilerParams(
            dimension_semantics=("arbitrary",), vmem_limit_bytes=VMEM_LIMIT_BYTES),
        name="mixer",
    )(*[a for a, _ in inputs])


def _ffn_rows(x3, mod_ref, n2g_ref, wgate_ref, wup_ref, wdown_ref, gf_ref, act_ref):
    groups, mrows, _ = x3.shape
    rows = groups * mrows
    sh2 = mod_ref[:, 0:D_MODEL]
    sc2 = mod_ref[:, D_MODEL:2 * D_MODEL]
    g2 = mod_ref[:, 2 * D_MODEL:3 * D_MODEL]

    hn = _rms(x3, (n2g_ref[...] * (1.0 + sc2))[None]) + sh2[None]
    hb = hn.reshape(rows, D_MODEL).astype(BF16)
    for c in range(FFN_CHUNKS):
        cols = slice(c * FFN_CHUNK, (c + 1) * FFN_CHUNK)
        g = _dot(hb, wgate_ref[:, cols])
        up = _dot(hb, wup_ref[:, cols])
        act_ref[:, cols] = (g * jax.nn.sigmoid(g) * up).astype(BF16)
    ffn = _dot(act_ref[...], wdown_ref[...]).reshape(groups, mrows, D_MODEL)
    return _rms(x3 + g2[None] * ffn, gf_ref[...])


def _ffn_kernel(
        n_chunks,
        x_ref, mod2p_ref, mod2s_ref, xs_ref, n2g_ref, wgate_ref, wup_ref, wdown_ref, gf_ref,
        y_hbm, ys_o,
        act_buf, y_buf, y_sems):
    step = pl.program_id(0)
    slot = step % 2

    @pl.when(step < n_chunks)
    def _prompt_step():
        @pl.when(step >= 2)
        def _():
            for c in _batch_major_copies(y_buf, slot, y_hbm, y_sems, step - 2, to_hbm=True):
                c.wait()

        tl, nb, _ = x_ref.shape
        sub = FFN_SUB_ROWS // nb
        for h in range(tl // sub):
            y_buf[slot, pl.ds(h * sub, sub)] = _ffn_rows(
                x_ref[h * sub:(h + 1) * sub], mod2p_ref, n2g_ref, wgate_ref, wup_ref, wdown_ref, gf_ref,
                act_buf.at[pl.ds(h * FFN_SUB_ROWS, FFN_SUB_ROWS), :])
        for c in _batch_major_copies(y_buf, slot, y_hbm, y_sems, step, to_hbm=True):
            c.start()

    @pl.when(step == n_chunks)
    def _sample_step():
        ns = xs_ref.shape[0]
        ys_o[...] = _ffn_rows(xs_ref[...][None], mod2s_ref, n2g_ref, wgate_ref, wup_ref, wdown_ref, gf_ref,
                              act_buf.at[pl.ds(0, ns), :])[0]
        for s in range(max(n_chunks - 2, 0), n_chunks):
            for c in _batch_major_copies(y_buf, s % 2, y_hbm, y_sems, s, to_hbm=True):
                c.wait()


def _ffn(x32_t, mod2p, mod2s, x32_s, wgate_b, wup_b, wdown_b, w, out_shape_p, out_shape_s):
    t_len, nb, _ = x32_t.shape
    tl = FFN_TIME_CHUNK
    n_chunks = t_len // tl
    rows = tl * nb
    inputs = [
        (x32_t, pl.BlockSpec((tl, nb, D_MODEL), lambda i: (jnp.minimum(i, n_chunks - 1), 0, 0))),
        (mod2p, None), (mod2s, None), (x32_s, None), (w["n2g"], None),
        (wgate_b, None), (wup_b, None), (wdown_b, None), (w["gf"], None),
    ]
    in_specs = [spec if spec is not None else _const_spec(a.shape) for a, spec in inputs]
    scratch = [
        pltpu.VMEM((rows, D_FF), BF16),
        pltpu.VMEM((2, tl, nb, D_MODEL), F32),
        pltpu.SemaphoreType.DMA((2, nb)),
    ]
    return pl.pallas_call(
        functools.partial(_ffn_kernel, n_chunks),
        grid=(n_chunks + 1,),
        in_specs=in_specs,
        out_specs=[_HBM, _out_const_spec(out_shape_s)],
        out_shape=[jax.ShapeDtypeStruct(out_shape_p, F32), jax.ShapeDtypeStruct(out_shape_s, F32)],
        scratch_shapes=scratch,
        compiler_params=pltpu.CompilerParams(
            dimension_semantics=("arbitrary",), vmem_limit_bytes=VMEM_LIMIT_BYTES),
        name="ffn",
    )(*[a for a, _ in inputs])


def kernel(x_prompt, x_sample, state_conv, state_lru, state_s5_re, state_s5_im, c_prompt, c_sample,
           ada_w, ada_b, norm1_g, w_in, conv_w, conv_b, lru_wa, lru_ba, lru_wx, lru_bx, lru_lambda,
           s5_lambda_re, s5_lambda_im, s5_log_dt, s5_b_re, s5_b_im, s5_c_re, s5_c_im, s5_d, s5_w_glu,
           g_lru_out, g_s5_out, w_out, norm2_g, ffn_w_gate, ffn_w_up, ffn_w_down, final_norm_g):
    depth = ada_w.shape[0]
    assert depth == 1, "single-layer decoder step"
    bp, t_len, _ = x_prompt.shape
    bs, dec_t, _ = x_sample.shape
    assert bp == SUBLANES and dec_t == 1 and t_len % TIME_CHUNK == 0 and t_len % FFN_TIME_CHUNK == 0
    assert (FFN_TIME_CHUNK * SUBLANES) % FFN_SUB_ROWS == 0 and bs <= FFN_SUB_ROWS

    w = dict(
        n1g=norm1_g, win=w_in[0], cw=conv_w[0], cb=conv_b,
        wa=lru_wa.reshape(D_LRU, LRU_HEAD_DIM), wx=lru_wx.reshape(D_LRU, LRU_HEAD_DIM),
        ba=lru_ba, bx=lru_bx, lam=lru_lambda,
        slr=s5_lambda_re[0], sli=s5_lambda_im[0], sldt=s5_log_dt,
        bre=s5_b_re.transpose(0, 1, 3, 2).reshape(D_S5, S5_STATE),
        bim=s5_b_im.transpose(0, 1, 3, 2).reshape(D_S5, S5_STATE),
        cre=s5_c_re.reshape(D_S5, S5_STATE), cim=s5_c_im.reshape(D_S5, S5_STATE),
        d=s5_d, wglu=s5_w_glu[0], glru=g_lru_out, gs5=g_s5_out, wout=w_out[0],
        n2g=norm2_g, wgate=ffn_w_gate[0], wup=ffn_w_up[0], wdown=ffn_w_down[0],
        gf=final_norm_g.reshape(1, D_MODEL),
    )

    (x32_t, mod2p, mod2s, conv_p, lru_p, s5r_p, s5i_p,
     x32_s, conv_s, lru_s, s5r_s, s5i_s, wgate_b, wup_b, wdown_b) = _mixer(
        x_prompt, c_prompt, c_sample, ada_w[0], ada_b, x_sample.reshape(bs, D_MODEL),
        state_conv[0].transpose(1, 0, 2), state_lru[0],
        state_s5_re[0].transpose(1, 2, 0), state_s5_im[0].transpose(1, 2, 0), w)
    y_prompt, y_s = _ffn(x32_t, mod2p, mod2s, x32_s, wgate_b, wup_b, wdown_b, w, x_prompt.shape, x32_s.shape)

    return (y_prompt, y_s.reshape(x_sample.shape), conv_p.transpose(1, 0, 2)[None], lru_p[None],
            s5r_p[None], s5i_p[None], conv_s.transpose(1, 0, 2)[None], lru_s[None],
            s5r_s.transpose(2, 0, 1)[None], s5i_s.transpose(2, 0, 1)[None])
```

```python
import functools

import jax
import jax.numpy as jnp
from jax import lax
from jax.experimental import pallas as pl
from jax.experimental.pallas import tpu as pltpu

F32 = jnp.float32
BF16 = jnp.bfloat16
I32 = jnp.int32

D_MODEL = 1024
D_LRU = 512
D_S5 = 512
D_IN = 2 * D_LRU + D_S5
LRU_HEAD_DIM = 64
CONV_W = 4
LRU_C = 8.0
S5_GROUP = 16
S5_GROUPS = D_S5 // S5_GROUP
S5_STATE = 64
S5_LANES = S5_GROUPS * S5_STATE
D_FF = 2816
N_MOD = 6
EPS = 1e-6

SUBLANES = 8
MXU_DIM = 256
VMEM_LIMIT_BYTES = 60 * 1024 * 1024

TIME_CHUNK = 64
FRONT_SLICES = 2
GATE_BLOCK = MXU_DIM
GATE_BLOCKS = D_LRU // GATE_BLOCK
S5_BLOCKS = 4
S5_BLOCK_GROUPS = S5_GROUPS // S5_BLOCKS
S5_BLOCK_CH = D_S5 // S5_BLOCKS
S5_BLOCK_LANES = S5_LANES // S5_BLOCKS
FFN_CHUNK = MXU_DIM
FFN_CHUNKS = D_FF // FFN_CHUNK
STAGE_ROWS = 256
STAGE_COLS = D_IN
STAGE_DEPTH = 3
BG_COLS = D_FF // 2
BG_DMA_PRIORITY = 1
FFN_TIME_CHUNK = 128
FFN_SUB_ROWS = 512


def _dot(a, b):
    return jnp.dot(a, b, preferred_element_type=F32)


def _rms(x, g):
    return x * lax.rsqrt(jnp.mean(x * x, axis=-1, keepdims=True) + EPS) * g


def _log_sigmoid(x):
    return jnp.minimum(x, 0.0) - jnp.log1p(jnp.exp(-jnp.abs(x)))


def _sigmoid(x):
    return 0.5 * jnp.tanh(0.5 * x) + 0.5


def _sqrt_nonneg(x):
    return jnp.where(x > 0.0, x * lax.rsqrt(x), 0.0)


def _log2(n):
    assert n & (n - 1) == 0
    return n.bit_length() - 1


def _same_group(shape, row_group, col_group):
    r = lax.broadcasted_iota(I32, shape, 0) >> _log2(row_group)
    c = lax.broadcasted_iota(I32, shape, 1) >> _log2(col_group)
    return r == c


def _lane_tiled_block_diag(blk, reps, row_group):
    r, w = blk.shape
    n = w * reps
    sel = (lax.broadcasted_iota(I32, (w, n), 1) & (w - 1)) == lax.broadcasted_iota(I32, (w, n), 0)
    tiled = _dot(blk.astype(BF16), jnp.where(sel, 1.0, 0.0).astype(BF16))
    return jnp.where(_same_group((r, n), row_group, w), tiled, 0.0).astype(BF16)


def _split3(v):
    hi = v.astype(BF16)
    r1 = v - hi.astype(F32)
    mid = r1.astype(BF16)
    lo = (r1 - mid.astype(F32)).astype(BF16)
    return hi, mid, lo


def _flatten_rows(v):
    g, n = v.shape
    sel = (lax.broadcasted_iota(I32, (n, g * n), 1) & (n - 1)) == lax.broadcasted_iota(I32, (n, g * n), 0)
    sel = jnp.where(sel, 1.0, 0.0).astype(BF16)
    hi, mid, lo = (_dot(p, sel) for p in _split3(v))
    tiled = (hi + mid) + lo
    return jnp.sum(jnp.where(_same_group((g, g * n), 1, n), tiled, 0.0), axis=0, keepdims=True)


def _repeat_rows(v, reps):
    g, n = v.shape
    sel = (lax.broadcasted_iota(I32, (g * reps, g), 0) >> _log2(reps)) == lax.broadcasted_iota(I32, (g * reps, g), 1)
    sel = jnp.where(sel, 1.0, 0.0).astype(BF16)
    hi, mid, lo = (_dot(sel, p) for p in _split3(v))
    return (hi + mid) + lo


def _lru_gate_block(conv_j, wg_j, ba_j, bx_j, lam_j):
    gz = _dot(conv_j.astype(BF16), wg_j)
    r = _sigmoid(gz[:, :GATE_BLOCK] + ba_j)
    ig = _sigmoid(gz[:, GATE_BLOCK:] + bx_j)
    log_a = (LRU_C * _log_sigmoid(lam_j)) * r
    a = jnp.exp(log_a)
    mult = _sqrt_nonneg(-jnp.tanh(log_a) * (a * a + 1.0))
    return a, mult * (ig * conv_j)


def _s5_bu_block(ub, wb_ref, j):
    bu = _dot(ub[:, j * S5_BLOCK_CH:(j + 1) * S5_BLOCK_CH], wb_ref[j])
    return bu[:, :S5_BLOCK_LANES], bu[:, S5_BLOCK_LANES:]


def _s5_y_block(hr, hi, wc_ref, j):
    return (_dot(hr.astype(BF16), wc_ref[j, :S5_BLOCK_LANES, :])
            + _dot(hi.astype(BF16), wc_ref[j, S5_BLOCK_LANES:, :]))


def _s5_glu(y, u, d, wglu_ref):
    z = _dot(jax.nn.gelu(y + d * u).astype(BF16), wglu_ref[...])
    return z[:, :D_S5] * _sigmoid(z[:, D_S5:])


def _stream_pieces(pieces, stage, sems):
    depth = stage.shape[0]

    def dst(k):
        r, c = pieces[k][0].shape
        return stage.at[k % depth, pl.ds(0, r), pl.ds(0, c)]

    copies = [pltpu.make_async_copy(pieces[k][0], dst(k), sems.at[k % depth]) for k in range(len(pieces))]
    for k in range(min(depth - 1, len(pieces))):
        copies[k].start()
    for k, (_, consume) in enumerate(pieces):
        if k + depth - 1 < len(pieces):
            copies[k + depth - 1].start()
        copies[k].wait()
        consume(dst(k))


def _batch_major_copies(tm_bufs, slot, bm_hbm, sems, chunk, to_hbm):
    _, tl, nb, _ = tm_bufs.shape
    copies = []
    for b in range(nb):
        hbm = bm_hbm.at[b, pl.ds(chunk * tl, tl), :]
        vmem = tm_bufs.at[slot, :, b, :]
        src, dst = (vmem, hbm) if to_hbm else (hbm, vmem)
        copies.append(pltpu.make_async_copy(src, dst, sems.at[slot, b]))
    return copies


def _mixer_kernel(
        n_chunks,
        x_hbm, cp_ref, cs_ref, adaw_hbm, adab_ref,
        xs_ref, sconv_ref, slru_ref, ss5r_ref, ss5i_ref,
        n1g_ref, win_hbm, cw_ref, cb_ref, wa_ref, wx_ref, ba_ref, bx_ref, lam_ref,
        slr_ref, sli_ref, sldt_ref, bre_ref, bim_ref, cre_ref, cim_ref,
        d_ref, wglu_hbm, glru_ref, gs5_ref, wout_hbm, wgate_hbm, wup_hbm, wdown_hbm,
        o_ref, mod2p_o, mod2s_o, convp_o, lrup_o, s5rp_o, s5ip_o,
        xs_o, convs_o, lrus_o, s5rs_o, s5is_o, wgate_o, wup_o, wdown_o,
        x_buf, x_sems, stage, stage_sems, win_b, wglu_b, wout_b, wg_b, wb_b, wc_b, abr_s, abi_s,
        mod1p, mod1s, xc_buf, h_lru, h_s5r, h_s5i, p_buf, hr_buf, hi_buf, a_buf, b_buf, y_buf, m_buf,
        bg_buf, bg_sems):
    step = pl.program_id(0)
    slot = step % 2
    _, tl, nb, _ = x_buf.shape
    rows = tl * nb
    hist = (CONV_W - 1) * nb
    half_mod = (N_MOD // 2) * D_MODEL

    bg_pieces = []
    for src, dst in ((wgate_hbm, wgate_o), (wup_hbm, wup_o)):
        for rb in range(D_MODEL // STAGE_ROWS):
            for ch in range(D_FF // BG_COLS):
                win = (pl.ds(rb * STAGE_ROWS, STAGE_ROWS), pl.ds(ch * BG_COLS, BG_COLS))
                bg_pieces.append((src.at[win], dst.at[win]))
    for rb in range(D_FF // STAGE_ROWS):
        win = (pl.ds(rb * STAGE_ROWS, STAGE_ROWS), pl.ds(0, D_MODEL))
        bg_pieces.append((wdown_hbm.at[win], wdown_o.at[win]))
    assert len(bg_pieces) < n_chunks

    def bg_in(k):
        r, c = bg_pieces[k][0].shape
        return pltpu.make_async_copy(bg_pieces[k][0], stage.at[k % STAGE_DEPTH, pl.ds(0, r), pl.ds(0, c)],
                                     stage_sems.at[k % STAGE_DEPTH])

    def bg_out(k):
        r, c = bg_pieces[k][0].shape
        return pltpu.make_async_copy(bg_buf.at[k % 2, pl.ds(0, r), pl.ds(0, c)], bg_pieces[k][1],
                                     bg_sems.at[k % 2])

    @pl.when(step == 0)
    def _prologue():
        for c in _batch_major_copies(x_buf, 0, x_hbm, x_sems, 0, to_hbm=False):
            c.start()
        xc_buf[0:hist, :] = jnp.zeros((hist, D_LRU), F32)
        h_lru[...] = jnp.zeros_like(h_lru)
        h_s5r[...] = jnp.zeros_like(h_s5r)
        h_s5i[...] = jnp.zeros_like(h_s5i)

        cp = cp_ref[...]
        cs = cs_ref[...]
        silu_p = (cp * _sigmoid(cp)).astype(BF16)
        silu_s = (cs * _sigmoid(cs)).astype(BF16)
        pieces = []

        def ada_piece(cg, kb):
            cols = slice(cg * STAGE_COLS, (cg + 1) * STAGE_COLS)
            first = cols.start < half_mod
            dst_p, dst_s = (mod1p, mod1s) if first else (mod2p_o, mod2s_o)
            dcols = cols if first else slice(cols.start - half_mod, cols.stop - half_mod)
            krows = slice(kb * STAGE_ROWS, (kb + 1) * STAGE_ROWS)

            def consume(view):
                w = view[...].astype(BF16)
                part_p = _dot(silu_p[:, krows], w)
                part_s = _dot(silu_s[:, krows], w)
                if kb == 0:
                    dst_p[:, dcols] = part_p + adab_ref[:, cols]
                    dst_s[:, dcols] = part_s + adab_ref[:, cols]
                else:
                    dst_p[:, dcols] += part_p
                    dst_s[:, dcols] += part_s
            return adaw_hbm.at[pl.ds(kb * STAGE_ROWS, STAGE_ROWS), pl.ds(cg * STAGE_COLS, STAGE_COLS)], consume

        def cast_piece(src_hbm, dst, rb):
            ncols = dst.shape[1]

            def consume(view):
                dst[rb * STAGE_ROWS:(rb + 1) * STAGE_ROWS, :] = view[...].astype(BF16)
            return src_hbm.at[pl.ds(rb * STAGE_ROWS, STAGE_ROWS), pl.ds(0, ncols)], consume

        for rb in range(win_b.shape[0] // STAGE_ROWS):
            pieces.append(cast_piece(win_hbm, win_b, rb))
        for cg in range(N_MOD * D_MODEL // STAGE_COLS):
            for kb in range(D_MODEL // STAGE_ROWS):
                pieces.append(ada_piece(cg, kb))
        for src_hbm, dst in ((wglu_hbm, wglu_b), (wout_hbm, wout_b)):
            for rb in range(dst.shape[0] // STAGE_ROWS):
                pieces.append(cast_piece(src_hbm, dst, rb))
        _stream_pieces(pieces, stage, stage_sems)
        bg_in(0).start(priority=BG_DMA_PRIORITY)

        for j in range(GATE_BLOCKS):
            rs = slice(j * GATE_BLOCK, (j + 1) * GATE_BLOCK)
            reps = GATE_BLOCK // LRU_HEAD_DIM
            wg_b[j, :, :GATE_BLOCK] = _lane_tiled_block_diag(wa_ref[rs, :], reps, LRU_HEAD_DIM)
            wg_b[j, :, GATE_BLOCK:] = _lane_tiled_block_diag(wx_ref[rs, :], reps, LRU_HEAD_DIM)

        lr = slr_ref[...]
        li = sli_ref[...]
        diag = (lax.broadcasted_iota(I32, (S5_GROUPS, S5_GROUPS), 0)
                == lax.broadcasted_iota(I32, (S5_GROUPS, S5_GROUPS), 1))
        log_dt = jnp.sum(jnp.where(diag, sldt_ref[...], 0.0), axis=1, keepdims=True)
        dt = jnp.exp(log_dt)
        mag = jnp.exp(lr * dt)
        ab_r = mag * jnp.cos(li * dt)
        ab_i = mag * jnp.sin(li * dt)
        den = lr * lr + li * li
        fr = ((ab_r - 1.0) * lr + ab_i * li) / den
        fi = (ab_i * lr - (ab_r - 1.0) * li) / den
        abr_s[...] = _flatten_rows(ab_r)
        abi_s[...] = _flatten_rows(ab_i)
        br = bre_ref[...]
        bi = bim_ref[...]
        fr_c = _repeat_rows(fr, S5_GROUP)
        fi_c = _repeat_rows(fi, S5_GROUP)
        bb_r = fr_c * br - fi_c * bi
        bb_i = fr_c * bi + fi_c * br
        for j in range(S5_BLOCKS):
            cs_rows = slice(j * S5_BLOCK_CH, (j + 1) * S5_BLOCK_CH)
            bt_r = _lane_tiled_block_diag(bb_r[cs_rows, :], S5_BLOCK_GROUPS, S5_GROUP)
            bt_i = _lane_tiled_block_diag(bb_i[cs_rows, :], S5_BLOCK_GROUPS, S5_GROUP)
            wb_b[j, :, :S5_BLOCK_LANES] = bt_r
            wb_b[j, :, S5_BLOCK_LANES:] = bt_i
            ct_r = _lane_tiled_block_diag(cre_ref[cs_rows, :], S5_BLOCK_GROUPS, S5_GROUP)
            ct_i = _lane_tiled_block_diag(-cim_ref[cs_rows, :], S5_BLOCK_GROUPS, S5_GROUP)
            wc_b[j, :S5_BLOCK_LANES, :] = ct_r.astype(F32).T.astype(BF16)
            wc_b[j, S5_BLOCK_LANES:, :] = ct_i.astype(F32).T.astype(BF16)

    @pl.when(step < n_chunks)
    def _prompt_step():
        @pl.when(step + 1 < n_chunks)
        def _():
            for c in _batch_major_copies(x_buf, 1 - slot, x_hbm, x_sems, step + 1, to_hbm=False):
                c.start()

        for c in _batch_major_copies(x_buf, slot, x_hbm, x_sems, step, to_hbm=False):
            c.wait()

        sh1 = mod1p[:, 0:D_MODEL]
        sc1 = mod1p[:, D_MODEL:2 * D_MODEL]
        g1 = mod1p[:, 2 * D_MODEL:3 * D_MODEL]

        scale1 = (n1g_ref[...] * (1.0 + sc1))[None]
        tsub = tl // FRONT_SLICES

        def project(s):
            r0, r1 = s * tsub * nb, (s + 1) * tsub * nb
            xs3 = x_buf[slot, pl.ds(s * tsub, tsub)]
            hn = _rms(xs3, scale1) + sh1[None]
            p_buf[r0:r1, :] = _dot(hn.reshape(r1 - r0, D_MODEL).astype(BF16), win_b[...])
            xc_buf[hist + r0:hist + r1, :] = p_buf[r0:r1, 0:D_LRU]

        def lru_gates(s):
            r0, r1 = s * tsub * nb, (s + 1) * tsub * nb
            for j in range(GATE_BLOCKS):
                sl = slice(j * GATE_BLOCK, (j + 1) * GATE_BLOCK)
                conv_j = cb_ref[:, sl]
                for k in range(CONV_W):
                    conv_j = conv_j + cw_ref[k:k + 1, sl] * xc_buf[k * nb + r0:k * nb + r1, sl]
                a, b = _lru_gate_block(conv_j, wg_b[j], ba_ref[:, sl], bx_ref[:, sl], lam_ref[:, sl])
                a_buf[r0:r1, sl] = a
                b_buf[r0:r1, sl] = b

        def lru_scan():
            h = h_lru[...]
            for t in range(tl):
                h = a_buf[t * nb:(t + 1) * nb, :] * h + b_buf[t * nb:(t + 1) * nb, :]
                b_buf[t * nb:(t + 1) * nb, :] = h
            h_lru[...] = h
            lru_out = b_buf[...] * jax.nn.gelu(p_buf[:, D_LRU:2 * D_LRU])
            m_buf[:, 0:D_LRU] = _rms(lru_out, glru_ref[...]).astype(BF16)

        def s5_input(j):
            ub = p_buf[:, 2 * D_LRU + j * S5_BLOCK_CH:2 * D_LRU + (j + 1) * S5_BLOCK_CH].astype(BF16)
            bu = _dot(ub, wb_b[j])
            hr_buf[j % 2] = bu[:, :S5_BLOCK_LANES]
            hi_buf[j % 2] = bu[:, S5_BLOCK_LANES:]

        def s5_scan(j):
            sl = slice(j * S5_BLOCK_LANES, (j + 1) * S5_BLOCK_LANES)
            hr_j = hr_buf.at[j % 2]
            hi_j = hi_buf.at[j % 2]
            ar = jnp.broadcast_to(abr_s[:, sl], (nb, S5_BLOCK_LANES))
            ai = jnp.broadcast_to(abi_s[:, sl], (nb, S5_BLOCK_LANES))
            hr = h_s5r[:, sl]
            hi = h_s5i[:, sl]
            for t in range(tl):
                rs = slice(t * nb, (t + 1) * nb)
                hr, hi = ar * hr - ai * hi + hr_j[rs, :], ar * hi + ai * hr + hi_j[rs, :]
                hr_j[rs, :] = hr
                hi_j[rs, :] = hi
            h_s5r[:, sl] = hr
            h_s5i[:, sl] = hi

        def s5_readout(j):
            y_buf[:, j * S5_BLOCK_CH:(j + 1) * S5_BLOCK_CH] = _s5_y_block(
                hr_buf[j % 2], hi_buf[j % 2], wc_b, j)

        for s in range(FRONT_SLICES):
            project(s)
        lru_gates(0)
        s5_input(0)
        s5_input(1)
        for s in range(1, FRONT_SLICES):
            lru_gates(s)
        for j in range(S5_BLOCKS):
            s5_scan(j)
            s5_readout(j)
            if j + 2 < S5_BLOCKS:
                s5_input(j + 2)
            if j == 1:
                lru_scan()

        s5_out = _s5_glu(y_buf[...], p_buf[:, 2 * D_LRU:], d_ref[...], wglu_b)
        m_buf[:, D_LRU:] = _rms(s5_out, gs5_ref[...]).astype(BF16)

        mix = _dot(m_buf[...], wout_b[...]).reshape(tl, nb, D_MODEL)
        o_ref[...] = x_buf[slot] + g1[None] * mix

        xc_buf[0:hist, :] = xc_buf[rows:rows + hist, :]

    for k in range(len(bg_pieces)):
        @pl.when(step == k + 1)
        def _bg_task(k=k):
            r, c = bg_pieces[k][0].shape
            bg_in(k).wait()
            if k >= 2:
                bg_out(k - 2).wait()
            bg_buf[k % 2, 0:r, 0:c] = stage[k % STAGE_DEPTH, 0:r, 0:c].astype(BF16)
            bg_out(k).start(priority=BG_DMA_PRIORITY)
            if k + 1 < len(bg_pieces):
                bg_in(k + 1).start(priority=BG_DMA_PRIORITY)

    @pl.when(step == n_chunks)
    def _sample_step():
        for k in range(max(len(bg_pieces) - 2, 0), len(bg_pieces)):
            bg_out(k).wait()
        for k in range(CONV_W - 1):
            convp_o[k] = xc_buf[k * nb:(k + 1) * nb, :]
        lrup_o[...] = h_lru[...]
        s5rp_o[...] = h_s5r[...].reshape(s5rp_o.shape)
        s5ip_o[...] = h_s5i[...].reshape(s5ip_o.shape)

        ns = xs_ref.shape[0]
        sh1 = mod1s[:, 0:D_MODEL]
        sc1 = mod1s[:, D_MODEL:2 * D_MODEL]
        g1 = mod1s[:, 2 * D_MODEL:3 * D_MODEL]

        x = xs_ref[...]
        hn = _rms(x, n1g_ref[...] * (1.0 + sc1)) + sh1
        p = _dot(hn.astype(BF16), win_b[...])
        lru_x = p[:, 0:D_LRU]

        for k in range(CONV_W - 2):
            convs_o[k] = sconv_ref[k + 1]
        convs_o[CONV_W - 2] = lru_x
        for j in range(GATE_BLOCKS):
            sl = slice(j * GATE_BLOCK, (j + 1) * GATE_BLOCK)
            conv_j = cb_ref[:, sl] + cw_ref[CONV_W - 1:CONV_W, sl] * lru_x[:, sl]
            for k in range(CONV_W - 1):
                conv_j = conv_j + cw_ref[k:k + 1, sl] * sconv_ref[k, :, sl]
            a, b = _lru_gate_block(conv_j, wg_b[j], ba_ref[:, sl], bx_ref[:, sl], lam_ref[:, sl])
            lrus_o[:, sl] = a * slru_ref[:, sl] + b
        lru_out = lrus_o[...] * jax.nn.gelu(p[:, D_LRU:2 * D_LRU])
        ms_buf = m_buf.at[pl.ds(0, ns), :]
        ms_buf[:, 0:D_LRU] = _rms(lru_out, glru_ref[...]).astype(BF16)

        u = p[:, 2 * D_LRU:]
        ub = u.astype(BF16)
        ys_buf = y_buf.at[pl.ds(0, ns), :]
        h0r = ss5r_ref[...].reshape(S5_LANES, ns).T
        h0i = ss5i_ref[...].reshape(S5_LANES, ns).T
        for j in range(S5_BLOCKS):
            sl = slice(j * S5_BLOCK_LANES, (j + 1) * S5_BLOCK_LANES)
            bu_r, bu_i = _s5_bu_block(ub, wb_b, j)
            ar = abr_s[:, sl]
            ai = abi_s[:, sl]
            hr = ar * h0r[:, sl] - ai * h0i[:, sl] + bu_r
            hi = ar * h0i[:, sl] + ai * h0r[:, sl] + bu_i
            gs = slice(j * S5_BLOCK_GROUPS, (j + 1) * S5_BLOCK_GROUPS)
            s5rs_o[gs] = hr.T.reshape(S5_BLOCK_GROUPS, S5_STATE, ns)
            s5is_o[gs] = hi.T.reshape(S5_BLOCK_GROUPS, S5_STATE, ns)
            ys_buf[:, j * S5_BLOCK_CH:(j + 1) * S5_BLOCK_CH] = _s5_y_block(hr, hi, wc_b, j)
        s5_out = _s5_glu(ys_buf[...], u, d_ref[...], wglu_b)
        ms_buf[:, D_LRU:] = _rms(s5_out, gs5_ref[...]).astype(BF16)
        xs_o[...] = x + g1 * _dot(ms_buf[...], wout_b[...])


def _const_spec(shape):
    nd = len(shape)
    return pl.BlockSpec(shape, lambda i, _nd=nd: (0,) * _nd, pipeline_mode=pl.Buffered(1))


def _out_const_spec(shape):
    nd = len(shape)
    return pl.BlockSpec(shape, lambda i, _nd=nd: (0,) * _nd)


_HBM = pl.BlockSpec(memory_space=pl.ANY)


def _mixer(x_prompt, c_prompt, c_sample, ada_w, ada_b, x_s, sconv, slru, ss5r, ss5i, w):
    nb, t_len, _ = x_prompt.shape
    ns = x_s.shape[0]
    tl = TIME_CHUNK
    n_chunks = t_len // tl
    rows = tl * nb
    hist = (CONV_W - 1) * nb
    half_mod = (N_MOD // 2) * D_MODEL

    inputs = [
        (x_prompt, _HBM), (c_prompt, None), (c_sample, None), (ada_w, _HBM), (ada_b, None),
        (x_s, None), (sconv, None), (slru, None), (ss5r, None), (ss5i, None),
        (w["n1g"], None), (w["win"], _HBM), (w["cw"], None), (w["cb"], None), (w["wa"], None), (w["wx"], None),
        (w["ba"], None), (w["bx"], None), (w["lam"], None),
        (w["slr"], None), (w["sli"], None), (w["sldt"], None), (w["bre"], None), (w["bim"], None),
        (w["cre"], None), (w["cim"], None),
        (w["d"], None), (w["wglu"], _HBM), (w["glru"], None), (w["gs5"], None), (w["wout"], _HBM),
        (w["wgate"], _HBM), (w["wup"], _HBM), (w["wdown"], _HBM),
    ]
    in_specs = [spec if spec is not None else _const_spec(a.shape) for a, spec in inputs]

    out_shapes = [
        ((t_len, nb, D_MODEL), pl.BlockSpec((tl, nb, D_MODEL), lambda i: (jnp.minimum(i, n_chunks - 1), 0, 0))),
        ((nb, half_mod), None), ((ns, half_mod), None),
        ((CONV_W - 1, nb, D_LRU), None), ((nb, D_LRU), None),
        ((nb, S5_GROUPS, S5_STATE), None), ((nb, S5_GROUPS, S5_STATE), None),
        ((ns, D_MODEL), None), ((CONV_W - 1, ns, D_LRU), None), ((ns, D_LRU), None),
        ((S5_GROUPS, S5_STATE, ns), None), ((S5_GROUPS, S5_STATE, ns), None),
    ]
    out_specs = [spec if spec is not None else _out_const_spec(s) for s, spec in out_shapes]
    out_shape = [jax.ShapeDtypeStruct(s, F32) for s, _ in out_shapes]
    for wkey in ("wgate", "wup", "wdown"):
        out_specs.append(_HBM)
        out_shape.append(jax.ShapeDtypeStruct(w[wkey].shape, BF16))

    scratch = [
        pltpu.VMEM((2, tl, nb, D_MODEL), F32),
        pltpu.SemaphoreType.DMA((2, nb)),
        pltpu.VMEM((STAGE_DEPTH, STAGE_ROWS, STAGE_COLS), F32),
        pltpu.SemaphoreType.DMA((STAGE_DEPTH,)),
        pltpu.VMEM((D_MODEL, D_IN), BF16),
        pltpu.VMEM((D_S5, 2 * D_S5), BF16),
        pltpu.VMEM((D_LRU + D_S5, D_MODEL), BF16),
        pltpu.VMEM((GATE_BLOCKS, GATE_BLOCK, 2 * GATE_BLOCK), BF16),
        pltpu.VMEM((S5_BLOCKS, S5_BLOCK_CH, 2 * S5_BLOCK_LANES), BF16),
        pltpu.VMEM((S5_BLOCKS, 2 * S5_BLOCK_LANES, S5_BLOCK_CH), BF16),
        pltpu.VMEM((1, S5_LANES), F32),
        pltpu.VMEM((1, S5_LANES), F32),
        pltpu.VMEM((nb, half_mod), F32),
        pltpu.VMEM((ns, half_mod), F32),
        pltpu.VMEM((hist + rows, D_LRU), F32),
        pltpu.VMEM((nb, D_LRU), F32),
        pltpu.VMEM((nb, S5_LANES), F32),
        pltpu.VMEM((nb, S5_LANES), F32),
        pltpu.VMEM((rows, D_IN), F32),
        pltpu.VMEM((2, rows, S5_BLOCK_LANES), F32),
        pltpu.VMEM((2, rows, S5_BLOCK_LANES), F32),
        pltpu.VMEM((rows, D_LRU), F32),
        pltpu.VMEM((rows, D_LRU), F32),
        pltpu.VMEM((rows, D_S5), F32),
        pltpu.VMEM((rows, D_LRU + D_S5), BF16),
        pltpu.VMEM((2, STAGE_ROWS, BG_COLS), BF16),
        pltpu.SemaphoreType.DMA((2,)),
    ]
    return pl.pallas_call(
        functools.partial(_mixer_kernel, n_chunks),
        grid=(n_chunks + 1,),
        in_specs=in_specs,
        out_specs=out_specs,
        out_shape=out_shape,
        scratch_shapes=scratch,
        compiler_params=pltpu.CompilerParams(
            dimension_semantics=("arbitrary",), vmem_limit_bytes=VMEM_LIMIT_BYTES),
        name="mixer",
    )(*[a for a, _ in inputs])


def _ffn_rows(x3, mod_ref, n2g_ref, wgate_ref, wup_ref, wdown_ref, gf_ref, act_ref):
    groups, mrows, _ = x3.shape
    rows = groups * mrows
    sh2 = mod_ref[:, 0:D_MODEL]
    sc2 = mod_ref[:, D_MODEL:2 * D_MODEL]
    g2 = mod_ref[:, 2 * D_MODEL:3 * D_MODEL]

    hn = _rms(x3, (n2g_ref[...] * (1.0 + sc2))[None]) + sh2[None]
    hb = hn.reshape(rows, D_MODEL).astype(BF16)
    for c in range(FFN_CHUNKS):
        cols = slice(c * FFN_CHUNK, (c + 1) * FFN_CHUNK)
        g = _dot(hb, wgate_ref[:, cols])
        up = _dot(hb, wup_ref[:, cols])
        act_ref[:, cols] = (g * jax.nn.sigmoid(g) * up).astype(BF16)
    ffn = _dot(act_ref[...], wdown_ref[...]).reshape(groups, mrows, D_MODEL)
    return _rms(x3 + g2[None] * ffn, gf_ref[...])


def _ffn_kernel(
        n_chunks,
        x_ref, mod2p_ref, mod2s_ref, xs_ref, n2g_ref, wgate_ref, wup_ref, wdown_ref, gf_ref,
        y_hbm, ys_o,
        act_buf, y_buf, y_sems):
    step = pl.program_id(0)
    slot = step % 2

    @pl.when(step < n_chunks)
    def _prompt_step():
        @pl.when(step >= 2)
        def _():
            for c in _batch_major_copies(y_buf, slot, y_hbm, y_sems, step - 2, to_hbm=True):
                c.wait()

        tl, nb, _ = x_ref.shape
        sub = FFN_SUB_ROWS // nb
        for h in range(tl // sub):
            y_buf[slot, pl.ds(h * sub, sub)] = _ffn_rows(
                x_ref[h * sub:(h + 1) * sub], mod2p_ref, n2g_ref, wgate_ref, wup_ref, wdown_ref, gf_ref,
                act_buf.at[pl.ds(h * FFN_SUB_ROWS, FFN_SUB_ROWS), :])
        for c in _batch_major_copies(y_buf, slot, y_hbm, y_sems, step, to_hbm=True):
            c.start()

    @pl.when(step == n_chunks)
    def _sample_step():
        ns = xs_ref.shape[0]
        ys_o[...] = _ffn_rows(xs_ref[...][None], mod2s_ref, n2g_ref, wgate_ref, wup_ref, wdown_ref, gf_ref,
                              act_buf.at[pl.ds(0, ns), :])[0]
        for s in range(max(n_chunks - 2, 0), n_chunks):
            for c in _batch_major_copies(y_buf, s % 2, y_hbm, y_sems, s, to_hbm=True):
                c.wait()


def _ffn(x32_t, mod2p, mod2s, x32_s, wgate_b, wup_b, wdown_b, w, out_shape_p, out_shape_s):
    t_len, nb, _ = x32_t.shape
    tl = FFN_TIME_CHUNK
    n_chunks = t_len // tl
    rows = tl * nb
    inputs = [
        (x32_t, pl.BlockSpec((tl, nb, D_MODEL), lambda i: (jnp.minimum(i, n_chunks - 1), 0, 0))),
        (mod2p, None), (mod2s, None), (x32_s, None), (w["n2g"], None),
        (wgate_b, None), (wup_b, None), (wdown_b, None), (w["gf"], None),
    ]
    in_specs = [spec if spec is not None else _const_spec(a.shape) for a, spec in inputs]
    scratch = [
        pltpu.VMEM((rows, D_FF), BF16),
        pltpu.VMEM((2, tl, nb, D_MODEL), F32),
        pltpu.SemaphoreType.DMA((2, nb)),
    ]
    return pl.pallas_call(
        functools.partial(_ffn_kernel, n_chunks),
        grid=(n_chunks + 1,),
        in_specs=in_specs,
        out_specs=[_HBM, _out_const_spec(out_shape_s)],
        out_shape=[jax.ShapeDtypeStruct(out_shape_p, F32), jax.ShapeDtypeStruct(out_shape_s, F32)],
        scratch_shapes=scratch,
        compiler_params=pltpu.CompilerParams(
            dimension_semantics=("arbitrary",), vmem_limit_bytes=VMEM_LIMIT_BYTES),
        name="ffn",
    )(*[a for a, _ in inputs])


def kernel(x_prompt, x_sample, state_conv, state_lru, state_s5_re, state_s5_im, c_prompt, c_sample,
           ada_w, ada_b, norm1_g, w_in, conv_w, conv_b, lru_wa, lru_ba, lru_wx, lru_bx, lru_lambda,
           s5_lambda_re, s5_lambda_im, s5_log_dt, s5_b_re, s5_b_im, s5_c_re, s5_c_im, s5_d, s5_w_glu,
           g_lru_out, g_s5_out, w_out, norm2_g, ffn_w_gate, ffn_w_up, ffn_w_down, final_norm_g):
    depth = ada_w.shape[0]
    assert depth == 1, "single-layer decoder step"
    bp, t_len, _ = x_prompt.shape
    bs, dec_t, _ = x_sample.shape
    assert bp == SUBLANES and dec_t == 1 and t_len % TIME_CHUNK == 0 and t_len % FFN_TIME_CHUNK == 0
    assert (FFN_TIME_CHUNK * SUBLANES) % FFN_SUB_ROWS == 0 and bs <= FFN_SUB_ROWS

    w = dict(
        n1g=norm1_g, win=w_in[0], cw=conv_w[0], cb=conv_b,
        wa=lru_wa.reshape(D_LRU, LRU_HEAD_DIM), wx=lru_wx.reshape(D_LRU, LRU_HEAD_DIM),
        ba=lru_ba, bx=lru_bx, lam=lru_lambda,
        slr=s5_lambda_re[0], sli=s5_lambda_im[0], sldt=s5_log_dt,
        bre=s5_b_re.transpose(0, 1, 3, 2).reshape(D_S5, S5_STATE),
        bim=s5_b_im.transpose(0, 1, 3, 2).reshape(D_S5, S5_STATE),
        cre=s5_c_re.reshape(D_S5, S5_STATE), cim=s5_c_im.reshape(D_S5, S5_STATE),
        d=s5_d, wglu=s5_w_glu[0], glru=g_lru_out, gs5=g_s5_out, wout=w_out[0],
        n2g=norm2_g, wgate=ffn_w_gate[0], wup=ffn_w_up[0], wdown=ffn_w_down[0],
        gf=final_norm_g.reshape(1, D_MODEL),
    )

    (x32_t, mod2p, mod2s, conv_p, lru_p, s5r_p, s5i_p,
     x32_s, conv_s, lru_s, s5r_s, s5i_s, wgate_b, wup_b, wdown_b) = _mixer(
        x_prompt, c_prompt, c_sample, ada_w[0], ada_b, x_sample.reshape(bs, D_MODEL),
        state_conv[0].transpose(1, 0, 2), state_lru[0],
        state_s5_re[0].transpose(1, 2, 0), state_s5_im[0].transpose(1, 2, 0), w)
    y_prompt, y_s = _ffn(x32_t, mod2p, mod2s, x32_s, wgate_b, wup_b, wdown_b, w, x_prompt.shape, x32_s.shape)

    return (y_prompt, y_s.reshape(x_sample.shape), conv_p.transpose(1, 0, 2)[None], lru_p[None],
            s5r_p[None], s5i_p[None], conv_s.transpose(1, 0, 2)[None], lru_s[None],
            s5r_s.transpose(2, 0, 1)[None], s5i_s.transpose(2, 0, 1)[None])
```

```python
import functools

import jax
import jax.numpy as jnp
from jax import lax
from jax.experimental import pallas as pl
from jax.experimental.pallas import tpu as pltpu

F32 = jnp.float32
BF16 = jnp.bfloat16
I32 = jnp.int32

D_MODEL = 1024
D_LRU = 512
D_S5 = 512
D_IN = 2 * D_LRU + D_S5
LRU_HEAD_DIM = 64
CONV_W = 4
LRU_C = 8.0
S5_GROUP = 16
S5_GROUPS = D_S5 // S5_GROUP
S5_STATE = 64
S5_LANES = S5_GROUPS * S5_STATE
D_FF = 2816
N_MOD = 6
EPS = 1e-6

SUBLANES = 8
MXU_DIM = 256
VMEM_LIMIT_BYTES = 60 * 1024 * 1024

TIME_CHUNK = 64
FRONT_SLICES = 2
GATE_BLOCK = MXU_DIM
GATE_BLOCKS = D_LRU // GATE_BLOCK
S5_BLOCKS = 4
S5_BLOCK_GROUPS = S5_GROUPS // S5_BLOCKS
S5_BLOCK_CH = D_S5 // S5_BLOCKS
S5_BLOCK_LANES = S5_LANES // S5_BLOCKS
FFN_CHUNK = MXU_DIM
FFN_CHUNKS = D_FF // FFN_CHUNK
STAGE_ROWS = 256
STAGE_COLS = D_IN
STAGE_DEPTH = 3
FFN_STAGE_ROWS = 128
FFN_TIME_CHUNK = 128
FFN_SUB_ROWS = 512


def _dot(a, b):
    return jnp.dot(a, b, preferred_element_type=F32)


def _rms(x, g):
    return x * lax.rsqrt(jnp.mean(x * x, axis=-1, keepdims=True) + EPS) * g


def _log_sigmoid(x):
    return jnp.minimum(x, 0.0) - jnp.log1p(jnp.exp(-jnp.abs(x)))


def _sigmoid(x):
    return 0.5 * jnp.tanh(0.5 * x) + 0.5


def _sqrt_nonneg(x):
    return jnp.where(x > 0.0, x * lax.rsqrt(x), 0.0)


def _log2(n):
    assert n & (n - 1) == 0
    return n.bit_length() - 1


def _same_group(shape, row_group, col_group):
    r = lax.broadcasted_iota(I32, shape, 0) >> _log2(row_group)
    c = lax.broadcasted_iota(I32, shape, 1) >> _log2(col_group)
    return r == c


def _lane_tiled_block_diag(blk, reps, row_group):
    r, w = blk.shape
    n = w * reps
    sel = (lax.broadcasted_iota(I32, (w, n), 1) & (w - 1)) == lax.broadcasted_iota(I32, (w, n), 0)
    tiled = _dot(blk.astype(BF16), jnp.where(sel, 1.0, 0.0).astype(BF16))
    return jnp.where(_same_group((r, n), row_group, w), tiled, 0.0).astype(BF16)


def _split3(v):
    hi = v.astype(BF16)
    r1 = v - hi.astype(F32)
    mid = r1.astype(BF16)
    lo = (r1 - mid.astype(F32)).astype(BF16)
    return hi, mid, lo


def _flatten_rows(v):
    g, n = v.shape
    sel = (lax.broadcasted_iota(I32, (n, g * n), 1) & (n - 1)) == lax.broadcasted_iota(I32, (n, g * n), 0)
    sel = jnp.where(sel, 1.0, 0.0).astype(BF16)
    hi, mid, lo = (_dot(p, sel) for p in _split3(v))
    tiled = (hi + mid) + lo
    return jnp.sum(jnp.where(_same_group((g, g * n), 1, n), tiled, 0.0), axis=0, keepdims=True)


def _repeat_rows(v, reps):
    g, n = v.shape
    sel = (lax.broadcasted_iota(I32, (g * reps, g), 0) >> _log2(reps)) == lax.broadcasted_iota(I32, (g * reps, g), 1)
    sel = jnp.where(sel, 1.0, 0.0).astype(BF16)
    hi, mid, lo = (_dot(sel, p) for p in _split3(v))
    return (hi + mid) + lo


def _lru_gate_block(conv_j, wg_j, ba_j, bx_j, lam_j):
    gz = _dot(conv_j.astype(BF16), wg_j)
    r = _sigmoid(gz[:, :GATE_BLOCK] + ba_j)
    ig = _sigmoid(gz[:, GATE_BLOCK:] + bx_j)
    log_a = (LRU_C * _log_sigmoid(lam_j)) * r
    a = jnp.exp(log_a)
    mult = _sqrt_nonneg(-jnp.tanh(log_a) * (a * a + 1.0))
    return a, mult * (ig * conv_j)


def _s5_bu_block(ub, wb_ref, j):
    bu = _dot(ub[:, j * S5_BLOCK_CH:(j + 1) * S5_BLOCK_CH], wb_ref[j])
    return bu[:, :S5_BLOCK_LANES], bu[:, S5_BLOCK_LANES:]


def _s5_y_block(hr, hi, wc_ref, j):
    return (_dot(hr.astype(BF16), wc_ref[j, :S5_BLOCK_LANES, :])
            + _dot(hi.astype(BF16), wc_ref[j, S5_BLOCK_LANES:, :]))


def _s5_glu(y, u, d, wglu_ref):
    z = _dot(jax.nn.gelu(y + d * u).astype(BF16), wglu_ref[...])
    return z[:, :D_S5] * _sigmoid(z[:, D_S5:])


def _stream_pieces(pieces, stage, sems):
    depth = stage.shape[0]

    def dst(k):
        r, c = pieces[k][0].shape
        return stage.at[k % depth, pl.ds(0, r), pl.ds(0, c)]

    copies = [pltpu.make_async_copy(pieces[k][0], dst(k), sems.at[k % depth]) for k in range(len(pieces))]
    for k in range(min(depth - 1, len(pieces))):
        copies[k].start()
    for k, (_, consume) in enumerate(pieces):
        if k + depth - 1 < len(pieces):
            copies[k + depth - 1].start()
        copies[k].wait()
        consume(dst(k))


def _batch_major_copies(tm_bufs, slot, bm_hbm, sems, chunk, to_hbm):
    _, tl, nb, _ = tm_bufs.shape
    copies = []
    for b in range(nb):
        hbm = bm_hbm.at[b, pl.ds(chunk * tl, tl), :]
        vmem = tm_bufs.at[slot, :, b, :]
        src, dst = (vmem, hbm) if to_hbm else (hbm, vmem)
        copies.append(pltpu.make_async_copy(src, dst, sems.at[slot, b]))
    return copies


def _mixer_kernel(
        n_chunks,
        x_hbm, cp_ref, cs_ref, adaw_hbm, adab_ref,
        xs_ref, sconv_ref, slru_ref, ss5r_ref, ss5i_ref,
        n1g_ref, win_hbm, cw_ref, cb_ref, wa_ref, wx_ref, ba_ref, bx_ref, lam_ref,
        slr_ref, sli_ref, sldt_ref, bre_ref, bim_ref, cre_ref, cim_ref,
        d_ref, wglu_hbm, glru_ref, gs5_ref, wout_hbm,
        o_ref, mod2p_o, mod2s_o, convp_o, lrup_o, s5rp_o, s5ip_o,
        xs_o, convs_o, lrus_o, s5rs_o, s5is_o,
        x_buf, x_sems, stage, stage_sems, win_b, wglu_b, wout_b, wg_b, wb_b, wc_b, abr_s, abi_s,
        mod1p, mod1s, xc_buf, h_lru, h_s5r, h_s5i, p_buf, hr_buf, hi_buf, a_buf, b_buf, y_buf, m_buf):
    step = pl.program_id(0)
    slot = step % 2
    _, tl, nb, _ = x_buf.shape
    rows = tl * nb
    hist = (CONV_W - 1) * nb
    half_mod = (N_MOD // 2) * D_MODEL

    @pl.when(step == 0)
    def _prologue():
        for c in _batch_major_copies(x_buf, 0, x_hbm, x_sems, 0, to_hbm=False):
            c.start()
        xc_buf[0:hist, :] = jnp.zeros((hist, D_LRU), F32)
        h_lru[...] = jnp.zeros_like(h_lru)
        h_s5r[...] = jnp.zeros_like(h_s5r)
        h_s5i[...] = jnp.zeros_like(h_s5i)

        cp = cp_ref[...]
        cs = cs_ref[...]
        silu_p = (cp * _sigmoid(cp)).astype(BF16)
        silu_s = (cs * _sigmoid(cs)).astype(BF16)
        pieces = []

        def ada_piece(cg, kb):
            cols = slice(cg * STAGE_COLS, (cg + 1) * STAGE_COLS)
            first = cols.start < half_mod
            dst_p, dst_s = (mod1p, mod1s) if first else (mod2p_o, mod2s_o)
            dcols = cols if first else slice(cols.start - half_mod, cols.stop - half_mod)
            krows = slice(kb * STAGE_ROWS, (kb + 1) * STAGE_ROWS)

            def consume(view):
                w = view[...].astype(BF16)
                part_p = _dot(silu_p[:, krows], w)
                part_s = _dot(silu_s[:, krows], w)
                if kb == 0:
                    dst_p[:, dcols] = part_p + adab_ref[:, cols]
                    dst_s[:, dcols] = part_s + adab_ref[:, cols]
                else:
                    dst_p[:, dcols] += part_p
                    dst_s[:, dcols] += part_s
            return adaw_hbm.at[pl.ds(kb * STAGE_ROWS, STAGE_ROWS), pl.ds(cg * STAGE_COLS, STAGE_COLS)], consume

        def cast_piece(src_hbm, dst, rb):
            ncols = dst.shape[1]

            def consume(view):
                dst[rb * STAGE_ROWS:(rb + 1) * STAGE_ROWS, :] = view[...].astype(BF16)
            return src_hbm.at[pl.ds(rb * STAGE_ROWS, STAGE_ROWS), pl.ds(0, ncols)], consume

        for rb in range(win_b.shape[0] // STAGE_ROWS):
            pieces.append(cast_piece(win_hbm, win_b, rb))
        for cg in range(N_MOD * D_MODEL // STAGE_COLS):
            for kb in range(D_MODEL // STAGE_ROWS):
                pieces.append(ada_piece(cg, kb))
        for src_hbm, dst in ((wglu_hbm, wglu_b), (wout_hbm, wout_b)):
            for rb in range(dst.shape[0] // STAGE_ROWS):
                pieces.append(cast_piece(src_hbm, dst, rb))
        _stream_pieces(pieces, stage, stage_sems)

        for j in range(GATE_BLOCKS):
            rs = slice(j * GATE_BLOCK, (j + 1) * GATE_BLOCK)
            reps = GATE_BLOCK // LRU_HEAD_DIM
            wg_b[j, :, :GATE_BLOCK] = _lane_tiled_block_diag(wa_ref[rs, :], reps, LRU_HEAD_DIM)
            wg_b[j, :, GATE_BLOCK:] = _lane_tiled_block_diag(wx_ref[rs, :], reps, LRU_HEAD_DIM)

        lr = slr_ref[...]
        li = sli_ref[...]
        diag = (lax.broadcasted_iota(I32, (S5_GROUPS, S5_GROUPS), 0)
                == lax.broadcasted_iota(I32, (S5_GROUPS, S5_GROUPS), 1))
        log_dt = jnp.sum(jnp.where(diag, sldt_ref[...], 0.0), axis=1, keepdims=True)
        dt = jnp.exp(log_dt)
        mag = jnp.exp(lr * dt)
        ab_r = mag * jnp.cos(li * dt)
        ab_i = mag * jnp.sin(li * dt)
        den = lr * lr + li * li
        fr = ((ab_r - 1.0) * lr + ab_i * li) / den
        fi = (ab_i * lr - (ab_r - 1.0) * li) / den
        abr_s[...] = _flatten_rows(ab_r)
        abi_s[...] = _flatten_rows(ab_i)
        br = bre_ref[...]
        bi = bim_ref[...]
        fr_c = _repeat_rows(fr, S5_GROUP)
        fi_c = _repeat_rows(fi, S5_GROUP)
        bb_r = fr_c * br - fi_c * bi
        bb_i = fr_c * bi + fi_c * br
        for j in range(S5_BLOCKS):
            cs_rows = slice(j * S5_BLOCK_CH, (j + 1) * S5_BLOCK_CH)
            bt_r = _lane_tiled_block_diag(bb_r[cs_rows, :], S5_BLOCK_GROUPS, S5_GROUP)
            bt_i = _lane_tiled_block_diag(bb_i[cs_rows, :], S5_BLOCK_GROUPS, S5_GROUP)
            wb_b[j, :, :S5_BLOCK_LANES] = bt_r
            wb_b[j, :, S5_BLOCK_LANES:] = bt_i
            ct_r = _lane_tiled_block_diag(cre_ref[cs_rows, :], S5_BLOCK_GROUPS, S5_GROUP)
            ct_i = _lane_tiled_block_diag(-cim_ref[cs_rows, :], S5_BLOCK_GROUPS, S5_GROUP)
            wc_b[j, :S5_BLOCK_LANES, :] = ct_r.astype(F32).T.astype(BF16)
            wc_b[j, S5_BLOCK_LANES:, :] = ct_i.astype(F32).T.astype(BF16)

    @pl.when(step < n_chunks)
    def _prompt_step():
        @pl.when(step + 1 < n_chunks)
        def _():
            for c in _batch_major_copies(x_buf, 1 - slot, x_hbm, x_sems, step + 1, to_hbm=False):
                c.start()

        for c in _batch_major_copies(x_buf, slot, x_hbm, x_sems, step, to_hbm=False):
            c.wait()

        sh1 = mod1p[:, 0:D_MODEL]
        sc1 = mod1p[:, D_MODEL:2 * D_MODEL]
        g1 = mod1p[:, 2 * D_MODEL:3 * D_MODEL]

        scale1 = (n1g_ref[...] * (1.0 + sc1))[None]
        tsub = tl // FRONT_SLICES

        def project(s):
            r0, r1 = s * tsub * nb, (s + 1) * tsub * nb
            xs3 = x_buf[slot, pl.ds(s * tsub, tsub)]
            hn = _rms(xs3, scale1) + sh1[None]
            p_buf[r0:r1, :] = _dot(hn.reshape(r1 - r0, D_MODEL).astype(BF16), win_b[...])
            xc_buf[hist + r0:hist + r1, :] = p_buf[r0:r1, 0:D_LRU]

        def lru_gates(s):
            r0, r1 = s * tsub * nb, (s + 1) * tsub * nb
            for j in range(GATE_BLOCKS):
                sl = slice(j * GATE_BLOCK, (j + 1) * GATE_BLOCK)
                conv_j = cb_ref[:, sl]
                for k in range(CONV_W):
                    conv_j = conv_j + cw_ref[k:k + 1, sl] * xc_buf[k * nb + r0:k * nb + r1, sl]
                a, b = _lru_gate_block(conv_j, wg_b[j], ba_ref[:, sl], bx_ref[:, sl], lam_ref[:, sl])
                a_buf[r0:r1, sl] = a
                b_buf[r0:r1, sl] = b

        def lru_scan():
            h = h_lru[...]
            for t in range(tl):
                h = a_buf[t * nb:(t + 1) * nb, :] * h + b_buf[t * nb:(t + 1) * nb, :]
                b_buf[t * nb:(t + 1) * nb, :] = h
            h_lru[...] = h
            lru_out = b_buf[...] * jax.nn.gelu(p_buf[:, D_LRU:2 * D_LRU])
            m_buf[:, 0:D_LRU] = _rms(lru_out, glru_ref[...]).astype(BF16)

        def s5_input(j):
            ub = p_buf[:, 2 * D_LRU + j * S5_BLOCK_CH:2 * D_LRU + (j + 1) * S5_BLOCK_CH].astype(BF16)
            bu = _dot(ub, wb_b[j])
            hr_buf[j % 2] = bu[:, :S5_BLOCK_LANES]
            hi_buf[j % 2] = bu[:, S5_BLOCK_LANES:]

        def s5_scan(j):
            sl = slice(j * S5_BLOCK_LANES, (j + 1) * S5_BLOCK_LANES)
            hr_j = hr_buf.at[j % 2]
            hi_j = hi_buf.at[j % 2]
            ar = jnp.broadcast_to(abr_s[:, sl], (nb, S5_BLOCK_LANES))
            ai = jnp.broadcast_to(abi_s[:, sl], (nb, S5_BLOCK_LANES))
            hr = h_s5r[:, sl]
            hi = h_s5i[:, sl]
            for t in range(tl):
                rs = slice(t * nb, (t + 1) * nb)
                hr, hi = ar * hr - ai * hi + hr_j[rs, :], ar * hi + ai * hr + hi_j[rs, :]
                hr_j[rs, :] = hr
                hi_j[rs, :] = hi
            h_s5r[:, sl] = hr
            h_s5i[:, sl] = hi

        def s5_readout(j):
            y_buf[:, j * S5_BLOCK_CH:(j + 1) * S5_BLOCK_CH] = _s5_y_block(
                hr_buf[j % 2], hi_buf[j % 2], wc_b, j)

        for s in range(FRONT_SLICES):
            project(s)
        lru_gates(0)
        s5_input(0)
        s5_input(1)
        for s in range(1, FRONT_SLICES):
            lru_gates(s)
        for j in range(S5_BLOCKS):
            s5_scan(j)
            s5_readout(j)
            if j + 2 < S5_BLOCKS:
                s5_input(j + 2)
            if j == 1:
                lru_scan()

        s5_out = _s5_glu(y_buf[...], p_buf[:, 2 * D_LRU:], d_ref[...], wglu_b)
        m_buf[:, D_LRU:] = _rms(s5_out, gs5_ref[...]).astype(BF16)

        mix = _dot(m_buf[...], wout_b[...]).reshape(tl, nb, D_MODEL)
        o_ref[...] = x_buf[slot] + g1[None] * mix

        xc_buf[0:hist, :] = xc_buf[rows:rows + hist, :]

    @pl.when(step == n_chunks)
    def _sample_step():
        for k in range(CONV_W - 1):
            convp_o[k] = xc_buf[k * nb:(k + 1) * nb, :]
        lrup_o[...] = h_lru[...]
        s5rp_o[...] = h_s5r[...].reshape(s5rp_o.shape)
        s5ip_o[...] = h_s5i[...].reshape(s5ip_o.shape)

        ns = xs_ref.shape[0]
        sh1 = mod1s[:, 0:D_MODEL]
        sc1 = mod1s[:, D_MODEL:2 * D_MODEL]
        g1 = mod1s[:, 2 * D_MODEL:3 * D_MODEL]

        x = xs_ref[...]
        hn = _rms(x, n1g_ref[...] * (1.0 + sc1)) + sh1
        p = _dot(hn.astype(BF16), win_b[...])
        lru_x = p[:, 0:D_LRU]

        for k in range(CONV_W - 2):
            convs_o[k] = sconv_ref[k + 1]
        convs_o[CONV_W - 2] = lru_x
        for j in range(GATE_BLOCKS):
            sl = slice(j * GATE_BLOCK, (j + 1) * GATE_BLOCK)
            conv_j = cb_ref[:, sl] + cw_ref[CONV_W - 1:CONV_W, sl] * lru_x[:, sl]
            for k in range(CONV_W - 1):
                conv_j = conv_j + cw_ref[k:k + 1, sl] * sconv_ref[k, :, sl]
            a, b = _lru_gate_block(conv_j, wg_b[j], ba_ref[:, sl], bx_ref[:, sl], lam_ref[:, sl])
            lrus_o[:, sl] = a * slru_ref[:, sl] + b
        lru_out = lrus_o[...] * jax.nn.gelu(p[:, D_LRU:2 * D_LRU])
        ms_buf = m_buf.at[pl.ds(0, ns), :]
        ms_buf[:, 0:D_LRU] = _rms(lru_out, glru_ref[...]).astype(BF16)

        u = p[:, 2 * D_LRU:]
        ub = u.astype(BF16)
        ys_buf = y_buf.at[pl.ds(0, ns), :]
        h0r = ss5r_ref[...].reshape(S5_LANES, ns).T
        h0i = ss5i_ref[...].reshape(S5_LANES, ns).T
        for j in range(S5_BLOCKS):
            sl = slice(j * S5_BLOCK_LANES, (j + 1) * S5_BLOCK_LANES)
            bu_r, bu_i = _s5_bu_block(ub, wb_b, j)
            ar = abr_s[:, sl]
            ai = abi_s[:, sl]
            hr = ar * h0r[:, sl] - ai * h0i[:, sl] + bu_r
            hi = ar * h0i[:, sl] + ai * h0r[:, sl] + bu_i
            gs = slice(j * S5_BLOCK_GROUPS, (j + 1) * S5_BLOCK_GROUPS)
            s5rs_o[gs] = hr.T.reshape(S5_BLOCK_GROUPS, S5_STATE, ns)
            s5is_o[gs] = hi.T.reshape(S5_BLOCK_GROUPS, S5_STATE, ns)
            ys_buf[:, j * S5_BLOCK_CH:(j + 1) * S5_BLOCK_CH] = _s5_y_block(hr, hi, wc_b, j)
        s5_out = _s5_glu(ys_buf[...], u, d_ref[...], wglu_b)
        ms_buf[:, D_LRU:] = _rms(s5_out, gs5_ref[...]).astype(BF16)
        xs_o[...] = x + g1 * _dot(ms_buf[...], wout_b[...])


def _const_spec(shape):
    nd = len(shape)
    return pl.BlockSpec(shape, lambda i, _nd=nd: (0,) * _nd, pipeline_mode=pl.Buffered(1))


def _out_const_spec(shape):
    nd = len(shape)
    return pl.BlockSpec(shape, lambda i, _nd=nd: (0,) * _nd)


_HBM = pl.BlockSpec(memory_space=pl.ANY)


def _mixer(x_prompt, c_prompt, c_sample, ada_w, ada_b, x_s, sconv, slru, ss5r, ss5i, w):
    nb, t_len, _ = x_prompt.shape
    ns = x_s.shape[0]
    tl = TIME_CHUNK
    n_chunks = t_len // tl
    rows = tl * nb
    hist = (CONV_W - 1) * nb
    half_mod = (N_MOD // 2) * D_MODEL

    inputs = [
        (x_prompt, _HBM), (c_prompt, None), (c_sample, None), (ada_w, _HBM), (ada_b, None),
        (x_s, None), (sconv, None), (slru, None), (ss5r, None), (ss5i, None),
        (w["n1g"], None), (w["win"], _HBM), (w["cw"], None), (w["cb"], None), (w["wa"], None), (w["wx"], None),
        (w["ba"], None), (w["bx"], None), (w["lam"], None),
        (w["slr"], None), (w["sli"], None), (w["sldt"], None), (w["bre"], None), (w["bim"], None),
        (w["cre"], None), (w["cim"], None),
        (w["d"], None), (w["wglu"], _HBM), (w["glru"], None), (w["gs5"], None), (w["wout"], _HBM),
    ]
    in_specs = [spec if spec is not None else _const_spec(a.shape) for a, spec in inputs]

    out_shapes = [
        ((t_len, nb, D_MODEL), pl.BlockSpec((tl, nb, D_MODEL), lambda i: (jnp.minimum(i, n_chunks - 1), 0, 0))),
        ((nb, half_mod), None), ((ns, half_mod), None),
        ((CONV_W - 1, nb, D_LRU), None), ((nb, D_LRU), None),
        ((nb, S5_GROUPS, S5_STATE), None), ((nb, S5_GROUPS, S5_STATE), None),
        ((ns, D_MODEL), None), ((CONV_W - 1, ns, D_LRU), None), ((ns, D_LRU), None),
        ((S5_GROUPS, S5_STATE, ns), None), ((S5_GROUPS, S5_STATE, ns), None),
    ]
    out_specs = [spec if spec is not None else _out_const_spec(s) for s, spec in out_shapes]
    out_shape = [jax.ShapeDtypeStruct(s, F32) for s, _ in out_shapes]

    scratch = [
        pltpu.VMEM((2, tl, nb, D_MODEL), F32),
        pltpu.SemaphoreType.DMA((2, nb)),
        pltpu.VMEM((STAGE_DEPTH, STAGE_ROWS, STAGE_COLS), F32),
        pltpu.SemaphoreType.DMA((STAGE_DEPTH,)),
        pltpu.VMEM((D_MODEL, D_IN), BF16),
        pltpu.VMEM((D_S5, 2 * D_S5), BF16),
        pltpu.VMEM((D_LRU + D_S5, D_MODEL), BF16),
        pltpu.VMEM((GATE_BLOCKS, GATE_BLOCK, 2 * GATE_BLOCK), BF16),
        pltpu.VMEM((S5_BLOCKS, S5_BLOCK_CH, 2 * S5_BLOCK_LANES), BF16),
        pltpu.VMEM((S5_BLOCKS, 2 * S5_BLOCK_LANES, S5_BLOCK_CH), BF16),
        pltpu.VMEM((1, S5_LANES), F32),
        pltpu.VMEM((1, S5_LANES), F32),
        pltpu.VMEM((nb, half_mod), F32),
        pltpu.VMEM((ns, half_mod), F32),
        pltpu.VMEM((hist + rows, D_LRU), F32),
        pltpu.VMEM((nb, D_LRU), F32),
        pltpu.VMEM((nb, S5_LANES), F32),
        pltpu.VMEM((nb, S5_LANES), F32),
        pltpu.VMEM((rows, D_IN), F32),
        pltpu.VMEM((2, rows, S5_BLOCK_LANES), F32),
        pltpu.VMEM((2, rows, S5_BLOCK_LANES), F32),
        pltpu.VMEM((rows, D_LRU), F32),
        pltpu.VMEM((rows, D_LRU), F32),
        pltpu.VMEM((rows, D_S5), F32),
        pltpu.VMEM((rows, D_LRU + D_S5), BF16),
    ]
    return pl.pallas_call(
        functools.partial(_mixer_kernel, n_chunks),
        grid=(n_chunks + 1,),
        in_specs=in_specs,
        out_specs=out_specs,
        out_shape=out_shape,
        scratch_shapes=scratch,
        compiler_params=pltpu.CompilerParams(
            dimension_semantics=("arbitrary",), vmem_limit_bytes=VMEM_LIMIT_BYTES),
        name="mixer",
    )(*[a for a, _ in inputs])


def _ffn_rows(x3, mod_ref, n2g_ref, wgate_ref, wup_ref, wdown_ref, gf_ref, act_ref):
    groups, mrows, _ = x3.shape
    rows = groups * mrows
    sh2 = mod_ref[:, 0:D_MODEL]
    sc2 = mod_ref[:, D_MODEL:2 * D_MODEL]
    g2 = mod_ref[:, 2 * D_MODEL:3 * D_MODEL]

    hn = _rms(x3, (n2g_ref[...] * (1.0 + sc2))[None]) + sh2[None]
    hb = hn.reshape(rows, D_MODEL).astype(BF16)
    for c in range(FFN_CHUNKS):
        cols = slice(c * FFN_CHUNK, (c + 1) * FFN_CHUNK)
        g = _dot(hb, wgate_ref[:, cols])
        up = _dot(hb, wup_ref[:, cols])
        act_ref[:, cols] = (g * jax.nn.sigmoid(g) * up).astype(BF16)
    ffn = _dot(act_ref[...], wdown_ref[...]).reshape(groups, mrows, D_MODEL)
    return _rms(x3 + g2[None] * ffn, gf_ref[...])


def _ffn_kernel(
        n_chunks,
        x_ref, mod2p_ref, mod2s_ref, xs_ref, n2g_ref, wgate_hbm, wup_hbm, wdown_hbm, gf_ref,
        y_hbm, ys_o,
        stage_a, stage_b, stage_a_sems, stage_b_sems, wgate_ref, wup_ref, wdown_ref, act_buf, y_buf, y_sems):
    step = pl.program_id(0)
    slot = step % 2

    @pl.when(step == 0)
    def _prologue():
        def cast_piece(src_hbm, dst, rb, nrows):
            def consume(view):
                dst[rb * nrows:(rb + 1) * nrows, :] = view[...].astype(BF16)
            return src_hbm.at[pl.ds(rb * nrows, nrows), :], consume

        pieces = []
        for rb in range(D_MODEL // FFN_STAGE_ROWS):
            pieces.append(cast_piece(wgate_hbm, wgate_ref, rb, FFN_STAGE_ROWS))
            pieces.append(cast_piece(wup_hbm, wup_ref, rb, FFN_STAGE_ROWS))
        _stream_pieces(pieces, stage_a, stage_a_sems)
        _stream_pieces([cast_piece(wdown_hbm, wdown_ref, rb, FFN_CHUNK) for rb in range(D_FF // FFN_CHUNK)],
                       stage_b, stage_b_sems)

    @pl.when(step < n_chunks)
    def _prompt_step():
        @pl.when(step >= 2)
        def _():
            for c in _batch_major_copies(y_buf, slot, y_hbm, y_sems, step - 2, to_hbm=True):
                c.wait()

        tl, nb, _ = x_ref.shape
        sub = FFN_SUB_ROWS // nb
        for h in range(tl // sub):
            y_buf[slot, pl.ds(h * sub, sub)] = _ffn_rows(
                x_ref[h * sub:(h + 1) * sub], mod2p_ref, n2g_ref, wgate_ref, wup_ref, wdown_ref, gf_ref,
                act_buf.at[pl.ds(h * FFN_SUB_ROWS, FFN_SUB_ROWS), :])
        for c in _batch_major_copies(y_buf, slot, y_hbm, y_sems, step, to_hbm=True):
            c.start()

    @pl.when(step == n_chunks)
    def _sample_step():
        ns = xs_ref.shape[0]
        ys_o[...] = _ffn_rows(xs_ref[...][None], mod2s_ref, n2g_ref, wgate_ref, wup_ref, wdown_ref, gf_ref,
                              act_buf.at[pl.ds(0, ns), :])[0]
        for s in range(max(n_chunks - 2, 0), n_chunks):
            for c in _batch_major_copies(y_buf, s % 2, y_hbm, y_sems, s, to_hbm=True):
                c.wait()


def _ffn(x32_t, mod2p, mod2s, x32_s, w, out_shape_p, out_shape_s):
    t_len, nb, _ = x32_t.shape
    tl = FFN_TIME_CHUNK
    n_chunks = t_len // tl
    rows = tl * nb
    inputs = [
        (x32_t, pl.BlockSpec((tl, nb, D_MODEL), lambda i: (jnp.minimum(i, n_chunks - 1), 0, 0))),
        (mod2p, None), (mod2s, None), (x32_s, None), (w["n2g"], None),
        (w["wgate"], _HBM), (w["wup"], _HBM), (w["wdown"], _HBM), (w["gf"], None),
    ]
    in_specs = [spec if spec is not None else _const_spec(a.shape) for a, spec in inputs]
    scratch = [
        pltpu.VMEM((STAGE_DEPTH, FFN_STAGE_ROWS, D_FF), F32),
        pltpu.VMEM((STAGE_DEPTH, FFN_CHUNK, D_MODEL), F32),
        pltpu.SemaphoreType.DMA((STAGE_DEPTH,)),
        pltpu.SemaphoreType.DMA((STAGE_DEPTH,)),
        pltpu.VMEM((D_MODEL, D_FF), BF16),
        pltpu.VMEM((D_MODEL, D_FF), BF16),
        pltpu.VMEM((D_FF, D_MODEL), BF16),
        pltpu.VMEM((rows, D_FF), BF16),
        pltpu.VMEM((2, tl, nb, D_MODEL), F32),
        pltpu.SemaphoreType.DMA((2, nb)),
    ]
    return pl.pallas_call(
        functools.partial(_ffn_kernel, n_chunks),
        grid=(n_chunks + 1,),
        in_specs=in_specs,
        out_specs=[_HBM, _out_const_spec(out_shape_s)],
        out_shape=[jax.ShapeDtypeStruct(out_shape_p, F32), jax.ShapeDtypeStruct(out_shape_s, F32)],
        scratch_shapes=scratch,
        compiler_params=pltpu.CompilerParams(
            dimension_semantics=("arbitrary",), vmem_limit_bytes=VMEM_LIMIT_BYTES),
        name="ffn",
    )(*[a for a, _ in inputs])


def kernel(x_prompt, x_sample, state_conv, state_lru, state_s5_re, state_s5_im, c_prompt, c_sample,
           ada_w, ada_b, norm1_g, w_in, conv_w, conv_b, lru_wa, lru_ba, lru_wx, lru_bx, lru_lambda,
           s5_lambda_re, s5_lambda_im, s5_log_dt, s5_b_re, s5_b_im, s5_c_re, s5_c_im, s5_d, s5_w_glu,
           g_lru_out, g_s5_out, w_out, norm2_g, ffn_w_gate, ffn_w_up, ffn_w_down, final_norm_g):
    depth = ada_w.shape[0]
    assert depth == 1, "single-layer decoder step"
    bp, t_len, _ = x_prompt.shape
    bs, dec_t, _ = x_sample.shape
    assert bp == SUBLANES and dec_t == 1 and t_len % TIME_CHUNK == 0 and t_len % FFN_TIME_CHUNK == 0
    assert (FFN_TIME_CHUNK * SUBLANES) % FFN_SUB_ROWS == 0 and bs <= FFN_SUB_ROWS

    w = dict(
        n1g=norm1_g, win=w_in[0], cw=conv_w[0], cb=conv_b,
        wa=lru_wa.reshape(D_LRU, LRU_HEAD_DIM), wx=lru_wx.reshape(D_LRU, LRU_HEAD_DIM),
        ba=lru_ba, bx=lru_bx, lam=lru_lambda,
        slr=s5_lambda_re[0], sli=s5_lambda_im[0], sldt=s5_log_dt,
        bre=s5_b_re.transpose(0, 1, 3, 2).reshape(D_S5, S5_STATE),
        bim=s5_b_im.transpose(0, 1, 3, 2).reshape(D_S5, S5_STATE),
        cre=s5_c_re.reshape(D_S5, S5_STATE), cim=s5_c_im.reshape(D_S5, S5_STATE),
        d=s5_d, wglu=s5_w_glu[0], glru=g_lru_out, gs5=g_s5_out, wout=w_out[0],
        n2g=norm2_g, wgate=ffn_w_gate[0], wup=ffn_w_up[0], wdown=ffn_w_down[0],
        gf=final_norm_g.reshape(1, D_MODEL),
    )

    (x32_t, mod2p, mod2s, conv_p, lru_p, s5r_p, s5i_p,
     x32_s, conv_s, lru_s, s5r_s, s5i_s) = _mixer(
        x_prompt, c_prompt, c_sample, ada_w[0], ada_b, x_sample.reshape(bs, D_MODEL),
        state_conv[0].transpose(1, 0, 2), state_lru[0],
        state_s5_re[0].transpose(1, 2, 0), state_s5_im[0].transpose(1, 2, 0), w)
    y_prompt, y_s = _ffn(x32_t, mod2p, mod2s, x32_s, w, x_prompt.shape, x32_s.shape)

    return (y_prompt, y_s.reshape(x_sample.shape), conv_p.transpose(1, 0, 2)[None], lru_p[None],
            s5r_p[None], s5i_p[None], conv_s.transpose(1, 0, 2)[None], lru_s[None],
            s5r_s.transpose(2, 0, 1)[None], s5i_s.transpose(2, 0, 1)[None])
```

```python
import functools

import jax
import jax.numpy as jnp
from jax import lax
from jax.experimental import pallas as pl
from jax.experimental.pallas import tpu as pltpu

F32 = jnp.float32
BF16 = jnp.bfloat16
I32 = jnp.int32

D_MODEL = 1024
D_LRU = 512
D_S5 = 512
D_IN = 2 * D_LRU + D_S5
LRU_HEAD_DIM = 64
CONV_W = 4
LRU_C = 8.0
S5_GROUP = 16
S5_GROUPS = D_S5 // S5_GROUP
S5_STATE = 64
S5_LANES = S5_GROUPS * S5_STATE
D_FF = 2816
N_MOD = 6
EPS = 1e-6

SUBLANES = 8
MXU_DIM = 256
VMEM_LIMIT_BYTES = 60 * 1024 * 1024

TIME_CHUNK = 64
FRONT_SLICES = 2
GATE_BLOCK = MXU_DIM
GATE_BLOCKS = D_LRU // GATE_BLOCK
S5_BLOCKS = 4
S5_BLOCK_GROUPS = S5_GROUPS // S5_BLOCKS
S5_BLOCK_CH = D_S5 // S5_BLOCKS
S5_BLOCK_LANES = S5_LANES // S5_BLOCKS
FFN_CHUNK = MXU_DIM
FFN_CHUNKS = D_FF // FFN_CHUNK
STAGE_ROWS = 256
STAGE_COLS = D_IN
STAGE_DEPTH = 3
BG_COLS = D_FF // 2
FFN_TIME_CHUNK = 128
FFN_SUB_ROWS = 512


def _dot(a, b):
    return jnp.dot(a, b, preferred_element_type=F32)


def _rms(x, g):
    return x * lax.rsqrt(jnp.mean(x * x, axis=-1, keepdims=True) + EPS) * g


def _log_sigmoid(x):
    return jnp.minimum(x, 0.0) - jnp.log1p(jnp.exp(-jnp.abs(x)))


def _sigmoid(x):
    return 0.5 * jnp.tanh(0.5 * x) + 0.5


def _sqrt_nonneg(x):
    return jnp.where(x > 0.0, x * lax.rsqrt(x), 0.0)


def _log2(n):
    assert n & (n - 1) == 0
    return n.bit_length() - 1


def _same_group(shape, row_group, col_group):
    r = lax.broadcasted_iota(I32, shape, 0) >> _log2(row_group)
    c = lax.broadcasted_iota(I32, shape, 1) >> _log2(col_group)
    return r == c


def _lane_tiled_block_diag(blk, reps, row_group):
    r, w = blk.shape
    n = w * reps
    sel = (lax.broadcasted_iota(I32, (w, n), 1) & (w - 1)) == lax.broadcasted_iota(I32, (w, n), 0)
    tiled = _dot(blk.astype(BF16), jnp.where(sel, 1.0, 0.0).astype(BF16))
    return jnp.where(_same_group((r, n), row_group, w), tiled, 0.0).astype(BF16)


def _split3(v):
    hi = v.astype(BF16)
    r1 = v - hi.astype(F32)
    mid = r1.astype(BF16)
    lo = (r1 - mid.astype(F32)).astype(BF16)
    return hi, mid, lo


def _flatten_rows(v):
    g, n = v.shape
    sel = (lax.broadcasted_iota(I32, (n, g * n), 1) & (n - 1)) == lax.broadcasted_iota(I32, (n, g * n), 0)
    sel = jnp.where(sel, 1.0, 0.0).astype(BF16)
    hi, mid, lo = (_dot(p, sel) for p in _split3(v))
    tiled = (hi + mid) + lo
    return jnp.sum(jnp.where(_same_group((g, g * n), 1, n), tiled, 0.0), axis=0, keepdims=True)


def _repeat_rows(v, reps):
    g, n = v.shape
    sel = (lax.broadcasted_iota(I32, (g * reps, g), 0) >> _log2(reps)) == lax.broadcasted_iota(I32, (g * reps, g), 1)
    sel = jnp.where(sel, 1.0, 0.0).astype(BF16)
    hi, mid, lo = (_dot(sel, p) for p in _split3(v))
    return (hi + mid) + lo


def _lru_gate_block(conv_j, wg_j, ba_j, bx_j, lam_j):
    gz = _dot(conv_j.astype(BF16), wg_j)
    r = _sigmoid(gz[:, :GATE_BLOCK] + ba_j)
    ig = _sigmoid(gz[:, GATE_BLOCK:] + bx_j)
    log_a = (LRU_C * _log_sigmoid(lam_j)) * r
    a = jnp.exp(log_a)
    mult = _sqrt_nonneg(-jnp.tanh(log_a) * (a * a + 1.0))
    return a, mult * (ig * conv_j)


def _s5_bu_block(ub, wb_ref, j):
    bu = _dot(ub[:, j * S5_BLOCK_CH:(j + 1) * S5_BLOCK_CH], wb_ref[j])
    return bu[:, :S5_BLOCK_LANES], bu[:, S5_BLOCK_LANES:]


def _s5_y_block(hr, hi, wc_ref, j):
    return (_dot(hr.astype(BF16), wc_ref[j, :S5_BLOCK_LANES, :])
            + _dot(hi.astype(BF16), wc_ref[j, S5_BLOCK_LANES:, :]))


def _s5_glu(y, u, d, wglu_ref):
    z = _dot(jax.nn.gelu(y + d * u).astype(BF16), wglu_ref[...])
    return z[:, :D_S5] * _sigmoid(z[:, D_S5:])


def _stream_pieces(pieces, stage, sems):
    depth = stage.shape[0]

    def dst(k):
        r, c = pieces[k][0].shape
        return stage.at[k % depth, pl.ds(0, r), pl.ds(0, c)]

    copies = [pltpu.make_async_copy(pieces[k][0], dst(k), sems.at[k % depth]) for k in range(len(pieces))]
    for k in range(min(depth - 1, len(pieces))):
        copies[k].start()
    for k, (_, consume) in enumerate(pieces):
        if k + depth - 1 < len(pieces):
            copies[k + depth - 1].start()
        copies[k].wait()
        consume(dst(k))


def _batch_major_copies(tm_bufs, slot, bm_hbm, sems, chunk, to_hbm):
    _, tl, nb, _ = tm_bufs.shape
    copies = []
    for b in range(nb):
        hbm = bm_hbm.at[b, pl.ds(chunk * tl, tl), :]
        vmem = tm_bufs.at[slot, :, b, :]
        src, dst = (vmem, hbm) if to_hbm else (hbm, vmem)
        copies.append(pltpu.make_async_copy(src, dst, sems.at[slot, b]))
    return copies


def _mixer_kernel(
        n_chunks,
        x_hbm, cp_ref, cs_ref, adaw_hbm, adab_ref,
        xs_ref, sconv_ref, slru_ref, ss5r_ref, ss5i_ref,
        n1g_ref, win_hbm, cw_ref, cb_ref, wa_ref, wx_ref, ba_ref, bx_ref, lam_ref,
        slr_ref, sli_ref, sldt_ref, bre_ref, bim_ref, cre_ref, cim_ref,
        d_ref, wglu_hbm, glru_ref, gs5_ref, wout_hbm, wgate_hbm, wup_hbm, wdown_hbm,
        o_ref, mod2p_o, mod2s_o, convp_o, lrup_o, s5rp_o, s5ip_o,
        xs_o, convs_o, lrus_o, s5rs_o, s5is_o, wgate_o, wup_o, wdown_o,
        x_buf, x_sems, stage, stage_sems, win_b, wglu_b, wout_b, wg_b, wb_b, wc_b, abr_s, abi_s,
        mod1p, mod1s, xc_buf, h_lru, h_s5r, h_s5i, p_buf, hr_buf, hi_buf, a_buf, b_buf, y_buf, m_buf,
        bg_buf, bg_sems):
    step = pl.program_id(0)
    slot = step % 2
    _, tl, nb, _ = x_buf.shape
    rows = tl * nb
    hist = (CONV_W - 1) * nb
    half_mod = (N_MOD // 2) * D_MODEL

    bg_pieces = []
    for src, dst in ((wgate_hbm, wgate_o), (wup_hbm, wup_o)):
        for rb in range(D_MODEL // STAGE_ROWS):
            for ch in range(D_FF // BG_COLS):
                win = (pl.ds(rb * STAGE_ROWS, STAGE_ROWS), pl.ds(ch * BG_COLS, BG_COLS))
                bg_pieces.append((src.at[win], dst.at[win]))
    for rb in range(D_FF // STAGE_ROWS):
        win = (pl.ds(rb * STAGE_ROWS, STAGE_ROWS), pl.ds(0, D_MODEL))
        bg_pieces.append((wdown_hbm.at[win], wdown_o.at[win]))
    assert len(bg_pieces) < n_chunks

    def bg_in(k):
        r, c = bg_pieces[k][0].shape
        return pltpu.make_async_copy(bg_pieces[k][0], stage.at[k % STAGE_DEPTH, pl.ds(0, r), pl.ds(0, c)],
                                     stage_sems.at[k % STAGE_DEPTH])

    def bg_out(k):
        r, c = bg_pieces[k][0].shape
        return pltpu.make_async_copy(bg_buf.at[k % 2, pl.ds(0, r), pl.ds(0, c)], bg_pieces[k][1],
                                     bg_sems.at[k % 2])

    @pl.when(step == 0)
    def _prologue():
        for c in _batch_major_copies(x_buf, 0, x_hbm, x_sems, 0, to_hbm=False):
            c.start()
        xc_buf[0:hist, :] = jnp.zeros((hist, D_LRU), F32)
        h_lru[...] = jnp.zeros_like(h_lru)
        h_s5r[...] = jnp.zeros_like(h_s5r)
        h_s5i[...] = jnp.zeros_like(h_s5i)

        ns_rows = cs_ref.shape[0]
        c_all = jnp.concatenate([cs_ref[...], cp_ref[...]], axis=0)
        silu_all = (c_all * _sigmoid(c_all)).astype(BF16)
        pieces = []

        def ada_piece(cg, kb):
            cols = slice(cg * STAGE_COLS, (cg + 1) * STAGE_COLS)
            first = cols.start < half_mod
            dst_p, dst_s = (mod1p, mod1s) if first else (mod2p_o, mod2s_o)
            dcols = cols if first else slice(cols.start - half_mod, cols.stop - half_mod)
            krows = slice(kb * STAGE_ROWS, (kb + 1) * STAGE_ROWS)

            def consume(view):
                part = _dot(silu_all[:, krows], view[...].astype(BF16))
                part_s = part[:ns_rows]
                part_p = part[ns_rows:]
                if kb == 0:
                    dst_p[:, dcols] = part_p + adab_ref[:, cols]
                    dst_s[:, dcols] = part_s + adab_ref[:, cols]
                else:
                    dst_p[:, dcols] += part_p
                    dst_s[:, dcols] += part_s
            return adaw_hbm.at[pl.ds(kb * STAGE_ROWS, STAGE_ROWS), pl.ds(cg * STAGE_COLS, STAGE_COLS)], consume

        def cast_piece(src_hbm, dst, rb):
            ncols = dst.shape[1]

            def consume(view):
                dst[rb * STAGE_ROWS:(rb + 1) * STAGE_ROWS, :] = view[...].astype(BF16)
            return src_hbm.at[pl.ds(rb * STAGE_ROWS, STAGE_ROWS), pl.ds(0, ncols)], consume

        for rb in range(win_b.shape[0] // STAGE_ROWS):
            pieces.append(cast_piece(win_hbm, win_b, rb))
        for cg in range(N_MOD * D_MODEL // STAGE_COLS):
            for kb in range(D_MODEL // STAGE_ROWS):
                pieces.append(ada_piece(cg, kb))
        for src_hbm, dst in ((wglu_hbm, wglu_b), (wout_hbm, wout_b)):
            for rb in range(dst.shape[0] // STAGE_ROWS):
                pieces.append(cast_piece(src_hbm, dst, rb))
        _stream_pieces(pieces, stage, stage_sems)
        bg_in(0).start()

        for j in range(GATE_BLOCKS):
            rs = slice(j * GATE_BLOCK, (j + 1) * GATE_BLOCK)
            reps = GATE_BLOCK // LRU_HEAD_DIM
            wg_b[j, :, :GATE_BLOCK] = _lane_tiled_block_diag(wa_ref[rs, :], reps, LRU_HEAD_DIM)
            wg_b[j, :, GATE_BLOCK:] = _lane_tiled_block_diag(wx_ref[rs, :], reps, LRU_HEAD_DIM)

        lr = slr_ref[...]
        li = sli_ref[...]
        diag = (lax.broadcasted_iota(I32, (S5_GROUPS, S5_GROUPS), 0)
                == lax.broadcasted_iota(I32, (S5_GROUPS, S5_GROUPS), 1))
        log_dt = jnp.sum(jnp.where(diag, sldt_ref[...], 0.0), axis=1, keepdims=True)
        dt = jnp.exp(log_dt)
        mag = jnp.exp(lr * dt)
        ab_r = mag * jnp.cos(li * dt)
        ab_i = mag * jnp.sin(li * dt)
        den = lr * lr + li * li
        fr = ((ab_r - 1.0) * lr + ab_i * li) / den
        fi = (ab_i * lr - (ab_r - 1.0) * li) / den
        abr_s[...] = _flatten_rows(ab_r)
        abi_s[...] = _flatten_rows(ab_i)
        br = bre_ref[...]
        bi = bim_ref[...]
        fr_c = _repeat_rows(fr, S5_GROUP)
        fi_c = _repeat_rows(fi, S5_GROUP)
        bb_r = fr_c * br - fi_c * bi
        bb_i = fr_c * bi + fi_c * br
        for j in range(S5_BLOCKS):
            cs_rows = slice(j * S5_BLOCK_CH, (j + 1) * S5_BLOCK_CH)
            bt_r = _lane_tiled_block_diag(bb_r[cs_rows, :], S5_BLOCK_GROUPS, S5_GROUP)
            bt_i = _lane_tiled_block_diag(bb_i[cs_rows, :], S5_BLOCK_GROUPS, S5_GROUP)
            wb_b[j, :, :S5_BLOCK_LANES] = bt_r
            wb_b[j, :, S5_BLOCK_LANES:] = bt_i
            ct_r = _lane_tiled_block_diag(cre_ref[cs_rows, :], S5_BLOCK_GROUPS, S5_GROUP)
            ct_i = _lane_tiled_block_diag(-cim_ref[cs_rows, :], S5_BLOCK_GROUPS, S5_GROUP)
            wc_b[j, :S5_BLOCK_LANES, :] = ct_r.astype(F32).T.astype(BF16)
            wc_b[j, S5_BLOCK_LANES:, :] = ct_i.astype(F32).T.astype(BF16)

    @pl.when(step < n_chunks)
    def _prompt_step():
        @pl.when(step + 1 < n_chunks)
        def _():
            for c in _batch_major_copies(x_buf, 1 - slot, x_hbm, x_sems, step + 1, to_hbm=False):
                c.start()

        for c in _batch_major_copies(x_buf, slot, x_hbm, x_sems, step, to_hbm=False):
            c.wait()

        sh1 = mod1p[:, 0:D_MODEL]
        sc1 = mod1p[:, D_MODEL:2 * D_MODEL]
        g1 = mod1p[:, 2 * D_MODEL:3 * D_MODEL]

        scale1 = (n1g_ref[...] * (1.0 + sc1))[None]
        tsub = tl // FRONT_SLICES

        def project(s):
            r0, r1 = s * tsub * nb, (s + 1) * tsub * nb
            xs3 = x_buf[slot, pl.ds(s * tsub, tsub)]
            hn = _rms(xs3, scale1) + sh1[None]
            p_buf[r0:r1, :] = _dot(hn.reshape(r1 - r0, D_MODEL).astype(BF16), win_b[...])
            xc_buf[hist + r0:hist + r1, :] = p_buf[r0:r1, 0:D_LRU]

        def lru_gates(s):
            r0, r1 = s * tsub * nb, (s + 1) * tsub * nb
            for j in range(GATE_BLOCKS):
                sl = slice(j * GATE_BLOCK, (j + 1) * GATE_BLOCK)
                conv_j = cb_ref[:, sl]
                for k in range(CONV_W):
                    conv_j = conv_j + cw_ref[k:k + 1, sl] * xc_buf[k * nb + r0:k * nb + r1, sl]
                a, b = _lru_gate_block(conv_j, wg_b[j], ba_ref[:, sl], bx_ref[:, sl], lam_ref[:, sl])
                a_buf[r0:r1, sl] = a
                b_buf[r0:r1, sl] = b

        def lru_scan():
            h = h_lru[...]
            for t in range(tl):
                h = a_buf[t * nb:(t + 1) * nb, :] * h + b_buf[t * nb:(t + 1) * nb, :]
                b_buf[t * nb:(t + 1) * nb, :] = h
            h_lru[...] = h
            lru_out = b_buf[...] * jax.nn.gelu(p_buf[:, D_LRU:2 * D_LRU])
            m_buf[:, 0:D_LRU] = _rms(lru_out, glru_ref[...]).astype(BF16)

        def s5_input(j):
            ub = p_buf[:, 2 * D_LRU + j * S5_BLOCK_CH:2 * D_LRU + (j + 1) * S5_BLOCK_CH].astype(BF16)
            bu = _dot(ub, wb_b[j])
            hr_buf[j % 2] = bu[:, :S5_BLOCK_LANES]
            hi_buf[j % 2] = bu[:, S5_BLOCK_LANES:]

        def s5_scan(j):
            sl = slice(j * S5_BLOCK_LANES, (j + 1) * S5_BLOCK_LANES)
            hr_j = hr_buf.at[j % 2]
            hi_j = hi_buf.at[j % 2]
            ar = jnp.broadcast_to(abr_s[:, sl], (nb, S5_BLOCK_LANES))
            ai = jnp.broadcast_to(abi_s[:, sl], (nb, S5_BLOCK_LANES))
            hr = h_s5r[:, sl]
            hi = h_s5i[:, sl]
            for t in range(tl):
                rs = slice(t * nb, (t + 1) * nb)
                hr, hi = ar * hr - ai * hi + hr_j[rs, :], ar * hi + ai * hr + hi_j[rs, :]
                hr_j[rs, :] = hr
                hi_j[rs, :] = hi
            h_s5r[:, sl] = hr
            h_s5i[:, sl] = hi

        def s5_readout(j):
            y_buf[:, j * S5_BLOCK_CH:(j + 1) * S5_BLOCK_CH] = _s5_y_block(
                hr_buf[j % 2], hi_buf[j % 2], wc_b, j)

        for s in range(FRONT_SLICES):
            project(s)
        lru_gates(0)
        s5_input(0)
        s5_input(1)
        for s in range(1, FRONT_SLICES):
            lru_gates(s)
        for j in range(S5_BLOCKS):
            s5_scan(j)
            s5_readout(j)
            if j + 2 < S5_BLOCKS:
                s5_input(j + 2)
            if j == 1:
                lru_scan()

        s5_out = _s5_glu(y_buf[...], p_buf[:, 2 * D_LRU:], d_ref[...], wglu_b)
        m_buf[:, D_LRU:] = _rms(s5_out, gs5_ref[...]).astype(BF16)

        mix = _dot(m_buf[...], wout_b[...]).reshape(tl, nb, D_MODEL)
        o_ref[...] = x_buf[slot] + g1[None] * mix

        xc_buf[0:hist, :] = xc_buf[rows:rows + hist, :]

    for k in range(len(bg_pieces)):
        @pl.when(step == k + 1)
        def _bg_task(k=k):
            r, c = bg_pieces[k][0].shape
            bg_in(k).wait()
            if k >= 2:
                bg_out(k - 2).wait()
            bg_buf[k % 2, 0:r, 0:c] = stage[k % STAGE_DEPTH, 0:r, 0:c].astype(BF16)
            bg_out(k).start()
            if k + 1 < len(bg_pieces):
                bg_in(k + 1).start()

    @pl.when(step == n_chunks)
    def _sample_step():
        for k in range(max(len(bg_pieces) - 2, 0), len(bg_pieces)):
            bg_out(k).wait()
        for k in range(CONV_W - 1):
            convp_o[k] = xc_buf[k * nb:(k + 1) * nb, :]
        lrup_o[...] = h_lru[...]
        s5rp_o[...] = h_s5r[...].reshape(s5rp_o.shape)
        s5ip_o[...] = h_s5i[...].reshape(s5ip_o.shape)

        ns = xs_ref.shape[0]
        sh1 = mod1s[:, 0:D_MODEL]
        sc1 = mod1s[:, D_MODEL:2 * D_MODEL]
        g1 = mod1s[:, 2 * D_MODEL:3 * D_MODEL]

        x = xs_ref[...]
        hn = _rms(x, n1g_ref[...] * (1.0 + sc1)) + sh1
        p = _dot(hn.astype(BF16), win_b[...])
        lru_x = p[:, 0:D_LRU]

        for k in range(CONV_W - 2):
            convs_o[k] = sconv_ref[k + 1]
        convs_o[CONV_W - 2] = lru_x
        for j in range(GATE_BLOCKS):
            sl = slice(j * GATE_BLOCK, (j + 1) * GATE_BLOCK)
            conv_j = cb_ref[:, sl] + cw_ref[CONV_W - 1:CONV_W, sl] * lru_x[:, sl]
            for k in range(CONV_W - 1):
                conv_j = conv_j + cw_ref[k:k + 1, sl] * sconv_ref[k, :, sl]
            a, b = _lru_gate_block(conv_j, wg_b[j], ba_ref[:, sl], bx_ref[:, sl], lam_ref[:, sl])
            lrus_o[:, sl] = a * slru_ref[:, sl] + b
        lru_out = lrus_o[...] * jax.nn.gelu(p[:, D_LRU:2 * D_LRU])
        ms_buf = m_buf.at[pl.ds(0, ns), :]
        ms_buf[:, 0:D_LRU] = _rms(lru_out, glru_ref[...]).astype(BF16)

        u = p[:, 2 * D_LRU:]
        ub = u.astype(BF16)
        ys_buf = y_buf.at[pl.ds(0, ns), :]
        h0r = ss5r_ref[...].reshape(S5_LANES, ns).T
        h0i = ss5i_ref[...].reshape(S5_LANES, ns).T
        for j in range(S5_BLOCKS):
            sl = slice(j * S5_BLOCK_LANES, (j + 1) * S5_BLOCK_LANES)
            bu_r, bu_i = _s5_bu_block(ub, wb_b, j)
            ar = abr_s[:, sl]
            ai = abi_s[:, sl]
            hr = ar * h0r[:, sl] - ai * h0i[:, sl] + bu_r
            hi = ar * h0i[:, sl] + ai * h0r[:, sl] + bu_i
            gs = slice(j * S5_BLOCK_GROUPS, (j + 1) * S5_BLOCK_GROUPS)
            s5rs_o[gs] = hr.T.reshape(S5_BLOCK_GROUPS, S5_STATE, ns)
            s5is_o[gs] = hi.T.reshape(S5_BLOCK_GROUPS, S5_STATE, ns)
            ys_buf[:, j * S5_BLOCK_CH:(j + 1) * S5_BLOCK_CH] = _s5_y_block(hr, hi, wc_b, j)
        s5_out = _s5_glu(ys_buf[...], u, d_ref[...], wglu_b)
        ms_buf[:, D_LRU:] = _rms(s5_out, gs5_ref[...]).astype(BF16)
        xs_o[...] = x + g1 * _dot(ms_buf[...], wout_b[...])


def _const_spec(shape):
    nd = len(shape)
    return pl.BlockSpec(shape, lambda i, _nd=nd: (0,) * _nd, pipeline_mode=pl.Buffered(1))


def _out_const_spec(shape):
    nd = len(shape)
    return pl.BlockSpec(shape, lambda i, _nd=nd: (0,) * _nd)


_HBM = pl.BlockSpec(memory_space=pl.ANY)


def _mixer(x_prompt, c_prompt, c_sample, ada_w, ada_b, x_s, sconv, slru, ss5r, ss5i, w):
    nb, t_len, _ = x_prompt.shape
    ns = x_s.shape[0]
    tl = TIME_CHUNK
    n_chunks = t_len // tl
    rows = tl * nb
    hist = (CONV_W - 1) * nb
    half_mod = (N_MOD // 2) * D_MODEL

    inputs = [
        (x_prompt, _HBM), (c_prompt, None), (c_sample, None), (ada_w, _HBM), (ada_b, None),
        (x_s, None), (sconv, None), (slru, None), (ss5r, None), (ss5i, None),
        (w["n1g"], None), (w["win"], _HBM), (w["cw"], None), (w["cb"], None), (w["wa"], None), (w["wx"], None),
        (w["ba"], None), (w["bx"], None), (w["lam"], None),
        (w["slr"], None), (w["sli"], None), (w["sldt"], None), (w["bre"], None), (w["bim"], None),
        (w["cre"], None), (w["cim"], None),
        (w["d"], None), (w["wglu"], _HBM), (w["glru"], None), (w["gs5"], None), (w["wout"], _HBM),
        (w["wgate"], _HBM), (w["wup"], _HBM), (w["wdown"], _HBM),
    ]
    in_specs = [spec if spec is not None else _const_spec(a.shape) for a, spec in inputs]

    out_shapes = [
        ((t_len, nb, D_MODEL), pl.BlockSpec((tl, nb, D_MODEL), lambda i: (jnp.minimum(i, n_chunks - 1), 0, 0))),
        ((nb, half_mod), None), ((ns, half_mod), None),
        ((CONV_W - 1, nb, D_LRU), None), ((nb, D_LRU), None),
        ((nb, S5_GROUPS, S5_STATE), None), ((nb, S5_GROUPS, S5_STATE), None),
        ((ns, D_MODEL), None), ((CONV_W - 1, ns, D_LRU), None), ((ns, D_LRU), None),
        ((S5_GROUPS, S5_STATE, ns), None), ((S5_GROUPS, S5_STATE, ns), None),
    ]
    out_specs = [spec if spec is not None else _out_const_spec(s) for s, spec in out_shapes]
    out_shape = [jax.ShapeDtypeStruct(s, F32) for s, _ in out_shapes]
    for wkey in ("wgate", "wup", "wdown"):
        out_specs.append(_HBM)
        out_shape.append(jax.ShapeDtypeStruct(w[wkey].shape, BF16))

    scratch = [
        pltpu.VMEM((2, tl, nb, D_MODEL), F32),
        pltpu.SemaphoreType.DMA((2, nb)),
        pltpu.VMEM((STAGE_DEPTH, STAGE_ROWS, STAGE_COLS), F32),
        pltpu.SemaphoreType.DMA((STAGE_DEPTH,)),
        pltpu.VMEM((D_MODEL, D_IN), BF16),
        pltpu.VMEM((D_S5, 2 * D_S5), BF16),
        pltpu.VMEM((D_LRU + D_S5, D_MODEL), BF16),
        pltpu.VMEM((GATE_BLOCKS, GATE_BLOCK, 2 * GATE_BLOCK), BF16),
        pltpu.VMEM((S5_BLOCKS, S5_BLOCK_CH, 2 * S5_BLOCK_LANES), BF16),
        pltpu.VMEM((S5_BLOCKS, 2 * S5_BLOCK_LANES, S5_BLOCK_CH), BF16),
        pltpu.VMEM((1, S5_LANES), F32),
        pltpu.VMEM((1, S5_LANES), F32),
        pltpu.VMEM((nb, half_mod), F32),
        pltpu.VMEM((ns, half_mod), F32),
        pltpu.VMEM((hist + rows, D_LRU), F32),
        pltpu.VMEM((nb, D_LRU), F32),
        pltpu.VMEM((nb, S5_LANES), F32),
        pltpu.VMEM((nb, S5_LANES), F32),
        pltpu.VMEM((rows, D_IN), F32),
        pltpu.VMEM((2, rows, S5_BLOCK_LANES), F32),
        pltpu.VMEM((2, rows, S5_BLOCK_LANES), F32),
        pltpu.VMEM((rows, D_LRU), F32),
        pltpu.VMEM((rows, D_LRU), F32),
        pltpu.VMEM((rows, D_S5), F32),
        pltpu.VMEM((rows, D_LRU + D_S5), BF16),
        pltpu.VMEM((2, STAGE_ROWS, BG_COLS), BF16),
        pltpu.SemaphoreType.DMA((2,)),
    ]
    return pl.pallas_call(
        functools.partial(_mixer_kernel, n_chunks),
        grid=(n_chunks + 1,),
        in_specs=in_specs,
        out_specs=out_specs,
        out_shape=out_shape,
        scratch_shapes=scratch,
        compiler_params=pltpu.CompilerParams(
            dimension_semantics=("arbitrary",), vmem_limit_bytes=VMEM_LIMIT_BYTES),
        name="mixer",
    )(*[a for a, _ in inputs])


def _ffn_rows(x3, mod_ref, n2g_ref, wgate_ref, wup_ref, wdown_ref, gf_ref, act_ref):
    groups, mrows, _ = x3.shape
    rows = groups * mrows
    sh2 = mod_ref[:, 0:D_MODEL]
    sc2 = mod_ref[:, D_MODEL:2 * D_MODEL]
    g2 = mod_ref[:, 2 * D_MODEL:3 * D_MODEL]

    hn = _rms(x3, (n2g_ref[...] * (1.0 + sc2))[None]) + sh2[None]
    hb = hn.reshape(rows, D_MODEL).astype(BF16)
    for c in range(FFN_CHUNKS):
        cols = slice(c * FFN_CHUNK, (c + 1) * FFN_CHUNK)
        g = _dot(hb, wgate_ref[:, cols])
        up = _dot(hb, wup_ref[:, cols])
        act_ref[:, cols] = (g * jax.nn.sigmoid(g) * up).astype(BF16)
    ffn = _dot(act_ref[...], wdown_ref[...]).reshape(groups, mrows, D_MODEL)
    return _rms(x3 + g2[None] * ffn, gf_ref[...])


def _ffn_kernel(
        n_chunks,
        x_ref, mod2p_ref, mod2s_ref, xs_ref, n2g_ref, wgate_ref, wup_ref, wdown_ref, gf_ref,
        y_hbm, ys_o,
        act_buf, y_buf, y_sems):
    step = pl.program_id(0)
    slot = step % 2

    @pl.when(step < n_chunks)
    def _prompt_step():
        @pl.when(step >= 2)
        def _():
            for c in _batch_major_copies(y_buf, slot, y_hbm, y_sems, step - 2, to_hbm=True):
                c.wait()

        tl, nb, _ = x_ref.shape
        sub = FFN_SUB_ROWS // nb
        for h in range(tl // sub):
            y_buf[slot, pl.ds(h * sub, sub)] = _ffn_rows(
                x_ref[h * sub:(h + 1) * sub], mod2p_ref, n2g_ref, wgate_ref, wup_ref, wdown_ref, gf_ref,
                act_buf.at[pl.ds(h * FFN_SUB_ROWS, FFN_SUB_ROWS), :])
        for c in _batch_major_copies(y_buf, slot, y_hbm, y_sems, step, to_hbm=True):
            c.start()

    @pl.when(step == n_chunks)
    def _sample_step():
        ns = xs_ref.shape[0]
        ys_o[...] = _ffn_rows(xs_ref[...][None], mod2s_ref, n2g_ref, wgate_ref, wup_ref, wdown_ref, gf_ref,
                              act_buf.at[pl.ds(0, ns), :])[0]
        for s in range(max(n_chunks - 2, 0), n_chunks):
            for c in _batch_major_copies(y_buf, s % 2, y_hbm, y_sems, s, to_hbm=True):
                c.wait()


def _ffn(x32_t, mod2p, mod2s, x32_s, wgate_b, wup_b, wdown_b, w, out_shape_p, out_shape_s):
    t_len, nb, _ = x32_t.shape
    tl = FFN_TIME_CHUNK
    n_chunks = t_len // tl
    rows = tl * nb
    inputs = [
        (x32_t, pl.BlockSpec((tl, nb, D_MODEL), lambda i: (jnp.minimum(i, n_chunks - 1), 0, 0))),
        (mod2p, None), (mod2s, None), (x32_s, None), (w["n2g"], None),
        (wgate_b, None), (wup_b, None), (wdown_b, None), (w["gf"], None),
    ]
    in_specs = [spec if spec is not None else _const_spec(a.shape) for a, spec in inputs]
    scratch = [
        pltpu.VMEM((rows, D_FF), BF16),
        pltpu.VMEM((2, tl, nb, D_MODEL), F32),
        pltpu.SemaphoreType.DMA((2, nb)),
    ]
    return pl.pallas_call(
        functools.partial(_ffn_kernel, n_chunks),
        grid=(n_chunks + 1,),
        in_specs=in_specs,
        out_specs=[_HBM, _out_const_spec(out_shape_s)],
        out_shape=[jax.ShapeDtypeStruct(out_shape_p, F32), jax.ShapeDtypeStruct(out_shape_s, F32)],
        scratch_shapes=scratch,
        compiler_params=pltpu.CompilerParams(
            dimension_semantics=("arbitrary",), vmem_limit_bytes=VMEM_LIMIT_BYTES),
        name="ffn",
    )(*[a for a, _ in inputs])


def kernel(x_prompt, x_sample, state_conv, state_lru, state_s5_re, state_s5_im, c_prompt, c_sample,
           ada_w, ada_b, norm1_g, w_in, conv_w, conv_b, lru_wa, lru_ba, lru_wx, lru_bx, lru_lambda,
           s5_lambda_re, s5_lambda_im, s5_log_dt, s5_b_re, s5_b_im, s5_c_re, s5_c_im, s5_d, s5_w_glu,
           g_lru_out, g_s5_out, w_out, norm2_g, ffn_w_gate, ffn_w_up, ffn_w_down, final_norm_g):
    depth = ada_w.shape[0]
    assert depth == 1, "single-layer decoder step"
    bp, t_len, _ = x_prompt.shape
    bs, dec_t, _ = x_sample.shape
    assert bp == SUBLANES and dec_t == 1 and t_len % TIME_CHUNK == 0 and t_len % FFN_TIME_CHUNK == 0
    assert (FFN_TIME_CHUNK * SUBLANES) % FFN_SUB_ROWS == 0 and bs <= FFN_SUB_ROWS

    w = dict(
        n1g=norm1_g, win=w_in[0], cw=conv_w[0], cb=conv_b,
        wa=lru_wa.reshape(D_LRU, LRU_HEAD_DIM), wx=lru_wx.reshape(D_LRU, LRU_HEAD_DIM),
        ba=lru_ba, bx=lru_bx, lam=lru_lambda,
        slr=s5_lambda_re[0], sli=s5_lambda_im[0], sldt=s5_log_dt,
        bre=s5_b_re.transpose(0, 1, 3, 2).reshape(D_S5, S5_STATE),
        bim=s5_b_im.transpose(0, 1, 3, 2).reshape(D_S5, S5_STATE),
        cre=s5_c_re.reshape(D_S5, S5_STATE), cim=s5_c_im.reshape(D_S5, S5_STATE),
        d=s5_d, wglu=s5_w_glu[0], glru=g_lru_out, gs5=g_s5_out, wout=w_out[0],
        n2g=norm2_g, wgate=ffn_w_gate[0], wup=ffn_w_up[0], wdown=ffn_w_down[0],
        gf=final_norm_g.reshape(1, D_MODEL),
    )

    (x32_t, mod2p, mod2s, conv_p, lru_p, s5r_p, s5i_p,
     x32_s, conv_s, lru_s, s5r_s, s5i_s, wgate_b, wup_b, wdown_b) = _mixer(
        x_prompt, c_prompt, c_sample, ada_w[0], ada_b, x_sample.reshape(bs, D_MODEL),
        state_conv[0].transpose(1, 0, 2), state_lru[0],
        state_s5_re[0].transpose(1, 2, 0), state_s5_im[0].transpose(1, 2, 0), w)
    y_prompt, y_s = _ffn(x32_t, mod2p, mod2s, x32_s, wgate_b, wup_b, wdown_b, w, x_prompt.shape, x32_s.shape)

    return (y_prompt, y_s.reshape(x_sample.shape), conv_p.transpose(1, 0, 2)[None], lru_p[None],
            s5r_p[None], s5i_p[None], conv_s.transpose(1, 0, 2)[None], lru_s[None],
            s5r_s.transpose(2, 0, 1)[None], s5i_s.transpose(2, 0, 1)[None])
```
